```python
import math
import jax, jax.numpy as jnp
from jax import lax
import numpy as np

D_MODEL = 1024
BATCH = 8
SEQ = 8192
DEPTH = 1

D_MIX = D_MODEL
ATT_HEADS = 8
HEAD_DIM = 64
D_ATT = ATT_HEADS * HEAD_DIM
D_CONV = D_MIX - D_ATT
DILATED_PATTERNS = ((128, 1), (512, 4), (2048, 16))
ROPE_THETA = 500000.0
ROT_DIM = HEAD_DIM // 4
CONV_WIDTH = 31
N_MEM = 256
XATT_HEADS = 4
XATT_HEAD_DIM = D_MODEL // XATT_HEADS
D_FF = 4 * D_MODEL
D_IN = 3 * D_ATT + 2 * D_CONV
EPS = 1e-6
NEG_INF = -1e30

kernel_name = "hybrid_dilated_swa_conformer_encoder"


def rmsnorm(x, g):
    xf = x.astype(jnp.float32)
    var = jnp.mean(xf * xf, axis=-1, keepdims=True)
    return (xf * lax.rsqrt(var + EPS) * g.astype(jnp.float32)).astype(x.dtype)


def layernorm(x, g, b):
    xf = x.astype(jnp.float32)
    mu = jnp.mean(xf, axis=-1, keepdims=True)
    var = jnp.mean(jnp.square(xf - mu), axis=-1, keepdims=True)
    y = (xf - mu) * lax.rsqrt(var + EPS) * g.astype(jnp.float32) + b.astype(jnp.float32)
    return y.astype(x.dtype)


def partial_rotary(t):
    S = t.shape[1]
    half = ROT_DIM // 2
    freqs = ROPE_THETA ** (-jnp.arange(0, ROT_DIM, 2, dtype=jnp.float32) / ROT_DIM)
    ang = jnp.arange(S, dtype=jnp.float32)[:, None] * freqs[None, :]
    cos = jnp.cos(ang)[None, :, None, :]
    sin = jnp.sin(ang)[None, :, None, :]
    tf = t.astype(jnp.float32)
    x1, x2, rest = tf[..., :half], tf[..., half:ROT_DIM], tf[..., ROT_DIM:]
    rot = jnp.concatenate([x1 * cos - x2 * sin, x2 * cos + x1 * sin, rest], axis=-1)
    return rot.astype(t.dtype)


def to_strided(t, d):
    B, S = t.shape[:2]
    rest = t.shape[2:]
    t = jnp.moveaxis(t.reshape(B, S // d, d, *rest), 2, 1)
    return t.reshape(B * d, S // d, *rest)


def from_strided(t, d, B):
    N, L = t.shape[:2]
    rest = t.shape[2:]
    t = jnp.moveaxis(t.reshape(B, d, L, *rest), 1, 2)
    return t.reshape(B, L * d, *rest)


def banded_attention(q, k, v, half):
    N, L, H, Dh = q.shape
    blk = half
    nb = -(-L // blk)
    Lp = nb * blk
    pad = Lp - L
    qb = jnp.pad(q, ((0, 0), (0, pad), (0, 0), (0, 0))).reshape(N, nb, blk, H, Dh)
    kp = jnp.pad(k, ((0, 0), (blk, blk + pad), (0, 0), (0, 0)))
    vp = jnp.pad(v, ((0, 0), (blk, blk + pad), (0, 0), (0, 0)))

    def window(t):
        return jnp.concatenate(
            [t[:, i * blk:i * blk + Lp].reshape(N, nb, blk, H, Dh) for i in range(3)], axis=2)

    kb, vb = window(kp), window(vp)
    s = jnp.einsum('nbqhd,nbkhd->nbhqk', qb, kb).astype(jnp.float32) * (Dh ** -0.5)
    qpos = jnp.arange(nb)[:, None] * blk + jnp.arange(blk)[None, :]
    kpos = jnp.arange(nb)[:, None] * blk - blk + jnp.arange(3 * blk)[None, :]
    valid = ((jnp.abs(kpos[:, None, :] - qpos[:, :, None]) <= half)
             & (kpos >= 0)[:, None, :] & (kpos < L)[:, None, :])
    s = jnp.where(valid[None, :, None], s, NEG_INF)
    m = jnp.max(s, axis=-1, keepdims=True)
    p = jnp.exp(s - m)
    den = jnp.sum(p, axis=-1, keepdims=True)
    o = jnp.einsum('nbhqk,nbkhd->nbqhd', (p / den).astype(v.dtype), vb)
    lse = (m + jnp.log(den))[..., 0]
    o = o.reshape(N, Lp, H, Dh)[:, :L]
    lse = jnp.transpose(lse, (0, 1, 3, 2)).reshape(N, Lp, H)[:, :L]
    return o, lse


def dilated_sliding_attention(q, k, v):
    B = q.shape[0]
    outs, lses = [], []
    for window, d in DILATED_PATTERNS:
        half = window // (2 * d)
        o, lse = banded_attention(to_strided(q, d), to_strided(k, d), to_strided(v, d), half)
        outs.append(from_strided(o, d, B))
        lses.append(from_strided(lse, d, B))
    w = jax.nn.softmax(jnp.stack(lses, axis=0), axis=0)
    o = jnp.sum(w[..., None] * jnp.stack(outs, axis=0).astype(jnp.float32), axis=0)
    return o.astype(q.dtype)


def conformer_conv(a, g, conv_w, conv_b, ln_g, ln_b):
    u = a * jax.nn.sigmoid(g)
    C = u.shape[-1]
    u = lax.conv_general_dilated(
        u, conv_w.reshape(CONV_WIDTH, 1, C).astype(u.dtype),
        window_strides=(1,), padding=[((CONV_WIDTH - 1) // 2, (CONV_WIDTH - 1) // 2)],
        dimension_numbers=('NWC', 'WIO', 'NWC'), feature_group_count=C) + conv_b
    u = layernorm(u, ln_g, ln_b)
    return jax.nn.silu(u)


def _fwd_setup_inputs(seed: int = 0) -> dict:
    key = jax.random.key(seed)
    ks = jax.random.split(key, 20)
    f32 = jnp.float32

    def w(k, shape, fan_in):
        return jax.random.normal(k, shape, f32) * (fan_in ** -0.5)

    def gain(k, shape):
        return 1.0 + 0.02 * jax.random.normal(k, shape, f32)

    return {
        "x": jax.random.normal(ks[0], (BATCH, SEQ, D_MODEL), f32),
        "mem": jax.random.normal(ks[1], (BATCH, N_MEM, D_MODEL), f32),
        "norm_mix_g": gain(ks[2], (DEPTH, D_MODEL)),
        "w_in": w(ks[3], (DEPTH, D_MODEL, D_IN), D_MODEL),
        "conv_w": w(ks[4], (DEPTH, CONV_WIDTH, D_CONV), CONV_WIDTH),
        "conv_b": 0.02 * jax.random.normal(ks[5], (DEPTH, D_CONV), f32),
        "conv_ln_g": gain(ks[6], (DEPTH, D_CONV)),
        "conv_ln_b": 0.02 * jax.random.normal(ks[7], (DEPTH, D_CONV), f32),
        "w_out": w(ks[8], (DEPTH, D_MIX, D_MODEL), D_MIX),
        "norm_x_g": gain(ks[9], (DEPTH, D_MODEL)),
        "norm_mem_g": gain(ks[10], (DEPTH, D_MODEL)),
        "w_xq": w(ks[11], (DEPTH, D_MODEL, D_MODEL), D_MODEL),
        "w_xk": w(ks[12], (DEPTH, D_MODEL, D_MODEL), D_MODEL),
        "w_xv": w(ks[13], (DEPTH, D_MODEL, D_MODEL), D_MODEL),
        "w_xo": w(ks[14], (DEPTH, D_MODEL, D_MODEL), D_MODEL),
        "norm_mlp_g": gain(ks[15], (DEPTH, D_MODEL)),
        "w_up": w(ks[16], (DEPTH, D_MODEL, D_FF), D_MODEL),
        "w_down": w(ks[17], (DEPTH, D_FF, D_MODEL), D_FF),
        "norm_final_g": gain(ks[18], (D_MODEL,)),
    }


def _fwd_reference(x, mem, norm_mix_g, w_in, conv_w, conv_b, conv_ln_g, conv_ln_b, w_out,
              norm_x_g, norm_mem_g, w_xq, w_xk, w_xv, w_xo, norm_mlp_g, w_up, w_down,
              norm_final_g):
    B, S, _ = x.shape
    M = mem.shape[1]
    h = x
    for l in range(DEPTH):
        y = rmsnorm(h, norm_mix_g[l]) @ w_in[l]
        q = partial_rotary(y[..., 0:D_ATT].reshape(B, S, ATT_HEADS, HEAD_DIM))
        k = partial_rotary(y[..., D_ATT:2 * D_ATT].reshape(B, S, ATT_HEADS, HEAD_DIM))
        v = y[..., 2 * D_ATT:3 * D_ATT].reshape(B, S, ATT_HEADS, HEAD_DIM)
        att = dilated_sliding_attention(q, k, v).reshape(B, S, D_ATT)
        c0 = 3 * D_ATT
        conv = conformer_conv(y[..., c0:c0 + D_CONV], y[..., c0 + D_CONV:c0 + 2 * D_CONV],
                              conv_w[l], conv_b[l], conv_ln_g[l], conv_ln_b[l])
        h = h + jnp.concatenate([att, conv], axis=-1) @ w_out[l]

        xq = (rmsnorm(h, norm_x_g[l]) @ w_xq[l]).reshape(B, S, XATT_HEADS, XATT_HEAD_DIM)
        mn = rmsnorm(mem, norm_mem_g[l])
        xk = (mn @ w_xk[l]).reshape(B, M, XATT_HEADS, XATT_HEAD_DIM)
        xv = (mn @ w_xv[l]).reshape(B, M, XATT_HEADS, XATT_HEAD_DIM)
        sc = jnp.einsum('bshd,bmhd->bhsm', xq, xk).astype(jnp.float32) * (XATT_HEAD_DIM ** -0.5)
        pr = jax.nn.softmax(sc, axis=-1).astype(xv.dtype)
        xo = jnp.einsum('bhsm,bmhd->bshd', pr, xv).reshape(B, S, D_MODEL)
        h = h + xo @ w_xo[l]

        u = rmsnorm(h, norm_mlp_g[l]) @ w_up[l]
        h = h + jnp.square(jax.nn.relu(u)) @ w_down[l]
    return rmsnorm(h, norm_final_g)


import jax as _jax
import jax.numpy as _jnp

TWIN_FORMAT = 'train_step'
FWD_PARAMS = ['x', 'mem', 'norm_mix_g', 'w_in', 'conv_w', 'conv_b', 'conv_ln_g', 'conv_ln_b', 'w_out', 'norm_x_g', 'norm_mem_g', 'w_xq', 'w_xk', 'w_xv', 'w_xo', 'norm_mlp_g', 'w_up', 'w_down', 'norm_final_g']
TWIN_WEIGHTS = ['norm_mix_g', 'w_in', 'conv_w', 'conv_b', 'conv_ln_g', 'conv_ln_b', 'w_out', 'norm_x_g', 'norm_mem_g', 'w_xq', 'w_xk', 'w_xv', 'w_xo', 'norm_mlp_g', 'w_up', 'w_down', 'norm_final_g']
TWIN_DIFF_INPUT = 'x'
TWIN_INPUTS = ['x', 'mem', 'norm_mix_g', 'w_in', 'conv_w', 'conv_b', 'conv_ln_g', 'conv_ln_b', 'w_out', 'norm_x_g', 'norm_mem_g', 'w_xq', 'w_xk', 'w_xv', 'w_xo', 'norm_mlp_g', 'w_up', 'w_down', 'norm_final_g', 'loss_target', 'm_norm_mix_g', 'm_w_in', 'm_conv_w', 'm_conv_b', 'm_conv_ln_g', 'm_conv_ln_b', 'm_w_out', 'm_norm_x_g', 'm_norm_mem_g', 'm_w_xq', 'm_w_xk', 'm_w_xv', 'm_w_xo', 'm_norm_mlp_g', 'm_w_up', 'm_w_down', 'm_norm_final_g', 'v_norm_mix_g', 'v_w_in', 'v_conv_w', 'v_conv_b', 'v_conv_ln_g', 'v_conv_ln_b', 'v_w_out', 'v_norm_x_g', 'v_norm_mem_g', 'v_w_xq', 'v_w_xk', 'v_w_xv', 'v_w_xo', 'v_norm_mlp_g', 'v_w_up', 'v_w_down', 'v_norm_final_g']
TWIN_OUTPUTS = ['loss', 'grad_x', 'grad_norm_mix_g', 'grad_w_in', 'grad_conv_w', 'grad_conv_b', 'grad_conv_ln_g', 'grad_conv_ln_b', 'grad_w_out', 'grad_norm_x_g', 'grad_norm_mem_g', 'grad_w_xq', 'grad_w_xk', 'grad_w_xv', 'grad_w_xo', 'grad_norm_mlp_g', 'grad_w_up', 'grad_w_down', 'grad_norm_final_g', 'delta_norm_mix_g', 'delta_w_in', 'delta_conv_w', 'delta_conv_b', 'delta_conv_ln_g', 'delta_conv_ln_b', 'delta_w_out', 'delta_norm_x_g', 'delta_norm_mem_g', 'delta_w_xq', 'delta_w_xk', 'delta_w_xv', 'delta_w_xo', 'delta_norm_mlp_g', 'delta_w_up', 'delta_w_down', 'delta_norm_final_g', 'new_m_norm_mix_g', 'new_m_w_in', 'new_m_conv_w', 'new_m_conv_b', 'new_m_conv_ln_g', 'new_m_conv_ln_b', 'new_m_w_out', 'new_m_norm_x_g', 'new_m_norm_mem_g', 'new_m_w_xq', 'new_m_w_xk', 'new_m_w_xv', 'new_m_w_xo', 'new_m_norm_mlp_g', 'new_m_w_up', 'new_m_w_down', 'new_m_norm_final_g', 'new_v_norm_mix_g', 'new_v_w_in', 'new_v_conv_w', 'new_v_conv_b', 'new_v_conv_ln_g', 'new_v_conv_ln_b', 'new_v_w_out', 'new_v_norm_x_g', 'new_v_norm_mem_g', 'new_v_w_xq', 'new_v_w_xk', 'new_v_w_xv', 'new_v_w_xo', 'new_v_norm_mlp_g', 'new_v_w_up', 'new_v_w_down', 'new_v_norm_final_g']
TWIN_LEAF_KINDS = {'loss': 'loss', 'grad_x': 'grad_x', 'grad_norm_mix_g': 'grad_w', 'grad_w_in': 'grad_w', 'grad_conv_w': 'grad_w', 'grad_conv_b': 'grad_w', 'grad_conv_ln_g': 'grad_w', 'grad_conv_ln_b': 'grad_w', 'grad_w_out': 'grad_w', 'grad_norm_x_g': 'grad_w', 'grad_norm_mem_g': 'grad_w', 'grad_w_xq': 'grad_w', 'grad_w_xk': 'grad_w', 'grad_w_xv': 'grad_w', 'grad_w_xo': 'grad_w', 'grad_norm_mlp_g': 'grad_w', 'grad_w_up': 'grad_w', 'grad_w_down': 'grad_w', 'grad_norm_final_g': 'grad_w', 'delta_norm_mix_g': 'delta_w', 'delta_w_in': 'delta_w', 'delta_conv_w': 'delta_w', 'delta_conv_b': 'delta_w', 'delta_conv_ln_g': 'delta_w', 'delta_conv_ln_b': 'delta_w', 'delta_w_out': 'delta_w', 'delta_norm_x_g': 'delta_w', 'delta_norm_mem_g': 'delta_w', 'delta_w_xq': 'delta_w', 'delta_w_xk': 'delta_w', 'delta_w_xv': 'delta_w', 'delta_w_xo': 'delta_w', 'delta_norm_mlp_g': 'delta_w', 'delta_w_up': 'delta_w', 'delta_w_down': 'delta_w', 'delta_norm_final_g': 'delta_w', 'new_m_norm_mix_g': 'new_m', 'new_m_w_in': 'new_m', 'new_m_conv_w': 'new_m', 'new_m_conv_b': 'new_m', 'new_m_conv_ln_g': 'new_m', 'new_m_conv_ln_b': 'new_m', 'new_m_w_out': 'new_m', 'new_m_norm_x_g': 'new_m', 'new_m_norm_mem_g': 'new_m', 'new_m_w_xq': 'new_m', 'new_m_w_xk': 'new_m', 'new_m_w_xv': 'new_m', 'new_m_w_xo': 'new_m', 'new_m_norm_mlp_g': 'new_m', 'new_m_w_up': 'new_m', 'new_m_w_down': 'new_m', 'new_m_norm_final_g': 'new_m', 'new_v_norm_mix_g': 'new_v', 'new_v_w_in': 'new_v', 'new_v_conv_w': 'new_v', 'new_v_conv_b': 'new_v', 'new_v_conv_ln_g': 'new_v', 'new_v_conv_ln_b': 'new_v', 'new_v_w_out': 'new_v', 'new_v_norm_x_g': 'new_v', 'new_v_norm_mem_g': 'new_v', 'new_v_w_xq': 'new_v', 'new_v_w_xk': 'new_v', 'new_v_w_xv': 'new_v', 'new_v_w_xo': 'new_v', 'new_v_norm_mlp_g': 'new_v', 'new_v_w_up': 'new_v', 'new_v_w_down': 'new_v', 'new_v_norm_final_g': 'new_v'}


def _forward(args):
    return _fwd_reference(*[args[k] for k in FWD_PARAMS])


def _output_shape():
    def fwd():
        inp = _fwd_setup_inputs(0)
        return _fwd_reference(*[inp[k] for k in FWD_PARAMS])
    out = _jax.eval_shape(fwd)
    return out.shape, out.dtype

N_MICROBATCH = 1
ADAM_LR = 0.001
ADAM_B1 = 0.9
ADAM_B2 = 0.999
ADAM_EPS = 1e-08
ADAM_WD = 0.01
ADAM_STEP = 10
PER_EXAMPLE_BATCH_AXIS = {'x': 0, 'mem': 0, 'loss_target': 0}
SHARED_INPUTS = []
_WEIGHT_DTYPES = {'norm_mix_g': _jnp.float32, 'w_in': _jnp.float32, 'conv_w': _jnp.float32, 'conv_b': _jnp.float32, 'conv_ln_g': _jnp.float32, 'conv_ln_b': _jnp.float32, 'w_out': _jnp.float32, 'norm_x_g': _jnp.float32, 'norm_mem_g': _jnp.float32, 'w_xq': _jnp.float32, 'w_xk': _jnp.float32, 'w_xv': _jnp.float32, 'w_xo': _jnp.float32, 'norm_mlp_g': _jnp.float32, 'w_up': _jnp.float32, 'w_down': _jnp.float32, 'norm_final_g': _jnp.float32}
MOMENT_SCALE = {'norm_mix_g': 1.269351e-01, 'w_in': 7.956738e-02, 'conv_w': 1.544645e-01, 'conv_b': 4.593084e-01, 'conv_ln_g': 2.263117e-01, 'conv_ln_b': 2.513922e-01, 'w_out': 1.179817e-01, 'norm_x_g': 2.657040e-02, 'norm_mem_g': 3.968971e-02, 'w_xq': 2.636668e-02, 'w_xk': 2.653164e-02, 'w_xv': 2.784111e-02, 'w_xo': 2.760950e-02, 'norm_mlp_g': 2.172729e-01, 'w_up': 1.095887e-01, 'w_down': 2.310493e-01, 'norm_final_g': 6.460015e+01}


def _to_microbatches(a, axis):
    t = _jnp.moveaxis(a, axis, 0)
    t = t.reshape((N_MICROBATCH, t.shape[0] // N_MICROBATCH) + t.shape[1:])
    return _jnp.moveaxis(t, 1, axis + 1)


def setup_inputs(seed: int = 0) -> dict:
    inp = _fwd_setup_inputs(seed)
    key = _jax.random.fold_in(_jax.random.key(seed), 7919)
    shape, _ = _output_shape()
    out = dict(inp)
    out["loss_target"] = _jax.random.normal(_jax.random.fold_in(key, 0), shape, _jnp.float32)
    for i, name in enumerate(TWIN_WEIGHTS):
        w = inp[name].astype(_jnp.float32)
        if MOMENT_SCALE is None:
            s = _jnp.sqrt(_jnp.mean(_jnp.square(w)) + 1e-30)
        else:
            s = MOMENT_SCALE[name]
        km, kv = _jax.random.split(_jax.random.fold_in(key, i + 1))
        out[name] = w
        out["m_" + name] = s * _jax.random.normal(km, w.shape, _jnp.float32)
        out["v_" + name] = (s * s) * _jax.random.uniform(kv, w.shape, _jnp.float32, 0.5, 1.5)
    if N_MICROBATCH > 1:
        for name, axis in PER_EXAMPLE_BATCH_AXIS.items():
            out[name] = _to_microbatches(out[name], axis)
    return {'x': out['x'], 'mem': out['mem'], 'norm_mix_g': out['norm_mix_g'], 'w_in': out['w_in'], 'conv_w': out['conv_w'], 'conv_b': out['conv_b'], 'conv_ln_g': out['conv_ln_g'], 'conv_ln_b': out['conv_ln_b'], 'w_out': out['w_out'], 'norm_x_g': out['norm_x_g'], 'norm_mem_g': out['norm_mem_g'], 'w_xq': out['w_xq'], 'w_xk': out['w_xk'], 'w_xv': out['w_xv'], 'w_xo': out['w_xo'], 'norm_mlp_g': out['norm_mlp_g'], 'w_up': out['w_up'], 'w_down': out['w_down'], 'norm_final_g': out['norm_final_g'], 'loss_target': out['loss_target'], 'm_norm_mix_g': out['m_norm_mix_g'], 'm_w_in': out['m_w_in'], 'm_conv_w': out['m_conv_w'], 'm_conv_b': out['m_conv_b'], 'm_conv_ln_g': out['m_conv_ln_g'], 'm_conv_ln_b': out['m_conv_ln_b'], 'm_w_out': out['m_w_out'], 'm_norm_x_g': out['m_norm_x_g'], 'm_norm_mem_g': out['m_norm_mem_g'], 'm_w_xq': out['m_w_xq'], 'm_w_xk': out['m_w_xk'], 'm_w_xv': out['m_w_xv'], 'm_w_xo': out['m_w_xo'], 'm_norm_mlp_g': out['m_norm_mlp_g'], 'm_w_up': out['m_w_up'], 'm_w_down': out['m_w_down'], 'm_norm_final_g': out['m_norm_final_g'], 'v_norm_mix_g': out['v_norm_mix_g'], 'v_w_in': out['v_w_in'], 'v_conv_w': out['v_conv_w'], 'v_conv_b': out['v_conv_b'], 'v_conv_ln_g': out['v_conv_ln_g'], 'v_conv_ln_b': out['v_conv_ln_b'], 'v_w_out': out['v_w_out'], 'v_norm_x_g': out['v_norm_x_g'], 'v_norm_mem_g': out['v_norm_mem_g'], 'v_w_xq': out['v_w_xq'], 'v_w_xk': out['v_w_xk'], 'v_w_xv': out['v_w_xv'], 'v_w_xo': out['v_w_xo'], 'v_norm_mlp_g': out['v_norm_mlp_g'], 'v_w_up': out['v_w_up'], 'v_w_down': out['v_w_down'], 'v_norm_final_g': out['v_norm_final_g']}


def _loss(weights, diff, rest, loss_target):
    with _jax.named_scope("forward"):
        args = {**rest, TWIN_DIFF_INPUT: diff, **{k: w.astype(_WEIGHT_DTYPES[k]) for k, w in weights.items()}}
        y = _forward(args)
    with _jax.named_scope("loss_head"):
        err = _jnp.square(y.astype(_jnp.float32) - loss_target)
        return 0.5 * _jnp.sum(_jnp.mean(err, axis=-1)) if err.ndim else 0.5 * err


def _adamw(w, g, m, v):
    m = ADAM_B1 * m + (1.0 - ADAM_B1) * g
    v = ADAM_B2 * v + (1.0 - ADAM_B2) * _jnp.square(g)
    m_hat = m / (1.0 - ADAM_B1 ** ADAM_STEP)
    v_hat = v / (1.0 - ADAM_B2 ** ADAM_STEP)
    delta = -ADAM_LR * (m_hat / (_jnp.sqrt(v_hat) + ADAM_EPS) + ADAM_WD * w)
    return delta, m, v


def reference(x, mem, norm_mix_g, w_in, conv_w, conv_b, conv_ln_g, conv_ln_b, w_out, norm_x_g, norm_mem_g, w_xq, w_xk, w_xv, w_xo, norm_mlp_g, w_up, w_down, norm_final_g, loss_target, m_norm_mix_g, m_w_in, m_conv_w, m_conv_b, m_conv_ln_g, m_conv_ln_b, m_w_out, m_norm_x_g, m_norm_mem_g, m_w_xq, m_w_xk, m_w_xv, m_w_xo, m_norm_mlp_g, m_w_up, m_w_down, m_norm_final_g, v_norm_mix_g, v_w_in, v_conv_w, v_conv_b, v_conv_ln_g, v_conv_ln_b, v_w_out, v_norm_x_g, v_norm_mem_g, v_w_xq, v_w_xk, v_w_xv, v_w_xo, v_norm_mlp_g, v_w_up, v_w_down, v_norm_final_g):
    given = dict(x=x, mem=mem, norm_mix_g=norm_mix_g, w_in=w_in, conv_w=conv_w, conv_b=conv_b, conv_ln_g=conv_ln_g, conv_ln_b=conv_ln_b, w_out=w_out, norm_x_g=norm_x_g, norm_mem_g=norm_mem_g, w_xq=w_xq, w_xk=w_xk, w_xv=w_xv, w_xo=w_xo, norm_mlp_g=norm_mlp_g, w_up=w_up, w_down=w_down, norm_final_g=norm_final_g, loss_target=loss_target, m_norm_mix_g=m_norm_mix_g, m_w_in=m_w_in, m_conv_w=m_conv_w, m_conv_b=m_conv_b, m_conv_ln_g=m_conv_ln_g, m_conv_ln_b=m_conv_ln_b, m_w_out=m_w_out, m_norm_x_g=m_norm_x_g, m_norm_mem_g=m_norm_mem_g, m_w_xq=m_w_xq, m_w_xk=m_w_xk, m_w_xv=m_w_xv, m_w_xo=m_w_xo, m_norm_mlp_g=m_norm_mlp_g, m_w_up=m_w_up, m_w_down=m_w_down, m_norm_final_g=m_norm_final_g, v_norm_mix_g=v_norm_mix_g, v_w_in=v_w_in, v_conv_w=v_conv_w, v_conv_b=v_conv_b, v_conv_ln_g=v_conv_ln_g, v_conv_ln_b=v_conv_ln_b, v_w_out=v_w_out, v_norm_x_g=v_norm_x_g, v_norm_mem_g=v_norm_mem_g, v_w_xq=v_w_xq, v_w_xk=v_w_xk, v_w_xv=v_w_xv, v_w_xo=v_w_xo, v_norm_mlp_g=v_norm_mlp_g, v_w_up=v_w_up, v_w_down=v_w_down, v_norm_final_g=v_norm_final_g)
    weights = {n: given[n] for n in TWIN_WEIGHTS}
    shared = {n: given[n] for n in SHARED_INPUTS}
    per_example = {n: given[n] for n in ['x', 'mem']}
    grad_fn = _jax.value_and_grad(_loss, argnums=(0, 1))

    def one_microbatch(ex, loss_target):
        ex = dict(ex)
        diff = ex.pop(TWIN_DIFF_INPUT)
        return grad_fn(weights, diff, {**shared, **ex}, loss_target)

    if N_MICROBATCH == 1:
        loss, (grad_w, grad_x) = one_microbatch(per_example, given["loss_target"])
    else:
        def body(carry, xs):
            loss_sum, grad_sum = carry
            l_k, (gw_k, gx_k) = one_microbatch(xs[0], xs[1])
            with _jax.named_scope("update"):
                return (loss_sum + l_k, _jax.tree.map(_jnp.add, grad_sum, gw_k)), gx_k

        init = (_jnp.zeros((), _jnp.float32), _jax.tree.map(_jnp.zeros_like, weights))
        (loss, grad_w), grad_x = _jax.lax.scan(body, init, (per_example, given["loss_target"]))
    with _jax.named_scope("update"):
        delta_w, new_m, new_v = {}, {}, {}
        for n in TWIN_WEIGHTS:
            delta_w[n], new_m[n], new_v[n] = _adamw(weights[n], grad_w[n], given["m_" + n], given["v_" + n])
    return (loss, grad_x, *[grad_w[n] for n in TWIN_WEIGHTS], *[delta_w[n] for n in TWIN_WEIGHTS],
            *[new_m[n] for n in TWIN_WEIGHTS], *[new_v[n] for n in TWIN_WEIGHTS])
```

```python
import functools
import math

import jax
import jax.numpy as jnp
from jax import lax
from jax.experimental import pallas as pl
from jax.experimental.pallas import tpu as pltpu

F32 = jnp.float32
BF16 = jnp.bfloat16

N_DEV = 8
EPS = 1e-6
NEG_INF = -1e30
ATT_HEADS = 8
HEAD_DIM = 64
D_ATT = ATT_HEADS * HEAD_DIM
D_CONV = 512
DILATIONS = (1, 4, 16)
HALF = 64
ROPE_THETA = 500000.0
ROT_DIM = HEAD_DIM // 4
CONV_WIDTH = 31
CONV_PAD = (CONV_WIDTH - 1) // 2
XATT_HEADS = 4
ADAM_LR = 0.001
ADAM_B1 = 0.9
ADAM_B2 = 0.999
ADAM_EPS = 1e-08
ADAM_WD = 0.01
ADAM_STEP = 10

LANES = 128
BF16_ROWS = 16
BQ = 128
WIN = BQ + 2 * HALF
VMEM_LIMIT = 56 * 1024 * 1024
MESH = pl.DeviceIdType.MESH
ANY = pl.BlockSpec(memory_space=pl.ANY)

_NT = (((1,), (1,)), ((), ()))
_TN = (((0,), (0,)), ((), ()))


def _dot(a, b):
    return jnp.dot(a, b, preferred_element_type=F32)


def _dot_nt(a, b):
    return lax.dot_general(a, b, _NT, preferred_element_type=F32)


def _dot_tn(a, b):
    return lax.dot_general(a, b, _TN, preferred_element_type=F32)


def _params(*sem):
    return pltpu.CompilerParams(dimension_semantics=sem or None, vmem_limit_bytes=VMEM_LIMIT)


def _sigmoid(v):
    return 1.0 / (1.0 + jnp.exp(-v))


def _mean(v):
    return jnp.mean(v, axis=-1, keepdims=True)


def _rms_fwd(h, g):
    r = lax.rsqrt(_mean(h * h) + EPS)
    return h * r * g, r


def _rms_bwd(h, g, d_out):
    r = lax.rsqrt(_mean(h * h) + EPS)
    hn = h * r
    gd = d_out * g
    return r * (gd - hn * _mean(gd * hn)), d_out * hn


def _row_tile(s):
    return min(512, s)


def _full(shape):
    return pl.BlockSpec(shape, lambda *_: (0,) * len(shape))


def _mesh_pos():
    return lax.axis_index("x"), lax.axis_index("y"), lax.axis_index("c")


def _dev_index(p):
    return 4 * p[0] + 2 * p[1] + p[2]


def _all_gather(xs, name):
    n = len(xs)

    def body(*refs):
        x_refs, o_refs = refs[:n], refs[n:2 * n]
        send_sems, recv_sems, local_sems = refs[2 * n:]
        x, y, c = _mesh_pos()
        me, sibling = (x, y, c), (x, y, 1 - c)
        chips = [(1 - x, y), (x, 1 - y), (1 - x, 1 - y)]

        def copy(a, k, block, to, src=None):
            slot = o_refs[a].at[_dev_index(block)]
            return pltpu.make_async_remote_copy(
                src_ref=slot if src is None else src, dst_ref=slot,
                send_sem=send_sems.at[7 * a + k], recv_sem=recv_sems.at[7 * a + k],
                device_id=to, device_id_type=MESH)

        mine = [pltpu.make_async_copy(x_refs[a], o_refs[a].at[_dev_index(me)], local_sems.at[a])
                for a in range(n)]
        for cp in mine:
            cp.start()
        first = []
        for a in range(n):
            first.append(copy(a, 0, me, sibling, src=x_refs[a]))
            first += [copy(a, 1 + j, me, (*chip, c), src=x_refs[a]) for j, chip in enumerate(chips)]
        for cp in first:
            cp.start()
        passed = []
        for j, chip in enumerate(chips):
            for a in range(n):
                copy(a, 1 + j, (*chip, c), me).wait_recv()
                fwd = copy(a, 4 + j, (*chip, c), sibling)
                fwd.start()
                passed.append(fwd)
        for a in range(n):
            copy(a, 0, sibling, me).wait_recv()
            for j, chip in enumerate(chips):
                copy(a, 4 + j, (*chip, 1 - c), me).wait_recv()
        for cp in first + passed:
            cp.wait_send()
        for cp in mine:
            cp.wait()

    return pl.pallas_call(
        body, name=name,
        out_shape=[jax.ShapeDtypeStruct((N_DEV,) + x.shape, x.dtype) for x in xs],
        in_specs=[ANY] * n, out_specs=[ANY] * n,
        scratch_shapes=[pltpu.SemaphoreType.DMA((7 * n,)), pltpu.SemaphoreType.DMA((7 * n,)),
                        pltpu.SemaphoreType.DMA((n,))],
    )(*xs)


def _all_to_all(gs, name):
    n = len(gs)
    flips = [(dx, dy, dc) for dx in (0, 1) for dy in (0, 1) for dc in (0, 1)][1:]

    def body(*refs):
        g_refs, o_refs = refs[:n], refs[n:2 * n]
        send_sems, recv_sems, local_sems = refs[2 * n:]
        x, y, c = _mesh_pos()
        me = (x, y, c)

        def peer(f):
            return tuple(1 - v if fl else v for v, fl in zip(me, f))

        def copy(a, k):
            to = peer(flips[k])
            return pltpu.make_async_remote_copy(
                src_ref=g_refs[a].at[_dev_index(to)], dst_ref=o_refs[a].at[_dev_index(me)],
                send_sem=send_sems.at[7 * a + k], recv_sem=recv_sems.at[7 * a + k],
                device_id=to, device_id_type=MESH)

        def landed(a, k):
            frm = peer(flips[k])
            return pltpu.make_async_remote_copy(
                src_ref=g_refs[a].at[_dev_index(frm)], dst_ref=o_refs[a].at[_dev_index(frm)],
                send_sem=send_sems.at[7 * a + k], recv_sem=recv_sems.at[7 * a + k],
                device_id=frm, device_id_type=MESH)

        mine = [pltpu.make_async_copy(g_refs[a].at[_dev_index(me)], o_refs[a].at[_dev_index(me)],
                                      local_sems.at[a]) for a in range(n)]
        sends = [copy(a, k) for k in range(7) for a in range(n)]
        for cp in mine + sends:
            cp.start()
        for k in range(7):
            for a in range(n):
                landed(a, k).wait_recv()
        for cp in sends:
            cp.wait_send()
        for cp in mine:
            cp.wait()

    return pl.pallas_call(
        body, name=name,
        out_shape=[jax.ShapeDtypeStruct(g.shape, g.dtype) for g in gs],
        in_specs=[ANY] * n, out_specs=[ANY] * n,
        scratch_shapes=[pltpu.SemaphoreType.DMA((7 * n,)), pltpu.SemaphoreType.DMA((7 * n,)),
                        pltpu.SemaphoreType.DMA((n,))],
    )(*gs)


def _rotary_tables(s):
    half = ROT_DIM // 2
    freqs = ROPE_THETA ** (-jnp.arange(0, ROT_DIM, 2, dtype=F32) / ROT_DIM)
    ang = jnp.arange(s, dtype=F32)[:, None] * freqs[None, :]
    cos, sin = jnp.cos(ang), jnp.sin(ang)
    one = jnp.ones((s, HEAD_DIM - ROT_DIM), F32)
    zero = jnp.zeros((s, HEAD_DIM - ROT_DIM), F32)
    zh = jnp.zeros((s, half), F32)
    c64 = jnp.concatenate([cos, cos, one], axis=1)
    a64 = jnp.concatenate([-sin, zh, zero], axis=1)
    b64 = jnp.concatenate([zh, sin, zero], axis=1)
    return tuple(jnp.tile(t, (1, LANES // HEAD_DIM)) for t in (c64, a64, b64))


def _fwd_in(x, g, w_in, rot):
    s, d = x.shape
    n = w_in.shape[1]
    tm = _row_tile(s)

    def body(x_ref, g_ref, w_ref, c_ref, a_ref, b_ref, xn_ref, qkv_ref, ag_ref):
        xn = _rms_fwd(x_ref[...], g_ref[...])[0].astype(BF16)
        xn_ref[...] = xn
        y = _dot(xn, w_ref[...])
        reps = (1, D_ATT // LANES)
        cc, aa, bb = jnp.tile(c_ref[...], reps), jnp.tile(a_ref[...], reps), jnp.tile(b_ref[...], reps)
        for blk in range(2):
            t = y[:, blk * D_ATT:(blk + 1) * D_ATT]
            rot_t = t * cc + pltpu.roll(t, D_ATT - ROT_DIM // 2, 1) * aa + pltpu.roll(t, ROT_DIM // 2, 1) * bb
            qkv_ref[:, blk * D_ATT:(blk + 1) * D_ATT] = rot_t.astype(BF16)
        qkv_ref[:, 2 * D_ATT:] = y[:, 2 * D_ATT:3 * D_ATT].astype(BF16)
        ag_ref[...] = y[:, 3 * D_ATT:].astype(BF16)

    row = lambda w: pl.BlockSpec((tm, w), lambda i: (i, 0))
    return pl.pallas_call(
        body, name="fwd_in", grid=(s // tm,),
        in_specs=[row(d), _full((1, d)), _full((d, n)), row(LANES), row(LANES), row(LANES)],
        out_specs=[row(d), row(3 * D_ATT), row(2 * D_CONV)],
        out_shape=[jax.ShapeDtypeStruct((s, d), BF16), jax.ShapeDtypeStruct((s, 3 * D_ATT), BF16),
                   jax.ShapeDtypeStruct((s, 2 * D_CONV), BF16)],
        compiler_params=_params("parallel"),
    )(x, g, w_in, *rot)


def _win_in_specs(length, col, width):
    per = BQ // HALF
    last = length // HALF - 1
    return [
        pl.BlockSpec((None, HALF, width), lambda r, j: (r, jnp.maximum(j * per - 1, 0), col)),
        pl.BlockSpec((None, BQ, width), lambda r, j: (r, j, col)),
        pl.BlockSpec((None, HALF, width), lambda r, j: (r, jnp.minimum(j * per + per, last), col)),
    ]


def _fill_window(win, prev_ref, main_ref, next_ref):
    win[0:HALF] = prev_ref[...]
    win[HALF:HALF + BQ] = main_ref[...]
    win[HALF + BQ:] = next_ref[...]


def _band_mask(j, length, rows_are_queries):
    shape = (BQ, WIN) if rows_are_queries else (WIN, BQ)
    blk = lax.broadcasted_iota(jnp.int32, shape, 0 if rows_are_queries else 1)
    win = lax.broadcasted_iota(jnp.int32, shape, 1 if rows_are_queries else 0)
    pos = j * BQ - HALF + win
    return (jnp.abs(win - HALF - blk) <= HALF) & (pos >= 0) & (pos < length)


def _head_masks():
    lane = lax.broadcasted_iota(jnp.int32, (1, LANES), 1)
    first = lane < HEAD_DIM
    return first, jnp.logical_not(first)


def _head_col(v, msk):
    return jnp.max(jnp.where(msk, v, -jnp.inf), axis=-1, keepdims=True)


def _swa_fwd(qkv3, name):
    dil, length, _ = qkv3.shape
    scale = HEAD_DIM ** -0.5

    def body(q_ref, kp, km, kn, vp, vm, vn, o_ref, lse_ref, kwin, vwin):
        j = pl.program_id(1)
        _fill_window(kwin, kp, km, kn)
        _fill_window(vwin, vp, vm, vn)
        valid = _band_mask(j, length, True)
        masks = _head_masks()
        for pr in range(D_ATT // LANES):
            cols = slice(pr * LANES, (pr + 1) * LANES)
            q2, k2, v2 = q_ref[:, cols], kwin[:, cols], vwin[:, cols]
            o2 = jnp.zeros((BQ, LANES), F32)
            l2 = jnp.zeros((BQ, LANES), F32)
            for msk in masks:
                qh = jnp.where(msk, q2, jnp.zeros_like(q2))
                sc = jnp.where(valid, _dot_nt(qh, k2) * scale, NEG_INF)
                m = jnp.max(sc, axis=-1, keepdims=True)
                p = jnp.exp(sc - m)
                den = jnp.sum(p, axis=-1, keepdims=True)
                pv = _dot((p / den).astype(BF16), v2)
                o2 = jnp.where(msk, pv, o2)
                l2 = jnp.where(msk, m + jnp.log(den), l2)
            o_ref[:, cols] = o2.astype(BF16)
            lse_ref[:, cols] = l2

    blk = lambda w: pl.BlockSpec((None, BQ, w), lambda r, j: (r, j, 0))
    return pl.pallas_call(
        body, name=name, grid=(dil, length // BQ),
        in_specs=[pl.BlockSpec((None, BQ, D_ATT), lambda r, j: (r, j, 0))]
        + _win_in_specs(length, 1, D_ATT) + _win_in_specs(length, 2, D_ATT),
        out_specs=[blk(D_ATT), blk(D_ATT)],
        out_shape=[jax.ShapeDtypeStruct((dil, length, D_ATT), BF16),
                   jax.ShapeDtypeStruct((dil, length, D_ATT), F32)],
        scratch_shapes=[pltpu.VMEM((WIN, D_ATT), BF16), pltpu.VMEM((WIN, D_ATT), BF16)],
        compiler_params=_params("parallel", "parallel"),
    )(*([qkv3] * 7))


def _glu(v):
    return v[:, :D_CONV].astype(F32) * _sigmoid(v[:, D_CONV:].astype(F32))


def _halo_specs(tm, width, col=0):
    per = tm // BF16_ROWS
    return lambda nblk: [
        pl.BlockSpec((BF16_ROWS, width), lambda i: (jnp.maximum(i * per - 1, 0), col)),
        pl.BlockSpec((tm, width), lambda i: (i, col)),
        pl.BlockSpec((BF16_ROWS, width), lambda i: (jnp.minimum(i * per + per, nblk - 1), col)),
    ]


def _fill_halo(buf, i, ntiles, tm, prev, main, nxt):
    buf[0:BF16_ROWS] = jnp.where(i == 0, 0.0, prev)
    buf[BF16_ROWS:BF16_ROWS + tm] = main
    buf[BF16_ROWS + tm:] = jnp.where(i == ntiles - 1, 0.0, nxt)


def _fwd_conv(ag, cw, cb, lg, lb):
    s = ag.shape[0]
    tm = _row_tile(s)
    nt = s // tm

    def body(agp, agm, agn, cw_ref, cb_ref, lg_ref, lb_ref, c1_ref, co_ref, ubuf):
        i = pl.program_id(0)
        _fill_halo(ubuf, i, nt, tm, _glu(agp[...]), _glu(agm[...]), _glu(agn[...]))
        w = cw_ref[...]
        acc = jnp.zeros((tm, D_CONV), F32)
        for k in range(CONV_WIDTH):
            acc = acc + ubuf[pl.ds(k + 1, tm), :] * w[k:k + 1, :]
        acc = acc + cb_ref[...]
        c1_ref[...] = acc
        xc = acc - _mean(acc)
        ln = xc * lax.rsqrt(_mean(xc * xc) + EPS) * lg_ref[...] + lb_ref[...]
        co_ref[...] = (ln * _sigmoid(ln)).astype(BF16)

    vec = _full((1, D_CONV))
    return pl.pallas_call(
        body, name="fwd_conv", grid=(nt,),
        in_specs=_halo_specs(tm, 2 * D_CONV)(s // BF16_ROWS) + [_full((32, D_CONV)), vec, vec, vec],
        out_specs=[pl.BlockSpec((tm, D_CONV), lambda i: (i, 0))] * 2,
        out_shape=[jax.ShapeDtypeStruct((s, D_CONV), F32), jax.ShapeDtypeStruct((s, D_CONV), BF16)],
        scratch_shapes=[pltpu.VMEM((tm + 2 * BF16_ROWS, D_CONV), F32)],
        compiler_params=_params("parallel"),
    )(ag, ag, ag, cw, cb, lg, lb)


def _fwd_mix_out(outs, lses, conv_out, x, w_out):
    s, d = x.shape
    tm = _row_tile(s)

    def body(o1, o2, o3, l1, l2, l3, co, x_ref, w_ref, h_ref, cat_ref, lt_ref):
        a, b, c = l1[...], l2[...], l3[...]
        m = jnp.maximum(jnp.maximum(a, b), c)
        ea, eb, ec = jnp.exp(a - m), jnp.exp(b - m), jnp.exp(c - m)
        den = ea + eb + ec
        att = (ea * o1[...].astype(F32) + eb * o2[...].astype(F32) + ec * o3[...].astype(F32)) / den
        lt_ref[...] = m + jnp.log(den)
        att = att.astype(BF16)
        cat_ref[:, :D_ATT] = att
        cat_ref[:, D_ATT:] = co[...]
        h_ref[...] = x_ref[...] + _dot(att, w_ref[:D_ATT, :]) + _dot(co[...], w_ref[D_ATT:, :])

    row = lambda w: pl.BlockSpec((tm, w), lambda i: (i, 0))
    return pl.pallas_call(
        body, name="fwd_mix_out", grid=(s // tm,),
        in_specs=[row(D_ATT)] * 7 + [row(d), _full((d, d))],
        out_specs=[row(d), row(d), row(D_ATT)],
        out_shape=[jax.ShapeDtypeStruct((s, d), F32), jax.ShapeDtypeStruct((s, d), BF16),
                   jax.ShapeDtypeStruct((s, D_ATT), F32)],
        compiler_params=_params("parallel"),
    )(*outs, *lses, conv_out, x, w_out)


def _fwd_mem(mem, g, wk, wv):
    m, d = mem.shape

    def body(mem_ref, g_ref, wk_ref, wv_ref, mn_ref, xk_ref, xv_ref):
        mn = _rms_fwd(mem_ref[...], g_ref[...])[0].astype(BF16)
        mn_ref[...] = mn
        xk_ref[...] = _dot(mn, wk_ref[...]).astype(BF16)
        xv_ref[...] = _dot(mn, wv_ref[...]).astype(BF16)

    return pl.pallas_call(
        body, name="fwd_mem",
        out_shape=[jax.ShapeDtypeStruct((m, d), BF16)] * 3,
        compiler_params=_params(),
    )(mem, g, wk, wv)


def _xatt_probs(q, k):
    sc = _dot_nt(q, k) * (q.shape[1] ** -0.5)
    p = jnp.exp(sc - jnp.max(sc, axis=-1, keepdims=True))
    return p / jnp.sum(p, axis=-1, keepdims=True)


def _fwd_xattn(h1, g, wq, xk, xv, wo):
    s, d = h1.shape
    m = xk.shape[0]
    tm = _row_tile(s)
    hd = d // XATT_HEADS

    def body(h_ref, g_ref, wq_ref, xk_ref, xv_ref, wo_ref, h2_ref, hn_ref, xq_ref, xo_ref):
        h = h_ref[...]
        hn = _rms_fwd(h, g_ref[...])[0].astype(BF16)
        hn_ref[...] = hn
        xq = _dot(hn, wq_ref[...]).astype(BF16)
        xq_ref[...] = xq
        for i in range(XATT_HEADS):
            cols = slice(i * hd, (i + 1) * hd)
            pr = _xatt_probs(xq[:, cols], xk_ref[:, cols])
            xo_ref[:, cols] = _dot(pr.astype(BF16), xv_ref[:, cols]).astype(BF16)
        h2_ref[...] = h + _dot(xo_ref[...], wo_ref[...])

    row = pl.BlockSpec((tm, d), lambda i: (i, 0))
    return pl.pallas_call(
        body, name="fwd_xattn", grid=(s // tm,),
        in_specs=[row, _full((1, d)), _full((d, d)), _full((m, d)), _full((m, d)), _full((d, d))],
        out_specs=[row] * 4,
        out_shape=[jax.ShapeDtypeStruct((s, d), F32)] + [jax.ShapeDtypeStruct((s, d), BF16)] * 3,
        compiler_params=_params("parallel"),
    )(h1, g, wq, xk, xv, wo)


def _fwd_mlp_loss(h2, g, w_up, w_down, gf, target):
    s, d = h2.shape
    nb, _, f = w_up.shape
    tm = _row_tile(s)

    def body(h_ref, g_ref, wu_ref, wd_ref, gf_ref, t_ref,
             hn_ref, act_ref, dh_ref, dhb_ref, loss_ref, ggf_ref, acc):
        i, k = pl.program_id(0), pl.program_id(1)

        @pl.when(k == 0)
        def _():
            hn_ref[...] = _rms_fwd(h_ref[...], g_ref[...])[0].astype(BF16)
            acc[...] = jnp.zeros_like(acc)

        @pl.when((i == 0) & (k == 0))
        def _():
            loss_ref[...] = jnp.zeros_like(loss_ref)
            ggf_ref[...] = jnp.zeros_like(ggf_ref)

        act = jnp.square(jnp.maximum(_dot(hn_ref[...], wu_ref[...]), 0.0)).astype(BF16)
        act_ref[...] = act
        acc[...] += _dot(act, wd_ref[...])

        @pl.when(k == nb - 1)
        def _():
            h3 = h_ref[...] + acc[...]
            gfv = gf_ref[...]
            y, _ = _rms_fwd(h3, gfv)
            err = y - t_ref[...]
            loss_ref[...] += 0.5 * jnp.sum(_mean(err * err))
            dh3, gg = _rms_bwd(h3, gfv, err * (1.0 / d))
            ggf_ref[...] += jnp.sum(gg, axis=0, keepdims=True)
            dh_ref[...] = dh3
            dhb_ref[...] = dh3.astype(BF16)

    row = pl.BlockSpec((tm, d), lambda i, k: (i, 0))
    return pl.pallas_call(
        body, name="fwd_mlp_loss", grid=(s // tm, nb),
        in_specs=[row, _full((1, d)),
                  pl.BlockSpec((None, d, f), lambda i, k: (k, 0, 0)),
                  pl.BlockSpec((None, f, d), lambda i, k: (k, 0, 0)),
                  _full((1, d)), row],
        out_specs=[row, pl.BlockSpec((tm, f), lambda i, k: (i, k)), row, row,
                   _full((1, LANES)), _full((1, d))],
        out_shape=[jax.ShapeDtypeStruct((s, d), BF16), jax.ShapeDtypeStruct((s, nb * f), BF16),
                   jax.ShapeDtypeStruct((s, d), F32), jax.ShapeDtypeStruct((s, d), BF16),
                   jax.ShapeDtypeStruct((1, LANES), F32), jax.ShapeDtypeStruct((1, d), F32)],
        scratch_shapes=[pltpu.VMEM((tm, d), F32)],
        compiler_params=_params("arbitrary", "arbitrary"),
    )(h2, g, w_up, w_down, gf, target)


def _bwd_mlp(dh3, dh3b, act, w_up, w_down, h2, g):
    s, d = h2.shape
    nb, _, f = w_up.shape
    tm = _row_tile(s)

    def body(dh_ref, dhb_ref, act_ref, wu_ref, wd_ref, h_ref, g_ref,
             du_ref, dh2_ref, dh2b_ref, gg_ref, acc):
        i, k = pl.program_id(0), pl.program_id(1)

        @pl.when(k == 0)
        def _():
            acc[...] = jnp.zeros_like(acc)

        @pl.when((i == 0) & (k == 0))
        def _():
            gg_ref[...] = jnp.zeros_like(gg_ref)

        dact = _dot_nt(dhb_ref[...], wd_ref[...])
        du = (dact * (2.0 * jnp.sqrt(act_ref[...].astype(F32)))).astype(BF16)
        du_ref[...] = du
        acc[...] += _dot_nt(du, wu_ref[...])

        @pl.when(k == nb - 1)
        def _():
            dh, gg = _rms_bwd(h_ref[...], g_ref[...], acc[...])
            gg_ref[...] += jnp.sum(gg, axis=0, keepdims=True)
            dh2 = dh_ref[...] + dh
            dh2_ref[...] = dh2
            dh2b_ref[...] = dh2.astype(BF16)

    row = pl.BlockSpec((tm, d), lambda i, k: (i, 0))
    col = pl.BlockSpec((tm, f), lambda i, k: (i, k))
    return pl.pallas_call(
        body, name="bwd_mlp", grid=(s // tm, nb),
        in_specs=[row, row, col,
                  pl.BlockSpec((None, d, f), lambda i, k: (k, 0, 0)),
                  pl.BlockSpec((None, f, d), lambda i, k: (k, 0, 0)),
                  row, _full((1, d))],
        out_specs=[col, row, row, _full((1, d))],
        out_shape=[jax.ShapeDtypeStruct((s, nb * f), BF16), jax.ShapeDtypeStruct((s, d), F32),
                   jax.ShapeDtypeStruct((s, d), BF16), jax.ShapeDtypeStruct((1, d), F32)],
        scratch_shapes=[pltpu.VMEM((tm, d), F32)],
        compiler_params=_params("arbitrary", "arbitrary"),
    )(dh3, dh3b, act, w_up, w_down, h2, g)


def _bwd_xattn(dh2, dh2b, h1, g, xq, xk, xv, wq, wo):
    s, d = h1.shape
    m = xk.shape[0]
    tm = _row_tile(s)
    hd = d // XATT_HEADS
    scale = hd ** -0.5

    def body(dh_ref, dhb_ref, h_ref, g_ref, xq_ref, xk_ref, xv_ref, wq_ref, wo_ref,
             dh1_ref, dh1b_ref, dxq_ref, dxk_ref, dxv_ref, gg_ref):
        @pl.when(pl.program_id(0) == 0)
        def _():
            dxk_ref[...] = jnp.zeros_like(dxk_ref)
            dxv_ref[...] = jnp.zeros_like(dxv_ref)
            gg_ref[...] = jnp.zeros_like(gg_ref)

        dxo = _dot_nt(dhb_ref[...], wo_ref[...])
        for i in range(XATT_HEADS):
            cols = slice(i * hd, (i + 1) * hd)
            q, k, v = xq_ref[:, cols], xk_ref[:, cols], xv_ref[:, cols]
            pr = _xatt_probs(q, k)
            dxo_h = dxo[:, cols].astype(BF16)
            dpr = _dot_nt(dxo_h, v)
            dsc = (pr * (dpr - jnp.sum(dpr * pr, axis=-1, keepdims=True)) * scale).astype(BF16)
            dxq_ref[:, cols] = _dot(dsc, k).astype(BF16)
            dxk_ref[:, cols] += _dot_tn(dsc, q)
            dxv_ref[:, cols] += _dot_tn(pr.astype(BF16), dxo_h)
        dh, gg = _rms_bwd(h_ref[...], g_ref[...], _dot_nt(dxq_ref[...], wq_ref[...]))
        gg_ref[...] += jnp.sum(gg, axis=0, keepdims=True)
        dh1 = dh_ref[...] + dh
        dh1_ref[...] = dh1
        dh1b_ref[...] = dh1.astype(BF16)

    row = pl.BlockSpec((tm, d), lambda i: (i, 0))
    return pl.pallas_call(
        body, name="bwd_xattn", grid=(s // tm,),
        in_specs=[row, row, row, _full((1, d)), row, _full((m, d)), _full((m, d)), _full((d, d)), _full((d, d))],
        out_specs=[row, row, row, _full((m, d)), _full((m, d)), _full((1, d))],
        out_shape=[jax.ShapeDtypeStruct((s, d), F32), jax.ShapeDtypeStruct((s, d), BF16),
                   jax.ShapeDtypeStruct((s, d), BF16), jax.ShapeDtypeStruct((m, d), F32),
                   jax.ShapeDtypeStruct((m, d), F32), jax.ShapeDtypeStruct((1, d), F32)],
        compiler_params=_params("arbitrary"),
    )(dh2, dh2b, h1, g, xq, xk, xv, wq, wo)


def _bwd_mem(mem, g, mn, dxk, dxv, wk, wv):
    m, d = mem.shape

    def body(mem_ref, g_ref, mn_ref, dxk_ref, dxv_ref, wk_ref, wv_ref, gk_ref, gv_ref, gg_ref):
        dk, dv = dxk_ref[...].astype(BF16), dxv_ref[...].astype(BF16)
        gk_ref[...] = _dot_tn(mn_ref[...], dk).astype(BF16)
        gv_ref[...] = _dot_tn(mn_ref[...], dv).astype(BF16)
        dmn = _dot_nt(dk, wk_ref[...]) + _dot_nt(dv, wv_ref[...])
        _, gg = _rms_bwd(mem_ref[...], g_ref[...], dmn)
        gg_ref[...] = jnp.sum(gg, axis=0, keepdims=True)

    return pl.pallas_call(
        body, name="bwd_mem",
        out_shape=[jax.ShapeDtypeStruct((d, d), BF16), jax.ShapeDtypeStruct((d, d), BF16),
                   jax.ShapeDtypeStruct((1, d), F32)],
        compiler_params=_params(),
    )(mem, g, mn, dxk, dxv, wk, wv)


def _bwd_mix_out(dh1b, w_out, cat, head_ones):
    s, d = dh1b.shape
    tm = _row_tile(s)

    def body(dh_ref, w_ref, cat_ref, ones_ref, dcat_ref, dsum_ref):
        dcat = _dot_nt(dh_ref[...], w_ref[...])
        dcat_ref[...] = dcat.astype(BF16)
        prod = dcat[:, :D_ATT] * cat_ref[...].astype(F32)
        hi = prod.astype(BF16)
        lo = (prod - hi.astype(F32)).astype(BF16)
        dsum_ref[...] = _dot(hi, ones_ref[...]) + _dot(lo, ones_ref[...])

    row = lambda w: pl.BlockSpec((tm, w), lambda i: (i, 0))
    return pl.pallas_call(
        body, name="bwd_mix_out", grid=(s // tm,),
        in_specs=[row(d), _full((d, d)), row(D_ATT), _full((D_ATT, D_ATT))],
        out_specs=[row(d), row(D_ATT)],
        out_shape=[jax.ShapeDtypeStruct((s, d), BF16), jax.ShapeDtypeStruct((s, D_ATT), F32)],
        compiler_params=_params("parallel"),
    )(dh1b, w_out, cat, head_ones)


def _bwd_conv(dcat, c1, ag, cw, lg, lb):
    s = ag.shape[0]
    tm = _row_tile(s)
    nt = s // tm

    def body(dp, dm, dn, cp, cm, cn, agp, agm, agn, cw_ref, lg_ref, lb_ref,
             dag_ref, gcw_ref, gcb_ref, glg_ref, glb_ref, ubuf, dbuf):
        i = pl.program_id(0)

        @pl.when(i == 0)
        def _():
            gcw_ref[...] = jnp.zeros_like(gcw_ref)
            gcb_ref[...] = jnp.zeros_like(gcb_ref)
            glg_ref[...] = jnp.zeros_like(glg_ref)
            glb_ref[...] = jnp.zeros_like(glb_ref)

        lgv, lbv = lg_ref[...], lb_ref[...]

        def norm_bwd(dco, c1v):
            xc = c1v - _mean(c1v)
            rs = lax.rsqrt(_mean(xc * xc) + EPS)
            z = xc * rs
            ln = z * lgv + lbv
            sg = _sigmoid(ln)
            dln = dco.astype(F32) * (sg * (1.0 + ln * (1.0 - sg)))
            dz = dln * lgv
            return rs * (dz - _mean(dz) - z * _mean(dz * z)), dln, z

        dc_m, dln, z = norm_bwd(dm[...], cm[...])
        glg_ref[...] += jnp.sum(dln * z, axis=0, keepdims=True)
        glb_ref[...] += jnp.sum(dln, axis=0, keepdims=True)
        gcb_ref[...] += jnp.sum(dc_m, axis=0, keepdims=True)
        _fill_halo(dbuf, i, nt, tm, norm_bwd(dp[...], cp[...])[0], dc_m, norm_bwd(dn[...], cn[...])[0])
        _fill_halo(ubuf, i, nt, tm, _glu(agp[...]), _glu(agm[...]), _glu(agn[...]))

        w = cw_ref[...]
        tap = lax.broadcasted_iota(jnp.int32, (32, D_CONV), 0)
        du = jnp.zeros((tm, D_CONV), F32)
        gcw = jnp.zeros((32, D_CONV), F32)
        for k in range(CONV_WIDTH):
            du = du + dbuf[pl.ds(CONV_WIDTH - k, tm), :] * w[k:k + 1, :]
            gk = jnp.sum(dc_m * ubuf[pl.ds(k + 1, tm), :], axis=0, keepdims=True)
            gcw = jnp.where(tap == k, gk, gcw)
        gcw_ref[...] += gcw
        a = agm[:, :D_CONV].astype(F32)
        sg = _sigmoid(agm[:, D_CONV:].astype(F32))
        dag_ref[:, :D_CONV] = (du * sg).astype(BF16)
        dag_ref[:, D_CONV:] = (du * a * sg * (1.0 - sg)).astype(BF16)

    vec = _full((1, D_CONV))
    nblk = s // BF16_ROWS
    return pl.pallas_call(
        body, name="bwd_conv", grid=(nt,),
        in_specs=_halo_specs(tm, D_CONV, 1)(nblk) + _halo_specs(tm, D_CONV)(nblk) + _halo_specs(tm, 2 * D_CONV)(nblk)
        + [_full((32, D_CONV)), vec, vec],
        out_specs=[pl.BlockSpec((tm, 2 * D_CONV), lambda i: (i, 0)), _full((32, D_CONV)), vec, vec, vec],
        out_shape=[jax.ShapeDtypeStruct((s, 2 * D_CONV), BF16), jax.ShapeDtypeStruct((32, D_CONV), F32)]
        + [jax.ShapeDtypeStruct((1, D_CONV), F32)] * 3,
        scratch_shapes=[pltpu.VMEM((tm + 2 * BF16_ROWS, D_CONV), F32)] * 2,
        compiler_params=_params("arbitrary"),
    )(dcat, dcat, dcat, c1, c1, c1, ag, ag, ag, cw, lg, lb)


def _swa_bwd(qkv3, do3, lt3, ds3, name):
    dil, length, _ = qkv3.shape
    scale = HEAD_DIM ** -0.5

    def body(qp, qm, qn, kp, km, kn, vp, vm, vn, dop, dom, don, lp, lm, ln, sp, sm, sn,
             out_ref, qwin, kwin, vwin, dowin, lwin, swin):
        j = pl.program_id(1)
        for win, refs in ((qwin, (qp, qm, qn)), (kwin, (kp, km, kn)), (vwin, (vp, vm, vn)),
                          (dowin, (dop, dom, don)), (lwin, (lp, lm, ln)), (swin, (sp, sm, sn))):
            _fill_window(win, *refs)
        valid_q = _band_mask(j, length, True)
        valid_k = _band_mask(j, length, False)
        masks = _head_masks()
        zero = jnp.zeros((), BF16)
        for pr in range(D_ATT // LANES):
            cols = slice(pr * LANES, (pr + 1) * LANES)
            q2, k2, v2, do2 = qm[:, cols], km[:, cols], vm[:, cols], dom[:, cols]
            qw, kw, vw, dow = qwin[:, cols], kwin[:, cols], vwin[:, cols], dowin[:, cols]
            l2, s2, lw, sw = lm[:, cols], sm[:, cols], lwin[:, cols], swin[:, cols]
            dq2 = jnp.zeros((BQ, LANES), F32)
            dk2 = jnp.zeros((BQ, LANES), F32)
            dv2 = jnp.zeros((BQ, LANES), F32)
            for msk in masks:
                sc = _dot_nt(jnp.where(msk, q2, zero), kw) * scale
                p = jnp.where(valid_q, jnp.exp(sc - _head_col(l2, msk)), 0.0)
                dp = _dot_nt(jnp.where(msk, do2, zero), vw)
                dsc = (p * (dp - _head_col(s2, msk)) * scale).astype(BF16)
                dq2 = jnp.where(msk, _dot(dsc, kw), dq2)
                sc = _dot_nt(jnp.where(msk, qw, zero), k2) * scale
                p = jnp.where(valid_k, jnp.exp(sc - _head_col(lw, msk)), 0.0)
                dp = _dot_nt(jnp.where(msk, dow, zero), v2)
                dsc = (p * (dp - _head_col(sw, msk)) * scale).astype(BF16)
                dk2 = jnp.where(msk, _dot_tn(dsc, qw), dk2)
                dv2 = jnp.where(msk, _dot_tn(p.astype(BF16), dow), dv2)
            out_ref[:, cols] = dq2.astype(BF16)
            out_ref[:, D_ATT + pr * LANES:D_ATT + (pr + 1) * LANES] = dk2.astype(BF16)
            out_ref[:, 2 * D_ATT + pr * LANES:2 * D_ATT + (pr + 1) * LANES] = dv2.astype(BF16)

    wins = sum((_win_in_specs(length, c, D_ATT) for c in range(3)), [])
    wide = _win_in_specs(length, 0, D_ATT)
    return pl.pallas_call(
        body, name=name, grid=(dil, length // BQ),
        in_specs=wins + wide * 3,
        out_specs=pl.BlockSpec((None, BQ, 3 * D_ATT), lambda r, j: (r, j, 0)),
        out_shape=jax.ShapeDtypeStruct((dil, length, 3 * D_ATT), BF16),
        scratch_shapes=[pltpu.VMEM((WIN, D_ATT), BF16)] * 4 + [pltpu.VMEM((WIN, D_ATT), F32)] * 2,
        compiler_params=_params("parallel", "parallel"),
    )(*([qkv3] * 9), *([do3] * 3), *([lt3] * 3), *([ds3] * 3))


def _bwd_in(dqkvs, dag, w_in, x, g, dh1, rot):
    s, d = x.shape
    n = w_in.shape[1]
    tm = _row_tile(s)

    def body(d1, d2, d3, dag_ref, w_ref, x_ref, g_ref, dh_ref, c_ref, a_ref, b_ref, gx_ref, dy_ref, gg_ref):
        @pl.when(pl.program_id(0) == 0)
        def _():
            gg_ref[...] = jnp.zeros_like(gg_ref)

        dsum = d1[...].astype(F32) + d2[...].astype(F32) + d3[...].astype(F32)
        reps = (1, D_ATT // LANES)
        cc, aa, bb = jnp.tile(c_ref[...], reps), jnp.tile(a_ref[...], reps), jnp.tile(b_ref[...], reps)
        for blk in range(2):
            t = dsum[:, blk * D_ATT:(blk + 1) * D_ATT]
            dt = t * cc + pltpu.roll(t * aa, ROT_DIM // 2, 1) + pltpu.roll(t * bb, D_ATT - ROT_DIM // 2, 1)
            dy_ref[:, blk * D_ATT:(blk + 1) * D_ATT] = dt.astype(BF16)
        dy_ref[:, 2 * D_ATT:3 * D_ATT] = dsum[:, 2 * D_ATT:].astype(BF16)
        dy_ref[:, 3 * D_ATT:] = dag_ref[...]
        dx, gg = _rms_bwd(x_ref[...], g_ref[...], _dot_nt(dy_ref[...], w_ref[...]))
        gg_ref[...] += jnp.sum(gg, axis=0, keepdims=True)
        gx_ref[...] = dh_ref[...] + dx

    row = lambda w: pl.BlockSpec((tm, w), lambda i: (i, 0))
    return pl.pallas_call(
        body, name="bwd_in", grid=(s // tm,),
        in_specs=[row(3 * D_ATT)] * 3 + [row(2 * D_CONV), _full((d, n)), row(d), _full((1, d)), row(d)]
        + [row(LANES)] * 3,
        out_specs=[row(d), row(n), _full((1, d))],
        out_shape=[jax.ShapeDtypeStruct((s, d), F32), jax.ShapeDtypeStruct((s, n), BF16),
                   jax.ShapeDtypeStruct((1, d), F32)],
        compiler_params=_params("arbitrary"),
    )(*dqkvs, dag, w_in, x, g, dh1, *rot)


def _wgrad(a, b, name, a_blk=None, b_blk=None, blocked_out=False):
    s, ka = a.shape
    nb = b.shape[1]
    a_blk, b_blk = a_blk or ka, b_blk or nb
    na, nbl = ka // a_blk, nb // b_blk
    assert na == 1 or nbl == 1
    tm = min(1024, s)
    nt = s // tm

    def body(a_ref, b_ref, o_ref, acc):
        t = pl.program_id(1)

        @pl.when(t == 0)
        def _():
            acc[...] = jnp.zeros_like(acc)

        acc[...] += _dot_tn(a_ref[...], b_ref[...])

        @pl.when(t == nt - 1)
        def _():
            o_ref[...] = acc[...].astype(BF16)

    if blocked_out:
        out_spec = pl.BlockSpec((None, ka, b_blk), lambda k, t: (k, 0, 0))
        out_shape = jax.ShapeDtypeStruct((nbl, ka, b_blk), BF16)
    elif na > 1:
        out_spec = pl.BlockSpec((a_blk, nb), lambda k, t: (k, 0))
        out_shape = jax.ShapeDtypeStruct((ka, nb), BF16)
    else:
        out_spec = pl.BlockSpec((ka, b_blk), lambda k, t: (0, k))
        out_shape = jax.ShapeDtypeStruct((ka, nb), BF16)
    return pl.pallas_call(
        body, name=name, grid=(na * nbl, nt),
        in_specs=[pl.BlockSpec((tm, a_blk), (lambda k, t: (t, k)) if na > 1 else (lambda k, t: (t, 0))),
                  pl.BlockSpec((tm, b_blk), (lambda k, t: (t, k)) if nbl > 1 else (lambda k, t: (t, 0)))],
        out_specs=out_spec, out_shape=out_shape,
        scratch_shapes=[pltpu.VMEM((a_blk, b_blk), F32)],
        compiler_params=_params("parallel", "arbitrary"),
    )(a, b)


def _adamw(w, gsrc, m, v, name):
    summed = gsrc.ndim == w.ndim + 1
    rows, cols = w.shape
    tr = rows if rows <= 256 else 256
    assert rows % tr == 0
    c1 = 1.0 - ADAM_B1 ** ADAM_STEP
    c2 = 1.0 - ADAM_B2 ** ADAM_STEP

    def body(w_ref, g_ref, m_ref, v_ref, go_ref, d_ref, mo_ref, vo_ref):
        if summed:
            g = g_ref[0].astype(F32)
            for i in range(1, N_DEV):
                g = g + g_ref[i].astype(F32)
        else:
            g = g_ref[...]
        mn = ADAM_B1 * m_ref[...] + (1.0 - ADAM_B1) * g
        vn = ADAM_B2 * v_ref[...] + (1.0 - ADAM_B2) * jnp.square(g)
        go_ref[...] = g
        mo_ref[...] = mn
        vo_ref[...] = vn
        d_ref[...] = -ADAM_LR * ((mn / c1) / (jnp.sqrt(vn / c2) + ADAM_EPS) + ADAM_WD * w_ref[...])

    blk = pl.BlockSpec((tr, cols), lambda i: (i, 0))
    gblk = pl.BlockSpec((N_DEV, tr, cols), lambda i: (0, i, 0)) if summed else blk
    return pl.pallas_call(
        body, name=name, grid=(rows // tr,),
        in_specs=[blk, gblk, blk, blk], out_specs=[blk] * 4,
        out_shape=[jax.ShapeDtypeStruct(w.shape, F32)] * 4,
        compiler_params=_params("parallel"),
    )(w, gsrc, m, v)


def _sum_slots(g, name):
    _, rows, cols = g.shape

    def body(g_ref, o_ref):
        acc = g_ref[0]
        for i in range(1, N_DEV):
            acc = acc + g_ref[i]
        o_ref[...] = acc

    return pl.pallas_call(body, name=name, out_shape=jax.ShapeDtypeStruct((rows, cols), F32),
                          compiler_params=_params())(g)


def _to_strided(t, dil):
    s, c = t.shape
    if dil == 1:
        return t[None]
    return jnp.transpose(t.reshape(s // dil, dil, c), (1, 0, 2))


def _from_strided(t):
    dil, length, c = t.shape
    if dil == 1:
        return t[0]
    return jnp.transpose(t, (1, 0, 2)).reshape(dil * length, c)


def _pad_rows(t, rows):
    return jnp.pad(t, ((0, rows - t.shape[0]), (0, 0)))


def kernel(x, mem, norm_mix_g, w_in, conv_w, conv_b, conv_ln_g, conv_ln_b, w_out, norm_x_g, norm_mem_g, w_xq, w_xk, w_xv, w_xo, norm_mlp_g, w_up, w_down, norm_final_g, loss_target, m_norm_mix_g, m_w_in, m_conv_w, m_conv_b, m_conv_ln_g, m_conv_ln_b, m_w_out, m_norm_x_g, m_norm_mem_g, m_w_xq, m_w_xk, m_w_xv, m_w_xo, m_norm_mlp_g, m_w_up, m_w_down, m_norm_final_g, v_norm_mix_g, v_w_in, v_conv_w, v_conv_b, v_conv_ln_g, v_conv_ln_b, v_w_out, v_norm_x_g, v_norm_mem_g, v_w_xq, v_w_xk, v_w_xv, v_w_xo, v_norm_mlp_g, v_w_up, v_w_down, v_norm_final_g):
    x2, mem2, tgt = x[0], mem[0], loss_target[0]
    s, d = x2.shape
    gf = norm_final_g[None, :]

    row_names = (w_out, w_xq, w_xk, w_xv, w_xo, w_down)
    row_sizes = [w.shape[1] for w in row_names]
    rows_local = jnp.concatenate([w[0].astype(BF16) for w in row_names], axis=0)
    cw_local = jnp.pad(conv_w[0], ((0, 1), (0, LANES - conv_w.shape[2])))
    rows_g, win_g, wup_g, cw_g = _all_gather(
        [rows_local, w_in[0].astype(BF16), w_up[0].astype(BF16), cw_local], "gather_weights")
    full_rows = []
    off = 0
    for sz in row_sizes:
        full_rows.append(rows_g[:, off:off + sz, :].reshape(N_DEV * sz, d))
        off += sz
    w_out_f, w_xq_f, w_xk_f, w_xv_f, w_xo_f, w_down_f = full_rows
    w_down_b = w_down_f.reshape(N_DEV, -1, d)
    w_in_f = jnp.transpose(win_g, (1, 0, 2)).reshape(d, -1)
    cw_f = jnp.transpose(cw_g[:, :, :conv_w.shape[2]], (1, 0, 2)).reshape(32, D_CONV)

    rot = _rotary_tables(s)
    xn, qkv, ag = _fwd_in(x2, norm_mix_g, w_in_f, rot)
    qkv3 = [_to_strided(qkv, dil) for dil in DILATIONS]
    outs, lses = [], []
    for dil, q3 in zip(DILATIONS, qkv3):
        o3, l3 = _swa_fwd(q3, f"swa_fwd_d{dil}")
        outs.append(_from_strided(o3))
        lses.append(_from_strided(l3))
    c1, conv_out = _fwd_conv(ag, cw_f, conv_b, conv_ln_g, conv_ln_b)
    h1, cat, ltot = _fwd_mix_out(outs, lses, conv_out, x2, w_out_f)
    mn, xk, xv = _fwd_mem(mem2, norm_mem_g, w_xk_f, w_xv_f)
    h2, hn2, xq, xo = _fwd_xattn(h1, norm_x_g, w_xq_f, xk, xv, w_xo_f)
    hn3, act, dh3, dh3b, loss_part, g_final = _fwd_mlp_loss(h2, norm_mlp_g, wup_g, w_down_b, gf, tgt)

    du, dh2, dh2b, g_mlp = _bwd_mlp(dh3, dh3b, act, wup_g, w_down_b, h2, norm_mlp_g)
    dh1, dh1b, dxq, dxk, dxv, g_x = _bwd_xattn(dh2, dh2b, h1, norm_x_g, xq, xk, xv, w_xq_f, w_xo_f)
    gw_xk, gw_xv, g_mem = _bwd_mem(mem2, norm_mem_g, mn, dxk, dxv, w_xk_f, w_xv_f)
    head = jnp.arange(D_ATT) // HEAD_DIM
    head_ones = (head[:, None] == head[None, :]).astype(BF16)
    dcat, dsum = _bwd_mix_out(dh1b, w_out_f, cat, head_ones)
    dag, g_cw, g_cb, g_lg, g_lb = _bwd_conv(dcat, c1, ag, cw_f, conv_ln_g, conv_ln_b)
    dqkvs = []
    for dil, q3 in zip(DILATIONS, qkv3):
        dq3 = _swa_bwd(q3, _to_strided(dcat, dil), _to_strided(ltot, dil), _to_strided(dsum, dil),
                       f"swa_bwd_d{dil}")
        dqkvs.append(_from_strided(dq3))
    grad_x, dy, g_mix = _bwd_in(dqkvs, dag, w_in_f, x2, norm_mix_g, dh1, rot)

    f_blk = w_up.shape[2]
    gw_in = _wgrad(xn, dy, "wgrad_in", b_blk=dy.shape[1] // 2)
    gw_out = _wgrad(cat, dh1b, "wgrad_out")
    gw_xq = _wgrad(hn2, dxq, "wgrad_xq")
    gw_xo = _wgrad(xo, dh2b, "wgrad_xo")
    gw_up = _wgrad(hn3, du, "wgrad_up", b_blk=f_blk, blocked_out=True)
    gw_down = _wgrad(act, dh3b, "wgrad_down", a_blk=f_blk)

    n_in = w_in.shape[2]
    blocks = [jnp.transpose(gw_in.reshape(d, N_DEV, n_in), (1, 0, 2)), gw_up]
    blocks += [g.reshape(N_DEV, -1, d) for g in (gw_out, gw_xq, gw_xk, gw_xv, gw_xo, gw_down)]
    r_in, r_up, r_out, r_xq, r_xk, r_xv, r_xo, r_down = _all_to_all(blocks, "scatter_grads")

    def widen(t):
        return jnp.pad(t, ((0, 0), (0, d - t.shape[1])))

    small = jnp.concatenate([g_mix, g_x, g_mem, g_mlp, g_final, widen(g_cb), widen(g_lg), widen(g_lb),
                             g_cw.reshape(16, d)], axis=0)
    (small_g,) = _all_gather([small], "gather_small_grads")
    small_sum = _sum_slots(small_g, "sum_small_grads")
    loss = lax.psum(loss_part[0, 0], ("x", "y", "c"))

    res = {}

    def step(name, w, gsrc, m, v):
        shape = w.shape
        w2, m2, v2 = (t.reshape(-1, shape[-1]) for t in (w, m, v))
        res[name] = [t.reshape(shape) for t in _adamw(w2, gsrc, m2, v2, "adamw_" + name)]

    step("w_in", w_in, r_in, m_w_in, v_w_in)
    step("w_up", w_up, r_up, m_w_up, v_w_up)
    step("w_out", w_out, r_out, m_w_out, v_w_out)
    step("w_xq", w_xq, r_xq, m_w_xq, v_w_xq)
    step("w_xk", w_xk, r_xk, m_w_xk, v_w_xk)
    step("w_xv", w_xv, r_xv, m_w_xv, v_w_xv)
    step("w_xo", w_xo, r_xo, m_w_xo, v_w_xo)
    step("w_down", w_down, r_down, m_w_down, v_w_down)

    me = _dev_index((lax.axis_index("x"), lax.axis_index("y"), lax.axis_index("c")))
    n_cw = conv_w.shape[2]
    g_cw_full = small_sum[8:24].reshape(32, D_CONV)[:CONV_WIDTH]
    g_cw_mine = lax.dynamic_slice_in_dim(g_cw_full, me * n_cw, n_cw, axis=1)
    step("conv_w", conv_w, g_cw_mine, m_conv_w, v_conv_w)

    vec_names = ["norm_mix_g", "norm_x_g", "norm_mem_g", "norm_mlp_g", "norm_final_g", "conv_b", "conv_ln_g", "conv_ln_b"]
    vec_w = [norm_mix_g, norm_x_g, norm_mem_g, norm_mlp_g, gf, conv_b, conv_ln_g, conv_ln_b]
    vec_m = [m_norm_mix_g, m_norm_x_g, m_norm_mem_g, m_norm_mlp_g, m_norm_final_g[None, :], m_conv_b, m_conv_ln_g, m_conv_ln_b]
    vec_v = [v_norm_mix_g, v_norm_x_g, v_norm_mem_g, v_norm_mlp_g, v_norm_final_g[None, :], v_conv_b, v_conv_ln_g, v_conv_ln_b]

    def pack(ts):
        return jnp.concatenate([widen(t) for t in ts], axis=0)

    packed = _adamw(pack(vec_w), small_sum[0:8], pack(vec_m), pack(vec_v), "adamw_vectors")
    for i, name in enumerate(vec_names):
        width = vec_w[i].shape[1]
        shape = (width,) if name == "norm_final_g" else (1, width)
        res[name] = [t[i, :width].reshape(shape) for t in packed]

    order = ["norm_mix_g", "w_in", "conv_w", "conv_b", "conv_ln_g", "conv_ln_b", "w_out", "norm_x_g", "norm_mem_g",
             "w_xq", "w_xk", "w_xv", "w_xo", "norm_mlp_g", "w_up", "w_down", "norm_final_g"]
    out = [loss, grad_x[None]]
    for kind in range(4):
        out += [res[name][kind] for name in order]
    return tuple(out)
```

```python
import jax
import jax.numpy as jnp
from jax import lax
from jax.experimental import pallas as pl
from jax.experimental.pallas import tpu as pltpu

F32 = jnp.float32
BF16 = jnp.bfloat16

N_DEV = 8
EPS = 1e-6
NEG_INF = -1e30
ATT_HEADS = 8
HEAD_DIM = 64
D_ATT = ATT_HEADS * HEAD_DIM
D_CONV = 512
DILATIONS = (1, 4, 16)
HALF = 64
ROPE_THETA = 500000.0
ROT_DIM = HEAD_DIM // 4
CONV_WIDTH = 31
CONV_PAD = (CONV_WIDTH - 1) // 2
XATT_HEADS = 4
ADAM_LR = 0.001
ADAM_B1 = 0.9
ADAM_B2 = 0.999
ADAM_EPS = 1e-08
ADAM_WD = 0.01
ADAM_STEP = 10

LANES = 128
BF16_ROWS = 16
BQ = 128
WIN = BQ + 2 * HALF
MLP_SHARDS = 4
VMEM_LIMIT = 56 * 1024 * 1024
MESH = pl.DeviceIdType.MESH
ANY = pl.BlockSpec(memory_space=pl.ANY)

_NT = (((1,), (1,)), ((), ()))
_TN = (((0,), (0,)), ((), ()))


def _dot(a, b):
    return jnp.dot(a, b, preferred_element_type=F32)


def _dot_nt(a, b):
    return lax.dot_general(a, b, _NT, preferred_element_type=F32)


def _dot_tn(a, b):
    return lax.dot_general(a, b, _TN, preferred_element_type=F32)


def _params(*sem):
    return pltpu.CompilerParams(dimension_semantics=sem or None, vmem_limit_bytes=VMEM_LIMIT)


def _sigmoid(v):
    return 1.0 / (1.0 + jnp.exp(-v))


def _mean(v):
    return jnp.mean(v, axis=-1, keepdims=True)


def _rms_fwd(h, g):
    r = lax.rsqrt(_mean(h * h) + EPS)
    return h * r * g, r


def _rms_bwd(h, g, d_out):
    r = lax.rsqrt(_mean(h * h) + EPS)
    hn = h * r
    gd = d_out * g
    return r * (gd - hn * _mean(gd * hn)), d_out * hn


def _row_tile(s):
    return min(512, s)


def _full(shape):
    return pl.BlockSpec(shape, lambda *_: (0,) * len(shape))


def _mesh_pos():
    return lax.axis_index("x"), lax.axis_index("y"), lax.axis_index("c")


def _dev_index(p):
    return 4 * p[0] + 2 * p[1] + p[2]


def _all_gather(xs, name):
    n = len(xs)

    def body(*refs):
        x_refs, o_refs = refs[:n], refs[n:2 * n]
        send_sems, recv_sems, local_sems = refs[2 * n:]
        x, y, c = _mesh_pos()
        me, sibling = (x, y, c), (x, y, 1 - c)
        chips = [(1 - x, y), (x, 1 - y), (1 - x, 1 - y)]

        def copy(a, k, block, to, src=None):
            slot = o_refs[a].at[_dev_index(block)]
            return pltpu.make_async_remote_copy(
                src_ref=slot if src is None else src, dst_ref=slot,
                send_sem=send_sems.at[7 * a + k], recv_sem=recv_sems.at[7 * a + k],
                device_id=to, device_id_type=MESH)

        mine = [pltpu.make_async_copy(x_refs[a], o_refs[a].at[_dev_index(me)], local_sems.at[a])
                for a in range(n)]
        for cp in mine:
            cp.start()
        first = []
        for a in range(n):
            first.append(copy(a, 0, me, sibling, src=x_refs[a]))
            first += [copy(a, 1 + j, me, (*chip, c), src=x_refs[a]) for j, chip in enumerate(chips)]
        for cp in first:
            cp.start()
        passed = []
        for j, chip in enumerate(chips):
            for a in range(n):
                copy(a, 1 + j, (*chip, c), me).wait_recv()
                fwd = copy(a, 4 + j, (*chip, c), sibling)
                fwd.start()
                passed.append(fwd)
        for a in range(n):
            copy(a, 0, sibling, me).wait_recv()
            for j, chip in enumerate(chips):
                copy(a, 4 + j, (*chip, 1 - c), me).wait_recv()
        for cp in first + passed:
            cp.wait_send()
        for cp in mine:
            cp.wait()

    return pl.pallas_call(
        body, name=name,
        out_shape=[jax.ShapeDtypeStruct((N_DEV,) + x.shape, x.dtype) for x in xs],
        in_specs=[ANY] * n, out_specs=[ANY] * n,
        scratch_shapes=[pltpu.SemaphoreType.DMA((7 * n,)), pltpu.SemaphoreType.DMA((7 * n,)),
                        pltpu.SemaphoreType.DMA((n,))],
    )(*xs)


def _all_to_all(gs, name):
    n = len(gs)
    flips = [(dx, dy, dc) for dx in (0, 1) for dy in (0, 1) for dc in (0, 1)][1:]

    def body(*refs):
        g_refs, o_refs = refs[:n], refs[n:2 * n]
        send_sems, recv_sems, local_sems = refs[2 * n:]
        x, y, c = _mesh_pos()
        me = (x, y, c)

        def peer(f):
            return tuple(1 - v if fl else v for v, fl in zip(me, f))

        def copy(a, k):
            to = peer(flips[k])
            return pltpu.make_async_remote_copy(
                src_ref=g_refs[a].at[_dev_index(to)], dst_ref=o_refs[a].at[_dev_index(me)],
                send_sem=send_sems.at[7 * a + k], recv_sem=recv_sems.at[7 * a + k],
                device_id=to, device_id_type=MESH)

        def landed(a, k):
            frm = peer(flips[k])
            return pltpu.make_async_remote_copy(
                src_ref=g_refs[a].at[_dev_index(frm)], dst_ref=o_refs[a].at[_dev_index(frm)],
                send_sem=send_sems.at[7 * a + k], recv_sem=recv_sems.at[7 * a + k],
                device_id=frm, device_id_type=MESH)

        mine = [pltpu.make_async_copy(g_refs[a].at[_dev_index(me)], o_refs[a].at[_dev_index(me)],
                                      local_sems.at[a]) for a in range(n)]
        sends = [copy(a, k) for k in range(7) for a in range(n)]
        for cp in mine + sends:
            cp.start()
        for k in range(7):
            for a in range(n):
                landed(a, k).wait_recv()
        for cp in sends:
            cp.wait_send()
        for cp in mine:
            cp.wait()

    return pl.pallas_call(
        body, name=name,
        out_shape=[jax.ShapeDtypeStruct(g.shape, g.dtype) for g in gs],
        in_specs=[ANY] * n, out_specs=[ANY] * n,
        scratch_shapes=[pltpu.SemaphoreType.DMA((7 * n,)), pltpu.SemaphoreType.DMA((7 * n,)),
                        pltpu.SemaphoreType.DMA((n,))],
    )(*gs)


def _rotary_tables(s):
    half = ROT_DIM // 2
    freqs = ROPE_THETA ** (-jnp.arange(0, ROT_DIM, 2, dtype=F32) / ROT_DIM)
    ang = jnp.arange(s, dtype=F32)[:, None] * freqs[None, :]
    cos, sin = jnp.cos(ang), jnp.sin(ang)
    one = jnp.ones((s, HEAD_DIM - ROT_DIM), F32)
    zero = jnp.zeros((s, HEAD_DIM - ROT_DIM), F32)
    zh = jnp.zeros((s, half), F32)
    c64 = jnp.concatenate([cos, cos, one], axis=1)
    a64 = jnp.concatenate([-sin, zh, zero], axis=1)
    b64 = jnp.concatenate([zh, sin, zero], axis=1)
    return tuple(jnp.tile(t, (1, LANES // HEAD_DIM)) for t in (c64, a64, b64))


def _fwd_in(x, g, w_in, rot):
    s, d = x.shape
    n = w_in.shape[1]
    tm = _row_tile(s)

    def body(x_ref, g_ref, w_ref, c_ref, a_ref, b_ref, xn_ref, qkv_ref, ag_ref):
        xn = _rms_fwd(x_ref[...], g_ref[...])[0].astype(BF16)
        xn_ref[...] = xn
        y = _dot(xn, w_ref[...])
        reps = (1, D_ATT // LANES)
        cc, aa, bb = jnp.tile(c_ref[...], reps), jnp.tile(a_ref[...], reps), jnp.tile(b_ref[...], reps)
        for blk in range(2):
            t = y[:, blk * D_ATT:(blk + 1) * D_ATT]
            rot_t = t * cc + pltpu.roll(t, D_ATT - ROT_DIM // 2, 1) * aa + pltpu.roll(t, ROT_DIM // 2, 1) * bb
            qkv_ref[:, blk * D_ATT:(blk + 1) * D_ATT] = rot_t.astype(BF16)
        qkv_ref[:, 2 * D_ATT:] = y[:, 2 * D_ATT:3 * D_ATT].astype(BF16)
        ag_ref[...] = y[:, 3 * D_ATT:].astype(BF16)

    row = lambda w: pl.BlockSpec((tm, w), lambda i: (i, 0))
    return pl.pallas_call(
        body, name="fwd_in", grid=(s // tm,),
        in_specs=[row(d), _full((1, d)), _full((d, n)), row(LANES), row(LANES), row(LANES)],
        out_specs=[row(d), row(3 * D_ATT), row(2 * D_CONV)],
        out_shape=[jax.ShapeDtypeStruct((s, d), BF16), jax.ShapeDtypeStruct((s, 3 * D_ATT), BF16),
                   jax.ShapeDtypeStruct((s, 2 * D_CONV), BF16)],
        compiler_params=_params("parallel"),
    )(x, g, w_in, *rot)


def _win_in_specs(length, col, width):
    per = BQ // HALF
    last = length // HALF - 1
    return [
        pl.BlockSpec((None, HALF, width), lambda r, j: (r, jnp.maximum(j * per - 1, 0), col)),
        pl.BlockSpec((None, BQ, width), lambda r, j: (r, j, col)),
        pl.BlockSpec((None, HALF, width), lambda r, j: (r, jnp.minimum(j * per + per, last), col)),
    ]


def _fill_window(win, prev_ref, main_ref, next_ref):
    win[0:HALF] = prev_ref[...]
    win[HALF:HALF + BQ] = main_ref[...]
    win[HALF + BQ:] = next_ref[...]


def _band_mask(j, length):
    shape = (2 * BQ, WIN)
    blk = lax.broadcasted_iota(jnp.int32, shape, 0) & (BQ - 1)
    win = lax.broadcasted_iota(jnp.int32, shape, 1)
    pos = j * BQ - HALF + win
    return (jnp.abs(win - HALF - blk) <= HALF) & (pos >= 0) & (pos < length)


def _first_head():
    return lax.broadcasted_iota(jnp.int32, (1, LANES), 1) < HEAD_DIM


def _stack_heads(v, first):
    zero = jnp.zeros((), v.dtype)
    return jnp.concatenate([jnp.where(first, v, zero), jnp.where(first, zero, v)], axis=0)


def _unstack_heads(v, first):
    rows = v.shape[0] // 2
    return jnp.where(first, v[:rows], v[rows:])


def _stack_cols(v, first):
    top = jnp.max(jnp.where(first, v, -jnp.inf), axis=-1, keepdims=True)
    bot = jnp.max(jnp.where(first, -jnp.inf, v), axis=-1, keepdims=True)
    return jnp.concatenate([top, bot], axis=0)


def _swa_fwd(qkv3, name):
    dil, length, _ = qkv3.shape
    scale = HEAD_DIM ** -0.5

    def body(q_ref, kp, km, kn, vp, vm, vn, o_ref, lse_ref, kwin, vwin):
        j = pl.program_id(1)
        _fill_window(kwin, kp, km, kn)
        _fill_window(vwin, vp, vm, vn)
        valid = _band_mask(j, length)
        first = _first_head()
        for pr in range(D_ATT // LANES):
            cols = slice(pr * LANES, (pr + 1) * LANES)
            qs = _stack_heads(q_ref[:, cols], first)
            sc = jnp.where(valid, _dot_nt(qs, kwin[:, cols]) * scale, NEG_INF)
            m = jnp.max(sc, axis=-1, keepdims=True)
            p = jnp.exp(sc - m)
            den = jnp.sum(p, axis=-1, keepdims=True)
            pv = _dot(p.astype(BF16), vwin[:, cols]) * (1.0 / den)
            o_ref[:, cols] = _unstack_heads(pv, first).astype(BF16)
            lse_ref[:, cols] = _unstack_heads(jnp.broadcast_to(m + jnp.log(den), (2 * BQ, LANES)), first)

    blk = lambda w: pl.BlockSpec((None, BQ, w), lambda r, j: (r, j, 0))
    return pl.pallas_call(
        body, name=name, grid=(dil, length // BQ),
        in_specs=[pl.BlockSpec((None, BQ, D_ATT), lambda r, j: (r, j, 0))]
        + _win_in_specs(length, 1, D_ATT) + _win_in_specs(length, 2, D_ATT),
        out_specs=[blk(D_ATT), blk(D_ATT)],
        out_shape=[jax.ShapeDtypeStruct((dil, length, D_ATT), BF16),
                   jax.ShapeDtypeStruct((dil, length, D_ATT), F32)],
        scratch_shapes=[pltpu.VMEM((WIN, D_ATT), BF16), pltpu.VMEM((WIN, D_ATT), BF16)],
        compiler_params=_params("parallel", "parallel"),
    )(*([qkv3] * 7))


def _glu(v):
    return v[:, :D_CONV].astype(F32) * _sigmoid(v[:, D_CONV:].astype(F32))


def _halo_specs(tm, width, col=0):
    per = tm // BF16_ROWS
    return lambda nblk: [
        pl.BlockSpec((BF16_ROWS, width), lambda i: (jnp.maximum(i * per - 1, 0), col)),
        pl.BlockSpec((tm, width), lambda i: (i, col)),
        pl.BlockSpec((BF16_ROWS, width), lambda i: (jnp.minimum(i * per + per, nblk - 1), col)),
    ]


def _fill_halo(buf, i, ntiles, tm, prev, main, nxt):
    buf[0:BF16_ROWS] = jnp.where(i == 0, 0.0, prev)
    buf[BF16_ROWS:BF16_ROWS + tm] = main
    buf[BF16_ROWS + tm:] = jnp.where(i == ntiles - 1, 0.0, nxt)


def _fwd_conv(ag, cw, cb, lg, lb):
    s = ag.shape[0]
    tm = _row_tile(s)
    nt = s // tm

    def body(agp, agm, agn, cw_ref, cb_ref, lg_ref, lb_ref, c1_ref, co_ref, ubuf):
        i = pl.program_id(0)
        _fill_halo(ubuf, i, nt, tm, _glu(agp[...]), _glu(agm[...]), _glu(agn[...]))
        w = cw_ref[...]
        acc = jnp.zeros((tm, D_CONV), F32)
        for k in range(CONV_WIDTH):
            acc = acc + ubuf[pl.ds(k + 1, tm), :] * w[k:k + 1, :]
        acc = acc + cb_ref[...]
        c1_ref[...] = acc
        xc = acc - _mean(acc)
        ln = xc * lax.rsqrt(_mean(xc * xc) + EPS) * lg_ref[...] + lb_ref[...]
        co_ref[...] = (ln * _sigmoid(ln)).astype(BF16)

    vec = _full((1, D_CONV))
    return pl.pallas_call(
        body, name="fwd_conv", grid=(nt,),
        in_specs=_halo_specs(tm, 2 * D_CONV)(s // BF16_ROWS) + [_full((32, D_CONV)), vec, vec, vec],
        out_specs=[pl.BlockSpec((tm, D_CONV), lambda i: (i, 0))] * 2,
        out_shape=[jax.ShapeDtypeStruct((s, D_CONV), F32), jax.ShapeDtypeStruct((s, D_CONV), BF16)],
        scratch_shapes=[pltpu.VMEM((tm + 2 * BF16_ROWS, D_CONV), F32)],
        compiler_params=_params("parallel"),
    )(ag, ag, ag, cw, cb, lg, lb)


def _fwd_mix_out(outs, lses, conv_out, x, w_out):
    s, d = x.shape
    tm = _row_tile(s)

    def body(o1, o2, o3, l1, l2, l3, co, x_ref, w_ref, h_ref, cat_ref, lt_ref):
        a, b, c = l1[...], l2[...], l3[...]
        m = jnp.maximum(jnp.maximum(a, b), c)
        ea, eb, ec = jnp.exp(a - m), jnp.exp(b - m), jnp.exp(c - m)
        den = ea + eb + ec
        att = (ea * o1[...].astype(F32) + eb * o2[...].astype(F32) + ec * o3[...].astype(F32)) / den
        lt_ref[...] = m + jnp.log(den)
        att = att.astype(BF16)
        cat_ref[:, :D_ATT] = att
        cat_ref[:, D_ATT:] = co[...]
        h_ref[...] = x_ref[...] + _dot(att, w_ref[:D_ATT, :]) + _dot(co[...], w_ref[D_ATT:, :])

    row = lambda w: pl.BlockSpec((tm, w), lambda i: (i, 0))
    return pl.pallas_call(
        body, name="fwd_mix_out", grid=(s // tm,),
        in_specs=[row(D_ATT)] * 7 + [row(d), _full((d, d))],
        out_specs=[row(d), row(d), row(D_ATT)],
        out_shape=[jax.ShapeDtypeStruct((s, d), F32), jax.ShapeDtypeStruct((s, d), BF16),
                   jax.ShapeDtypeStruct((s, D_ATT), F32)],
        compiler_params=_params("parallel"),
    )(*outs, *lses, conv_out, x, w_out)


def _fwd_mem(mem, g, wk, wv):
    m, d = mem.shape

    def body(mem_ref, g_ref, wk_ref, wv_ref, mn_ref, xk_ref, xv_ref):
        mn = _rms_fwd(mem_ref[...], g_ref[...])[0].astype(BF16)
        mn_ref[...] = mn
        xk_ref[...] = _dot(mn, wk_ref[...]).astype(BF16)
        xv_ref[...] = _dot(mn, wv_ref[...]).astype(BF16)

    return pl.pallas_call(
        body, name="fwd_mem",
        out_shape=[jax.ShapeDtypeStruct((m, d), BF16)] * 3,
        compiler_params=_params(),
    )(mem, g, wk, wv)


def _xatt_probs(q, k):
    sc = _dot_nt(q, k) * (q.shape[1] ** -0.5)
    p = jnp.exp(sc - jnp.max(sc, axis=-1, keepdims=True))
    return p / jnp.sum(p, axis=-1, keepdims=True)


def _fwd_xattn(h1, g, wq, xk, xv, wo):
    s, d = h1.shape
    m = xk.shape[0]
    tm = _row_tile(s)
    hd = d // XATT_HEADS

    def body(h_ref, g_ref, wq_ref, xk_ref, xv_ref, wo_ref, h2_ref, hn_ref, xq_ref, xo_ref):
        h = h_ref[...]
        hn = _rms_fwd(h, g_ref[...])[0].astype(BF16)
        hn_ref[...] = hn
        xq = _dot(hn, wq_ref[...]).astype(BF16)
        xq_ref[...] = xq
        for i in range(XATT_HEADS):
            cols = slice(i * hd, (i + 1) * hd)
            pr = _xatt_probs(xq[:, cols], xk_ref[:, cols])
            xo_ref[:, cols] = _dot(pr.astype(BF16), xv_ref[:, cols]).astype(BF16)
        h2_ref[...] = h + _dot(xo_ref[...], wo_ref[...])

    row = pl.BlockSpec((tm, d), lambda i: (i, 0))
    return pl.pallas_call(
        body, name="fwd_xattn", grid=(s // tm,),
        in_specs=[row, _full((1, d)), _full((d, d)), _full((m, d)), _full((m, d)), _full((d, d))],
        out_specs=[row] * 4,
        out_shape=[jax.ShapeDtypeStruct((s, d), F32)] + [jax.ShapeDtypeStruct((s, d), BF16)] * 3,
        compiler_params=_params("parallel"),
    )(h1, g, wq, xk, xv, wo)


def _fwd_mlp_loss(h2, g, w_up, w_down, gf, target):
    s, d = h2.shape
    nsh, _, f = w_up.shape
    fb = MLP_SHARDS * f
    nb = nsh // MLP_SHARDS
    tm = _row_tile(s)

    def body(h_ref, g_ref, wu_ref, wd_ref, gf_ref, t_ref,
             hn_ref, act_ref, dh_ref, dhb_ref, loss_ref, ggf_ref, acc):
        i, k = pl.program_id(0), pl.program_id(1)

        @pl.when(k == 0)
        def _():
            hn_ref[...] = _rms_fwd(h_ref[...], g_ref[...])[0].astype(BF16)
            acc[...] = jnp.zeros_like(acc)

        @pl.when((i == 0) & (k == 0))
        def _():
            loss_ref[...] = jnp.zeros_like(loss_ref)
            ggf_ref[...] = jnp.zeros_like(ggf_ref)

        hn = hn_ref[...]
        for c in range(MLP_SHARDS):
            act_ref[:, c * f:(c + 1) * f] = jnp.square(jnp.maximum(_dot(hn, wu_ref[c]), 0.0)).astype(BF16)
        acc[...] += _dot(act_ref[...], wd_ref[...])

        @pl.when(k == nb - 1)
        def _():
            h3 = h_ref[...] + acc[...]
            gfv = gf_ref[...]
            y, _ = _rms_fwd(h3, gfv)
            err = y - t_ref[...]
            loss_ref[...] += 0.5 * jnp.sum(_mean(err * err))
            dh3, gg = _rms_bwd(h3, gfv, err * (1.0 / d))
            ggf_ref[...] += jnp.sum(gg, axis=0, keepdims=True)
            dh_ref[...] = dh3
            dhb_ref[...] = dh3.astype(BF16)

    row = pl.BlockSpec((tm, d), lambda i, k: (i, 0))
    return pl.pallas_call(
        body, name="fwd_mlp_loss", grid=(s // tm, nb),
        in_specs=[row, _full((1, d)),
                  pl.BlockSpec((MLP_SHARDS, d, f), lambda i, k: (k, 0, 0)),
                  pl.BlockSpec((fb, d), lambda i, k: (k, 0)),
                  _full((1, d)), row],
        out_specs=[row, pl.BlockSpec((tm, fb), lambda i, k: (i, k)), row, row,
                   _full((1, LANES)), _full((1, d))],
        out_shape=[jax.ShapeDtypeStruct((s, d), BF16), jax.ShapeDtypeStruct((s, nsh * f), BF16),
                   jax.ShapeDtypeStruct((s, d), F32), jax.ShapeDtypeStruct((s, d), BF16),
                   jax.ShapeDtypeStruct((1, LANES), F32), jax.ShapeDtypeStruct((1, d), F32)],
        scratch_shapes=[pltpu.VMEM((tm, d), F32)],
        compiler_params=_params("arbitrary", "arbitrary"),
    )(h2, g, w_up, w_down, gf, target)


def _bwd_mlp(dh3, dh3b, act, w_up_t, w_down, h2, g):
    s, d = h2.shape
    ff = w_down.shape[0]
    fb = MLP_SHARDS * (ff // N_DEV)
    nb = ff // fb
    tm = _row_tile(s)

    def body(dh_ref, dhb_ref, act_ref, wut_ref, wd_ref, h_ref, g_ref,
             du_ref, dh2_ref, dh2b_ref, gg_ref, acc):
        i, k = pl.program_id(0), pl.program_id(1)

        @pl.when(k == 0)
        def _():
            acc[...] = jnp.zeros_like(acc)

        @pl.when((i == 0) & (k == 0))
        def _():
            gg_ref[...] = jnp.zeros_like(gg_ref)

        dact = _dot_nt(dhb_ref[...], wd_ref[...])
        du_ref[...] = (dact * (2.0 * jnp.sqrt(act_ref[...].astype(F32)))).astype(BF16)
        acc[...] += _dot(du_ref[...], wut_ref[...])

        @pl.when(k == nb - 1)
        def _():
            dh, gg = _rms_bwd(h_ref[...], g_ref[...], acc[...])
            gg_ref[...] += jnp.sum(gg, axis=0, keepdims=True)
            dh2 = dh_ref[...] + dh
            dh2_ref[...] = dh2
            dh2b_ref[...] = dh2.astype(BF16)

    row = pl.BlockSpec((tm, d), lambda i, k: (i, 0))
    col = pl.BlockSpec((tm, fb), lambda i, k: (i, k))
    wblk = pl.BlockSpec((fb, d), lambda i, k: (k, 0))
    return pl.pallas_call(
        body, name="bwd_mlp", grid=(s // tm, nb),
        in_specs=[row, row, col, wblk, wblk, row, _full((1, d))],
        out_specs=[col, row, row, _full((1, d))],
        out_shape=[jax.ShapeDtypeStruct((s, ff), BF16), jax.ShapeDtypeStruct((s, d), F32),
                   jax.ShapeDtypeStruct((s, d), BF16), jax.ShapeDtypeStruct((1, d), F32)],
        scratch_shapes=[pltpu.VMEM((tm, d), F32)],
        compiler_params=_params("arbitrary", "arbitrary"),
    )(dh3, dh3b, act, w_up_t, w_down, h2, g)


def _bwd_xattn(dh2, dh2b, h1, g, xq, xk, xv, wq, wo):
    s, d = h1.shape
    m = xk.shape[0]
    tm = _row_tile(s)
    hd = d // XATT_HEADS
    scale = hd ** -0.5

    def body(dh_ref, dhb_ref, h_ref, g_ref, xq_ref, xk_ref, xv_ref, wq_ref, wo_ref,
             dh1_ref, dh1b_ref, dxq_ref, dxk_ref, dxv_ref, gg_ref):
        @pl.when(pl.program_id(0) == 0)
        def _():
            dxk_ref[...] = jnp.zeros_like(dxk_ref)
            dxv_ref[...] = jnp.zeros_like(dxv_ref)
            gg_ref[...] = jnp.zeros_like(gg_ref)

        dxo = _dot_nt(dhb_ref[...], wo_ref[...])
        for i in range(XATT_HEADS):
            cols = slice(i * hd, (i + 1) * hd)
            q, k, v = xq_ref[:, cols], xk_ref[:, cols], xv_ref[:, cols]
            pr = _xatt_probs(q, k)
            dxo_h = dxo[:, cols].astype(BF16)
            dpr = _dot_nt(dxo_h, v)
            dsc = (pr * (dpr - jnp.sum(dpr * pr, axis=-1, keepdims=True)) * scale).astype(BF16)
            dxq_ref[:, cols] = _dot(dsc, k).astype(BF16)
            dxk_ref[:, cols] += _dot_tn(dsc, q)
            dxv_ref[:, cols] += _dot_tn(pr.astype(BF16), dxo_h)
        dh, gg = _rms_bwd(h_ref[...], g_ref[...], _dot_nt(dxq_ref[...], wq_ref[...]))
        gg_ref[...] += jnp.sum(gg, axis=0, keepdims=True)
        dh1 = dh_ref[...] + dh
        dh1_ref[...] = dh1
        dh1b_ref[...] = dh1.astype(BF16)

    row = pl.BlockSpec((tm, d), lambda i: (i, 0))
    return pl.pallas_call(
        body, name="bwd_xattn", grid=(s // tm,),
        in_specs=[row, row, row, _full((1, d)), row, _full((m, d)), _full((m, d)), _full((d, d)), _full((d, d))],
        out_specs=[row, row, row, _full((m, d)), _full((m, d)), _full((1, d))],
        out_shape=[jax.ShapeDtypeStruct((s, d), F32), jax.ShapeDtypeStruct((s, d), BF16),
                   jax.ShapeDtypeStruct((s, d), BF16), jax.ShapeDtypeStruct((m, d), F32),
                   jax.ShapeDtypeStruct((m, d), F32), jax.ShapeDtypeStruct((1, d), F32)],
        compiler_params=_params("arbitrary"),
    )(dh2, dh2b, h1, g, xq, xk, xv, wq, wo)


def _bwd_mem(mem, g, mn, dxk, dxv, wk, wv):
    m, d = mem.shape

    def body(mem_ref, g_ref, mn_ref, dxk_ref, dxv_ref, wk_ref, wv_ref, gk_ref, gv_ref, gg_ref):
        dk, dv = dxk_ref[...].astype(BF16), dxv_ref[...].astype(BF16)
        gk_ref[...] = _dot_tn(mn_ref[...], dk).astype(BF16)
        gv_ref[...] = _dot_tn(mn_ref[...], dv).astype(BF16)
        dmn = _dot_nt(dk, wk_ref[...]) + _dot_nt(dv, wv_ref[...])
        _, gg = _rms_bwd(mem_ref[...], g_ref[...], dmn)
        gg_ref[...] = jnp.sum(gg, axis=0, keepdims=True)

    return pl.pallas_call(
        body, name="bwd_mem",
        out_shape=[jax.ShapeDtypeStruct((d, d), BF16), jax.ShapeDtypeStruct((d, d), BF16),
                   jax.ShapeDtypeStruct((1, d), F32)],
        compiler_params=_params(),
    )(mem, g, mn, dxk, dxv, wk, wv)


def _bwd_mix_out(dh1b, w_out, cat, head_ones):
    s, d = dh1b.shape
    tm = _row_tile(s)

    def body(dh_ref, w_ref, cat_ref, ones_ref, dcat_ref, dsum_ref):
        dcat = _dot_nt(dh_ref[...], w_ref[...])
        dcat_ref[...] = dcat.astype(BF16)
        prod = dcat[:, :D_ATT] * cat_ref[...].astype(F32)
        hi = prod.astype(BF16)
        lo = (prod - hi.astype(F32)).astype(BF16)
        dsum_ref[...] = _dot(hi, ones_ref[...]) + _dot(lo, ones_ref[...])

    row = lambda w: pl.BlockSpec((tm, w), lambda i: (i, 0))
    return pl.pallas_call(
        body, name="bwd_mix_out", grid=(s // tm,),
        in_specs=[row(d), _full((d, d)), row(D_ATT), _full((D_ATT, D_ATT))],
        out_specs=[row(d), row(D_ATT)],
        out_shape=[jax.ShapeDtypeStruct((s, d), BF16), jax.ShapeDtypeStruct((s, D_ATT), F32)],
        compiler_params=_params("parallel"),
    )(dh1b, w_out, cat, head_ones)


def _bwd_conv(dcat, c1, ag, cw, lg, lb):
    s = ag.shape[0]
    tm = _row_tile(s)
    nt = s // tm

    def body(dp, dm, dn, cp, cm, cn, agp, agm, agn, cw_ref, lg_ref, lb_ref,
             dag_ref, gcw_ref, gcb_ref, glg_ref, glb_ref, ubuf, dbuf):
        i = pl.program_id(0)

        @pl.when(i == 0)
        def _():
            gcw_ref[...] = jnp.zeros_like(gcw_ref)
            gcb_ref[...] = jnp.zeros_like(gcb_ref)
            glg_ref[...] = jnp.zeros_like(glg_ref)
            glb_ref[...] = jnp.zeros_like(glb_ref)

        lgv, lbv = lg_ref[...], lb_ref[...]

        def norm_bwd(dco, c1v):
            xc = c1v - _mean(c1v)
            rs = lax.rsqrt(_mean(xc * xc) + EPS)
            z = xc * rs
            ln = z * lgv + lbv
            sg = _sigmoid(ln)
            dln = dco.astype(F32) * (sg * (1.0 + ln * (1.0 - sg)))
            dz = dln * lgv
            return rs * (dz - _mean(dz) - z * _mean(dz * z)), dln, z

        dc_m, dln, z = norm_bwd(dm[...], cm[...])
        glg_ref[...] += jnp.sum(dln * z, axis=0, keepdims=True)
        glb_ref[...] += jnp.sum(dln, axis=0, keepdims=True)
        gcb_ref[...] += jnp.sum(dc_m, axis=0, keepdims=True)
        _fill_halo(dbuf, i, nt, tm, norm_bwd(dp[...], cp[...])[0], dc_m, norm_bwd(dn[...], cn[...])[0])
        _fill_halo(ubuf, i, nt, tm, _glu(agp[...]), _glu(agm[...]), _glu(agn[...]))

        w = cw_ref[...]
        tap = lax.broadcasted_iota(jnp.int32, (32, D_CONV), 0)
        du = jnp.zeros((tm, D_CONV), F32)
        gcw = jnp.zeros((32, D_CONV), F32)
        for k in range(CONV_WIDTH):
            du = du + dbuf[pl.ds(CONV_WIDTH - k, tm), :] * w[k:k + 1, :]
            gk = jnp.sum(dc_m * ubuf[pl.ds(k + 1, tm), :], axis=0, keepdims=True)
            gcw = jnp.where(tap == k, gk, gcw)
        gcw_ref[...] += gcw
        a = agm[:, :D_CONV].astype(F32)
        sg = _sigmoid(agm[:, D_CONV:].astype(F32))
        dag_ref[:, :D_CONV] = (du * sg).astype(BF16)
        dag_ref[:, D_CONV:] = (du * a * sg * (1.0 - sg)).astype(BF16)

    vec = _full((1, D_CONV))
    nblk = s // BF16_ROWS
    return pl.pallas_call(
        body, name="bwd_conv", grid=(nt,),
        in_specs=_halo_specs(tm, D_CONV, 1)(nblk) + _halo_specs(tm, D_CONV)(nblk) + _halo_specs(tm, 2 * D_CONV)(nblk)
        + [_full((32, D_CONV)), vec, vec],
        out_specs=[pl.BlockSpec((tm, 2 * D_CONV), lambda i: (i, 0)), _full((32, D_CONV)), vec, vec, vec],
        out_shape=[jax.ShapeDtypeStruct((s, 2 * D_CONV), BF16), jax.ShapeDtypeStruct((32, D_CONV), F32)]
        + [jax.ShapeDtypeStruct((1, D_CONV), F32)] * 3,
        scratch_shapes=[pltpu.VMEM((tm + 2 * BF16_ROWS, D_CONV), F32)] * 2,
        compiler_params=_params("arbitrary"),
    )(dcat, dcat, dcat, c1, c1, c1, ag, ag, ag, cw, lg, lb)


def _swa_bwd(qkv3, do3, lt3, ds3, name):
    dil, length, _ = qkv3.shape
    scale = HEAD_DIM ** -0.5

    def body(q_ref, kp, km, kn, vp, vm, vn, do_ref, l_ref, s_ref, dq_ref, part_ref, kwin, vwin):
        j = pl.program_id(1)
        _fill_window(kwin, kp, km, kn)
        _fill_window(vwin, vp, vm, vn)
        valid = _band_mask(j, length)
        first = _first_head()
        for pr in range(D_ATT // LANES):
            cols = slice(pr * LANES, (pr + 1) * LANES)
            qs = _stack_heads(q_ref[:, cols], first)
            dos = _stack_heads(do_ref[:, cols], first)
            kw, vw = kwin[:, cols], vwin[:, cols]
            sc = _dot_nt(qs, kw) * scale
            p = jnp.where(valid, jnp.exp(sc - _stack_cols(l_ref[:, cols], first)), 0.0)
            dp = _dot_nt(dos, vw)
            dsc = (p * (dp - _stack_cols(s_ref[:, cols], first)) * scale).astype(BF16)
            dq_ref[:, cols] = _unstack_heads(_dot(dsc, kw), first).astype(BF16)
            part_ref[:, cols] = _dot_tn(dsc, qs).astype(BF16)
            part_ref[:, D_ATT + pr * LANES:D_ATT + (pr + 1) * LANES] = _dot_tn(p.astype(BF16), dos).astype(BF16)

    main = lambda w: pl.BlockSpec((None, BQ, w), lambda r, j: (r, j, 0))
    return pl.pallas_call(
        body, name=name, grid=(dil, length // BQ),
        in_specs=[main(D_ATT)] + _win_in_specs(length, 1, D_ATT) + _win_in_specs(length, 2, D_ATT) + [main(D_ATT)] * 3,
        out_specs=[main(D_ATT), pl.BlockSpec((None, None, WIN, 2 * D_ATT), lambda r, j: (r, j, 0, 0))],
        out_shape=[jax.ShapeDtypeStruct((dil, length, D_ATT), BF16),
                   jax.ShapeDtypeStruct((dil, length // BQ, WIN, 2 * D_ATT), BF16)],
        scratch_shapes=[pltpu.VMEM((WIN, D_ATT), BF16)] * 2,
        compiler_params=_params("parallel", "parallel"),
    )(*([qkv3] * 7), do3, lt3, ds3)


def _swa_combine(part, name):
    dil, nb, _, width = part.shape
    assert BQ == 2 * HALF and WIN == 4 * HALF

    def body(prev_ref, lo_ref, hi_ref, next_ref, out_ref):
        j = pl.program_id(1)
        lo = lo_ref[...].astype(F32) + jnp.where(j > 0, prev_ref[...].astype(F32), 0.0)
        hi = hi_ref[...].astype(F32) + jnp.where(j < nb - 1, next_ref[...].astype(F32), 0.0)
        out_ref[0:HALF] = lo.astype(BF16)
        out_ref[HALF:] = hi.astype(BF16)

    def chunk(dj, c):
        return pl.BlockSpec((None, None, HALF, width),
                            lambda r, j: (r, jnp.clip(j + dj, 0, nb - 1), c, 0))

    return pl.pallas_call(
        body, name=name, grid=(dil, nb),
        in_specs=[chunk(-1, 3), chunk(0, 1), chunk(0, 2), chunk(1, 0)],
        out_specs=pl.BlockSpec((None, BQ, width), lambda r, j: (r, j, 0)),
        out_shape=jax.ShapeDtypeStruct((dil, nb * BQ, width), BF16),
        compiler_params=_params("parallel", "parallel"),
    )(part, part, part, part)


def _bwd_in(dqs, dkvs, dag, w_in, x, g, dh1, rot):
    s, d = x.shape
    n = w_in.shape[1]
    tm = _row_tile(s)

    def body(q1, q2, q3, kv1, kv2, kv3, dag_ref, w_ref, x_ref, g_ref, dh_ref, c_ref, a_ref, b_ref,
             gx_ref, dy_ref, gg_ref):
        @pl.when(pl.program_id(0) == 0)
        def _():
            gg_ref[...] = jnp.zeros_like(gg_ref)

        dq = q1[...].astype(F32) + q2[...].astype(F32) + q3[...].astype(F32)
        dkv = kv1[...].astype(F32) + kv2[...].astype(F32) + kv3[...].astype(F32)
        reps = (1, D_ATT // LANES)
        cc, aa, bb = jnp.tile(c_ref[...], reps), jnp.tile(a_ref[...], reps), jnp.tile(b_ref[...], reps)
        for blk, t in enumerate((dq, dkv[:, :D_ATT])):
            dt = t * cc + pltpu.roll(t * aa, ROT_DIM // 2, 1) + pltpu.roll(t * bb, D_ATT - ROT_DIM // 2, 1)
            dy_ref[:, blk * D_ATT:(blk + 1) * D_ATT] = dt.astype(BF16)
        dy_ref[:, 2 * D_ATT:3 * D_ATT] = dkv[:, D_ATT:].astype(BF16)
        dy_ref[:, 3 * D_ATT:] = dag_ref[...]
        dx, gg = _rms_bwd(x_ref[...], g_ref[...], _dot_nt(dy_ref[...], w_ref[...]))
        gg_ref[...] += jnp.sum(gg, axis=0, keepdims=True)
        gx_ref[...] = dh_ref[...] + dx

    row = lambda w: pl.BlockSpec((tm, w), lambda i: (i, 0))
    return pl.pallas_call(
        body, name="bwd_in", grid=(s // tm,),
        in_specs=[row(D_ATT)] * 3 + [row(2 * D_ATT)] * 3
        + [row(2 * D_CONV), _full((d, n)), row(d), _full((1, d)), row(d)] + [row(LANES)] * 3,
        out_specs=[row(d), row(n), _full((1, d))],
        out_shape=[jax.ShapeDtypeStruct((s, d), F32), jax.ShapeDtypeStruct((s, n), BF16),
                   jax.ShapeDtypeStruct((1, d), F32)],
        compiler_params=_params("arbitrary"),
    )(*dqs, *dkvs, dag, w_in, x, g, dh1, *rot)


def _wgrad(a, b, name, a_blk=None, b_blk=None, stack=None, tm=1024):
    s, ka = a.shape
    nb = b.shape[1]
    a_blk, b_blk = a_blk or ka, b_blk or nb
    na, nbl = ka // a_blk, nb // b_blk
    assert na == 1 or nbl == 1
    tm = min(tm, s)
    nt = s // tm
    per = b_blk // stack if stack else 0

    def body(a_ref, b_ref, o_ref, acc):
        t = pl.program_id(1)

        @pl.when(t == 0)
        def _():
            acc[...] = jnp.zeros_like(acc)

        acc[...] += _dot_tn(a_ref[...], b_ref[...])

        @pl.when(t == nt - 1)
        def _():
            if stack:
                for c in range(per):
                    o_ref[c] = acc[:, c * stack:(c + 1) * stack].astype(BF16)
            else:
                o_ref[...] = acc[...].astype(BF16)

    if stack:
        out_spec = pl.BlockSpec((per, ka, stack), lambda k, t: (k, 0, 0))
        out_shape = jax.ShapeDtypeStruct((nb // stack, ka, stack), BF16)
    elif na > 1:
        out_spec = pl.BlockSpec((a_blk, nb), lambda k, t: (k, 0))
        out_shape = jax.ShapeDtypeStruct((ka, nb), BF16)
    else:
        out_spec = pl.BlockSpec((ka, b_blk), lambda k, t: (0, k))
        out_shape = jax.ShapeDtypeStruct((ka, nb), BF16)
    return pl.pallas_call(
        body, name=name, grid=(na * nbl, nt),
        in_specs=[pl.BlockSpec((tm, a_blk), (lambda k, t: (t, k)) if na > 1 else (lambda k, t: (t, 0))),
                  pl.BlockSpec((tm, b_blk), (lambda k, t: (t, k)) if nbl > 1 else (lambda k, t: (t, 0)))],
        out_specs=out_spec, out_shape=out_shape,
        scratch_shapes=[pltpu.VMEM((a_blk, b_blk), F32)],
        compiler_params=_params("parallel", "arbitrary"),
    )(a, b)


def _adamw(w, gsrc, m, v, name):
    summed = gsrc.ndim == w.ndim + 1
    rows, cols = w.shape
    tr = rows if rows <= 256 else 256
    assert rows % tr == 0
    c1 = 1.0 - ADAM_B1 ** ADAM_STEP
    c2 = 1.0 - ADAM_B2 ** ADAM_STEP

    def body(w_ref, g_ref, m_ref, v_ref, go_ref, d_ref, mo_ref, vo_ref):
        if summed:
            g = g_ref[0].astype(F32)
            for i in range(1, N_DEV):
                g = g + g_ref[i].astype(F32)
        else:
            g = g_ref[...]
        mn = ADAM_B1 * m_ref[...] + (1.0 - ADAM_B1) * g
        vn = ADAM_B2 * v_ref[...] + (1.0 - ADAM_B2) * jnp.square(g)
        go_ref[...] = g
        mo_ref[...] = mn
        vo_ref[...] = vn
        d_ref[...] = -ADAM_LR * ((mn / c1) / (jnp.sqrt(vn / c2) + ADAM_EPS) + ADAM_WD * w_ref[...])

    blk = pl.BlockSpec((tr, cols), lambda i: (i, 0))
    gblk = pl.BlockSpec((N_DEV, tr, cols), lambda i: (0, i, 0)) if summed else blk
    return pl.pallas_call(
        body, name=name, grid=(rows // tr,),
        in_specs=[blk, gblk, blk, blk], out_specs=[blk] * 4,
        out_shape=[jax.ShapeDtypeStruct(w.shape, F32)] * 4,
        compiler_params=_params("parallel"),
    )(w, gsrc, m, v)


def _sum_slots(g, name):
    _, rows, cols = g.shape

    def body(g_ref, o_ref):
        acc = g_ref[0]
        for i in range(1, N_DEV):
            acc = acc + g_ref[i]
        o_ref[...] = acc

    return pl.pallas_call(body, name=name, out_shape=jax.ShapeDtypeStruct((rows, cols), F32),
                          compiler_params=_params())(g)


def _to_strided(t, dil):
    s, c = t.shape
    if dil == 1:
        return t[None]
    return jnp.transpose(t.reshape(s // dil, dil, c), (1, 0, 2))


def _from_strided(t):
    dil, length, c = t.shape
    if dil == 1:
        return t[0]
    return jnp.transpose(t, (1, 0, 2)).reshape(dil * length, c)


def kernel(x, mem, norm_mix_g, w_in, conv_w, conv_b, conv_ln_g, conv_ln_b, w_out, norm_x_g, norm_mem_g, w_xq, w_xk, w_xv, w_xo, norm_mlp_g, w_up, w_down, norm_final_g, loss_target, m_norm_mix_g, m_w_in, m_conv_w, m_conv_b, m_conv_ln_g, m_conv_ln_b, m_w_out, m_norm_x_g, m_norm_mem_g, m_w_xq, m_w_xk, m_w_xv, m_w_xo, m_norm_mlp_g, m_w_up, m_w_down, m_norm_final_g, v_norm_mix_g, v_w_in, v_conv_w, v_conv_b, v_conv_ln_g, v_conv_ln_b, v_w_out, v_norm_x_g, v_norm_mem_g, v_w_xq, v_w_xk, v_w_xv, v_w_xo, v_norm_mlp_g, v_w_up, v_w_down, v_norm_final_g):
    x2, mem2, tgt = x[0], mem[0], loss_target[0]
    s, d = x2.shape
    gf = norm_final_g[None, :]

    row_names = (w_out, w_xq, w_xk, w_xv, w_xo, w_down)
    row_sizes = [w.shape[1] for w in row_names]
    rows_local = jnp.concatenate([w[0].astype(BF16) for w in row_names], axis=0)
    cw_local = jnp.pad(conv_w[0], ((0, 1), (0, LANES - conv_w.shape[2])))
    rows_g, win_g, wup_g, cw_g = _all_gather(
        [rows_local, w_in[0].astype(BF16), w_up[0].astype(BF16), cw_local], "gather_weights")
    full_rows = []
    off = 0
    for sz in row_sizes:
        full_rows.append(rows_g[:, off:off + sz, :].reshape(N_DEV * sz, d))
        off += sz
    w_out_f, w_xq_f, w_xk_f, w_xv_f, w_xo_f, w_down_f = full_rows
    w_up_t = jnp.swapaxes(wup_g, 1, 2).reshape(-1, d)
    w_in_f = jnp.transpose(win_g, (1, 0, 2)).reshape(d, -1)
    cw_f = jnp.transpose(cw_g[:, :, :conv_w.shape[2]], (1, 0, 2)).reshape(32, D_CONV)

    rot = _rotary_tables(s)
    xn, qkv, ag = _fwd_in(x2, norm_mix_g, w_in_f, rot)
    qkv3 = [_to_strided(qkv, dil) for dil in DILATIONS]
    outs, lses = [], []
    for dil, q3 in zip(DILATIONS, qkv3):
        o3, l3 = _swa_fwd(q3, f"swa_fwd_d{dil}")
        outs.append(_from_strided(o3))
        lses.append(_from_strided(l3))
    c1, conv_out = _fwd_conv(ag, cw_f, conv_b, conv_ln_g, conv_ln_b)
    h1, cat, ltot = _fwd_mix_out(outs, lses, conv_out, x2, w_out_f)
    mn, xk, xv = _fwd_mem(mem2, norm_mem_g, w_xk_f, w_xv_f)
    h2, hn2, xq, xo = _fwd_xattn(h1, norm_x_g, w_xq_f, xk, xv, w_xo_f)
    hn3, act, dh3, dh3b, loss_part, g_final = _fwd_mlp_loss(h2, norm_mlp_g, wup_g, w_down_f, gf, tgt)

    du, dh2, dh2b, g_mlp = _bwd_mlp(dh3, dh3b, act, w_up_t, w_down_f, h2, norm_mlp_g)
    dh1, dh1b, dxq, dxk, dxv, g_x = _bwd_xattn(dh2, dh2b, h1, norm_x_g, xq, xk, xv, w_xq_f, w_xo_f)
    gw_xk, gw_xv, g_mem = _bwd_mem(mem2, norm_mem_g, mn, dxk, dxv, w_xk_f, w_xv_f)
    head = jnp.arange(D_ATT) // HEAD_DIM
    head_ones = (head[:, None] == head[None, :]).astype(BF16)
    dcat, dsum = _bwd_mix_out(dh1b, w_out_f, cat, head_ones)
    dag, g_cw, g_cb, g_lg, g_lb = _bwd_conv(dcat, c1, ag, cw_f, conv_ln_g, conv_ln_b)
    dqs, dkvs = [], []
    for dil, q3 in zip(DILATIONS, qkv3):
        dq3, part = _swa_bwd(q3, _to_strided(dcat, dil), _to_strided(ltot, dil), _to_strided(dsum, dil),
                             f"swa_bwd_d{dil}")
        dqs.append(_from_strided(dq3))
        dkvs.append(_from_strided(_swa_combine(part, f"swa_combine_d{dil}")))
    grad_x, dy, g_mix = _bwd_in(dqs, dkvs, dag, w_in_f, x2, norm_mix_g, dh1, rot)

    f_blk = w_up.shape[2]
    gw_in = _wgrad(xn, dy, "wgrad_in", b_blk=dy.shape[1] // 2)
    gw_out = _wgrad(cat, dh1b, "wgrad_out")
    gw_xq = _wgrad(hn2, dxq, "wgrad_xq")
    gw_xo = _wgrad(xo, dh2b, "wgrad_xo")
    gw_up = _wgrad(hn3, du, "wgrad_up", b_blk=4 * f_blk, stack=f_blk)
    gw_down = _wgrad(act, dh3b, "wgrad_down", a_blk=f_blk)

    n_in = w_in.shape[2]
    blocks = [jnp.transpose(gw_in.reshape(d, N_DEV, n_in), (1, 0, 2)), gw_up]
    blocks += [g.reshape(N_DEV, -1, d) for g in (gw_out, gw_xq, gw_xk, gw_xv, gw_xo, gw_down)]
    r_in, r_up, r_out, r_xq, r_xk, r_xv, r_xo, r_down = _all_to_all(blocks, "scatter_grads")

    def widen(t):
        return jnp.pad(t, ((0, 0), (0, d - t.shape[1])))

    small = jnp.concatenate([g_mix, g_x, g_mem, g_mlp, g_final, widen(g_cb), widen(g_lg), widen(g_lb),
                             g_cw.reshape(16, d)], axis=0)
    (small_g,) = _all_gather([small], "gather_small_grads")
    small_sum = _sum_slots(small_g, "sum_small_grads")
    loss = lax.psum(loss_part[0, 0], ("x", "y", "c"))

    res = {}

    def step(name, w, gsrc, m, v):
        shape = w.shape
        w2, m2, v2 = (t.reshape(-1, shape[-1]) for t in (w, m, v))
        res[name] = [t.reshape(shape) for t in _adamw(w2, gsrc, m2, v2, "adamw_" + name)]

    step("w_in", w_in, r_in, m_w_in, v_w_in)
    step("w_up", w_up, r_up, m_w_up, v_w_up)
    step("w_out", w_out, r_out, m_w_out, v_w_out)
    step("w_xq", w_xq, r_xq, m_w_xq, v_w_xq)
    step("w_xk", w_xk, r_xk, m_w_xk, v_w_xk)
    step("w_xv", w_xv, r_xv, m_w_xv, v_w_xv)
    step("w_xo", w_xo, r_xo, m_w_xo, v_w_xo)
    step("w_down", w_down, r_down, m_w_down, v_w_down)

    me = _dev_index((lax.axis_index("x"), lax.axis_index("y"), lax.axis_index("c")))
    n_cw = conv_w.shape[2]
    g_cw_full = small_sum[8:24].reshape(32, D_CONV)[:CONV_WIDTH]
    g_cw_mine = lax.dynamic_slice_in_dim(g_cw_full, me * n_cw, n_cw, axis=1)
    step("conv_w", conv_w, g_cw_mine, m_conv_w, v_conv_w)

    vec_names = ["norm_mix_g", "norm_x_g", "norm_mem_g", "norm_mlp_g", "norm_final_g", "conv_b", "conv_ln_g", "conv_ln_b"]
    vec_w = [norm_mix_g, norm_x_g, norm_mem_g, norm_mlp_g, gf, conv_b, conv_ln_g, conv_ln_b]
    vec_m = [m_norm_mix_g, m_norm_x_g, m_norm_mem_g, m_norm_mlp_g, m_norm_final_g[None, :], m_conv_b, m_conv_ln_g, m_conv_ln_b]
    vec_v = [v_norm_mix_g, v_norm_x_g, v_norm_mem_g, v_norm_mlp_g, v_norm_final_g[None, :], v_conv_b, v_conv_ln_g, v_conv_ln_b]

    def pack(ts):
        return jnp.concatenate([widen(t) for t in ts], axis=0)

    packed = _adamw(pack(vec_w), small_sum[0:8], pack(vec_m), pack(vec_v), "adamw_vectors")
    for i, name in enumerate(vec_names):
        width = vec_w[i].shape[1]
        shape = (width,) if name == "norm_final_g" else (1, width)
        res[name] = [t[i, :width].reshape(shape) for t in packed]

    order = ["norm_mix_g", "w_in", "conv_w", "conv_b", "conv_ln_g", "conv_ln_b", "w_out", "norm_x_g", "norm_mem_g",
             "w_xq", "w_xk", "w_xv", "w_xo", "norm_mlp_g", "w_up", "w_down", "norm_final_g"]
    out = [loss, grad_x[None]]
    for kind in range(4):
        out += [res[name][kind] for name in order]
    return tuple(out)
```

```python
import jax
import jax.numpy as jnp
from jax import lax
from jax.experimental import pallas as pl
from jax.experimental.pallas import tpu as pltpu

F32 = jnp.float32
BF16 = jnp.bfloat16

N_DEV = 8
EPS = 1e-6
NEG_INF = -1e30
ATT_HEADS = 8
HEAD_DIM = 64
D_ATT = ATT_HEADS * HEAD_DIM
D_CONV = 512
DILATIONS = (1, 4, 16)
HALF = 64
ROPE_THETA = 500000.0
ROT_DIM = HEAD_DIM // 4
CONV_WIDTH = 31
CONV_PAD = (CONV_WIDTH - 1) // 2
XATT_HEADS = 4
ADAM_LR = 0.001
ADAM_B1 = 0.9
ADAM_B2 = 0.999
ADAM_EPS = 1e-08
ADAM_WD = 0.01
ADAM_STEP = 10

LANES = 128
SUBLANES = 8
BF16_ROWS = 16
BQ = 128
WIN = BQ + 2 * HALF
MLP_SHARDS = 4
VMEM_LIMIT = 56 * 1024 * 1024
MESH = pl.DeviceIdType.MESH
ANY = pl.BlockSpec(memory_space=pl.ANY)

_NT = (((1,), (1,)), ((), ()))
_TN = (((0,), (0,)), ((), ()))


def _dot(a, b):
    return jnp.dot(a, b, preferred_element_type=F32)


def _dot_nt(a, b):
    return lax.dot_general(a, b, _NT, preferred_element_type=F32)


def _dot_tn(a, b):
    return lax.dot_general(a, b, _TN, preferred_element_type=F32)


def _params(*sem):
    return pltpu.CompilerParams(dimension_semantics=sem or None, vmem_limit_bytes=VMEM_LIMIT)


def _sigmoid(v):
    return 1.0 / (1.0 + jnp.exp(-v))


def _mean(v):
    return jnp.mean(v, axis=-1, keepdims=True)


def _rms_fwd(h, g):
    r = lax.rsqrt(_mean(h * h) + EPS)
    return h * r * g, r


def _rms_bwd(h, g, d_out):
    r = lax.rsqrt(_mean(h * h) + EPS)
    hn = h * r
    gd = d_out * g
    return r * (gd - hn * _mean(gd * hn)), d_out * hn


def _row_tile(s):
    return min(512, s)


def _full(shape):
    return pl.BlockSpec(shape, lambda *_: (0,) * len(shape))


def _mesh_pos():
    return lax.axis_index("x"), lax.axis_index("y"), lax.axis_index("c")


def _dev_index(p):
    return 4 * p[0] + 2 * p[1] + p[2]


class _Exchange:
    def __init__(self, arrays, gather):
        self.arrays, self.gather, self.n = list(arrays), gather, len(arrays)

    def out_shapes(self):
        return [jax.ShapeDtypeStruct(((N_DEV,) + a.shape) if self.gather else a.shape, a.dtype)
                for a in self.arrays]

    def sem_shapes(self):
        return [pltpu.SemaphoreType.DMA((7 * self.n,)), pltpu.SemaphoreType.DMA((7 * self.n,)),
                pltpu.SemaphoreType.DMA((self.n,))]

    def phases(self, x_refs, o_refs, send_sems, recv_sems, local_sems):
        n = self.n
        x, y, c = _mesh_pos()
        me, sibling = (x, y, c), (x, y, 1 - c)

        if self.gather:
            chips = [(1 - x, y), (x, 1 - y), (1 - x, 1 - y)]

            def copy(a, k, block, to, src=None):
                slot = o_refs[a].at[_dev_index(block)]
                return pltpu.make_async_remote_copy(
                    src_ref=slot if src is None else src, dst_ref=slot,
                    send_sem=send_sems.at[7 * a + k], recv_sem=recv_sems.at[7 * a + k],
                    device_id=to, device_id_type=MESH)

            def mine(a):
                return pltpu.make_async_copy(x_refs[a], o_refs[a].at[_dev_index(me)], local_sems.at[a])

            def first(a):
                return [copy(a, 0, me, sibling, src=x_refs[a])] + [
                    copy(a, 1 + j, me, (*chip, c), src=x_refs[a]) for j, chip in enumerate(chips)]

            def relayed(a, j):
                return copy(a, 4 + j, (*chips[j], c), sibling)

            def start():
                for a in range(n):
                    mine(a).start()
                    for cp in first(a):
                        cp.start()

            def relay():
                for j, chip in enumerate(chips):
                    for a in range(n):
                        copy(a, 1 + j, (*chip, c), me).wait_recv()
                        relayed(a, j).start()

            def finish():
                for a in range(n):
                    copy(a, 0, sibling, me).wait_recv()
                    for j, chip in enumerate(chips):
                        copy(a, 4 + j, (*chip, 1 - c), me).wait_recv()
                    for cp in first(a) + [relayed(a, j) for j in range(3)]:
                        cp.wait_send()
                    mine(a).wait()

            return start, relay, finish

        flips = [(dx, dy, dc) for dx in (0, 1) for dy in (0, 1) for dc in (0, 1)][1:]

        def peer(k):
            return tuple(1 - v if fl else v for v, fl in zip(me, flips[k]))

        def send(a, k):
            return pltpu.make_async_remote_copy(
                src_ref=x_refs[a].at[_dev_index(peer(k))], dst_ref=o_refs[a].at[_dev_index(me)],
                send_sem=send_sems.at[7 * a + k], recv_sem=recv_sems.at[7 * a + k],
                device_id=peer(k), device_id_type=MESH)

        def landed(a, k):
            slot = o_refs[a].at[_dev_index(peer(k))]
            return pltpu.make_async_remote_copy(
                src_ref=slot, dst_ref=slot, send_sem=send_sems.at[7 * a + k], recv_sem=recv_sems.at[7 * a + k],
                device_id=peer(k), device_id_type=MESH)

        def own(a):
            return pltpu.make_async_copy(x_refs[a].at[_dev_index(me)], o_refs[a].at[_dev_index(me)],
                                         local_sems.at[a])

        def start():
            for a in range(n):
                own(a).start()
            for k in range(7):
                for a in range(n):
                    send(a, k).start()

        def finish():
            for k in range(7):
                for a in range(n):
                    landed(a, k).wait_recv()
            for k in range(7):
                for a in range(n):
                    send(a, k).wait_send()
            for a in range(n):
                own(a).wait()

        return start, (lambda: None), finish


def _hosted_call(body, sides, *, name, grid, in_specs, out_specs, out_shape, scratch_shapes, args):
    n_in, n_out, ns = len(in_specs), len(out_specs), sum(s.n for s in sides)
    steps = 1
    for g in grid:
        steps *= g

    def wrapped(*refs):
        ins, s_ins = refs[:n_in], refs[n_in:n_in + ns]
        outs = refs[n_in + ns:n_in + ns + n_out]
        s_outs = refs[n_in + ns + n_out:n_in + 2 * ns + n_out]
        rest = refs[n_in + 2 * ns + n_out:]
        scratch, sems = rest[:len(rest) - 3 * len(sides)], rest[len(rest) - 3 * len(sides):]
        phases, off = [], 0
        for i, s in enumerate(sides):
            phases.append(s.phases(s_ins[off:off + s.n], s_outs[off:off + s.n], *sems[3 * i:3 * i + 3]))
            off += s.n
        lin = 0
        for ax, g in enumerate(grid):
            lin = lin * g + pl.program_id(ax)

        if sides:
            @pl.when(lin == 0)
            def _():
                for start, _, _ in phases:
                    start()

        body(*ins, *outs, *scratch)

        if sides:
            @pl.when(lin == min((3 * steps) // 4, steps - 1))
            def _():
                for _, relay, _ in phases:
                    relay()

            @pl.when(lin == steps - 1)
            def _():
                for _, _, finish in phases:
                    finish()

    res = pl.pallas_call(
        wrapped, name=name, grid=grid,
        in_specs=list(in_specs) + [ANY] * ns, out_specs=list(out_specs) + [ANY] * ns,
        out_shape=list(out_shape) + [sh for s in sides for sh in s.out_shapes()],
        scratch_shapes=list(scratch_shapes) + [sh for s in sides for sh in s.sem_shapes()],
        compiler_params=_params(*(("arbitrary",) * len(grid))),
    )(*args, *[a for s in sides for a in s.arrays])
    return res[:n_out], res[n_out:]


def _exchange_call(sides, name):
    ns = sum(s.n for s in sides)

    def body(*refs):
        x_refs, o_refs, sems = refs[:ns], refs[ns:2 * ns], refs[2 * ns:]
        phases, off = [], 0
        for i, s in enumerate(sides):
            phases.append(s.phases(x_refs[off:off + s.n], o_refs[off:off + s.n], *sems[3 * i:3 * i + 3]))
            off += s.n
        for step in range(3):
            for ph in phases:
                ph[step]()

    return pl.pallas_call(
        body, name=name,
        out_shape=[sh for s in sides for sh in s.out_shapes()],
        in_specs=[ANY] * ns, out_specs=[ANY] * ns,
        scratch_shapes=[sh for s in sides for sh in s.sem_shapes()],
    )(*[a for s in sides for a in s.arrays])


def _rotary_tables(s):
    half = ROT_DIM // 2
    freqs = ROPE_THETA ** (-jnp.arange(0, ROT_DIM, 2, dtype=F32) / ROT_DIM)
    ang = jnp.arange(s, dtype=F32)[:, None] * freqs[None, :]
    cos, sin = jnp.cos(ang), jnp.sin(ang)
    one = jnp.ones((s, HEAD_DIM - ROT_DIM), F32)
    zero = jnp.zeros((s, HEAD_DIM - ROT_DIM), F32)
    zh = jnp.zeros((s, half), F32)
    c64 = jnp.concatenate([cos, cos, one], axis=1)
    a64 = jnp.concatenate([-sin, zh, zero], axis=1)
    b64 = jnp.concatenate([zh, sin, zero], axis=1)
    return tuple(jnp.tile(t, (1, LANES // HEAD_DIM)) for t in (c64, a64, b64))


def _fwd_in(x, g, w_in, rot):
    s, d = x.shape
    n = w_in.shape[1]
    tm = _row_tile(s)

    def body(x_ref, g_ref, w_ref, c_ref, a_ref, b_ref, xn_ref, qkv_ref, ag_ref):
        xn = _rms_fwd(x_ref[...], g_ref[...])[0].astype(BF16)
        xn_ref[...] = xn
        y = _dot(xn, w_ref[...])
        reps = (1, D_ATT // LANES)
        cc, aa, bb = jnp.tile(c_ref[...], reps), jnp.tile(a_ref[...], reps), jnp.tile(b_ref[...], reps)
        for blk in range(2):
            t = y[:, blk * D_ATT:(blk + 1) * D_ATT]
            rot_t = t * cc + pltpu.roll(t, D_ATT - ROT_DIM // 2, 1) * aa + pltpu.roll(t, ROT_DIM // 2, 1) * bb
            qkv_ref[:, blk * D_ATT:(blk + 1) * D_ATT] = rot_t.astype(BF16)
        qkv_ref[:, 2 * D_ATT:] = y[:, 2 * D_ATT:3 * D_ATT].astype(BF16)
        ag_ref[...] = y[:, 3 * D_ATT:].astype(BF16)

    row = lambda w: pl.BlockSpec((tm, w), lambda i: (i, 0))
    return pl.pallas_call(
        body, name="fwd_in", grid=(s // tm,),
        in_specs=[row(d), _full((1, d)), _full((d, n)), row(LANES), row(LANES), row(LANES)],
        out_specs=[row(d), row(3 * D_ATT), row(2 * D_CONV)],
        out_shape=[jax.ShapeDtypeStruct((s, d), BF16), jax.ShapeDtypeStruct((s, 3 * D_ATT), BF16),
                   jax.ShapeDtypeStruct((s, 2 * D_CONV), BF16)],
        compiler_params=_params("parallel"),
    )(x, g, w_in, *rot)


def _win_in_specs(length, col, width):
    per = BQ // HALF
    last = length // HALF - 1
    return [
        pl.BlockSpec((None, HALF, width), lambda r, j: (r, jnp.maximum(j * per - 1, 0), col)),
        pl.BlockSpec((None, BQ, width), lambda r, j: (r, j, col)),
        pl.BlockSpec((None, HALF, width), lambda r, j: (r, jnp.minimum(j * per + per, last), col)),
    ]


def _fill_window(win, prev_ref, main_ref, next_ref):
    win[0:HALF] = prev_ref[...]
    win[HALF:HALF + BQ] = main_ref[...]
    win[HALF + BQ:] = next_ref[...]


def _band_mask(j, length):
    shape = (2 * BQ, WIN)
    blk = lax.broadcasted_iota(jnp.int32, shape, 0) & (BQ - 1)
    win = lax.broadcasted_iota(jnp.int32, shape, 1)
    pos = j * BQ - HALF + win
    return (jnp.abs(win - HALF - blk) <= HALF) & (pos >= 0) & (pos < length)


def _first_head():
    return lax.broadcasted_iota(jnp.int32, (1, LANES), 1) < HEAD_DIM


def _stack_heads(v, first):
    zero = jnp.zeros((), v.dtype)
    return jnp.concatenate([jnp.where(first, v, zero), jnp.where(first, zero, v)], axis=0)


def _unstack_heads(v, first):
    rows = v.shape[0] // 2
    return jnp.where(first, v[:rows], v[rows:])


def _stack_cols(v, first):
    top = jnp.max(jnp.where(first, v, -jnp.inf), axis=-1, keepdims=True)
    bot = jnp.max(jnp.where(first, -jnp.inf, v), axis=-1, keepdims=True)
    return jnp.concatenate([top, bot], axis=0)


def _swa_fwd(qkv3, name, sides=()):
    dil, length, _ = qkv3.shape
    scale = HEAD_DIM ** -0.5

    def body(q_ref, kp, km, kn, vp, vm, vn, o_ref, lse_ref, kwin, vwin):
        j = pl.program_id(1)
        _fill_window(kwin, kp, km, kn)
        _fill_window(vwin, vp, vm, vn)
        valid = _band_mask(j, length)
        first = _first_head()
        for pr in range(D_ATT // LANES):
            cols = slice(pr * LANES, (pr + 1) * LANES)
            qs = _stack_heads(q_ref[:, cols], first)
            sc = jnp.where(valid, _dot_nt(qs, kwin[:, cols]) * scale, NEG_INF)
            m = jnp.max(sc, axis=-1, keepdims=True)
            p = jnp.exp(sc - m)
            den = jnp.sum(p, axis=-1, keepdims=True)
            pv = _dot(p.astype(BF16), vwin[:, cols]) * (1.0 / den)
            o_ref[:, cols] = _unstack_heads(pv, first).astype(BF16)
            lse_ref[:, cols] = _unstack_heads(jnp.broadcast_to(m + jnp.log(den), (2 * BQ, LANES)), first)

    blk = lambda w: pl.BlockSpec((None, BQ, w), lambda r, j: (r, j, 0))
    return _hosted_call(
        body, sides, name=name, grid=(dil, length // BQ),
        in_specs=[pl.BlockSpec((None, BQ, D_ATT), lambda r, j: (r, j, 0))]
        + _win_in_specs(length, 1, D_ATT) + _win_in_specs(length, 2, D_ATT),
        out_specs=[blk(D_ATT), blk(D_ATT)],
        out_shape=[jax.ShapeDtypeStruct((dil, length, D_ATT), BF16),
                   jax.ShapeDtypeStruct((dil, length, D_ATT), F32)],
        scratch_shapes=[pltpu.VMEM((WIN, D_ATT), BF16), pltpu.VMEM((WIN, D_ATT), BF16)],
        args=[qkv3] * 7)


def _glu(v):
    return v[:, :D_CONV].astype(F32) * _sigmoid(v[:, D_CONV:].astype(F32))


def _halo_specs(tm, width, col=0):
    per = tm // BF16_ROWS
    return lambda nblk: [
        pl.BlockSpec((BF16_ROWS, width), lambda i: (jnp.maximum(i * per - 1, 0), col)),
        pl.BlockSpec((tm, width), lambda i: (i, col)),
        pl.BlockSpec((BF16_ROWS, width), lambda i: (jnp.minimum(i * per + per, nblk - 1), col)),
    ]


def _fill_halo(buf, i, ntiles, tm, prev, main, nxt):
    buf[0:BF16_ROWS] = jnp.where(i == 0, 0.0, prev)
    buf[BF16_ROWS:BF16_ROWS + tm] = main
    buf[BF16_ROWS + tm:] = jnp.where(i == ntiles - 1, 0.0, nxt)


def _halo_scratch(tm):
    return [pltpu.VMEM((tm + 2 * BF16_ROWS, D_CONV), F32),
            pltpu.VMEM((SUBLANES - 1, tm + 2 * BF16_ROWS - SUBLANES, D_CONV), F32)]


def _shift_copies(buf, shifted, tm):
    rows = tm + 2 * BF16_ROWS - SUBLANES
    for b in range(1, SUBLANES):
        shifted[b - 1] = buf[pl.ds(b, rows), :]


def _tap(buf, shifted, off, tm):
    a, b = divmod(off, SUBLANES)
    if b == 0:
        return buf[pl.ds(SUBLANES * a, tm), :]
    return shifted[b - 1, pl.ds(SUBLANES * a, tm), :]


def _fwd_conv(ag, cw, cb, lg, lb):
    s = ag.shape[0]
    tm = _row_tile(s)
    nt = s // tm

    def body(agp, agm, agn, cw_ref, cb_ref, lg_ref, lb_ref, c1_ref, co_ref, ubuf, ush):
        i = pl.program_id(0)
        _fill_halo(ubuf, i, nt, tm, _glu(agp[...]), _glu(agm[...]), _glu(agn[...]))
        _shift_copies(ubuf, ush, tm)
        w = cw_ref[...]
        acc = jnp.zeros((tm, D_CONV), F32)
        for k in range(CONV_WIDTH):
            acc = acc + _tap(ubuf, ush, k + 1, tm) * w[k:k + 1, :]
        acc = acc + cb_ref[...]
        c1_ref[...] = acc
        xc = acc - _mean(acc)
        ln = xc * lax.rsqrt(_mean(xc * xc) + EPS) * lg_ref[...] + lb_ref[...]
        co_ref[...] = (ln * _sigmoid(ln)).astype(BF16)

    vec = _full((1, D_CONV))
    return pl.pallas_call(
        body, name="fwd_conv", grid=(nt,),
        in_specs=_halo_specs(tm, 2 * D_CONV)(s // BF16_ROWS) + [_full((32, D_CONV)), vec, vec, vec],
        out_specs=[pl.BlockSpec((tm, D_CONV), lambda i: (i, 0))] * 2,
        out_shape=[jax.ShapeDtypeStruct((s, D_CONV), F32), jax.ShapeDtypeStruct((s, D_CONV), BF16)],
        scratch_shapes=_halo_scratch(tm),
        compiler_params=_params("parallel"),
    )(ag, ag, ag, cw, cb, lg, lb)


def _fwd_mix_out(outs, lses, conv_out, x, w_out):
    s, d = x.shape
    tm = _row_tile(s)

    def body(o1, o2, o3, l1, l2, l3, co, x_ref, w_ref, h_ref, cat_ref, lt_ref):
        a, b, c = l1[...], l2[...], l3[...]
        m = jnp.maximum(jnp.maximum(a, b), c)
        ea, eb, ec = jnp.exp(a - m), jnp.exp(b - m), jnp.exp(c - m)
        den = ea + eb + ec
        att = (ea * o1[...].astype(F32) + eb * o2[...].astype(F32) + ec * o3[...].astype(F32)) / den
        lt_ref[...] = m + jnp.log(den)
        att = att.astype(BF16)
        cat_ref[:, :D_ATT] = att
        cat_ref[:, D_ATT:] = co[...]
        h_ref[...] = x_ref[...] + _dot(att, w_ref[:D_ATT, :]) + _dot(co[...], w_ref[D_ATT:, :])

    row = lambda w: pl.BlockSpec((tm, w), lambda i: (i, 0))
    return pl.pallas_call(
        body, name="fwd_mix_out", grid=(s // tm,),
        in_specs=[row(D_ATT)] * 7 + [row(d), _full((d, d))],
        out_specs=[row(d), row(d), row(D_ATT)],
        out_shape=[jax.ShapeDtypeStruct((s, d), F32), jax.ShapeDtypeStruct((s, d), BF16),
                   jax.ShapeDtypeStruct((s, D_ATT), F32)],
        compiler_params=_params("parallel"),
    )(*outs, *lses, conv_out, x, w_out)


def _fwd_mem(mem, g, wk, wv):
    m, d = mem.shape

    def body(mem_ref, g_ref, wk_ref, wv_ref, mn_ref, xk_ref, xv_ref):
        mn = _rms_fwd(mem_ref[...], g_ref[...])[0].astype(BF16)
        mn_ref[...] = mn
        xk_ref[...] = _dot(mn, wk_ref[...]).astype(BF16)
        xv_ref[...] = _dot(mn, wv_ref[...]).astype(BF16)

    return pl.pallas_call(
        body, name="fwd_mem",
        out_shape=[jax.ShapeDtypeStruct((m, d), BF16)] * 3,
        compiler_params=_params(),
    )(mem, g, wk, wv)


def _xatt_probs(q, k):
    sc = _dot_nt(q, k) * (q.shape[1] ** -0.5)
    p = jnp.exp(sc - jnp.max(sc, axis=-1, keepdims=True))
    return p / jnp.sum(p, axis=-1, keepdims=True)


def _fwd_xattn(h1, g, wq, xk, xv, wo):
    s, d = h1.shape
    m = xk.shape[0]
    tm = _row_tile(s)
    hd = d // XATT_HEADS

    def body(h_ref, g_ref, wq_ref, xk_ref, xv_ref, wo_ref, h2_ref, hn_ref, xq_ref, xo_ref):
        h = h_ref[...]
        hn = _rms_fwd(h, g_ref[...])[0].astype(BF16)
        hn_ref[...] = hn
        xq = _dot(hn, wq_ref[...]).astype(BF16)
        xq_ref[...] = xq
        for i in range(XATT_HEADS):
            cols = slice(i * hd, (i + 1) * hd)
            pr = _xatt_probs(xq[:, cols], xk_ref[:, cols])
            xo_ref[:, cols] = _dot(pr.astype(BF16), xv_ref[:, cols]).astype(BF16)
        h2_ref[...] = h + _dot(xo_ref[...], wo_ref[...])

    row = pl.BlockSpec((tm, d), lambda i: (i, 0))
    return pl.pallas_call(
        body, name="fwd_xattn", grid=(s // tm,),
        in_specs=[row, _full((1, d)), _full((d, d)), _full((m, d)), _full((m, d)), _full((d, d))],
        out_specs=[row] * 4,
        out_shape=[jax.ShapeDtypeStruct((s, d), F32)] + [jax.ShapeDtypeStruct((s, d), BF16)] * 3,
        compiler_params=_params("parallel"),
    )(h1, g, wq, xk, xv, wo)


def _fwd_mlp_loss(h2, g, w_up, w_down, gf, target):
    s, d = h2.shape
    nsh, _, f = w_up.shape
    fb = MLP_SHARDS * f
    nb = nsh // MLP_SHARDS
    tm = _row_tile(s)

    def body(h_ref, g_ref, wu_ref, wd_ref, gf_ref, t_ref,
             hn_ref, act_ref, dh_ref, dhb_ref, loss_ref, ggf_ref, acc):
        i, k = pl.program_id(0), pl.program_id(1)

        @pl.when(k == 0)
        def _():
            hn_ref[...] = _rms_fwd(h_ref[...], g_ref[...])[0].astype(BF16)
            acc[...] = jnp.zeros_like(acc)

        @pl.when((i == 0) & (k == 0))
        def _():
            loss_ref[...] = jnp.zeros_like(loss_ref)
            ggf_ref[...] = jnp.zeros_like(ggf_ref)

        hn = hn_ref[...]
        for c in range(MLP_SHARDS):
            act_ref[:, c * f:(c + 1) * f] = jnp.square(jnp.maximum(_dot(hn, wu_ref[c]), 0.0)).astype(BF16)
        acc[...] += _dot(act_ref[...], wd_ref[...])

        @pl.when(k == nb - 1)
        def _():
            h3 = h_ref[...] + acc[...]
            gfv = gf_ref[...]
            y, _ = _rms_fwd(h3, gfv)
            err = y - t_ref[...]
            loss_ref[...] += 0.5 * jnp.sum(_mean(err * err))
            dh3, gg = _rms_bwd(h3, gfv, err * (1.0 / d))
            ggf_ref[...] += jnp.sum(gg, axis=0, keepdims=True)
            dh_ref[...] = dh3
            dhb_ref[...] = dh3.astype(BF16)

    row = pl.BlockSpec((tm, d), lambda i, k: (i, 0))
    return pl.pallas_call(
        body, name="fwd_mlp_loss", grid=(s // tm, nb),
        in_specs=[row, _full((1, d)),
                  pl.BlockSpec((MLP_SHARDS, d, f), lambda i, k: (k, 0, 0)),
                  pl.BlockSpec((fb, d), lambda i, k: (k, 0)),
                  _full((1, d)), row],
        out_specs=[row, pl.BlockSpec((tm, fb), lambda i, k: (i, k)), row, row,
                   _full((1, LANES)), _full((1, d))],
        out_shape=[jax.ShapeDtypeStruct((s, d), BF16), jax.ShapeDtypeStruct((s, nsh * f), BF16),
                   jax.ShapeDtypeStruct((s, d), F32), jax.ShapeDtypeStruct((s, d), BF16),
                   jax.ShapeDtypeStruct((1, LANES), F32), jax.ShapeDtypeStruct((1, d), F32)],
        scratch_shapes=[pltpu.VMEM((tm, d), F32)],
        compiler_params=_params("arbitrary", "arbitrary"),
    )(h2, g, w_up, w_down, gf, target)


def _bwd_mlp(dh3, dh3b, act, w_up_t, w_down, h2, g):
    s, d = h2.shape
    ff = w_down.shape[0]
    fb = MLP_SHARDS * (ff // N_DEV)
    nb = ff // fb
    tm = _row_tile(s)

    def body(dh_ref, dhb_ref, act_ref, wut_ref, wd_ref, h_ref, g_ref,
             du_ref, dh2_ref, dh2b_ref, gg_ref, acc):
        i, k = pl.program_id(0), pl.program_id(1)

        @pl.when(k == 0)
        def _():
            acc[...] = jnp.zeros_like(acc)

        @pl.when((i == 0) & (k == 0))
        def _():
            gg_ref[...] = jnp.zeros_like(gg_ref)

        dact = _dot_nt(dhb_ref[...], wd_ref[...])
        du_ref[...] = (dact * (2.0 * jnp.sqrt(act_ref[...].astype(F32)))).astype(BF16)
        acc[...] += _dot(du_ref[...], wut_ref[...])

        @pl.when(k == nb - 1)
        def _():
            dh, gg = _rms_bwd(h_ref[...], g_ref[...], acc[...])
            gg_ref[...] += jnp.sum(gg, axis=0, keepdims=True)
            dh2 = dh_ref[...] + dh
            dh2_ref[...] = dh2
            dh2b_ref[...] = dh2.astype(BF16)

    row = pl.BlockSpec((tm, d), lambda i, k: (i, 0))
    col = pl.BlockSpec((tm, fb), lambda i, k: (i, k))
    wblk = pl.BlockSpec((fb, d), lambda i, k: (k, 0))
    return pl.pallas_call(
        body, name="bwd_mlp", grid=(s // tm, nb),
        in_specs=[row, row, col, wblk, wblk, row, _full((1, d))],
        out_specs=[col, row, row, _full((1, d))],
        out_shape=[jax.ShapeDtypeStruct((s, ff), BF16), jax.ShapeDtypeStruct((s, d), F32),
                   jax.ShapeDtypeStruct((s, d), BF16), jax.ShapeDtypeStruct((1, d), F32)],
        scratch_shapes=[pltpu.VMEM((tm, d), F32)],
        compiler_params=_params("arbitrary", "arbitrary"),
    )(dh3, dh3b, act, w_up_t, w_down, h2, g)


def _bwd_xattn(dh2, dh2b, h1, g, xq, xk, xv, wq, wo, sides=()):
    s, d = h1.shape
    m = xk.shape[0]
    tm = _row_tile(s)
    hd = d // XATT_HEADS
    scale = hd ** -0.5

    def body(dh_ref, dhb_ref, h_ref, g_ref, xq_ref, xk_ref, xv_ref, wq_ref, wo_ref,
             dh1_ref, dh1b_ref, dxq_ref, dxk_ref, dxv_ref, gg_ref):
        @pl.when(pl.program_id(0) == 0)
        def _():
            dxk_ref[...] = jnp.zeros_like(dxk_ref)
            dxv_ref[...] = jnp.zeros_like(dxv_ref)
            gg_ref[...] = jnp.zeros_like(gg_ref)

        dxo = _dot_nt(dhb_ref[...], wo_ref[...])
        for i in range(XATT_HEADS):
            cols = slice(i * hd, (i + 1) * hd)
            q, k, v = xq_ref[:, cols], xk_ref[:, cols], xv_ref[:, cols]
            pr = _xatt_probs(q, k)
            dxo_h = dxo[:, cols].astype(BF16)
            dpr = _dot_nt(dxo_h, v)
            dsc = (pr * (dpr - jnp.sum(dpr * pr, axis=-1, keepdims=True)) * scale).astype(BF16)
            dxq_ref[:, cols] = _dot(dsc, k).astype(BF16)
            dxk_ref[:, cols] += _dot_tn(dsc, q)
            dxv_ref[:, cols] += _dot_tn(pr.astype(BF16), dxo_h)
        dh, gg = _rms_bwd(h_ref[...], g_ref[...], _dot_nt(dxq_ref[...], wq_ref[...]))
        gg_ref[...] += jnp.sum(gg, axis=0, keepdims=True)
        dh1 = dh_ref[...] + dh
        dh1_ref[...] = dh1
        dh1b_ref[...] = dh1.astype(BF16)

    row = pl.BlockSpec((tm, d), lambda i: (i, 0))
    return _hosted_call(
        body, sides, name="bwd_xattn", grid=(s // tm,),
        in_specs=[row, row, row, _full((1, d)), row, _full((m, d)), _full((m, d)), _full((d, d)), _full((d, d))],
        out_specs=[row, row, row, _full((m, d)), _full((m, d)), _full((1, d))],
        out_shape=[jax.ShapeDtypeStruct((s, d), F32), jax.ShapeDtypeStruct((s, d), BF16),
                   jax.ShapeDtypeStruct((s, d), BF16), jax.ShapeDtypeStruct((m, d), F32),
                   jax.ShapeDtypeStruct((m, d), F32), jax.ShapeDtypeStruct((1, d), F32)],
        scratch_shapes=[],
        args=[dh2, dh2b, h1, g, xq, xk, xv, wq, wo])


def _bwd_mem(mem, g, mn, dxk, dxv, wk, wv):
    m, d = mem.shape

    def body(mem_ref, g_ref, mn_ref, dxk_ref, dxv_ref, wk_ref, wv_ref, gk_ref, gv_ref, gg_ref):
        dk, dv = dxk_ref[...].astype(BF16), dxv_ref[...].astype(BF16)
        gk_ref[...] = _dot_tn(mn_ref[...], dk).astype(BF16)
        gv_ref[...] = _dot_tn(mn_ref[...], dv).astype(BF16)
        dmn = _dot_nt(dk, wk_ref[...]) + _dot_nt(dv, wv_ref[...])
        _, gg = _rms_bwd(mem_ref[...], g_ref[...], dmn)
        gg_ref[...] = jnp.sum(gg, axis=0, keepdims=True)

    return pl.pallas_call(
        body, name="bwd_mem",
        out_shape=[jax.ShapeDtypeStruct((d, d), BF16), jax.ShapeDtypeStruct((d, d), BF16),
                   jax.ShapeDtypeStruct((1, d), F32)],
        compiler_params=_params(),
    )(mem, g, mn, dxk, dxv, wk, wv)


def _bwd_mix_out(dh1b, w_out, cat, head_ones):
    s, d = dh1b.shape
    tm = _row_tile(s)

    def body(dh_ref, w_ref, cat_ref, ones_ref, dcat_ref, dsum_ref):
        dcat = _dot_nt(dh_ref[...], w_ref[...])
        dcat_ref[...] = dcat.astype(BF16)
        prod = dcat[:, :D_ATT] * cat_ref[...].astype(F32)
        hi = prod.astype(BF16)
        lo = (prod - hi.astype(F32)).astype(BF16)
        dsum_ref[...] = _dot(hi, ones_ref[...]) + _dot(lo, ones_ref[...])

    row = lambda w: pl.BlockSpec((tm, w), lambda i: (i, 0))
    return pl.pallas_call(
        body, name="bwd_mix_out", grid=(s // tm,),
        in_specs=[row(d), _full((d, d)), row(D_ATT), _full((D_ATT, D_ATT))],
        out_specs=[row(d), row(D_ATT)],
        out_shape=[jax.ShapeDtypeStruct((s, d), BF16), jax.ShapeDtypeStruct((s, D_ATT), F32)],
        compiler_params=_params("parallel"),
    )(dh1b, w_out, cat, head_ones)


def _bwd_conv(dcat, c1, ag, cw, lg, lb, sides=()):
    s = ag.shape[0]
    tm = _row_tile(s)
    nt = s // tm

    def body(dp, dm, dn, cp, cm, cn, agp, agm, agn, cw_ref, lg_ref, lb_ref,
             dag_ref, gcw_ref, gcb_ref, glg_ref, glb_ref, ubuf, ush, dbuf, dsh):
        i = pl.program_id(0)

        @pl.when(i == 0)
        def _():
            gcw_ref[...] = jnp.zeros_like(gcw_ref)
            gcb_ref[...] = jnp.zeros_like(gcb_ref)
            glg_ref[...] = jnp.zeros_like(glg_ref)
            glb_ref[...] = jnp.zeros_like(glb_ref)

        lgv, lbv = lg_ref[...], lb_ref[...]

        def norm_bwd(dco, c1v):
            xc = c1v - _mean(c1v)
            rs = lax.rsqrt(_mean(xc * xc) + EPS)
            z = xc * rs
            ln = z * lgv + lbv
            sg = _sigmoid(ln)
            dln = dco.astype(F32) * (sg * (1.0 + ln * (1.0 - sg)))
            dz = dln * lgv
            return rs * (dz - _mean(dz) - z * _mean(dz * z)), dln, z

        dc_m, dln, z = norm_bwd(dm[...], cm[...])
        glg_ref[...] += jnp.sum(dln * z, axis=0, keepdims=True)
        glb_ref[...] += jnp.sum(dln, axis=0, keepdims=True)
        gcb_ref[...] += jnp.sum(dc_m, axis=0, keepdims=True)
        _fill_halo(dbuf, i, nt, tm, norm_bwd(dp[...], cp[...])[0], dc_m, norm_bwd(dn[...], cn[...])[0])
        _fill_halo(ubuf, i, nt, tm, _glu(agp[...]), _glu(agm[...]), _glu(agn[...]))
        _shift_copies(dbuf, dsh, tm)
        _shift_copies(ubuf, ush, tm)

        w = cw_ref[...]
        tap = lax.broadcasted_iota(jnp.int32, (32, D_CONV), 0)
        du = jnp.zeros((tm, D_CONV), F32)
        gcw = jnp.zeros((32, D_CONV), F32)
        for k in range(CONV_WIDTH):
            du = du + _tap(dbuf, dsh, CONV_WIDTH - k, tm) * w[k:k + 1, :]
            gk = jnp.sum(dc_m * _tap(ubuf, ush, k + 1, tm), axis=0, keepdims=True)
            gcw = jnp.where(tap == k, gk, gcw)
        gcw_ref[...] += gcw
        a = agm[:, :D_CONV].astype(F32)
        sg = _sigmoid(agm[:, D_CONV:].astype(F32))
        dag_ref[:, :D_CONV] = (du * sg).astype(BF16)
        dag_ref[:, D_CONV:] = (du * a * sg * (1.0 - sg)).astype(BF16)

    vec = _full((1, D_CONV))
    nblk = s // BF16_ROWS
    return _hosted_call(
        body, sides, name="bwd_conv", grid=(nt,),
        in_specs=_halo_specs(tm, D_CONV, 1)(nblk) + _halo_specs(tm, D_CONV)(nblk) + _halo_specs(tm, 2 * D_CONV)(nblk)
        + [_full((32, D_CONV)), vec, vec],
        out_specs=[pl.BlockSpec((tm, 2 * D_CONV), lambda i: (i, 0)), _full((32, D_CONV)), vec, vec, vec],
        out_shape=[jax.ShapeDtypeStruct((s, 2 * D_CONV), BF16), jax.ShapeDtypeStruct((32, D_CONV), F32)]
        + [jax.ShapeDtypeStruct((1, D_CONV), F32)] * 3,
        scratch_shapes=_halo_scratch(tm) + _halo_scratch(tm),
        args=[dcat, dcat, dcat, c1, c1, c1, ag, ag, ag, cw, lg, lb])


def _swa_bwd(qkv3, do3, lt3, ds3, name, sides=()):
    dil, length, _ = qkv3.shape
    nb = length // BQ
    scale = HEAD_DIM ** -0.5
    assert WIN == 2 * BQ

    def body(q_ref, kp, km, kn, vp, vm, vn, do_ref, l_ref, s_ref, dq_ref, dkv_ref, kwin, vwin, pend):
        j = pl.program_id(1)

        @pl.when(j == 0)
        def _():
            pend[...] = jnp.zeros_like(pend)

        @pl.when(j < nb)
        def _():
            _fill_window(kwin, kp, km, kn)
            _fill_window(vwin, vp, vm, vn)
            valid = _band_mask(j, length)
            first = _first_head()
            for pr in range(D_ATT // LANES):
                cols = slice(pr * LANES, (pr + 1) * LANES)
                qs = _stack_heads(q_ref[:, cols], first)
                dos = _stack_heads(do_ref[:, cols], first)
                kw, vw = kwin[:, cols], vwin[:, cols]
                sc = _dot_nt(qs, kw) * scale
                p = jnp.where(valid, jnp.exp(sc - _stack_cols(l_ref[:, cols], first)), 0.0)
                dp = _dot_nt(dos, vw)
                dsc = (p * (dp - _stack_cols(s_ref[:, cols], first)) * scale).astype(BF16)
                dq_ref[:, cols] = _unstack_heads(_dot(dsc, kw), first).astype(BF16)
                for part, at in ((_dot_tn(dsc, qs), pr * LANES), (_dot_tn(p.astype(BF16), dos), D_ATT + pr * LANES)):
                    dkv_ref[:, at:at + LANES] = (pend[:, at:at + LANES] + part[:BQ]).astype(BF16)
                    pend[:, at:at + LANES] = part[BQ:]

        @pl.when(j == nb)
        def _():
            dkv_ref[...] = pend[...].astype(BF16)

    def clamp(idx):
        return lambda r, j: idx(r, jnp.minimum(j, nb - 1))

    main = pl.BlockSpec((None, BQ, D_ATT), clamp(lambda r, j: (r, j, 0)))
    wins = [pl.BlockSpec(sp.block_shape, clamp(sp.index_map))
            for c in (1, 2) for sp in _win_in_specs(length, c, D_ATT)]
    return _hosted_call(
        body, sides, name=name, grid=(dil, nb + 1),
        in_specs=[main] + wins + [main] * 3,
        out_specs=[main, pl.BlockSpec((None, BQ, 2 * D_ATT), lambda r, j: (r, j, 0))],
        out_shape=[jax.ShapeDtypeStruct((dil, length, D_ATT), BF16),
                   jax.ShapeDtypeStruct((dil, length + BQ, 2 * D_ATT), BF16)],
        scratch_shapes=[pltpu.VMEM((WIN, D_ATT), BF16)] * 2 + [pltpu.VMEM((BQ, 2 * D_ATT), F32)],
        args=[qkv3] * 7 + [do3, lt3, ds3])


def _bwd_in(dqs, dkvs, dag, w_in, x, g, dh1, rot):
    s, d = x.shape
    n = w_in.shape[1]
    tm = _row_tile(s)

    def body(q1, q2, q3, kv1, kv2, kv3, dag_ref, w_ref, x_ref, g_ref, dh_ref, c_ref, a_ref, b_ref,
             gx_ref, dy_ref, gg_ref):
        @pl.when(pl.program_id(0) == 0)
        def _():
            gg_ref[...] = jnp.zeros_like(gg_ref)

        dq = q1[...].astype(F32) + q2[...].astype(F32) + q3[...].astype(F32)
        dkv = kv1[...].astype(F32) + kv2[...].astype(F32) + kv3[...].astype(F32)
        reps = (1, D_ATT // LANES)
        cc, aa, bb = jnp.tile(c_ref[...], reps), jnp.tile(a_ref[...], reps), jnp.tile(b_ref[...], reps)
        for blk, t in enumerate((dq, dkv[:, :D_ATT])):
            dt = t * cc + pltpu.roll(t * aa, ROT_DIM // 2, 1) + pltpu.roll(t * bb, D_ATT - ROT_DIM // 2, 1)
            dy_ref[:, blk * D_ATT:(blk + 1) * D_ATT] = dt.astype(BF16)
        dy_ref[:, 2 * D_ATT:3 * D_ATT] = dkv[:, D_ATT:].astype(BF16)
        dy_ref[:, 3 * D_ATT:] = dag_ref[...]
        dx, gg = _rms_bwd(x_ref[...], g_ref[...], _dot_nt(dy_ref[...], w_ref[...]))
        gg_ref[...] += jnp.sum(gg, axis=0, keepdims=True)
        gx_ref[...] = dh_ref[...] + dx

    row = lambda w: pl.BlockSpec((tm, w), lambda i: (i, 0))
    return pl.pallas_call(
        body, name="bwd_in", grid=(s // tm,),
        in_specs=[row(D_ATT)] * 3 + [row(2 * D_ATT)] * 3
        + [row(2 * D_CONV), _full((d, n)), row(d), _full((1, d)), row(d)] + [row(LANES)] * 3,
        out_specs=[row(d), row(n), _full((1, d))],
        out_shape=[jax.ShapeDtypeStruct((s, d), F32), jax.ShapeDtypeStruct((s, n), BF16),
                   jax.ShapeDtypeStruct((1, d), F32)],
        compiler_params=_params("arbitrary"),
    )(*dqs, *dkvs, dag, w_in, x, g, dh1, *rot)


def _wgrad(a, b, name, a_blk=None, b_blk=None, stack=None, tm=1024):
    s, ka = a.shape
    nb = b.shape[1]
    a_blk, b_blk = a_blk or ka, b_blk or nb
    na, nbl = ka // a_blk, nb // b_blk
    assert na == 1 or nbl == 1
    tm = min(tm, s)
    nt = s // tm
    per = b_blk // stack if stack else 0

    def body(a_ref, b_ref, o_ref, acc):
        t = pl.program_id(1)

        @pl.when(t == 0)
        def _():
            acc[...] = jnp.zeros_like(acc)

        acc[...] += _dot_tn(a_ref[...], b_ref[...])

        @pl.when(t == nt - 1)
        def _():
            if stack:
                for c in range(per):
                    o_ref[c] = acc[:, c * stack:(c + 1) * stack].astype(BF16)
            else:
                o_ref[...] = acc[...].astype(BF16)

    if stack:
        out_spec = pl.BlockSpec((per, ka, stack), lambda k, t: (k, 0, 0))
        out_shape = jax.ShapeDtypeStruct((nb // stack, ka, stack), BF16)
    elif na > 1:
        out_spec = pl.BlockSpec((a_blk, nb), lambda k, t: (k, 0))
        out_shape = jax.ShapeDtypeStruct((ka, nb), BF16)
    else:
        out_spec = pl.BlockSpec((ka, b_blk), lambda k, t: (0, k))
        out_shape = jax.ShapeDtypeStruct((ka, nb), BF16)
    return pl.pallas_call(
        body, name=name, grid=(na * nbl, nt),
        in_specs=[pl.BlockSpec((tm, a_blk), (lambda k, t: (t, k)) if na > 1 else (lambda k, t: (t, 0))),
                  pl.BlockSpec((tm, b_blk), (lambda k, t: (t, k)) if nbl > 1 else (lambda k, t: (t, 0)))],
        out_specs=out_spec, out_shape=out_shape,
        scratch_shapes=[pltpu.VMEM((a_blk, b_blk), F32)],
        compiler_params=_params("parallel", "arbitrary"),
    )(a, b)


def _adamw(w, gsrc, m, v, name):
    summed = gsrc.ndim == w.ndim + 1
    rows, cols = w.shape
    tr = rows if rows <= 256 else 256
    assert rows % tr == 0
    c1 = 1.0 - ADAM_B1 ** ADAM_STEP
    c2 = 1.0 - ADAM_B2 ** ADAM_STEP

    def body(w_ref, g_ref, m_ref, v_ref, go_ref, d_ref, mo_ref, vo_ref):
        if summed:
            g = g_ref[0].astype(F32)
            for i in range(1, N_DEV):
                g = g + g_ref[i].astype(F32)
        else:
            g = g_ref[...]
        mn = ADAM_B1 * m_ref[...] + (1.0 - ADAM_B1) * g
        vn = ADAM_B2 * v_ref[...] + (1.0 - ADAM_B2) * jnp.square(g)
        go_ref[...] = g
        mo_ref[...] = mn
        vo_ref[...] = vn
        d_ref[...] = -ADAM_LR * ((mn / c1) / (jnp.sqrt(vn / c2) + ADAM_EPS) + ADAM_WD * w_ref[...])

    blk = pl.BlockSpec((tr, cols), lambda i: (i, 0))
    gblk = pl.BlockSpec((N_DEV, tr, cols), lambda i: (0, i, 0)) if summed else blk
    return pl.pallas_call(
        body, name=name, grid=(rows // tr,),
        in_specs=[blk, gblk, blk, blk], out_specs=[blk] * 4,
        out_shape=[jax.ShapeDtypeStruct(w.shape, F32)] * 4,
        compiler_params=_params("parallel"),
    )(w, gsrc, m, v)


def _sum_slots(g, name):
    _, rows, cols = g.shape

    def body(g_ref, o_ref):
        acc = g_ref[0]
        for i in range(1, N_DEV):
            acc = acc + g_ref[i]
        o_ref[...] = acc

    return pl.pallas_call(body, name=name, out_shape=jax.ShapeDtypeStruct((rows, cols), F32),
                          compiler_params=_params())(g)


def _to_strided(t, dil):
    s, c = t.shape
    if dil == 1:
        return t[None]
    return jnp.transpose(t.reshape(s // dil, dil, c), (1, 0, 2))


def _from_strided(t):
    dil, length, c = t.shape
    if dil == 1:
        return t[0]
    return jnp.transpose(t, (1, 0, 2)).reshape(dil * length, c)


def kernel(x, mem, norm_mix_g, w_in, conv_w, conv_b, conv_ln_g, conv_ln_b, w_out, norm_x_g, norm_mem_g, w_xq, w_xk, w_xv, w_xo, norm_mlp_g, w_up, w_down, norm_final_g, loss_target, m_norm_mix_g, m_w_in, m_conv_w, m_conv_b, m_conv_ln_g, m_conv_ln_b, m_w_out, m_norm_x_g, m_norm_mem_g, m_w_xq, m_w_xk, m_w_xv, m_w_xo, m_norm_mlp_g, m_w_up, m_w_down, m_norm_final_g, v_norm_mix_g, v_w_in, v_conv_w, v_conv_b, v_conv_ln_g, v_conv_ln_b, v_w_out, v_norm_x_g, v_norm_mem_g, v_w_xq, v_w_xk, v_w_xv, v_w_xo, v_norm_mlp_g, v_w_up, v_w_down, v_norm_final_g):
    x2, mem2, tgt = x[0], mem[0], loss_target[0]
    s, d = x2.shape
    gf = norm_final_g[None, :]

    cw_local = jnp.pad(conv_w[0], ((0, 1), (0, LANES - conv_w.shape[2])))
    win_g, cw_g = _exchange_call([_Exchange([w_in[0].astype(BF16), cw_local], gather=True)], "gather_w_in")
    w_in_f = jnp.transpose(win_g, (1, 0, 2)).reshape(d, -1)
    cw_f = jnp.transpose(cw_g[:, :, :conv_w.shape[2]], (1, 0, 2)).reshape(32, D_CONV)
    row_names = (w_out, w_xq, w_xk, w_xv, w_xo)
    rows_local = jnp.concatenate([w[0].astype(BF16) for w in row_names], axis=0)
    late = [_Exchange([rows_local], gather=True), _Exchange([w_up[0].astype(BF16)], gather=True),
            _Exchange([w_down[0].astype(BF16)], gather=True)]

    rot = _rotary_tables(s)
    xn, qkv, ag = _fwd_in(x2, norm_mix_g, w_in_f, rot)
    qkv3 = [_to_strided(qkv, dil) for dil in DILATIONS]
    outs, lses, gathered = [], [], []
    for dil, q3, side in zip(DILATIONS, qkv3, late):
        (o3, l3), got = _swa_fwd(q3, f"swa_fwd_d{dil}", [side])
        outs.append(_from_strided(o3))
        lses.append(_from_strided(l3))
        gathered += got
    rows_g, wup_g, wdown_g = gathered
    full_rows = []
    off = 0
    for w in row_names:
        full_rows.append(rows_g[:, off:off + w.shape[1], :].reshape(N_DEV * w.shape[1], d))
        off += w.shape[1]
    w_out_f, w_xq_f, w_xk_f, w_xv_f, w_xo_f = full_rows
    w_down_f = wdown_g.reshape(-1, d)
    w_up_t = jnp.swapaxes(wup_g, 1, 2).reshape(-1, d)
    c1, conv_out = _fwd_conv(ag, cw_f, conv_b, conv_ln_g, conv_ln_b)
    h1, cat, ltot = _fwd_mix_out(outs, lses, conv_out, x2, w_out_f)
    mn, xk, xv = _fwd_mem(mem2, norm_mem_g, w_xk_f, w_xv_f)
    h2, hn2, xq, xo = _fwd_xattn(h1, norm_x_g, w_xq_f, xk, xv, w_xo_f)
    hn3, act, dh3, dh3b, loss_part, g_final = _fwd_mlp_loss(h2, norm_mlp_g, wup_g, w_down_f, gf, tgt)

    def scatter(*grads):
        return _Exchange([g.reshape(N_DEV, -1, g.shape[-1]) for g in grads], gather=False)

    f_blk = w_up.shape[2]
    du, dh2, dh2b, g_mlp = _bwd_mlp(dh3, dh3b, act, w_up_t, w_down_f, h2, norm_mlp_g)
    gw_up = _wgrad(hn3, du, "wgrad_up", b_blk=4 * f_blk, stack=f_blk)
    gw_down = _wgrad(act, dh3b, "wgrad_down", a_blk=f_blk)
    (dh1, dh1b, dxq, dxk, dxv, g_x), (r_up,) = _bwd_xattn(
        dh2, dh2b, h1, norm_x_g, xq, xk, xv, w_xq_f, w_xo_f, [scatter(gw_up)])
    gw_xq = _wgrad(hn2, dxq, "wgrad_xq")
    gw_xo = _wgrad(xo, dh2b, "wgrad_xo")
    gw_xk, gw_xv, g_mem = _bwd_mem(mem2, norm_mem_g, mn, dxk, dxv, w_xk_f, w_xv_f)
    head = jnp.arange(D_ATT) // HEAD_DIM
    head_ones = (head[:, None] == head[None, :]).astype(BF16)
    dcat, dsum = _bwd_mix_out(dh1b, w_out_f, cat, head_ones)
    gw_out = _wgrad(cat, dh1b, "wgrad_out")
    (dag, g_cw, g_cb, g_lg, g_lb), (r_down,) = _bwd_conv(dcat, c1, ag, cw_f, conv_ln_g, conv_ln_b, [scatter(gw_down)])
    hosted = [[scatter(gw_out, gw_xq, gw_xk, gw_xv, gw_xo)], [], []]
    dqs, dkvs, landed = [], [], []
    for dil, q3, sides in zip(DILATIONS, qkv3, hosted):
        (dq3, dkv3), got = _swa_bwd(q3, _to_strided(dcat, dil), _to_strided(ltot, dil), _to_strided(dsum, dil),
                                    f"swa_bwd_d{dil}", sides)
        dqs.append(_from_strided(dq3))
        dkvs.append(_from_strided(dkv3[:, HALF:HALF + q3.shape[1]]))
        landed += got
    r_out, r_xq, r_xk, r_xv, r_xo = landed
    grad_x, dy, g_mix = _bwd_in(dqs, dkvs, dag, w_in_f, x2, norm_mix_g, dh1, rot)
    gw_in = _wgrad(xn, dy, "wgrad_in", b_blk=dy.shape[1] // 2)

    def widen(t):
        return jnp.pad(t, ((0, 0), (0, d - t.shape[1])))

    n_in = w_in.shape[2]
    small = jnp.concatenate([g_mix, g_x, g_mem, g_mlp, g_final, widen(g_cb), widen(g_lg), widen(g_lb),
                             g_cw.reshape(16, d)], axis=0)
    r_in, small_g = _exchange_call(
        [_Exchange([jnp.transpose(gw_in.reshape(d, N_DEV, n_in), (1, 0, 2))], gather=False),
         _Exchange([small], gather=True)], "scatter_w_in_gather_small")
    small_sum = _sum_slots(small_g, "sum_small_grads")
    loss = lax.psum(loss_part[0, 0], ("x", "y", "c"))

    res = {}

    def step(name, w, gsrc, m, v):
        shape = w.shape
        w2, m2, v2 = (t.reshape(-1, shape[-1]) for t in (w, m, v))
        res[name] = [t.reshape(shape) for t in _adamw(w2, gsrc, m2, v2, "adamw_" + name)]

    step("w_in", w_in, r_in, m_w_in, v_w_in)
    step("w_up", w_up, r_up, m_w_up, v_w_up)
    step("w_out", w_out, r_out, m_w_out, v_w_out)
    step("w_xq", w_xq, r_xq, m_w_xq, v_w_xq)
    step("w_xk", w_xk, r_xk, m_w_xk, v_w_xk)
    step("w_xv", w_xv, r_xv, m_w_xv, v_w_xv)
    step("w_xo", w_xo, r_xo, m_w_xo, v_w_xo)
    step("w_down", w_down, r_down, m_w_down, v_w_down)

    me = _dev_index((lax.axis_index("x"), lax.axis_index("y"), lax.axis_index("c")))
    n_cw = conv_w.shape[2]
    g_cw_full = small_sum[8:24].reshape(32, D_CONV)[:CONV_WIDTH]
    g_cw_mine = lax.dynamic_slice_in_dim(g_cw_full, me * n_cw, n_cw, axis=1)
    step("conv_w", conv_w, g_cw_mine, m_conv_w, v_conv_w)

    vec_names = ["norm_mix_g", "norm_x_g", "norm_mem_g", "norm_mlp_g", "norm_final_g", "conv_b", "conv_ln_g", "conv_ln_b"]
    vec_w = [norm_mix_g, norm_x_g, norm_mem_g, norm_mlp_g, gf, conv_b, conv_ln_g, conv_ln_b]
    vec_m = [m_norm_mix_g, m_norm_x_g, m_norm_mem_g, m_norm_mlp_g, m_norm_final_g[None, :], m_conv_b, m_conv_ln_g, m_conv_ln_b]
    vec_v = [v_norm_mix_g, v_norm_x_g, v_norm_mem_g, v_norm_mlp_g, v_norm_final_g[None, :], v_conv_b, v_conv_ln_g, v_conv_ln_b]

    def pack(ts):
        return jnp.concatenate([widen(t) for t in ts], axis=0)

    packed = _adamw(pack(vec_w), small_sum[0:8], pack(vec_m), pack(vec_v), "adamw_vectors")
    for i, name in enumerate(vec_names):
        width = vec_w[i].shape[1]
        shape = (width,) if name == "norm_final_g" else (1, width)
        res[name] = [t[i, :width].reshape(shape) for t in packed]

    order = ["norm_mix_g", "w_in", "conv_w", "conv_b", "conv_ln_g", "conv_ln_b", "w_out", "norm_x_g", "norm_mem_g",
             "w_xq", "w_xk", "w_xv", "w_xo", "norm_mlp_g", "w_up", "w_down", "norm_final_g"]
    out = [loss, grad_x[None]]
    for kind in range(4):
        out += [res[name][kind] for name in order]
    return tuple(out)
```

```python
import jax
import jax.numpy as jnp
from jax import lax
from jax.experimental import pallas as pl
from jax.experimental.pallas import tpu as pltpu

F32 = jnp.float32
BF16 = jnp.bfloat16

N_DEV = 8
EPS = 1e-6
NEG_INF = -1e30
ATT_HEADS = 8
HEAD_DIM = 64
D_ATT = ATT_HEADS * HEAD_DIM
D_CONV = 512
DILATIONS = (1, 4, 16)
HALF = 64
ROPE_THETA = 500000.0
ROT_DIM = HEAD_DIM // 4
CONV_WIDTH = 31
CONV_PAD = (CONV_WIDTH - 1) // 2
XATT_HEADS = 4
ADAM_LR = 0.001
ADAM_B1 = 0.9
ADAM_B2 = 0.999
ADAM_EPS = 1e-08
ADAM_WD = 0.01
ADAM_STEP = 10

LANES = 128
SUBLANES = 8
BF16_ROWS = 16
BQ = 128
WIN = BQ + 2 * HALF
MLP_SHARDS = 4
VMEM_LIMIT = 56 * 1024 * 1024
MESH = pl.DeviceIdType.MESH
ANY = pl.BlockSpec(memory_space=pl.ANY)

_NT = (((1,), (1,)), ((), ()))
_TN = (((0,), (0,)), ((), ()))


def _dot(a, b):
    return jnp.dot(a, b, preferred_element_type=F32)


def _dot_nt(a, b):
    return lax.dot_general(a, b, _NT, preferred_element_type=F32)


def _dot_tn(a, b):
    return lax.dot_general(a, b, _TN, preferred_element_type=F32)


def _params(*sem):
    return pltpu.CompilerParams(dimension_semantics=sem or None, vmem_limit_bytes=VMEM_LIMIT)


def _sigmoid(v):
    return 1.0 / (1.0 + jnp.exp(-v))


def _mean(v):
    return jnp.mean(v, axis=-1, keepdims=True)


def _rms_fwd(h, g):
    r = lax.rsqrt(_mean(h * h) + EPS)
    return h * r * g, r


def _rms_bwd(h, g, d_out):
    r = lax.rsqrt(_mean(h * h) + EPS)
    hn = h * r
    gd = d_out * g
    return r * (gd - hn * _mean(gd * hn)), d_out * hn


def _row_tile(s):
    return min(512, s)


def _full(shape):
    return pl.BlockSpec(shape, lambda *_: (0,) * len(shape))


def _mesh_pos():
    return lax.axis_index("x"), lax.axis_index("y"), lax.axis_index("c")


def _dev_index(p):
    return 4 * p[0] + 2 * p[1] + p[2]


class _Exchange:
    def __init__(self, arrays, gather):
        self.arrays, self.gather, self.n = list(arrays), gather, len(arrays)

    def out_shapes(self):
        return [jax.ShapeDtypeStruct(((N_DEV,) + a.shape) if self.gather else a.shape, a.dtype)
                for a in self.arrays]

    def sem_shapes(self):
        return [pltpu.SemaphoreType.DMA((7 * self.n,)), pltpu.SemaphoreType.DMA((7 * self.n,)),
                pltpu.SemaphoreType.DMA((self.n,))]

    def phases(self, x_refs, o_refs, send_sems, recv_sems, local_sems):
        n = self.n
        x, y, c = _mesh_pos()
        me, sibling = (x, y, c), (x, y, 1 - c)

        if self.gather:
            chips = [(1 - x, y), (x, 1 - y), (1 - x, 1 - y)]

            def copy(a, k, block, to, src=None):
                slot = o_refs[a].at[_dev_index(block)]
                return pltpu.make_async_remote_copy(
                    src_ref=slot if src is None else src, dst_ref=slot,
                    send_sem=send_sems.at[7 * a + k], recv_sem=recv_sems.at[7 * a + k],
                    device_id=to, device_id_type=MESH)

            def mine(a):
                return pltpu.make_async_copy(x_refs[a], o_refs[a].at[_dev_index(me)], local_sems.at[a])

            def first(a):
                return [copy(a, 0, me, sibling, src=x_refs[a])] + [
                    copy(a, 1 + j, me, (*chip, c), src=x_refs[a]) for j, chip in enumerate(chips)]

            def relayed(a, j):
                return copy(a, 4 + j, (*chips[j], c), sibling)

            def start():
                for a in range(n):
                    mine(a).start()
                    for cp in first(a):
                        cp.start()

            def relay():
                for j, chip in enumerate(chips):
                    for a in range(n):
                        copy(a, 1 + j, (*chip, c), me).wait_recv()
                        relayed(a, j).start()

            def finish():
                for a in range(n):
                    copy(a, 0, sibling, me).wait_recv()
                    for j, chip in enumerate(chips):
                        copy(a, 4 + j, (*chip, 1 - c), me).wait_recv()
                    for cp in first(a) + [relayed(a, j) for j in range(3)]:
                        cp.wait_send()
                    mine(a).wait()

            return start, relay, finish

        flips = [(dx, dy, dc) for dx in (0, 1) for dy in (0, 1) for dc in (0, 1)][1:]

        def peer(k):
            return tuple(1 - v if fl else v for v, fl in zip(me, flips[k]))

        def send(a, k):
            return pltpu.make_async_remote_copy(
                src_ref=x_refs[a].at[_dev_index(peer(k))], dst_ref=o_refs[a].at[_dev_index(me)],
                send_sem=send_sems.at[7 * a + k], recv_sem=recv_sems.at[7 * a + k],
                device_id=peer(k), device_id_type=MESH)

        def landed(a, k):
            slot = o_refs[a].at[_dev_index(peer(k))]
            return pltpu.make_async_remote_copy(
                src_ref=slot, dst_ref=slot, send_sem=send_sems.at[7 * a + k], recv_sem=recv_sems.at[7 * a + k],
                device_id=peer(k), device_id_type=MESH)

        def own(a):
            return pltpu.make_async_copy(x_refs[a].at[_dev_index(me)], o_refs[a].at[_dev_index(me)],
                                         local_sems.at[a])

        def start():
            for a in range(n):
                own(a).start()
            for k in range(7):
                for a in range(n):
                    send(a, k).start()

        def finish():
            for k in range(7):
                for a in range(n):
                    landed(a, k).wait_recv()
            for k in range(7):
                for a in range(n):
                    send(a, k).wait_send()
            for a in range(n):
                own(a).wait()

        return start, (lambda: None), finish


def _hosted_call(body, sides, *, name, grid, in_specs, out_specs, out_shape, scratch_shapes, args):
    n_in, n_out, ns = len(in_specs), len(out_specs), sum(s.n for s in sides)
    steps = 1
    for g in grid:
        steps *= g

    def wrapped(*refs):
        ins, s_ins = refs[:n_in], refs[n_in:n_in + ns]
        outs = refs[n_in + ns:n_in + ns + n_out]
        s_outs = refs[n_in + ns + n_out:n_in + 2 * ns + n_out]
        rest = refs[n_in + 2 * ns + n_out:]
        scratch, sems = rest[:len(rest) - 3 * len(sides)], rest[len(rest) - 3 * len(sides):]
        phases, off = [], 0
        for i, s in enumerate(sides):
            phases.append(s.phases(s_ins[off:off + s.n], s_outs[off:off + s.n], *sems[3 * i:3 * i + 3]))
            off += s.n
        lin = 0
        for ax, g in enumerate(grid):
            lin = lin * g + pl.program_id(ax)

        if sides:
            @pl.when(lin == 0)
            def _():
                for start, _, _ in phases:
                    start()

        body(*ins, *outs, *scratch)

        if sides:
            @pl.when(lin == min((3 * steps) // 4, steps - 1))
            def _():
                for _, relay, _ in phases:
                    relay()

            @pl.when(lin == steps - 1)
            def _():
                for _, _, finish in phases:
                    finish()

    res = pl.pallas_call(
        wrapped, name=name, grid=grid,
        in_specs=list(in_specs) + [ANY] * ns, out_specs=list(out_specs) + [ANY] * ns,
        out_shape=list(out_shape) + [sh for s in sides for sh in s.out_shapes()],
        scratch_shapes=list(scratch_shapes) + [sh for s in sides for sh in s.sem_shapes()],
        compiler_params=_params(*(("arbitrary",) * len(grid))),
    )(*args, *[a for s in sides for a in s.arrays])
    return res[:n_out], res[n_out:]


def _exchange_call(sides, name):
    ns = sum(s.n for s in sides)

    def body(*refs):
        x_refs, o_refs, sems = refs[:ns], refs[ns:2 * ns], refs[2 * ns:]
        phases, off = [], 0
        for i, s in enumerate(sides):
            phases.append(s.phases(x_refs[off:off + s.n], o_refs[off:off + s.n], *sems[3 * i:3 * i + 3]))
            off += s.n
        for step in range(3):
            for ph in phases:
                ph[step]()

    return pl.pallas_call(
        body, name=name,
        out_shape=[sh for s in sides for sh in s.out_shapes()],
        in_specs=[ANY] * ns, out_specs=[ANY] * ns,
        scratch_shapes=[sh for s in sides for sh in s.sem_shapes()],
    )(*[a for s in sides for a in s.arrays])


def _rotary_tables(s):
    half = ROT_DIM // 2
    freqs = ROPE_THETA ** (-jnp.arange(0, ROT_DIM, 2, dtype=F32) / ROT_DIM)
    ang = jnp.arange(s, dtype=F32)[:, None] * freqs[None, :]
    cos, sin = jnp.cos(ang), jnp.sin(ang)
    one = jnp.ones((s, HEAD_DIM - ROT_DIM), F32)
    zero = jnp.zeros((s, HEAD_DIM - ROT_DIM), F32)
    zh = jnp.zeros((s, half), F32)
    c64 = jnp.concatenate([cos, cos, one], axis=1)
    a64 = jnp.concatenate([-sin, zh, zero], axis=1)
    b64 = jnp.concatenate([zh, sin, zero], axis=1)
    return tuple(jnp.tile(t, (1, LANES // HEAD_DIM)) for t in (c64, a64, b64))


def _strided_spec(dil, tm, width):
    return pl.BlockSpec((dil, tm // dil, width), lambda i: (0, i, 0))


def _strided_shape(dil, s, width, dtype):
    return jax.ShapeDtypeStruct((dil, s // dil, width), dtype)


def _lane_scratch(tm, width):
    return pltpu.VMEM((width // LANES, tm, LANES), F32)


def _store_blocks(buf, v):
    for cb in range(buf.shape[0]):
        buf[cb] = v[:, cb * LANES:(cb + 1) * LANES]


def _load_blocks(buf):
    return jnp.concatenate([buf[cb] for cb in range(buf.shape[0])], axis=1)


def _write_strided(buf, dst_ref):
    dil, rows, _ = dst_ref.shape
    for r in range(dil):
        for cb in range(buf.shape[0]):
            dst_ref[r, :, cb * LANES:(cb + 1) * LANES] = buf[cb, pl.ds(r, rows, stride=dil), :].astype(dst_ref.dtype)


def _read_strided(src_ref, buf, add=False):
    dil, rows, _ = src_ref.shape
    for r in range(dil):
        for cb in range(buf.shape[0]):
            v = src_ref[r, :, cb * LANES:(cb + 1) * LANES].astype(F32)
            if add:
                v = v + buf[cb, pl.ds(r, rows, stride=dil), :]
            buf[cb, pl.ds(r, rows, stride=dil), :] = v


def _fwd_in(x, g, w_in, rot):
    s, d = x.shape
    n = w_in.shape[1]
    tm = _row_tile(s)
    dils = DILATIONS[1:]

    def body(x_ref, g_ref, w_ref, c_ref, a_ref, b_ref, xn_ref, qkv_ref, ag_ref, *rest):
        strided, ybuf = rest[:len(dils)], rest[len(dils)]
        xn = _rms_fwd(x_ref[...], g_ref[...])[0].astype(BF16)
        xn_ref[...] = xn
        y = _dot(xn, w_ref[...])
        reps = (1, D_ATT // LANES)
        cc, aa, bb = jnp.tile(c_ref[...], reps), jnp.tile(a_ref[...], reps), jnp.tile(b_ref[...], reps)
        parts = []
        for blk in range(2):
            t = y[:, blk * D_ATT:(blk + 1) * D_ATT]
            parts.append(t * cc + pltpu.roll(t, D_ATT - ROT_DIM // 2, 1) * aa + pltpu.roll(t, ROT_DIM // 2, 1) * bb)
        qkv = jnp.concatenate(parts + [y[:, 2 * D_ATT:3 * D_ATT]], axis=1)
        qkv_ref[...] = qkv.astype(BF16)
        _store_blocks(ybuf, qkv)
        for ref in strided:
            _write_strided(ybuf, ref)
        ag_ref[...] = y[:, 3 * D_ATT:].astype(BF16)

    row = lambda w: pl.BlockSpec((tm, w), lambda i: (i, 0))
    return pl.pallas_call(
        body, name="fwd_in", grid=(s // tm,),
        in_specs=[row(d), _full((1, d)), _full((d, n)), row(LANES), row(LANES), row(LANES)],
        out_specs=[row(d), row(3 * D_ATT), row(2 * D_CONV)] + [_strided_spec(dil, tm, 3 * D_ATT) for dil in dils],
        out_shape=[jax.ShapeDtypeStruct((s, d), BF16), jax.ShapeDtypeStruct((s, 3 * D_ATT), BF16),
                   jax.ShapeDtypeStruct((s, 2 * D_CONV), BF16)]
        + [_strided_shape(dil, s, 3 * D_ATT, BF16) for dil in dils],
        scratch_shapes=[_lane_scratch(tm, 3 * D_ATT)],
        compiler_params=_params("parallel"),
    )(x, g, w_in, *rot)


def _win_in_specs(length, col, width):
    per = BQ // HALF
    last = length // HALF - 1
    return [
        pl.BlockSpec((None, HALF, width), lambda r, j: (r, jnp.maximum(j * per - 1, 0), col)),
        pl.BlockSpec((None, BQ, width), lambda r, j: (r, j, col)),
        pl.BlockSpec((None, HALF, width), lambda r, j: (r, jnp.minimum(j * per + per, last), col)),
    ]


def _fill_window(win, prev_ref, main_ref, next_ref):
    win[0:HALF] = prev_ref[...]
    win[HALF:HALF + BQ] = main_ref[...]
    win[HALF + BQ:] = next_ref[...]


def _band_mask(j, length):
    shape = (2 * BQ, WIN)
    blk = lax.broadcasted_iota(jnp.int32, shape, 0) & (BQ - 1)
    win = lax.broadcasted_iota(jnp.int32, shape, 1)
    pos = j * BQ - HALF + win
    return (jnp.abs(win - HALF - blk) <= HALF) & (pos >= 0) & (pos < length)


def _first_head():
    return lax.broadcasted_iota(jnp.int32, (1, LANES), 1) < HEAD_DIM


def _stack_heads(v, first):
    zero = jnp.zeros((), v.dtype)
    return jnp.concatenate([jnp.where(first, v, zero), jnp.where(first, zero, v)], axis=0)


def _unstack_heads(v, first):
    rows = v.shape[0] // 2
    return jnp.where(first, v[:rows], v[rows:])


def _stack_cols(v, first):
    top = jnp.max(jnp.where(first, v, -jnp.inf), axis=-1, keepdims=True)
    bot = jnp.max(jnp.where(first, -jnp.inf, v), axis=-1, keepdims=True)
    return jnp.concatenate([top, bot], axis=0)


def _swa_fwd(qkv3, name, sides=()):
    dil, length, _ = qkv3.shape
    scale = HEAD_DIM ** -0.5

    def body(q_ref, kp, km, kn, vp, vm, vn, o_ref, lse_ref, kwin, vwin):
        j = pl.program_id(1)
        _fill_window(kwin, kp, km, kn)
        _fill_window(vwin, vp, vm, vn)
        valid = _band_mask(j, length)
        first = _first_head()
        for pr in range(D_ATT // LANES):
            cols = slice(pr * LANES, (pr + 1) * LANES)
            qs = _stack_heads(q_ref[:, cols], first)
            sc = jnp.where(valid, _dot_nt(qs, kwin[:, cols]) * scale, NEG_INF)
            m = jnp.max(sc, axis=-1, keepdims=True)
            p = jnp.exp(sc - m)
            den = jnp.sum(p, axis=-1, keepdims=True)
            pv = _dot(p.astype(BF16), vwin[:, cols]) * (1.0 / den)
            o_ref[:, cols] = _unstack_heads(pv, first).astype(BF16)
            lse_ref[:, cols] = _unstack_heads(jnp.broadcast_to(m + jnp.log(den), (2 * BQ, LANES)), first)

    blk = lambda w: pl.BlockSpec((None, BQ, w), lambda r, j: (r, j, 0))
    return _hosted_call(
        body, sides, name=name, grid=(dil, length // BQ),
        in_specs=[pl.BlockSpec((None, BQ, D_ATT), lambda r, j: (r, j, 0))]
        + _win_in_specs(length, 1, D_ATT) + _win_in_specs(length, 2, D_ATT),
        out_specs=[blk(D_ATT), blk(D_ATT)],
        out_shape=[jax.ShapeDtypeStruct((dil, length, D_ATT), BF16),
                   jax.ShapeDtypeStruct((dil, length, D_ATT), F32)],
        scratch_shapes=[pltpu.VMEM((WIN, D_ATT), BF16), pltpu.VMEM((WIN, D_ATT), BF16)],
        args=[qkv3] * 7)


def _glu(v):
    return v[:, :D_CONV].astype(F32) * _sigmoid(v[:, D_CONV:].astype(F32))


def _halo_specs(tm, width, col=0):
    per = tm // BF16_ROWS
    return lambda nblk: [
        pl.BlockSpec((BF16_ROWS, width), lambda i: (jnp.maximum(i * per - 1, 0), col)),
        pl.BlockSpec((tm, width), lambda i: (i, col)),
        pl.BlockSpec((BF16_ROWS, width), lambda i: (jnp.minimum(i * per + per, nblk - 1), col)),
    ]


def _fill_halo(buf, i, ntiles, tm, prev, main, nxt):
    buf[0:BF16_ROWS] = jnp.where(i == 0, 0.0, prev)
    buf[BF16_ROWS:BF16_ROWS + tm] = main
    buf[BF16_ROWS + tm:] = jnp.where(i == ntiles - 1, 0.0, nxt)


def _halo_scratch(tm):
    return [pltpu.VMEM((tm + 2 * BF16_ROWS, D_CONV), F32),
            pltpu.VMEM((SUBLANES - 1, tm + 2 * BF16_ROWS - SUBLANES, D_CONV), F32)]


def _shift_copies(buf, shifted, tm):
    rows = tm + 2 * BF16_ROWS - SUBLANES
    for b in range(1, SUBLANES):
        shifted[b - 1] = buf[pl.ds(b, rows), :]


def _tap(buf, shifted, off, tm):
    a, b = divmod(off, SUBLANES)
    if b == 0:
        return buf[pl.ds(SUBLANES * a, tm), :]
    return shifted[b - 1, pl.ds(SUBLANES * a, tm), :]


def _fwd_conv(ag, cw, cb, lg, lb):
    s = ag.shape[0]
    tm = _row_tile(s)
    nt = s // tm

    def body(agp, agm, agn, cw_ref, cb_ref, lg_ref, lb_ref, c1_ref, co_ref, ubuf, ush):
        i = pl.program_id(0)
        _fill_halo(ubuf, i, nt, tm, _glu(agp[...]), _glu(agm[...]), _glu(agn[...]))
        _shift_copies(ubuf, ush, tm)
        w = cw_ref[...]
        acc = jnp.zeros((tm, D_CONV), F32)
        for k in range(CONV_WIDTH):
            acc = acc + _tap(ubuf, ush, k + 1, tm) * w[k:k + 1, :]
        acc = acc + cb_ref[...]
        c1_ref[...] = acc
        xc = acc - _mean(acc)
        ln = xc * lax.rsqrt(_mean(xc * xc) + EPS) * lg_ref[...] + lb_ref[...]
        co_ref[...] = (ln * _sigmoid(ln)).astype(BF16)

    vec = _full((1, D_CONV))
    return pl.pallas_call(
        body, name="fwd_conv", grid=(nt,),
        in_specs=_halo_specs(tm, 2 * D_CONV)(s // BF16_ROWS) + [_full((32, D_CONV)), vec, vec, vec],
        out_specs=[pl.BlockSpec((tm, D_CONV), lambda i: (i, 0))] * 2,
        out_shape=[jax.ShapeDtypeStruct((s, D_CONV), F32), jax.ShapeDtypeStruct((s, D_CONV), BF16)],
        scratch_shapes=_halo_scratch(tm),
        compiler_params=_params("parallel"),
    )(ag, ag, ag, cw, cb, lg, lb)


def _fwd_mix_out(outs, lses, conv_out, x, w_out):
    s, d = x.shape
    tm = _row_tile(s)
    dils = DILATIONS[1:]
    nd = len(dils)

    def body(o1, *rest):
        o_str, l1, l_str = rest[:nd], rest[nd], rest[nd + 1:2 * nd + 1]
        co, x_ref, w_ref, h_ref, cat_ref, lt_ref = rest[2 * nd + 1:2 * nd + 7]
        lt_str = rest[2 * nd + 7:3 * nd + 7]
        obufs, lbufs, ltbuf = rest[3 * nd + 7:4 * nd + 7], rest[4 * nd + 7:5 * nd + 7], rest[5 * nd + 7]
        for ref, buf in zip(o_str + l_str, obufs + lbufs):
            _read_strided(ref, buf)
        lse = [l1[0]] + [_load_blocks(buf) for buf in lbufs]
        out = [o1[0].astype(F32)] + [_load_blocks(buf) for buf in obufs]
        m = lse[0]
        for v in lse[1:]:
            m = jnp.maximum(m, v)
        e = [jnp.exp(v - m) for v in lse]
        den = sum(e[1:], e[0])
        att = (sum((ev * ov for ev, ov in zip(e[1:], out[1:])), e[0] * out[0]) / den).astype(BF16)
        lt = m + jnp.log(den)
        lt_ref[...] = lt
        _store_blocks(ltbuf, lt)
        for ref in lt_str:
            _write_strided(ltbuf, ref)
        cat_ref[:, :D_ATT] = att
        cat_ref[:, D_ATT:] = co[...]
        h_ref[...] = x_ref[...] + _dot(att, w_ref[:D_ATT, :]) + _dot(co[...], w_ref[D_ATT:, :])

    row = lambda w: pl.BlockSpec((tm, w), lambda i: (i, 0))
    nat = pl.BlockSpec((1, tm, D_ATT), lambda i: (0, i, 0))
    strided = [_strided_spec(dil, tm, D_ATT) for dil in dils]
    return pl.pallas_call(
        body, name="fwd_mix_out", grid=(s // tm,),
        in_specs=[nat] + strided + [nat] + strided + [row(D_ATT), row(d), _full((d, d))],
        out_specs=[row(d), row(d), row(D_ATT)] + strided,
        out_shape=[jax.ShapeDtypeStruct((s, d), F32), jax.ShapeDtypeStruct((s, d), BF16),
                   jax.ShapeDtypeStruct((s, D_ATT), F32)] + [_strided_shape(dil, s, D_ATT, F32) for dil in dils],
        scratch_shapes=[_lane_scratch(tm, D_ATT)] * (2 * nd + 1),
        compiler_params=_params("parallel"),
    )(*outs, *lses, conv_out, x, w_out)


def _fwd_mem(mem, g, wk, wv):
    m, d = mem.shape

    def body(mem_ref, g_ref, wk_ref, wv_ref, mn_ref, xk_ref, xv_ref):
        mn = _rms_fwd(mem_ref[...], g_ref[...])[0].astype(BF16)
        mn_ref[...] = mn
        xk_ref[...] = _dot(mn, wk_ref[...]).astype(BF16)
        xv_ref[...] = _dot(mn, wv_ref[...]).astype(BF16)

    return pl.pallas_call(
        body, name="fwd_mem",
        out_shape=[jax.ShapeDtypeStruct((m, d), BF16)] * 3,
        compiler_params=_params(),
    )(mem, g, wk, wv)


def _xatt_probs(q, k):
    sc = _dot_nt(q, k) * (q.shape[1] ** -0.5)
    p = jnp.exp(sc - jnp.max(sc, axis=-1, keepdims=True))
    return p / jnp.sum(p, axis=-1, keepdims=True)


def _fwd_xattn(h1, g, wq, xk, xv, wo):
    s, d = h1.shape
    m = xk.shape[0]
    tm = _row_tile(s)
    hd = d // XATT_HEADS

    def body(h_ref, g_ref, wq_ref, xk_ref, xv_ref, wo_ref, h2_ref, hn_ref, xq_ref, xo_ref):
        h = h_ref[...]
        hn = _rms_fwd(h, g_ref[...])[0].astype(BF16)
        hn_ref[...] = hn
        xq = _dot(hn, wq_ref[...]).astype(BF16)
        xq_ref[...] = xq
        for i in range(XATT_HEADS):
            cols = slice(i * hd, (i + 1) * hd)
            pr = _xatt_probs(xq[:, cols], xk_ref[:, cols])
            xo_ref[:, cols] = _dot(pr.astype(BF16), xv_ref[:, cols]).astype(BF16)
        h2_ref[...] = h + _dot(xo_ref[...], wo_ref[...])

    row = pl.BlockSpec((tm, d), lambda i: (i, 0))
    return pl.pallas_call(
        body, name="fwd_xattn", grid=(s // tm,),
        in_specs=[row, _full((1, d)), _full((d, d)), _full((m, d)), _full((m, d)), _full((d, d))],
        out_specs=[row] * 4,
        out_shape=[jax.ShapeDtypeStruct((s, d), F32)] + [jax.ShapeDtypeStruct((s, d), BF16)] * 3,
        compiler_params=_params("parallel"),
    )(h1, g, wq, xk, xv, wo)


def _fwd_mlp_loss(h2, g, w_up, w_down, gf, target):
    s, d = h2.shape
    nsh, _, f = w_up.shape
    fb = MLP_SHARDS * f
    nb = nsh // MLP_SHARDS
    tm = _row_tile(s)

    def body(h_ref, g_ref, wu_ref, wd_ref, gf_ref, t_ref,
             hn_ref, act_ref, dh_ref, dhb_ref, loss_ref, ggf_ref, acc):
        i, k = pl.program_id(0), pl.program_id(1)

        @pl.when(k == 0)
        def _():
            hn_ref[...] = _rms_fwd(h_ref[...], g_ref[...])[0].astype(BF16)
            acc[...] = jnp.zeros_like(acc)

        @pl.when((i == 0) & (k == 0))
        def _():
            loss_ref[...] = jnp.zeros_like(loss_ref)
            ggf_ref[...] = jnp.zeros_like(ggf_ref)

        hn = hn_ref[...]
        for c in range(MLP_SHARDS):
            act_ref[:, c * f:(c + 1) * f] = jnp.square(jnp.maximum(_dot(hn, wu_ref[c]), 0.0)).astype(BF16)
        acc[...] += _dot(act_ref[...], wd_ref[...])

        @pl.when(k == nb - 1)
        def _():
            h3 = h_ref[...] + acc[...]
            gfv = gf_ref[...]
            y, _ = _rms_fwd(h3, gfv)
            err = y - t_ref[...]
            loss_ref[...] += 0.5 * jnp.sum(_mean(err * err))
            dh3, gg = _rms_bwd(h3, gfv, err * (1.0 / d))
            ggf_ref[...] += jnp.sum(gg, axis=0, keepdims=True)
            dh_ref[...] = dh3
            dhb_ref[...] = dh3.astype(BF16)

    row = pl.BlockSpec((tm, d), lambda i, k: (i, 0))
    return pl.pallas_call(
        body, name="fwd_mlp_loss", grid=(s // tm, nb),
        in_specs=[row, _full((1, d)),
                  pl.BlockSpec((MLP_SHARDS, d, f), lambda i, k: (k, 0, 0)),
                  pl.BlockSpec((fb, d), lambda i, k: (k, 0)),
                  _full((1, d)), row],
        out_specs=[row, pl.BlockSpec((tm, fb), lambda i, k: (i, k)), row, row,
                   _full((1, LANES)), _full((1, d))],
        out_shape=[jax.ShapeDtypeStruct((s, d), BF16), jax.ShapeDtypeStruct((s, nsh * f), BF16),
                   jax.ShapeDtypeStruct((s, d), F32), jax.ShapeDtypeStruct((s, d), BF16),
                   jax.ShapeDtypeStruct((1, LANES), F32), jax.ShapeDtypeStruct((1, d), F32)],
        scratch_shapes=[pltpu.VMEM((tm, d), F32)],
        compiler_params=_params("arbitrary", "arbitrary"),
    )(h2, g, w_up, w_down, gf, target)


def _bwd_mlp(dh3, dh3b, act, w_up_t, w_down, h2, g):
    s, d = h2.shape
    ff = w_down.shape[0]
    fb = MLP_SHARDS * (ff // N_DEV)
    nb = ff // fb
    tm = _row_tile(s)

    def body(dh_ref, dhb_ref, act_ref, wut_ref, wd_ref, h_ref, g_ref,
             du_ref, dh2_ref, dh2b_ref, gg_ref, acc):
        i, k = pl.program_id(0), pl.program_id(1)

        @pl.when(k == 0)
        def _():
            acc[...] = jnp.zeros_like(acc)

        @pl.when((i == 0) & (k == 0))
        def _():
            gg_ref[...] = jnp.zeros_like(gg_ref)

        dact = _dot_nt(dhb_ref[...], wd_ref[...])
        du_ref[...] = (dact * (2.0 * jnp.sqrt(act_ref[...].astype(F32)))).astype(BF16)
        acc[...] += _dot(du_ref[...], wut_ref[...])

        @pl.when(k == nb - 1)
        def _():
            dh, gg = _rms_bwd(h_ref[...], g_ref[...], acc[...])
            gg_ref[...] += jnp.sum(gg, axis=0, keepdims=True)
            dh2 = dh_ref[...] + dh
            dh2_ref[...] = dh2
            dh2b_ref[...] = dh2.astype(BF16)

    row = pl.BlockSpec((tm, d), lambda i, k: (i, 0))
    col = pl.BlockSpec((tm, fb), lambda i, k: (i, k))
    wblk = pl.BlockSpec((fb, d), lambda i, k: (k, 0))
    return pl.pallas_call(
        body, name="bwd_mlp", grid=(s // tm, nb),
        in_specs=[row, row, col, wblk, wblk, row, _full((1, d))],
        out_specs=[col, row, row, _full((1, d))],
        out_shape=[jax.ShapeDtypeStruct((s, ff), BF16), jax.ShapeDtypeStruct((s, d), F32),
                   jax.ShapeDtypeStruct((s, d), BF16), jax.ShapeDtypeStruct((1, d), F32)],
        scratch_shapes=[pltpu.VMEM((tm, d), F32)],
        compiler_params=_params("arbitrary", "arbitrary"),
    )(dh3, dh3b, act, w_up_t, w_down, h2, g)


def _bwd_xattn(dh2, dh2b, h1, g, xq, xk, xv, wq, wo, sides=()):
    s, d = h1.shape
    m = xk.shape[0]
    tm = _row_tile(s)
    hd = d // XATT_HEADS
    scale = hd ** -0.5

    def body(dh_ref, dhb_ref, h_ref, g_ref, xq_ref, xk_ref, xv_ref, wq_ref, wo_ref,
             dh1_ref, dh1b_ref, dxq_ref, dxk_ref, dxv_ref, gg_ref):
        @pl.when(pl.program_id(0) == 0)
        def _():
            dxk_ref[...] = jnp.zeros_like(dxk_ref)
            dxv_ref[...] = jnp.zeros_like(dxv_ref)
            gg_ref[...] = jnp.zeros_like(gg_ref)

        dxo = _dot_nt(dhb_ref[...], wo_ref[...])
        for i in range(XATT_HEADS):
            cols = slice(i * hd, (i + 1) * hd)
            q, k, v = xq_ref[:, cols], xk_ref[:, cols], xv_ref[:, cols]
            pr = _xatt_probs(q, k)
            dxo_h = dxo[:, cols].astype(BF16)
            dpr = _dot_nt(dxo_h, v)
            dsc = (pr * (dpr - jnp.sum(dpr * pr, axis=-1, keepdims=True)) * scale).astype(BF16)
            dxq_ref[:, cols] = _dot(dsc, k).astype(BF16)
            dxk_ref[:, cols] += _dot_tn(dsc, q)
            dxv_ref[:, cols] += _dot_tn(pr.astype(BF16), dxo_h)
        dh, gg = _rms_bwd(h_ref[...], g_ref[...], _dot_nt(dxq_ref[...], wq_ref[...]))
        gg_ref[...] += jnp.sum(gg, axis=0, keepdims=True)
        dh1 = dh_ref[...] + dh
        dh1_ref[...] = dh1
        dh1b_ref[...] = dh1.astype(BF16)

    row = pl.BlockSpec((tm, d), lambda i: (i, 0))
    return _hosted_call(
        body, sides, name="bwd_xattn", grid=(s // tm,),
        in_specs=[row, row, row, _full((1, d)), row, _full((m, d)), _full((m, d)), _full((d, d)), _full((d, d))],
        out_specs=[row, row, row, _full((m, d)), _full((m, d)), _full((1, d))],
        out_shape=[jax.ShapeDtypeStruct((s, d), F32), jax.ShapeDtypeStruct((s, d), BF16),
                   jax.ShapeDtypeStruct((s, d), BF16), jax.ShapeDtypeStruct((m, d), F32),
                   jax.ShapeDtypeStruct((m, d), F32), jax.ShapeDtypeStruct((1, d), F32)],
        scratch_shapes=[],
        args=[dh2, dh2b, h1, g, xq, xk, xv, wq, wo])


def _bwd_mem(mem, g, mn, dxk, dxv, wk, wv):
    m, d = mem.shape

    def body(mem_ref, g_ref, mn_ref, dxk_ref, dxv_ref, wk_ref, wv_ref, gk_ref, gv_ref, gg_ref):
        dk, dv = dxk_ref[...].astype(BF16), dxv_ref[...].astype(BF16)
        gk_ref[...] = _dot_tn(mn_ref[...], dk).astype(BF16)
        gv_ref[...] = _dot_tn(mn_ref[...], dv).astype(BF16)
        dmn = _dot_nt(dk, wk_ref[...]) + _dot_nt(dv, wv_ref[...])
        _, gg = _rms_bwd(mem_ref[...], g_ref[...], dmn)
        gg_ref[...] = jnp.sum(gg, axis=0, keepdims=True)

    return pl.pallas_call(
        body, name="bwd_mem",
        out_shape=[jax.ShapeDtypeStruct((d, d), BF16), jax.ShapeDtypeStruct((d, d), BF16),
                   jax.ShapeDtypeStruct((1, d), F32)],
        compiler_params=_params(),
    )(mem, g, mn, dxk, dxv, wk, wv)


def _bwd_mix_out(dh1b, w_out, cat, head_ones):
    s, d = dh1b.shape
    tm = _row_tile(s)
    dils = DILATIONS[1:]
    nd = len(dils)

    def body(dh_ref, w_ref, cat_ref, ones_ref, dcat_ref, dsum_ref, *rest):
        da_str, ds_str, dbuf, sbuf = rest[:nd], rest[nd:2 * nd], rest[2 * nd], rest[2 * nd + 1]
        dcat = _dot_nt(dh_ref[...], w_ref[...])
        dcat_ref[...] = dcat.astype(BF16)
        datt = dcat[:, :D_ATT]
        prod = datt * cat_ref[...].astype(F32)
        hi = prod.astype(BF16)
        lo = (prod - hi.astype(F32)).astype(BF16)
        dsum = _dot(hi, ones_ref[...]) + _dot(lo, ones_ref[...])
        dsum_ref[...] = dsum
        _store_blocks(dbuf, datt)
        _store_blocks(sbuf, dsum)
        for da_ref, ds_ref in zip(da_str, ds_str):
            _write_strided(dbuf, da_ref)
            _write_strided(sbuf, ds_ref)

    row = lambda w: pl.BlockSpec((tm, w), lambda i: (i, 0))
    strided = [_strided_spec(dil, tm, D_ATT) for dil in dils]
    return pl.pallas_call(
        body, name="bwd_mix_out", grid=(s // tm,),
        in_specs=[row(d), _full((d, d)), row(D_ATT), _full((D_ATT, D_ATT))],
        out_specs=[row(d), row(D_ATT)] + strided + strided,
        out_shape=[jax.ShapeDtypeStruct((s, d), BF16), jax.ShapeDtypeStruct((s, D_ATT), F32)]
        + [_strided_shape(dil, s, D_ATT, BF16) for dil in dils]
        + [_strided_shape(dil, s, D_ATT, F32) for dil in dils],
        scratch_shapes=[_lane_scratch(tm, D_ATT)] * 2,
        compiler_params=_params("parallel"),
    )(dh1b, w_out, cat, head_ones)


def _bwd_conv(dcat, c1, ag, cw, lg, lb, sides=()):
    s = ag.shape[0]
    tm = _row_tile(s)
    nt = s // tm

    def body(dp, dm, dn, cp, cm, cn, agp, agm, agn, cw_ref, lg_ref, lb_ref,
             dag_ref, gcw_ref, gcb_ref, glg_ref, glb_ref, ubuf, ush, dbuf, dsh):
        i = pl.program_id(0)

        @pl.when(i == 0)
        def _():
            gcw_ref[...] = jnp.zeros_like(gcw_ref)
            gcb_ref[...] = jnp.zeros_like(gcb_ref)
            glg_ref[...] = jnp.zeros_like(glg_ref)
            glb_ref[...] = jnp.zeros_like(glb_ref)

        lgv, lbv = lg_ref[...], lb_ref[...]

        def norm_bwd(dco, c1v):
            xc = c1v - _mean(c1v)
            rs = lax.rsqrt(_mean(xc * xc) + EPS)
            z = xc * rs
            ln = z * lgv + lbv
            sg = _sigmoid(ln)
            dln = dco.astype(F32) * (sg * (1.0 + ln * (1.0 - sg)))
            dz = dln * lgv
            return rs * (dz - _mean(dz) - z * _mean(dz * z)), dln, z

        dc_m, dln, z = norm_bwd(dm[...], cm[...])
        glg_ref[...] += jnp.sum(dln * z, axis=0, keepdims=True)
        glb_ref[...] += jnp.sum(dln, axis=0, keepdims=True)
        gcb_ref[...] += jnp.sum(dc_m, axis=0, keepdims=True)
        _fill_halo(dbuf, i, nt, tm, norm_bwd(dp[...], cp[...])[0], dc_m, norm_bwd(dn[...], cn[...])[0])
        _fill_halo(ubuf, i, nt, tm, _glu(agp[...]), _glu(agm[...]), _glu(agn[...]))
        _shift_copies(dbuf, dsh, tm)
        _shift_copies(ubuf, ush, tm)

        w = cw_ref[...]
        tap = lax.broadcasted_iota(jnp.int32, (32, D_CONV), 0)
        du = jnp.zeros((tm, D_CONV), F32)
        gcw = jnp.zeros((32, D_CONV), F32)
        for k in range(CONV_WIDTH):
            du = du + _tap(dbuf, dsh, CONV_WIDTH - k, tm) * w[k:k + 1, :]
            gk = jnp.sum(dc_m * _tap(ubuf, ush, k + 1, tm), axis=0, keepdims=True)
            gcw = jnp.where(tap == k, gk, gcw)
        gcw_ref[...] += gcw
        a = agm[:, :D_CONV].astype(F32)
        sg = _sigmoid(agm[:, D_CONV:].astype(F32))
        dag_ref[:, :D_CONV] = (du * sg).astype(BF16)
        dag_ref[:, D_CONV:] = (du * a * sg * (1.0 - sg)).astype(BF16)

    vec = _full((1, D_CONV))
    nblk = s // BF16_ROWS
    return _hosted_call(
        body, sides, name="bwd_conv", grid=(nt,),
        in_specs=_halo_specs(tm, D_CONV, 1)(nblk) + _halo_specs(tm, D_CONV)(nblk) + _halo_specs(tm, 2 * D_CONV)(nblk)
        + [_full((32, D_CONV)), vec, vec],
        out_specs=[pl.BlockSpec((tm, 2 * D_CONV), lambda i: (i, 0)), _full((32, D_CONV)), vec, vec, vec],
        out_shape=[jax.ShapeDtypeStruct((s, 2 * D_CONV), BF16), jax.ShapeDtypeStruct((32, D_CONV), F32)]
        + [jax.ShapeDtypeStruct((1, D_CONV), F32)] * 3,
        scratch_shapes=_halo_scratch(tm) + _halo_scratch(tm),
        args=[dcat, dcat, dcat, c1, c1, c1, ag, ag, ag, cw, lg, lb])


def _swa_bwd(qkv3, do3, lt3, ds3, name, sides=()):
    dil, length, _ = qkv3.shape
    nb = length // BQ
    scale = HEAD_DIM ** -0.5
    assert WIN == 2 * BQ and BQ == 2 * HALF

    def body(q_ref, kp, km, kn, vp, vm, vn, do_ref, l_ref, s_ref, dq_ref, dkv_ref, kwin, vwin, pend, keep):
        j = pl.program_id(1)

        @pl.when(j == 0)
        def _():
            pend[...] = jnp.zeros_like(pend)
            keep[...] = jnp.zeros_like(keep)

        @pl.when(j < nb)
        def _():
            _fill_window(kwin, kp, km, kn)
            _fill_window(vwin, vp, vm, vn)
            valid = _band_mask(j, length)
            first = _first_head()
            for pr in range(D_ATT // LANES):
                cols = slice(pr * LANES, (pr + 1) * LANES)
                qs = _stack_heads(q_ref[:, cols], first)
                dos = _stack_heads(do_ref[:, cols], first)
                kw, vw = kwin[:, cols], vwin[:, cols]
                sc = _dot_nt(qs, kw) * scale
                p = jnp.where(valid, jnp.exp(sc - _stack_cols(l_ref[:, cols], first)), 0.0)
                dp = _dot_nt(dos, vw)
                dsc = (p * (dp - _stack_cols(s_ref[:, cols], first)) * scale).astype(BF16)
                dq_ref[:, cols] = _unstack_heads(_dot(dsc, kw), first).astype(BF16)
                for part, at in ((_dot_tn(dsc, qs), pr * LANES), (_dot_tn(p.astype(BF16), dos), D_ATT + pr * LANES)):
                    at = slice(at, at + LANES)
                    dkv_ref[:HALF, at] = keep[:, at].astype(BF16)
                    dkv_ref[HALF:, at] = (pend[:HALF, at] + part[:HALF]).astype(BF16)
                    keep[:, at] = pend[HALF:, at] + part[HALF:BQ]
                    pend[:, at] = part[BQ:]

        @pl.when(j == nb)
        def _():
            dkv_ref[:HALF] = keep[...].astype(BF16)
            dkv_ref[HALF:] = pend[:HALF].astype(BF16)

    def clamp(idx):
        return lambda r, j: idx(r, jnp.minimum(j, nb - 1))

    main = pl.BlockSpec((None, BQ, D_ATT), clamp(lambda r, j: (r, j, 0)))
    wins = [pl.BlockSpec(sp.block_shape, clamp(sp.index_map))
            for c in (1, 2) for sp in _win_in_specs(length, c, D_ATT)]
    return _hosted_call(
        body, sides, name=name, grid=(dil, nb + 1),
        in_specs=[main] + wins + [main] * 3,
        out_specs=[main, pl.BlockSpec((None, BQ, 2 * D_ATT), lambda r, j: (r, jnp.maximum(j - 1, 0), 0))],
        out_shape=[jax.ShapeDtypeStruct((dil, length, D_ATT), BF16),
                   jax.ShapeDtypeStruct((dil, length, 2 * D_ATT), BF16)],
        scratch_shapes=[pltpu.VMEM((WIN, D_ATT), BF16)] * 2
        + [pltpu.VMEM((BQ, 2 * D_ATT), F32), pltpu.VMEM((HALF, 2 * D_ATT), F32)],
        args=[qkv3] * 7 + [do3, lt3, ds3])


def _bwd_in(dqs, dkvs, dag, w_in, x, g, dh1, rot):
    s, d = x.shape
    n = w_in.shape[1]
    tm = _row_tile(s)
    dils = DILATIONS[1:]
    nd = len(dils)

    def body(q1, *rest):
        q_str, kv1, kv_str = rest[:nd], rest[nd], rest[nd + 1:2 * nd + 1]
        dag_ref, w_ref, x_ref, g_ref, dh_ref, c_ref, a_ref, b_ref, gx_ref, dy_ref, gg_ref, qbuf, kvbuf = rest[2 * nd + 1:]

        @pl.when(pl.program_id(0) == 0)
        def _():
            gg_ref[...] = jnp.zeros_like(gg_ref)

        _store_blocks(qbuf, q1[0].astype(F32))
        _store_blocks(kvbuf, kv1[0].astype(F32))
        for ref in q_str:
            _read_strided(ref, qbuf, add=True)
        for ref in kv_str:
            _read_strided(ref, kvbuf, add=True)
        dq, dkv = _load_blocks(qbuf), _load_blocks(kvbuf)
        reps = (1, D_ATT // LANES)
        cc, aa, bb = jnp.tile(c_ref[...], reps), jnp.tile(a_ref[...], reps), jnp.tile(b_ref[...], reps)
        for blk, t in enumerate((dq, dkv[:, :D_ATT])):
            dt = t * cc + pltpu.roll(t * aa, ROT_DIM // 2, 1) + pltpu.roll(t * bb, D_ATT - ROT_DIM // 2, 1)
            dy_ref[:, blk * D_ATT:(blk + 1) * D_ATT] = dt.astype(BF16)
        dy_ref[:, 2 * D_ATT:3 * D_ATT] = dkv[:, D_ATT:].astype(BF16)
        dy_ref[:, 3 * D_ATT:] = dag_ref[...]
        dx, gg = _rms_bwd(x_ref[...], g_ref[...], _dot_nt(dy_ref[...], w_ref[...]))
        gg_ref[...] += jnp.sum(gg, axis=0, keepdims=True)
        gx_ref[...] = dh_ref[...] + dx

    row = lambda w: pl.BlockSpec((tm, w), lambda i: (i, 0))
    def strided(width):
        return [pl.BlockSpec((1, tm, width), lambda i: (0, i, 0))] + [_strided_spec(dil, tm, width) for dil in dils]

    return pl.pallas_call(
        body, name="bwd_in", grid=(s // tm,),
        in_specs=strided(D_ATT) + strided(2 * D_ATT)
        + [row(2 * D_CONV), _full((d, n)), row(d), _full((1, d)), row(d)] + [row(LANES)] * 3,
        out_specs=[row(d), row(n), _full((1, d))],
        out_shape=[jax.ShapeDtypeStruct((s, d), F32), jax.ShapeDtypeStruct((s, n), BF16),
                   jax.ShapeDtypeStruct((1, d), F32)],
        scratch_shapes=[_lane_scratch(tm, D_ATT), _lane_scratch(tm, 2 * D_ATT)],
        compiler_params=_params("arbitrary"),
    )(*dqs, *dkvs, dag, w_in, x, g, dh1, *rot)


def _wgrad(a, b, name, a_blk=None, b_blk=None, stack=None, tm=1024):
    s, ka = a.shape
    nb = b.shape[1]
    a_blk, b_blk = a_blk or ka, b_blk or nb
    na, nbl = ka // a_blk, nb // b_blk
    assert na == 1 or nbl == 1
    tm = min(tm, s)
    nt = s // tm
    per = b_blk // stack if stack else 0

    def body(a_ref, b_ref, o_ref, acc):
        t = pl.program_id(1)

        @pl.when(t == 0)
        def _():
            acc[...] = jnp.zeros_like(acc)

        acc[...] += _dot_tn(a_ref[...], b_ref[...])

        @pl.when(t == nt - 1)
        def _():
            if stack:
                for c in range(per):
                    o_ref[c] = acc[:, c * stack:(c + 1) * stack].astype(BF16)
            else:
                o_ref[...] = acc[...].astype(BF16)

    if stack:
        out_spec = pl.BlockSpec((per, ka, stack), lambda k, t: (k, 0, 0))
        out_shape = jax.ShapeDtypeStruct((nb // stack, ka, stack), BF16)
    elif na > 1:
        out_spec = pl.BlockSpec((a_blk, nb), lambda k, t: (k, 0))
        out_shape = jax.ShapeDtypeStruct((ka, nb), BF16)
    else:
        out_spec = pl.BlockSpec((ka, b_blk), lambda k, t: (0, k))
        out_shape = jax.ShapeDtypeStruct((ka, nb), BF16)
    return pl.pallas_call(
        body, name=name, grid=(na * nbl, nt),
        in_specs=[pl.BlockSpec((tm, a_blk), (lambda k, t: (t, k)) if na > 1 else (lambda k, t: (t, 0))),
                  pl.BlockSpec((tm, b_blk), (lambda k, t: (t, k)) if nbl > 1 else (lambda k, t: (t, 0)))],
        out_specs=out_spec, out_shape=out_shape,
        scratch_shapes=[pltpu.VMEM((a_blk, b_blk), F32)],
        compiler_params=_params("parallel", "arbitrary"),
    )(a, b)


def _adamw(w, gsrc, m, v, name):
    summed = gsrc.ndim == w.ndim + 1
    rows, cols = w.shape
    tr = rows if rows <= 256 else 256
    assert rows % tr == 0
    c1 = 1.0 - ADAM_B1 ** ADAM_STEP
    c2 = 1.0 - ADAM_B2 ** ADAM_STEP

    def body(w_ref, g_ref, m_ref, v_ref, go_ref, d_ref, mo_ref, vo_ref):
        if summed:
            g = g_ref[0].astype(F32)
            for i in range(1, N_DEV):
                g = g + g_ref[i].astype(F32)
        else:
            g = g_ref[...]
        mn = ADAM_B1 * m_ref[...] + (1.0 - ADAM_B1) * g
        vn = ADAM_B2 * v_ref[...] + (1.0 - ADAM_B2) * jnp.square(g)
        go_ref[...] = g
        mo_ref[...] = mn
        vo_ref[...] = vn
        d_ref[...] = -ADAM_LR * ((mn / c1) / (jnp.sqrt(vn / c2) + ADAM_EPS) + ADAM_WD * w_ref[...])

    blk = pl.BlockSpec((tr, cols), lambda i: (i, 0))
    gblk = pl.BlockSpec((N_DEV, tr, cols), lambda i: (0, i, 0)) if summed else blk
    return pl.pallas_call(
        body, name=name, grid=(rows // tr,),
        in_specs=[blk, gblk, blk, blk], out_specs=[blk] * 4,
        out_shape=[jax.ShapeDtypeStruct(w.shape, F32)] * 4,
        compiler_params=_params("parallel"),
    )(w, gsrc, m, v)


def _sum_slots(g, name):
    _, rows, cols = g.shape

    def body(g_ref, o_ref):
        acc = g_ref[0]
        for i in range(1, N_DEV):
            acc = acc + g_ref[i]
        o_ref[...] = acc

    return pl.pallas_call(body, name=name, out_shape=jax.ShapeDtypeStruct((rows, cols), F32),
                          compiler_params=_params())(g)


def kernel(x, mem, norm_mix_g, w_in, conv_w, conv_b, conv_ln_g, conv_ln_b, w_out, norm_x_g, norm_mem_g, w_xq, w_xk, w_xv, w_xo, norm_mlp_g, w_up, w_down, norm_final_g, loss_target, m_norm_mix_g, m_w_in, m_conv_w, m_conv_b, m_conv_ln_g, m_conv_ln_b, m_w_out, m_norm_x_g, m_norm_mem_g, m_w_xq, m_w_xk, m_w_xv, m_w_xo, m_norm_mlp_g, m_w_up, m_w_down, m_norm_final_g, v_norm_mix_g, v_w_in, v_conv_w, v_conv_b, v_conv_ln_g, v_conv_ln_b, v_w_out, v_norm_x_g, v_norm_mem_g, v_w_xq, v_w_xk, v_w_xv, v_w_xo, v_norm_mlp_g, v_w_up, v_w_down, v_norm_final_g):
    x2, mem2, tgt = x[0], mem[0], loss_target[0]
    s, d = x2.shape
    gf = norm_final_g[None, :]

    cw_local = jnp.pad(conv_w[0], ((0, 1), (0, LANES - conv_w.shape[2])))
    win_g, cw_g = _exchange_call([_Exchange([w_in[0].astype(BF16), cw_local], gather=True)], "gather_w_in")
    w_in_f = jnp.transpose(win_g, (1, 0, 2)).reshape(d, -1)
    cw_f = jnp.transpose(cw_g[:, :, :conv_w.shape[2]], (1, 0, 2)).reshape(32, D_CONV)
    row_names = (w_out, w_xq, w_xk, w_xv, w_xo)
    rows_local = jnp.concatenate([w[0].astype(BF16) for w in row_names], axis=0)
    late = [_Exchange([rows_local], gather=True), _Exchange([w_up[0].astype(BF16)], gather=True),
            _Exchange([w_down[0].astype(BF16)], gather=True)]

    rot = _rotary_tables(s)
    xn, qkv, ag, *qkv_strided = _fwd_in(x2, norm_mix_g, w_in_f, rot)
    qkv3 = [qkv[None]] + qkv_strided
    outs, lses, gathered = [], [], []
    for dil, q3, side in zip(DILATIONS, qkv3, late):
        (o3, l3), got = _swa_fwd(q3, f"swa_fwd_d{dil}", [side])
        outs.append(o3)
        lses.append(l3)
        gathered += got
    rows_g, wup_g, wdown_g = gathered
    full_rows = []
    off = 0
    for w in row_names:
        full_rows.append(rows_g[:, off:off + w.shape[1], :].reshape(N_DEV * w.shape[1], d))
        off += w.shape[1]
    w_out_f, w_xq_f, w_xk_f, w_xv_f, w_xo_f = full_rows
    w_down_f = wdown_g.reshape(-1, d)
    w_up_t = jnp.swapaxes(wup_g, 1, 2).reshape(-1, d)
    c1, conv_out = _fwd_conv(ag, cw_f, conv_b, conv_ln_g, conv_ln_b)
    h1, cat, ltot, *lt_strided = _fwd_mix_out(outs, lses, conv_out, x2, w_out_f)
    mn, xk, xv = _fwd_mem(mem2, norm_mem_g, w_xk_f, w_xv_f)
    h2, hn2, xq, xo = _fwd_xattn(h1, norm_x_g, w_xq_f, xk, xv, w_xo_f)
    hn3, act, dh3, dh3b, loss_part, g_final = _fwd_mlp_loss(h2, norm_mlp_g, wup_g, w_down_f, gf, tgt)

    def scatter(*grads):
        return _Exchange([g.reshape(N_DEV, -1, g.shape[-1]) for g in grads], gather=False)

    f_blk = w_up.shape[2]
    du, dh2, dh2b, g_mlp = _bwd_mlp(dh3, dh3b, act, w_up_t, w_down_f, h2, norm_mlp_g)
    gw_up = _wgrad(hn3, du, "wgrad_up", b_blk=4 * f_blk, stack=f_blk)
    gw_down = _wgrad(act, dh3b, "wgrad_down", a_blk=f_blk)
    (dh1, dh1b, dxq, dxk, dxv, g_x), (r_up,) = _bwd_xattn(
        dh2, dh2b, h1, norm_x_g, xq, xk, xv, w_xq_f, w_xo_f, [scatter(gw_up)])
    gw_xq = _wgrad(hn2, dxq, "wgrad_xq")
    gw_xo = _wgrad(xo, dh2b, "wgrad_xo")
    gw_xk, gw_xv, g_mem = _bwd_mem(mem2, norm_mem_g, mn, dxk, dxv, w_xk_f, w_xv_f)
    head = jnp.arange(D_ATT) // HEAD_DIM
    head_ones = (head[:, None] == head[None, :]).astype(BF16)
    dcat, dsum, *strided = _bwd_mix_out(dh1b, w_out_f, cat, head_ones)
    n_str = len(DILATIONS) - 1
    do3, lt3, ds3 = [dcat[None]] + strided[:n_str], [ltot[None]] + lt_strided, [dsum[None]] + strided[n_str:]
    gw_out = _wgrad(cat, dh1b, "wgrad_out")
    (dag, g_cw, g_cb, g_lg, g_lb), (r_down,) = _bwd_conv(dcat, c1, ag, cw_f, conv_ln_g, conv_ln_b, [scatter(gw_down)])
    hosted = [[scatter(gw_out, gw_xq, gw_xk, gw_xv, gw_xo)], [], []]
    dqs, dkvs, landed = [], [], []
    for i, dil in enumerate(DILATIONS):
        (dq3, dkv3), got = _swa_bwd(qkv3[i], do3[i], lt3[i], ds3[i], f"swa_bwd_d{dil}", hosted[i])
        dqs.append(dq3)
        dkvs.append(dkv3)
        landed += got
    r_out, r_xq, r_xk, r_xv, r_xo = landed
    grad_x, dy, g_mix = _bwd_in(dqs, dkvs, dag, w_in_f, x2, norm_mix_g, dh1, rot)
    gw_in = _wgrad(xn, dy, "wgrad_in", b_blk=dy.shape[1] // 2)

    def widen(t):
        return jnp.pad(t, ((0, 0), (0, d - t.shape[1])))

    n_in = w_in.shape[2]
    small = jnp.concatenate([g_mix, g_x, g_mem, g_mlp, g_final, widen(g_cb), widen(g_lg), widen(g_lb),
                             g_cw.reshape(16, d)], axis=0)
    r_in, small_g = _exchange_call(
        [_Exchange([jnp.transpose(gw_in.reshape(d, N_DEV, n_in), (1, 0, 2))], gather=False),
         _Exchange([small], gather=True)], "scatter_w_in_gather_small")
    small_sum = _sum_slots(small_g, "sum_small_grads")
    loss = lax.psum(loss_part[0, 0], ("x", "y", "c"))

    res = {}

    def step(name, w, gsrc, m, v):
        shape = w.shape
        w2, m2, v2 = (t.reshape(-1, shape[-1]) for t in (w, m, v))
        res[name] = [t.reshape(shape) for t in _adamw(w2, gsrc, m2, v2, "adamw_" + name)]

    step("w_in", w_in, r_in, m_w_in, v_w_in)
    step("w_up", w_up, r_up, m_w_up, v_w_up)
    step("w_out", w_out, r_out, m_w_out, v_w_out)
    step("w_xq", w_xq, r_xq, m_w_xq, v_w_xq)
    step("w_xk", w_xk, r_xk, m_w_xk, v_w_xk)
    step("w_xv", w_xv, r_xv, m_w_xv, v_w_xv)
    step("w_xo", w_xo, r_xo, m_w_xo, v_w_xo)
    step("w_down", w_down, r_down, m_w_down, v_w_down)

    me = _dev_index((lax.axis_index("x"), lax.axis_index("y"), lax.axis_index("c")))
    n_cw = conv_w.shape[2]
    g_cw_full = small_sum[8:24].reshape(32, D_CONV)[:CONV_WIDTH]
    g_cw_mine = lax.dynamic_slice_in_dim(g_cw_full, me * n_cw, n_cw, axis=1)
    step("conv_w", conv_w, g_cw_mine, m_conv_w, v_conv_w)

    vec_names = ["norm_mix_g", "norm_x_g", "norm_mem_g", "norm_mlp_g", "norm_final_g", "conv_b", "conv_ln_g", "conv_ln_b"]
    vec_w = [norm_mix_g, norm_x_g, norm_mem_g, norm_mlp_g, gf, conv_b, conv_ln_g, conv_ln_b]
    vec_m = [m_norm_mix_g, m_norm_x_g, m_norm_mem_g, m_norm_mlp_g, m_norm_final_g[None, :], m_conv_b, m_conv_ln_g, m_conv_ln_b]
    vec_v = [v_norm_mix_g, v_norm_x_g, v_norm_mem_g, v_norm_mlp_g, v_norm_final_g[None, :], v_conv_b, v_conv_ln_g, v_conv_ln_b]

    def pack(ts):
        return jnp.concatenate([widen(t) for t in ts], axis=0)

    packed = _adamw(pack(vec_w), small_sum[0:8], pack(vec_m), pack(vec_v), "adamw_vectors")
    for i, name in enumerate(vec_names):
        width = vec_w[i].shape[1]
        shape = (width,) if name == "norm_final_g" else (1, width)
        res[name] = [t[i, :width].reshape(shape) for t in packed]

    order = ["norm_mix_g", "w_in", "conv_w", "conv_b", "conv_ln_g", "conv_ln_b", "w_out", "norm_x_g", "norm_mem_g",
             "w_xq", "w_xk", "w_xv", "w_xo", "norm_mlp_g", "w_up", "w_down", "norm_final_g"]
    out = [loss, grad_x[None]]
    for kind in range(4):
        out += [res[name][kind] for name in order]
    return tuple(out)
```

```python
import jax
import jax.numpy as jnp
from jax import lax
from jax.experimental import pallas as pl
from jax.experimental.pallas import tpu as pltpu

F32 = jnp.float32
BF16 = jnp.bfloat16

N_DEV = 8
EPS = 1e-6
NEG_INF = -1e30
ATT_HEADS = 8
HEAD_DIM = 64
D_ATT = ATT_HEADS * HEAD_DIM
D_CONV = 512
DILATIONS = (1, 4, 16)
HALF = 64
ROPE_THETA = 500000.0
ROT_DIM = HEAD_DIM // 4
CONV_WIDTH = 31
CONV_PAD = (CONV_WIDTH - 1) // 2
XATT_HEADS = 4
ADAM_LR = 0.001
ADAM_B1 = 0.9
ADAM_B2 = 0.999
ADAM_EPS = 1e-08
ADAM_WD = 0.01
ADAM_STEP = 10

LANES = 128
SUBLANES = 8
BF16_ROWS = 16
BQ = 128
WIN = BQ + 2 * HALF
MLP_SHARDS = 4
VMEM_LIMIT = 56 * 1024 * 1024
MESH = pl.DeviceIdType.MESH
ANY = pl.BlockSpec(memory_space=pl.ANY)

_NT = (((1,), (1,)), ((), ()))
_TN = (((0,), (0,)), ((), ()))


def _dot(a, b):
    return jnp.dot(a, b, preferred_element_type=F32)


def _dot_nt(a, b):
    return lax.dot_general(a, b, _NT, preferred_element_type=F32)


def _dot_tn(a, b):
    return lax.dot_general(a, b, _TN, preferred_element_type=F32)


def _params(*sem):
    return pltpu.CompilerParams(dimension_semantics=sem or None, vmem_limit_bytes=VMEM_LIMIT)


def _sigmoid(v):
    return 1.0 / (1.0 + jnp.exp(-v))


def _mean(v):
    return jnp.mean(v, axis=-1, keepdims=True)


def _rms_fwd(h, g):
    r = lax.rsqrt(_mean(h * h) + EPS)
    return h * r * g, r


def _rms_bwd(h, g, d_out):
    r = lax.rsqrt(_mean(h * h) + EPS)
    hn = h * r
    gd = d_out * g
    return r * (gd - hn * _mean(gd * hn)), d_out * hn


def _row_tile(s):
    return min(512, s)


def _full(shape):
    return pl.BlockSpec(shape, lambda *_: (0,) * len(shape))


def _mesh_pos():
    return lax.axis_index("x"), lax.axis_index("y"), lax.axis_index("c")


def _dev_index(p):
    return 4 * p[0] + 2 * p[1] + p[2]


class _Exchange:
    def __init__(self, arrays, gather):
        self.arrays, self.gather, self.n = list(arrays), gather, len(arrays)

    def out_shapes(self):
        return [jax.ShapeDtypeStruct(((N_DEV,) + a.shape) if self.gather else a.shape, a.dtype)
                for a in self.arrays]

    def sem_shapes(self):
        return [pltpu.SemaphoreType.DMA((7 * self.n,)), pltpu.SemaphoreType.DMA((7 * self.n,)),
                pltpu.SemaphoreType.DMA((self.n,))]

    def phases(self, x_refs, o_refs, send_sems, recv_sems, local_sems):
        n = self.n
        x, y, c = _mesh_pos()
        me, sibling = (x, y, c), (x, y, 1 - c)

        if self.gather:
            chips = [(1 - x, y), (x, 1 - y), (1 - x, 1 - y)]

            def copy(a, k, block, to, src=None):
                slot = o_refs[a].at[_dev_index(block)]
                return pltpu.make_async_remote_copy(
                    src_ref=slot if src is None else src, dst_ref=slot,
                    send_sem=send_sems.at[7 * a + k], recv_sem=recv_sems.at[7 * a + k],
                    device_id=to, device_id_type=MESH)

            def mine(a):
                return pltpu.make_async_copy(x_refs[a], o_refs[a].at[_dev_index(me)], local_sems.at[a])

            def first(a):
                return [copy(a, 0, me, sibling, src=x_refs[a])] + [
                    copy(a, 1 + j, me, (*chip, c), src=x_refs[a]) for j, chip in enumerate(chips)]

            def relayed(a, j):
                return copy(a, 4 + j, (*chips[j], c), sibling)

            def start():
                for a in range(n):
                    mine(a).start()
                    for cp in first(a):
                        cp.start()

            def relay():
                for j, chip in enumerate(chips):
                    for a in range(n):
                        copy(a, 1 + j, (*chip, c), me).wait_recv()
                        relayed(a, j).start()

            def finish():
                for a in range(n):
                    copy(a, 0, sibling, me).wait_recv()
                    for j, chip in enumerate(chips):
                        copy(a, 4 + j, (*chip, 1 - c), me).wait_recv()
                    for cp in first(a) + [relayed(a, j) for j in range(3)]:
                        cp.wait_send()
                    mine(a).wait()

            return start, relay, finish

        flips = [(dx, dy, dc) for dx in (0, 1) for dy in (0, 1) for dc in (0, 1)][1:]

        def peer(k):
            return tuple(1 - v if fl else v for v, fl in zip(me, flips[k]))

        def send(a, k):
            return pltpu.make_async_remote_copy(
                src_ref=x_refs[a].at[_dev_index(peer(k))], dst_ref=o_refs[a].at[_dev_index(me)],
                send_sem=send_sems.at[7 * a + k], recv_sem=recv_sems.at[7 * a + k],
                device_id=peer(k), device_id_type=MESH)

        def landed(a, k):
            slot = o_refs[a].at[_dev_index(peer(k))]
            return pltpu.make_async_remote_copy(
                src_ref=slot, dst_ref=slot, send_sem=send_sems.at[7 * a + k], recv_sem=recv_sems.at[7 * a + k],
                device_id=peer(k), device_id_type=MESH)

        def own(a):
            return pltpu.make_async_copy(x_refs[a].at[_dev_index(me)], o_refs[a].at[_dev_index(me)],
                                         local_sems.at[a])

        def start():
            for a in range(n):
                own(a).start()
            for k in range(7):
                for a in range(n):
                    send(a, k).start()

        def finish():
            for k in range(7):
                for a in range(n):
                    landed(a, k).wait_recv()
            for k in range(7):
                for a in range(n):
                    send(a, k).wait_send()
            for a in range(n):
                own(a).wait()

        return start, (lambda: None), finish


def _hosted_call(body, sides, *, name, grid, in_specs, out_specs, out_shape, scratch_shapes, args):
    n_in, n_out, ns = len(in_specs), len(out_specs), sum(s.n for s in sides)
    steps = 1
    for g in grid:
        steps *= g

    def wrapped(*refs):
        ins, s_ins = refs[:n_in], refs[n_in:n_in + ns]
        outs = refs[n_in + ns:n_in + ns + n_out]
        s_outs = refs[n_in + ns + n_out:n_in + 2 * ns + n_out]
        rest = refs[n_in + 2 * ns + n_out:]
        scratch, sems = rest[:len(rest) - 3 * len(sides)], rest[len(rest) - 3 * len(sides):]
        phases, off = [], 0
        for i, s in enumerate(sides):
            phases.append(s.phases(s_ins[off:off + s.n], s_outs[off:off + s.n], *sems[3 * i:3 * i + 3]))
            off += s.n
        lin = 0
        for ax, g in enumerate(grid):
            lin = lin * g + pl.program_id(ax)

        if sides:
            @pl.when(lin == 0)
            def _():
                for start, _, _ in phases:
                    start()

        body(*ins, *outs, *scratch)

        if sides:
            @pl.when(lin == min((3 * steps) // 4, steps - 1))
            def _():
                for _, relay, _ in phases:
                    relay()

            @pl.when(lin == steps - 1)
            def _():
                for _, _, finish in phases:
                    finish()

    res = pl.pallas_call(
        wrapped, name=name, grid=grid,
        in_specs=list(in_specs) + [ANY] * ns, out_specs=list(out_specs) + [ANY] * ns,
        out_shape=list(out_shape) + [sh for s in sides for sh in s.out_shapes()],
        scratch_shapes=list(scratch_shapes) + [sh for s in sides for sh in s.sem_shapes()],
        compiler_params=_params(*(("arbitrary",) * len(grid))),
    )(*args, *[a for s in sides for a in s.arrays])
    return res[:n_out], res[n_out:]


def _exchange_call(sides, name):
    ns = sum(s.n for s in sides)

    def body(*refs):
        x_refs, o_refs, sems = refs[:ns], refs[ns:2 * ns], refs[2 * ns:]
        phases, off = [], 0
        for i, s in enumerate(sides):
            phases.append(s.phases(x_refs[off:off + s.n], o_refs[off:off + s.n], *sems[3 * i:3 * i + 3]))
            off += s.n
        for step in range(3):
            for ph in phases:
                ph[step]()

    return pl.pallas_call(
        body, name=name,
        out_shape=[sh for s in sides for sh in s.out_shapes()],
        in_specs=[ANY] * ns, out_specs=[ANY] * ns,
        scratch_shapes=[sh for s in sides for sh in s.sem_shapes()],
    )(*[a for s in sides for a in s.arrays])


def _rotary_tables(s):
    half = ROT_DIM // 2
    freqs = ROPE_THETA ** (-jnp.arange(0, ROT_DIM, 2, dtype=F32) / ROT_DIM)
    ang = jnp.arange(s, dtype=F32)[:, None] * freqs[None, :]
    cos, sin = jnp.cos(ang), jnp.sin(ang)
    one = jnp.ones((s, HEAD_DIM - ROT_DIM), F32)
    zero = jnp.zeros((s, HEAD_DIM - ROT_DIM), F32)
    zh = jnp.zeros((s, half), F32)
    c64 = jnp.concatenate([cos, cos, one], axis=1)
    a64 = jnp.concatenate([-sin, zh, zero], axis=1)
    b64 = jnp.concatenate([zh, sin, zero], axis=1)
    return tuple(jnp.tile(t, (1, LANES // HEAD_DIM)) for t in (c64, a64, b64))


def _strided_spec(dil, tm, width):
    return pl.BlockSpec((dil, tm // dil, width), lambda i: (0, i, 0))


def _strided_shape(dil, s, width, dtype):
    return jax.ShapeDtypeStruct((dil, s // dil, width), dtype)


def _lane_scratch(tm, width):
    return pltpu.VMEM((width // LANES, tm, LANES), F32)


def _store_blocks(buf, v):
    for cb in range(buf.shape[0]):
        buf[cb] = v[:, cb * LANES:(cb + 1) * LANES]


def _load_blocks(buf):
    return jnp.concatenate([buf[cb] for cb in range(buf.shape[0])], axis=1)


def _write_strided(buf, dst_ref):
    dil, rows, _ = dst_ref.shape
    for r in range(dil):
        for cb in range(buf.shape[0]):
            dst_ref[r, :, cb * LANES:(cb + 1) * LANES] = buf[cb, pl.ds(r, rows, stride=dil), :].astype(dst_ref.dtype)


def _read_strided(src_ref, buf, add=False):
    dil, rows, _ = src_ref.shape
    for r in range(dil):
        for cb in range(buf.shape[0]):
            v = src_ref[r, :, cb * LANES:(cb + 1) * LANES].astype(F32)
            if add:
                v = v + buf[cb, pl.ds(r, rows, stride=dil), :]
            buf[cb, pl.ds(r, rows, stride=dil), :] = v


def _fwd_in(x, g, w_in, rot):
    s, d = x.shape
    n = w_in.shape[1]
    tm = _row_tile(s)
    dils = DILATIONS[1:]

    def body(x_ref, g_ref, w_ref, c_ref, a_ref, b_ref, xn_ref, qkv_ref, ag_ref, *rest):
        strided, ybuf = rest[:len(dils)], rest[len(dils)]
        xn = _rms_fwd(x_ref[...], g_ref[...])[0].astype(BF16)
        xn_ref[...] = xn
        y = _dot(xn, w_ref[...])
        reps = (1, D_ATT // LANES)
        cc, aa, bb = jnp.tile(c_ref[...], reps), jnp.tile(a_ref[...], reps), jnp.tile(b_ref[...], reps)
        parts = []
        for blk in range(2):
            t = y[:, blk * D_ATT:(blk + 1) * D_ATT]
            parts.append(t * cc + pltpu.roll(t, D_ATT - ROT_DIM // 2, 1) * aa + pltpu.roll(t, ROT_DIM // 2, 1) * bb)
        qkv = jnp.concatenate(parts + [y[:, 2 * D_ATT:3 * D_ATT]], axis=1)
        qkv_ref[...] = qkv.astype(BF16)
        _store_blocks(ybuf, qkv)
        for ref in strided:
            _write_strided(ybuf, ref)
        ag_ref[...] = y[:, 3 * D_ATT:].astype(BF16)

    row = lambda w: pl.BlockSpec((tm, w), lambda i: (i, 0))
    return pl.pallas_call(
        body, name="fwd_in", grid=(s // tm,),
        in_specs=[row(d), _full((1, d)), _full((d, n)), row(LANES), row(LANES), row(LANES)],
        out_specs=[row(d), row(3 * D_ATT), row(2 * D_CONV)] + [_strided_spec(dil, tm, 3 * D_ATT) for dil in dils],
        out_shape=[jax.ShapeDtypeStruct((s, d), BF16), jax.ShapeDtypeStruct((s, 3 * D_ATT), BF16),
                   jax.ShapeDtypeStruct((s, 2 * D_CONV), BF16)]
        + [_strided_shape(dil, s, 3 * D_ATT, BF16) for dil in dils],
        scratch_shapes=[_lane_scratch(tm, 3 * D_ATT)],
        compiler_params=_params("parallel"),
    )(x, g, w_in, *rot)


def _win_in_specs(length, col, width):
    per = BQ // HALF
    last = length // HALF - 1
    return [
        pl.BlockSpec((None, HALF, width), lambda r, j: (r, jnp.maximum(j * per - 1, 0), col)),
        pl.BlockSpec((None, BQ, width), lambda r, j: (r, j, col)),
        pl.BlockSpec((None, HALF, width), lambda r, j: (r, jnp.minimum(j * per + per, last), col)),
    ]


def _fill_window(win, prev_ref, main_ref, next_ref):
    win[0:HALF] = prev_ref[...]
    win[HALF:HALF + BQ] = main_ref[...]
    win[HALF + BQ:] = next_ref[...]


def _band_bias():
    blk = jnp.arange(2 * BQ)[:, None] & (BQ - 1)
    win = jnp.arange(WIN)[None, :]
    return jnp.where(jnp.abs(win - HALF - blk) <= HALF, 0.0, NEG_INF).astype(F32)


def _window_bias(band_ref, j, length):
    pos = j * BQ - HALF + lax.broadcasted_iota(jnp.int32, (1, WIN), 1)
    return band_ref[...] + jnp.where((pos >= 0) & (pos < length), 0.0, NEG_INF)


def _first_head():
    return lax.broadcasted_iota(jnp.int32, (1, LANES), 1) < HEAD_DIM


def _stack_heads(v, first):
    zero = jnp.zeros((), v.dtype)
    return jnp.concatenate([jnp.where(first, v, zero), jnp.where(first, zero, v)], axis=0)


def _unstack_heads(v, first):
    rows = v.shape[0] // 2
    return jnp.where(first, v[:rows], v[rows:])


def _stack_cols(v, first):
    top = jnp.max(jnp.where(first, v, -jnp.inf), axis=-1, keepdims=True)
    bot = jnp.max(jnp.where(first, -jnp.inf, v), axis=-1, keepdims=True)
    return jnp.concatenate([top, bot], axis=0)


def _swa_fwd(qkv3, band, name, sides=()):
    dil, length, _ = qkv3.shape
    scale = HEAD_DIM ** -0.5

    def body(q_ref, kp, km, kn, vp, vm, vn, band_ref, o_ref, lse_ref, kwin, vwin):
        j = pl.program_id(1)
        _fill_window(kwin, kp, km, kn)
        _fill_window(vwin, vp, vm, vn)
        bias = _window_bias(band_ref, j, length)
        first = _first_head()
        for pr in range(D_ATT // LANES):
            cols = slice(pr * LANES, (pr + 1) * LANES)
            qs = _stack_heads(q_ref[:, cols] * scale, first)
            sc = _dot_nt(qs, kwin[:, cols]) + bias
            m = jnp.max(sc, axis=-1, keepdims=True)
            p = jnp.exp(sc - m)
            den = jnp.sum(p, axis=-1, keepdims=True)
            pv = _dot(p.astype(BF16), vwin[:, cols]) * (1.0 / den)
            o_ref[:, cols] = _unstack_heads(pv, first).astype(BF16)
            lse_ref[:, cols] = _unstack_heads(jnp.broadcast_to(m + jnp.log(den), (2 * BQ, LANES)), first)

    blk = lambda w: pl.BlockSpec((None, BQ, w), lambda r, j: (r, j, 0))
    return _hosted_call(
        body, sides, name=name, grid=(dil, length // BQ),
        in_specs=[pl.BlockSpec((None, BQ, D_ATT), lambda r, j: (r, j, 0))]
        + _win_in_specs(length, 1, D_ATT) + _win_in_specs(length, 2, D_ATT) + [_full((2 * BQ, WIN))],
        out_specs=[blk(D_ATT), blk(D_ATT)],
        out_shape=[jax.ShapeDtypeStruct((dil, length, D_ATT), BF16),
                   jax.ShapeDtypeStruct((dil, length, D_ATT), F32)],
        scratch_shapes=[pltpu.VMEM((WIN, D_ATT), BF16), pltpu.VMEM((WIN, D_ATT), BF16)],
        args=[qkv3] * 7 + [band])


def _glu(v):
    return v[:, :D_CONV].astype(F32) * _sigmoid(v[:, D_CONV:].astype(F32))


def _halo_specs(tm, width, col=0):
    per = tm // BF16_ROWS
    return lambda nblk: [
        pl.BlockSpec((BF16_ROWS, width), lambda i: (jnp.maximum(i * per - 1, 0), col)),
        pl.BlockSpec((tm, width), lambda i: (i, col)),
        pl.BlockSpec((BF16_ROWS, width), lambda i: (jnp.minimum(i * per + per, nblk - 1), col)),
    ]


def _fill_halo(buf, i, ntiles, tm, prev, main, nxt):
    buf[0:BF16_ROWS] = jnp.where(i == 0, 0.0, prev)
    buf[BF16_ROWS:BF16_ROWS + tm] = main
    buf[BF16_ROWS + tm:] = jnp.where(i == ntiles - 1, 0.0, nxt)


def _halo_scratch(tm):
    return [pltpu.VMEM((tm + 2 * BF16_ROWS, D_CONV), F32),
            pltpu.VMEM((SUBLANES - 1, tm + 2 * BF16_ROWS - SUBLANES, D_CONV), F32)]


def _shift_copies(buf, shifted, tm):
    rows = tm + 2 * BF16_ROWS - SUBLANES
    for b in range(1, SUBLANES):
        shifted[b - 1] = buf[pl.ds(b, rows), :]


def _tap(buf, shifted, off, tm):
    a, b = divmod(off, SUBLANES)
    if b == 0:
        return buf[pl.ds(SUBLANES * a, tm), :]
    return shifted[b - 1, pl.ds(SUBLANES * a, tm), :]


def _fwd_conv(ag, cw, cb, lg, lb):
    s = ag.shape[0]
    tm = _row_tile(s)
    nt = s // tm

    def body(agp, agm, agn, cw_ref, cb_ref, lg_ref, lb_ref, c1_ref, co_ref, ubuf, ush):
        i = pl.program_id(0)
        _fill_halo(ubuf, i, nt, tm, _glu(agp[...]), _glu(agm[...]), _glu(agn[...]))
        _shift_copies(ubuf, ush, tm)
        w = cw_ref[...]
        acc = jnp.zeros((tm, D_CONV), F32)
        for k in range(CONV_WIDTH):
            acc = acc + _tap(ubuf, ush, k + 1, tm) * w[k:k + 1, :]
        acc = acc + cb_ref[...]
        c1_ref[...] = acc
        xc = acc - _mean(acc)
        ln = xc * lax.rsqrt(_mean(xc * xc) + EPS) * lg_ref[...] + lb_ref[...]
        co_ref[...] = (ln * _sigmoid(ln)).astype(BF16)

    vec = _full((1, D_CONV))
    return pl.pallas_call(
        body, name="fwd_conv", grid=(nt,),
        in_specs=_halo_specs(tm, 2 * D_CONV)(s // BF16_ROWS) + [_full((32, D_CONV)), vec, vec, vec],
        out_specs=[pl.BlockSpec((tm, D_CONV), lambda i: (i, 0))] * 2,
        out_shape=[jax.ShapeDtypeStruct((s, D_CONV), F32), jax.ShapeDtypeStruct((s, D_CONV), BF16)],
        scratch_shapes=_halo_scratch(tm),
        compiler_params=_params("parallel"),
    )(ag, ag, ag, cw, cb, lg, lb)


def _fwd_mix_out(outs, lses, conv_out, x, w_out):
    s, d = x.shape
    tm = _row_tile(s)
    dils = DILATIONS[1:]
    nd = len(dils)

    def body(o1, *rest):
        o_str, l1, l_str = rest[:nd], rest[nd], rest[nd + 1:2 * nd + 1]
        co, x_ref, w_ref, h_ref, cat_ref, lt_ref = rest[2 * nd + 1:2 * nd + 7]
        lt_str = rest[2 * nd + 7:3 * nd + 7]
        obufs, lbufs, ltbuf = rest[3 * nd + 7:4 * nd + 7], rest[4 * nd + 7:5 * nd + 7], rest[5 * nd + 7]
        for ref, buf in zip(o_str + l_str, obufs + lbufs):
            _read_strided(ref, buf)
        lse = [l1[0]] + [_load_blocks(buf) for buf in lbufs]
        out = [o1[0].astype(F32)] + [_load_blocks(buf) for buf in obufs]
        m = lse[0]
        for v in lse[1:]:
            m = jnp.maximum(m, v)
        e = [jnp.exp(v - m) for v in lse]
        den = sum(e[1:], e[0])
        att = (sum((ev * ov for ev, ov in zip(e[1:], out[1:])), e[0] * out[0]) / den).astype(BF16)
        lt = m + jnp.log(den)
        lt_ref[...] = lt
        _store_blocks(ltbuf, lt)
        for ref in lt_str:
            _write_strided(ltbuf, ref)
        cat_ref[:, :D_ATT] = att
        cat_ref[:, D_ATT:] = co[...]
        h_ref[...] = x_ref[...] + _dot(att, w_ref[:D_ATT, :]) + _dot(co[...], w_ref[D_ATT:, :])

    row = lambda w: pl.BlockSpec((tm, w), lambda i: (i, 0))
    nat = pl.BlockSpec((1, tm, D_ATT), lambda i: (0, i, 0))
    strided = [_strided_spec(dil, tm, D_ATT) for dil in dils]
    return pl.pallas_call(
        body, name="fwd_mix_out", grid=(s // tm,),
        in_specs=[nat] + strided + [nat] + strided + [row(D_ATT), row(d), _full((d, d))],
        out_specs=[row(d), row(d), row(D_ATT)] + strided,
        out_shape=[jax.ShapeDtypeStruct((s, d), F32), jax.ShapeDtypeStruct((s, d), BF16),
                   jax.ShapeDtypeStruct((s, D_ATT), F32)] + [_strided_shape(dil, s, D_ATT, F32) for dil in dils],
        scratch_shapes=[_lane_scratch(tm, D_ATT)] * (2 * nd + 1),
        compiler_params=_params("parallel"),
    )(*outs, *lses, conv_out, x, w_out)


def _fwd_mem(mem, g, wk, wv):
    m, d = mem.shape

    def body(mem_ref, g_ref, wk_ref, wv_ref, mn_ref, xk_ref, xv_ref):
        mn = _rms_fwd(mem_ref[...], g_ref[...])[0].astype(BF16)
        mn_ref[...] = mn
        xk_ref[...] = _dot(mn, wk_ref[...]).astype(BF16)
        xv_ref[...] = _dot(mn, wv_ref[...]).astype(BF16)

    return pl.pallas_call(
        body, name="fwd_mem",
        out_shape=[jax.ShapeDtypeStruct((m, d), BF16)] * 3,
        compiler_params=_params(),
    )(mem, g, wk, wv)


def _xatt_probs(q, k):
    sc = _dot_nt(q, k) * (q.shape[1] ** -0.5)
    p = jnp.exp(sc - jnp.max(sc, axis=-1, keepdims=True))
    return p / jnp.sum(p, axis=-1, keepdims=True)


def _fwd_xattn(h1, g, wq, xk, xv, wo):
    s, d = h1.shape
    m = xk.shape[0]
    tm = _row_tile(s)
    hd = d // XATT_HEADS

    def body(h_ref, g_ref, wq_ref, xk_ref, xv_ref, wo_ref, h2_ref, hn_ref, xq_ref, xo_ref):
        h = h_ref[...]
        hn = _rms_fwd(h, g_ref[...])[0].astype(BF16)
        hn_ref[...] = hn
        xq = _dot(hn, wq_ref[...]).astype(BF16)
        xq_ref[...] = xq
        for i in range(XATT_HEADS):
            cols = slice(i * hd, (i + 1) * hd)
            pr = _xatt_probs(xq[:, cols], xk_ref[:, cols])
            xo_ref[:, cols] = _dot(pr.astype(BF16), xv_ref[:, cols]).astype(BF16)
        h2_ref[...] = h + _dot(xo_ref[...], wo_ref[...])

    row = pl.BlockSpec((tm, d), lambda i: (i, 0))
    return pl.pallas_call(
        body, name="fwd_xattn", grid=(s // tm,),
        in_specs=[row, _full((1, d)), _full((d, d)), _full((m, d)), _full((m, d)), _full((d, d))],
        out_specs=[row] * 4,
        out_shape=[jax.ShapeDtypeStruct((s, d), F32)] + [jax.ShapeDtypeStruct((s, d), BF16)] * 3,
        compiler_params=_params("parallel"),
    )(h1, g, wq, xk, xv, wo)


def _fwd_mlp_loss(h2, g, w_up, w_down, gf, target):
    s, d = h2.shape
    nsh, _, f = w_up.shape
    fb = MLP_SHARDS * f
    nb = nsh // MLP_SHARDS
    tm = _row_tile(s)

    def body(h_ref, g_ref, wu_ref, wd_ref, gf_ref, t_ref,
             hn_ref, act_ref, dh_ref, dhb_ref, loss_ref, ggf_ref, acc):
        i, k = pl.program_id(0), pl.program_id(1)

        @pl.when(k == 0)
        def _():
            hn_ref[...] = _rms_fwd(h_ref[...], g_ref[...])[0].astype(BF16)
            acc[...] = jnp.zeros_like(acc)

        @pl.when((i == 0) & (k == 0))
        def _():
            loss_ref[...] = jnp.zeros_like(loss_ref)
            ggf_ref[...] = jnp.zeros_like(ggf_ref)

        hn = hn_ref[...]
        for c in range(MLP_SHARDS):
            act_ref[:, c * f:(c + 1) * f] = jnp.square(jnp.maximum(_dot(hn, wu_ref[c]), 0.0)).astype(BF16)
        acc[...] += _dot(act_ref[...], wd_ref[...])

        @pl.when(k == nb - 1)
        def _():
            h3 = h_ref[...] + acc[...]
            gfv = gf_ref[...]
            y, _ = _rms_fwd(h3, gfv)
            err = y - t_ref[...]
            loss_ref[...] += 0.5 * jnp.sum(_mean(err * err))
            dh3, gg = _rms_bwd(h3, gfv, err * (1.0 / d))
            ggf_ref[...] += jnp.sum(gg, axis=0, keepdims=True)
            dh_ref[...] = dh3
            dhb_ref[...] = dh3.astype(BF16)

    row = pl.BlockSpec((tm, d), lambda i, k: (i, 0))
    return pl.pallas_call(
        body, name="fwd_mlp_loss", grid=(s // tm, nb),
        in_specs=[row, _full((1, d)),
                  pl.BlockSpec((MLP_SHARDS, d, f), lambda i, k: (k, 0, 0)),
                  pl.BlockSpec((fb, d), lambda i, k: (k, 0)),
                  _full((1, d)), row],
        out_specs=[row, pl.BlockSpec((tm, fb), lambda i, k: (i, k)), row, row,
                   _full((1, LANES)), _full((1, d))],
        out_shape=[jax.ShapeDtypeStruct((s, d), BF16), jax.ShapeDtypeStruct((s, nsh * f), BF16),
                   jax.ShapeDtypeStruct((s, d), F32), jax.ShapeDtypeStruct((s, d), BF16),
                   jax.ShapeDtypeStruct((1, LANES), F32), jax.ShapeDtypeStruct((1, d), F32)],
        scratch_shapes=[pltpu.VMEM((tm, d), F32)],
        compiler_params=_params("arbitrary", "arbitrary"),
    )(h2, g, w_up, w_down, gf, target)


def _bwd_mlp(dh3, dh3b, act, w_up_t, w_down, h2, g):
    s, d = h2.shape
    ff = w_down.shape[0]
    fb = MLP_SHARDS * (ff // N_DEV)
    nb = ff // fb
    tm = _row_tile(s)

    def body(dh_ref, dhb_ref, act_ref, wut_ref, wd_ref, h_ref, g_ref,
             du_ref, dh2_ref, dh2b_ref, gg_ref, acc):
        i, k = pl.program_id(0), pl.program_id(1)

        @pl.when(k == 0)
        def _():
            acc[...] = jnp.zeros_like(acc)

        @pl.when((i == 0) & (k == 0))
        def _():
            gg_ref[...] = jnp.zeros_like(gg_ref)

        dact = _dot_nt(dhb_ref[...], wd_ref[...])
        du_ref[...] = (dact * (2.0 * jnp.sqrt(act_ref[...].astype(F32)))).astype(BF16)
        acc[...] += _dot(du_ref[...], wut_ref[...])

        @pl.when(k == nb - 1)
        def _():
            dh, gg = _rms_bwd(h_ref[...], g_ref[...], acc[...])
            gg_ref[...] += jnp.sum(gg, axis=0, keepdims=True)
            dh2 = dh_ref[...] + dh
            dh2_ref[...] = dh2
            dh2b_ref[...] = dh2.astype(BF16)

    row = pl.BlockSpec((tm, d), lambda i, k: (i, 0))
    col = pl.BlockSpec((tm, fb), lambda i, k: (i, k))
    wblk = pl.BlockSpec((fb, d), lambda i, k: (k, 0))
    return pl.pallas_call(
        body, name="bwd_mlp", grid=(s // tm, nb),
        in_specs=[row, row, col, wblk, wblk, row, _full((1, d))],
        out_specs=[col, row, row, _full((1, d))],
        out_shape=[jax.ShapeDtypeStruct((s, ff), BF16), jax.ShapeDtypeStruct((s, d), F32),
                   jax.ShapeDtypeStruct((s, d), BF16), jax.ShapeDtypeStruct((1, d), F32)],
        scratch_shapes=[pltpu.VMEM((tm, d), F32)],
        compiler_params=_params("arbitrary", "arbitrary"),
    )(dh3, dh3b, act, w_up_t, w_down, h2, g)


def _bwd_xattn(dh2, dh2b, h1, g, xq, xk, xv, wq, wo, sides=()):
    s, d = h1.shape
    m = xk.shape[0]
    tm = _row_tile(s)
    hd = d // XATT_HEADS
    scale = hd ** -0.5

    def body(dh_ref, dhb_ref, h_ref, g_ref, xq_ref, xk_ref, xv_ref, wq_ref, wo_ref,
             dh1_ref, dh1b_ref, dxq_ref, dxk_ref, dxv_ref, gg_ref):
        @pl.when(pl.program_id(0) == 0)
        def _():
            dxk_ref[...] = jnp.zeros_like(dxk_ref)
            dxv_ref[...] = jnp.zeros_like(dxv_ref)
            gg_ref[...] = jnp.zeros_like(gg_ref)

        dxo = _dot_nt(dhb_ref[...], wo_ref[...])
        for i in range(XATT_HEADS):
            cols = slice(i * hd, (i + 1) * hd)
            q, k, v = xq_ref[:, cols], xk_ref[:, cols], xv_ref[:, cols]
            pr = _xatt_probs(q, k)
            dxo_h = dxo[:, cols].astype(BF16)
            dpr = _dot_nt(dxo_h, v)
            dsc = (pr * (dpr - jnp.sum(dpr * pr, axis=-1, keepdims=True)) * scale).astype(BF16)
            dxq_ref[:, cols] = _dot(dsc, k).astype(BF16)
            dxk_ref[:, cols] += _dot_tn(dsc, q)
            dxv_ref[:, cols] += _dot_tn(pr.astype(BF16), dxo_h)
        dh, gg = _rms_bwd(h_ref[...], g_ref[...], _dot_nt(dxq_ref[...], wq_ref[...]))
        gg_ref[...] += jnp.sum(gg, axis=0, keepdims=True)
        dh1 = dh_ref[...] + dh
        dh1_ref[...] = dh1
        dh1b_ref[...] = dh1.astype(BF16)

    row = pl.BlockSpec((tm, d), lambda i: (i, 0))
    return _hosted_call(
        body, sides, name="bwd_xattn", grid=(s // tm,),
        in_specs=[row, row, row, _full((1, d)), row, _full((m, d)), _full((m, d)), _full((d, d)), _full((d, d))],
        out_specs=[row, row, row, _full((m, d)), _full((m, d)), _full((1, d))],
        out_shape=[jax.ShapeDtypeStruct((s, d), F32), jax.ShapeDtypeStruct((s, d), BF16),
                   jax.ShapeDtypeStruct((s, d), BF16), jax.ShapeDtypeStruct((m, d), F32),
                   jax.ShapeDtypeStruct((m, d), F32), jax.ShapeDtypeStruct((1, d), F32)],
        scratch_shapes=[],
        args=[dh2, dh2b, h1, g, xq, xk, xv, wq, wo])


def _bwd_mem(mem, g, mn, dxk, dxv, wk, wv):
    m, d = mem.shape

    def body(mem_ref, g_ref, mn_ref, dxk_ref, dxv_ref, wk_ref, wv_ref, gk_ref, gv_ref, gg_ref):
        dk, dv = dxk_ref[...].astype(BF16), dxv_ref[...].astype(BF16)
        gk_ref[...] = _dot_tn(mn_ref[...], dk).astype(BF16)
        gv_ref[...] = _dot_tn(mn_ref[...], dv).astype(BF16)
        dmn = _dot_nt(dk, wk_ref[...]) + _dot_nt(dv, wv_ref[...])
        _, gg = _rms_bwd(mem_ref[...], g_ref[...], dmn)
        gg_ref[...] = jnp.sum(gg, axis=0, keepdims=True)

    return pl.pallas_call(
        body, name="bwd_mem",
        out_shape=[jax.ShapeDtypeStruct((d, d), BF16), jax.ShapeDtypeStruct((d, d), BF16),
                   jax.ShapeDtypeStruct((1, d), F32)],
        compiler_params=_params(),
    )(mem, g, mn, dxk, dxv, wk, wv)


def _bwd_mix_out(dh1b, w_out, cat, head_ones):
    s, d = dh1b.shape
    tm = _row_tile(s)
    dils = DILATIONS[1:]
    nd = len(dils)

    def body(dh_ref, w_ref, cat_ref, ones_ref, dcat_ref, dsum_ref, *rest):
        da_str, ds_str, dbuf, sbuf = rest[:nd], rest[nd:2 * nd], rest[2 * nd], rest[2 * nd + 1]
        dcat = _dot_nt(dh_ref[...], w_ref[...])
        dcat_ref[...] = dcat.astype(BF16)
        datt = dcat[:, :D_ATT]
        prod = datt * cat_ref[...].astype(F32)
        hi = prod.astype(BF16)
        lo = (prod - hi.astype(F32)).astype(BF16)
        dsum = _dot(hi, ones_ref[...]) + _dot(lo, ones_ref[...])
        dsum_ref[...] = dsum
        _store_blocks(dbuf, datt)
        _store_blocks(sbuf, dsum)
        for da_ref, ds_ref in zip(da_str, ds_str):
            _write_strided(dbuf, da_ref)
            _write_strided(sbuf, ds_ref)

    row = lambda w: pl.BlockSpec((tm, w), lambda i: (i, 0))
    strided = [_strided_spec(dil, tm, D_ATT) for dil in dils]
    return pl.pallas_call(
        body, name="bwd_mix_out", grid=(s // tm,),
        in_specs=[row(d), _full((d, d)), row(D_ATT), _full((D_ATT, D_ATT))],
        out_specs=[row(d), row(D_ATT)] + strided + strided,
        out_shape=[jax.ShapeDtypeStruct((s, d), BF16), jax.ShapeDtypeStruct((s, D_ATT), F32)]
        + [_strided_shape(dil, s, D_ATT, BF16) for dil in dils]
        + [_strided_shape(dil, s, D_ATT, F32) for dil in dils],
        scratch_shapes=[_lane_scratch(tm, D_ATT)] * 2,
        compiler_params=_params("parallel"),
    )(dh1b, w_out, cat, head_ones)


def _bwd_conv(dcat, c1, ag, cw, lg, lb, sides=()):
    s = ag.shape[0]
    tm = _row_tile(s)
    nt = s // tm

    def body(dp, dm, dn, cp, cm, cn, agp, agm, agn, cw_ref, lg_ref, lb_ref,
             dag_ref, gcw_ref, gcb_ref, glg_ref, glb_ref, ubuf, ush, dbuf, dsh):
        i = pl.program_id(0)

        @pl.when(i == 0)
        def _():
            gcw_ref[...] = jnp.zeros_like(gcw_ref)
            gcb_ref[...] = jnp.zeros_like(gcb_ref)
            glg_ref[...] = jnp.zeros_like(glg_ref)
            glb_ref[...] = jnp.zeros_like(glb_ref)

        lgv, lbv = lg_ref[...], lb_ref[...]

        def norm_bwd(dco, c1v):
            xc = c1v - _mean(c1v)
            rs = lax.rsqrt(_mean(xc * xc) + EPS)
            z = xc * rs
            ln = z * lgv + lbv
            sg = _sigmoid(ln)
            dln = dco.astype(F32) * (sg * (1.0 + ln * (1.0 - sg)))
            dz = dln * lgv
            return rs * (dz - _mean(dz) - z * _mean(dz * z)), dln, z

        dc_m, dln, z = norm_bwd(dm[...], cm[...])
        glg_ref[...] += jnp.sum(dln * z, axis=0, keepdims=True)
        glb_ref[...] += jnp.sum(dln, axis=0, keepdims=True)
        gcb_ref[...] += jnp.sum(dc_m, axis=0, keepdims=True)
        _fill_halo(dbuf, i, nt, tm, norm_bwd(dp[...], cp[...])[0], dc_m, norm_bwd(dn[...], cn[...])[0])
        _fill_halo(ubuf, i, nt, tm, _glu(agp[...]), _glu(agm[...]), _glu(agn[...]))
        _shift_copies(dbuf, dsh, tm)
        _shift_copies(ubuf, ush, tm)

        w = cw_ref[...]
        tap = lax.broadcasted_iota(jnp.int32, (32, D_CONV), 0)
        du = jnp.zeros((tm, D_CONV), F32)
        gcw = jnp.zeros((32, D_CONV), F32)
        for k in range(CONV_WIDTH):
            du = du + _tap(dbuf, dsh, CONV_WIDTH - k, tm) * w[k:k + 1, :]
            gk = jnp.sum(dc_m * _tap(ubuf, ush, k + 1, tm), axis=0, keepdims=True)
            gcw = jnp.where(tap == k, gk, gcw)
        gcw_ref[...] += gcw
        a = agm[:, :D_CONV].astype(F32)
        sg = _sigmoid(agm[:, D_CONV:].astype(F32))
        dag_ref[:, :D_CONV] = (du * sg).astype(BF16)
        dag_ref[:, D_CONV:] = (du * a * sg * (1.0 - sg)).astype(BF16)

    vec = _full((1, D_CONV))
    nblk = s // BF16_ROWS
    return _hosted_call(
        body, sides, name="bwd_conv", grid=(nt,),
        in_specs=_halo_specs(tm, D_CONV, 1)(nblk) + _halo_specs(tm, D_CONV)(nblk) + _halo_specs(tm, 2 * D_CONV)(nblk)
        + [_full((32, D_CONV)), vec, vec],
        out_specs=[pl.BlockSpec((tm, 2 * D_CONV), lambda i: (i, 0)), _full((32, D_CONV)), vec, vec, vec],
        out_shape=[jax.ShapeDtypeStruct((s, 2 * D_CONV), BF16), jax.ShapeDtypeStruct((32, D_CONV), F32)]
        + [jax.ShapeDtypeStruct((1, D_CONV), F32)] * 3,
        scratch_shapes=_halo_scratch(tm) + _halo_scratch(tm),
        args=[dcat, dcat, dcat, c1, c1, c1, ag, ag, ag, cw, lg, lb])


def _swa_bwd(qkv3, do3, lt3, ds3, band, name, sides=()):
    dil, length, _ = qkv3.shape
    nb = length // BQ
    scale = HEAD_DIM ** -0.5
    assert WIN == 2 * BQ and BQ == 2 * HALF

    def body(q_ref, kp, km, kn, vp, vm, vn, do_ref, l_ref, s_ref, band_ref, dq_ref, dkv_ref, kwin, vwin, pend, keep):
        j = pl.program_id(1)

        @pl.when(j == 0)
        def _():
            pend[...] = jnp.zeros_like(pend)
            keep[...] = jnp.zeros_like(keep)

        @pl.when(j < nb)
        def _():
            _fill_window(kwin, kp, km, kn)
            _fill_window(vwin, vp, vm, vn)
            bias = _window_bias(band_ref, j, length)
            first = _first_head()
            for pr in range(D_ATT // LANES):
                cols = slice(pr * LANES, (pr + 1) * LANES)
                qs = _stack_heads(q_ref[:, cols] * scale, first)
                dos = _stack_heads(do_ref[:, cols], first)
                kw, vw = kwin[:, cols], vwin[:, cols]
                p = jnp.exp(_dot_nt(qs, kw) + (bias - _stack_cols(l_ref[:, cols], first)))
                dp = _dot_nt(dos, vw)
                dsc = (p * (dp - _stack_cols(s_ref[:, cols], first))).astype(BF16)
                dq_ref[:, cols] = _unstack_heads(_dot(dsc, kw * scale), first).astype(BF16)
                for part, at in ((_dot_tn(dsc, qs), pr * LANES), (_dot_tn(p.astype(BF16), dos), D_ATT + pr * LANES)):
                    at = slice(at, at + LANES)
                    dkv_ref[:HALF, at] = keep[:, at].astype(BF16)
                    dkv_ref[HALF:, at] = (pend[:HALF, at] + part[:HALF]).astype(BF16)
                    keep[:, at] = pend[HALF:, at] + part[HALF:BQ]
                    pend[:, at] = part[BQ:]

        @pl.when(j == nb)
        def _():
            dkv_ref[:HALF] = keep[...].astype(BF16)
            dkv_ref[HALF:] = pend[:HALF].astype(BF16)

    def clamp(idx):
        return lambda r, j: idx(r, jnp.minimum(j, nb - 1))

    main = pl.BlockSpec((None, BQ, D_ATT), clamp(lambda r, j: (r, j, 0)))
    wins = [pl.BlockSpec(sp.block_shape, clamp(sp.index_map))
            for c in (1, 2) for sp in _win_in_specs(length, c, D_ATT)]
    return _hosted_call(
        body, sides, name=name, grid=(dil, nb + 1),
        in_specs=[main] + wins + [main] * 3 + [_full((2 * BQ, WIN))],
        out_specs=[main, pl.BlockSpec((None, BQ, 2 * D_ATT), lambda r, j: (r, jnp.maximum(j - 1, 0), 0))],
        out_shape=[jax.ShapeDtypeStruct((dil, length, D_ATT), BF16),
                   jax.ShapeDtypeStruct((dil, length, 2 * D_ATT), BF16)],
        scratch_shapes=[pltpu.VMEM((WIN, D_ATT), BF16)] * 2
        + [pltpu.VMEM((BQ, 2 * D_ATT), F32), pltpu.VMEM((HALF, 2 * D_ATT), F32)],
        args=[qkv3] * 7 + [do3, lt3, ds3, band])


def _bwd_in(dqs, dkvs, dag, w_in, x, g, dh1, rot):
    s, d = x.shape
    n = w_in.shape[1]
    tm = _row_tile(s)
    dils = DILATIONS[1:]
    nd = len(dils)

    def body(q1, *rest):
        q_str, kv1, kv_str = rest[:nd], rest[nd], rest[nd + 1:2 * nd + 1]
        dag_ref, w_ref, x_ref, g_ref, dh_ref, c_ref, a_ref, b_ref, gx_ref, dy_ref, gg_ref, qbuf, kvbuf = rest[2 * nd + 1:]

        @pl.when(pl.program_id(0) == 0)
        def _():
            gg_ref[...] = jnp.zeros_like(gg_ref)

        _store_blocks(qbuf, q1[0].astype(F32))
        _store_blocks(kvbuf, kv1[0].astype(F32))
        for ref in q_str:
            _read_strided(ref, qbuf, add=True)
        for ref in kv_str:
            _read_strided(ref, kvbuf, add=True)
        dq, dkv = _load_blocks(qbuf), _load_blocks(kvbuf)
        reps = (1, D_ATT // LANES)
        cc, aa, bb = jnp.tile(c_ref[...], reps), jnp.tile(a_ref[...], reps), jnp.tile(b_ref[...], reps)
        for blk, t in enumerate((dq, dkv[:, :D_ATT])):
            dt = t * cc + pltpu.roll(t * aa, ROT_DIM // 2, 1) + pltpu.roll(t * bb, D_ATT - ROT_DIM // 2, 1)
            dy_ref[:, blk * D_ATT:(blk + 1) * D_ATT] = dt.astype(BF16)
        dy_ref[:, 2 * D_ATT:3 * D_ATT] = dkv[:, D_ATT:].astype(BF16)
        dy_ref[:, 3 * D_ATT:] = dag_ref[...]
        dx, gg = _rms_bwd(x_ref[...], g_ref[...], _dot_nt(dy_ref[...], w_ref[...]))
        gg_ref[...] += jnp.sum(gg, axis=0, keepdims=True)
        gx_ref[...] = dh_ref[...] + dx

    row = lambda w: pl.BlockSpec((tm, w), lambda i: (i, 0))
    def strided(width):
        return [pl.BlockSpec((1, tm, width), lambda i: (0, i, 0))] + [_strided_spec(dil, tm, width) for dil in dils]

    return pl.pallas_call(
        body, name="bwd_in", grid=(s // tm,),
        in_specs=strided(D_ATT) + strided(2 * D_ATT)
        + [row(2 * D_CONV), _full((d, n)), row(d), _full((1, d)), row(d)] + [row(LANES)] * 3,
        out_specs=[row(d), row(n), _full((1, d))],
        out_shape=[jax.ShapeDtypeStruct((s, d), F32), jax.ShapeDtypeStruct((s, n), BF16),
                   jax.ShapeDtypeStruct((1, d), F32)],
        scratch_shapes=[_lane_scratch(tm, D_ATT), _lane_scratch(tm, 2 * D_ATT)],
        compiler_params=_params("arbitrary"),
    )(*dqs, *dkvs, dag, w_in, x, g, dh1, *rot)


def _wgrad(a, b, name, a_blk=None, b_blk=None, stack=None, tm=1024):
    s, ka = a.shape
    nb = b.shape[1]
    a_blk, b_blk = a_blk or ka, b_blk or nb
    na, nbl = ka // a_blk, nb // b_blk
    assert na == 1 or nbl == 1
    tm = min(tm, s)
    nt = s // tm
    per = b_blk // stack if stack else 0

    def body(a_ref, b_ref, o_ref, acc):
        t = pl.program_id(1)

        @pl.when(t == 0)
        def _():
            acc[...] = jnp.zeros_like(acc)

        acc[...] += _dot_tn(a_ref[...], b_ref[...])

        @pl.when(t == nt - 1)
        def _():
            if stack:
                for c in range(per):
                    o_ref[c] = acc[:, c * stack:(c + 1) * stack].astype(BF16)
            else:
                o_ref[...] = acc[...].astype(BF16)

    if stack:
        out_spec = pl.BlockSpec((per, ka, stack), lambda k, t: (k, 0, 0))
        out_shape = jax.ShapeDtypeStruct((nb // stack, ka, stack), BF16)
    elif na > 1:
        out_spec = pl.BlockSpec((a_blk, nb), lambda k, t: (k, 0))
        out_shape = jax.ShapeDtypeStruct((ka, nb), BF16)
    else:
        out_spec = pl.BlockSpec((ka, b_blk), lambda k, t: (0, k))
        out_shape = jax.ShapeDtypeStruct((ka, nb), BF16)
    return pl.pallas_call(
        body, name=name, grid=(na * nbl, nt),
        in_specs=[pl.BlockSpec((tm, a_blk), (lambda k, t: (t, k)) if na > 1 else (lambda k, t: (t, 0))),
                  pl.BlockSpec((tm, b_blk), (lambda k, t: (t, k)) if nbl > 1 else (lambda k, t: (t, 0)))],
        out_specs=out_spec, out_shape=out_shape,
        scratch_shapes=[pltpu.VMEM((a_blk, b_blk), F32)],
        compiler_params=_params("parallel", "arbitrary"),
    )(a, b)


def _adamw(w, gsrc, m, v, name, transposed=False):
    summed = gsrc.ndim == w.ndim + 1
    rows, cols = w.shape
    assert gsrc.shape[-2:] == ((cols, rows) if transposed else (rows, cols)) and (summed or not transposed)
    tr = rows if rows <= 256 else 256
    assert rows % tr == 0
    c1 = 1.0 - ADAM_B1 ** ADAM_STEP
    c2 = 1.0 - ADAM_B2 ** ADAM_STEP

    def body(w_ref, g_ref, m_ref, v_ref, go_ref, d_ref, mo_ref, vo_ref):
        if summed:
            g = g_ref[0].astype(F32)
            for i in range(1, N_DEV):
                g = g + g_ref[i].astype(F32)
            if transposed:
                g = g.T
        else:
            g = g_ref[...]
        mn = ADAM_B1 * m_ref[...] + (1.0 - ADAM_B1) * g
        vn = ADAM_B2 * v_ref[...] + (1.0 - ADAM_B2) * jnp.square(g)
        go_ref[...] = g
        mo_ref[...] = mn
        vo_ref[...] = vn
        d_ref[...] = -ADAM_LR * ((mn / c1) / (jnp.sqrt(vn / c2) + ADAM_EPS) + ADAM_WD * w_ref[...])

    blk = pl.BlockSpec((tr, cols), lambda i: (i, 0))
    if transposed:
        gblk = pl.BlockSpec((N_DEV, cols, tr), lambda i: (0, 0, i))
    else:
        gblk = pl.BlockSpec((N_DEV, tr, cols), lambda i: (0, i, 0)) if summed else blk
    return pl.pallas_call(
        body, name=name, grid=(rows // tr,),
        in_specs=[blk, gblk, blk, blk], out_specs=[blk] * 4,
        out_shape=[jax.ShapeDtypeStruct(w.shape, F32)] * 4,
        compiler_params=_params("parallel"),
    )(w, gsrc, m, v)


def _sum_slots(g, name):
    _, rows, cols = g.shape

    def body(g_ref, o_ref):
        acc = g_ref[0]
        for i in range(1, N_DEV):
            acc = acc + g_ref[i]
        o_ref[...] = acc

    return pl.pallas_call(body, name=name, out_shape=jax.ShapeDtypeStruct((rows, cols), F32),
                          compiler_params=_params())(g)


def kernel(x, mem, norm_mix_g, w_in, conv_w, conv_b, conv_ln_g, conv_ln_b, w_out, norm_x_g, norm_mem_g, w_xq, w_xk, w_xv, w_xo, norm_mlp_g, w_up, w_down, norm_final_g, loss_target, m_norm_mix_g, m_w_in, m_conv_w, m_conv_b, m_conv_ln_g, m_conv_ln_b, m_w_out, m_norm_x_g, m_norm_mem_g, m_w_xq, m_w_xk, m_w_xv, m_w_xo, m_norm_mlp_g, m_w_up, m_w_down, m_norm_final_g, v_norm_mix_g, v_w_in, v_conv_w, v_conv_b, v_conv_ln_g, v_conv_ln_b, v_w_out, v_norm_x_g, v_norm_mem_g, v_w_xq, v_w_xk, v_w_xv, v_w_xo, v_norm_mlp_g, v_w_up, v_w_down, v_norm_final_g):
    x2, mem2, tgt = x[0], mem[0], loss_target[0]
    s, d = x2.shape
    gf = norm_final_g[None, :]

    cw_local = jnp.pad(conv_w[0], ((0, 1), (0, LANES - conv_w.shape[2])))
    win_g, cw_g = _exchange_call([_Exchange([w_in[0].astype(BF16), cw_local], gather=True)], "gather_w_in")
    w_in_f = jnp.transpose(win_g, (1, 0, 2)).reshape(d, -1)
    cw_f = jnp.transpose(cw_g[:, :, :conv_w.shape[2]], (1, 0, 2)).reshape(32, D_CONV)
    row_names = (w_out, w_xq, w_xk, w_xv, w_xo)
    rows_local = jnp.concatenate([w[0].astype(BF16) for w in row_names], axis=0)
    late = [_Exchange([rows_local], gather=True), _Exchange([w_up[0].astype(BF16)], gather=True),
            _Exchange([w_down[0].astype(BF16)], gather=True)]

    rot = _rotary_tables(s)
    band = _band_bias()
    xn, qkv, ag, *qkv_strided = _fwd_in(x2, norm_mix_g, w_in_f, rot)
    qkv3 = [qkv[None]] + qkv_strided
    outs, lses, gathered = [], [], []
    for dil, q3, side in zip(DILATIONS, qkv3, late):
        (o3, l3), got = _swa_fwd(q3, band, f"swa_fwd_d{dil}", [side])
        outs.append(o3)
        lses.append(l3)
        gathered += got
    rows_g, wup_g, wdown_g = gathered
    full_rows = []
    off = 0
    for w in row_names:
        full_rows.append(rows_g[:, off:off + w.shape[1], :].reshape(N_DEV * w.shape[1], d))
        off += w.shape[1]
    w_out_f, w_xq_f, w_xk_f, w_xv_f, w_xo_f = full_rows
    w_down_f = wdown_g.reshape(-1, d)
    w_up_t = jnp.swapaxes(wup_g, 1, 2).reshape(-1, d)
    c1, conv_out = _fwd_conv(ag, cw_f, conv_b, conv_ln_g, conv_ln_b)
    h1, cat, ltot, *lt_strided = _fwd_mix_out(outs, lses, conv_out, x2, w_out_f)
    mn, xk, xv = _fwd_mem(mem2, norm_mem_g, w_xk_f, w_xv_f)
    h2, hn2, xq, xo = _fwd_xattn(h1, norm_x_g, w_xq_f, xk, xv, w_xo_f)
    hn3, act, dh3, dh3b, loss_part, g_final = _fwd_mlp_loss(h2, norm_mlp_g, wup_g, w_down_f, gf, tgt)

    def scatter(*grads):
        return _Exchange([g.reshape(N_DEV, -1, g.shape[-1]) for g in grads], gather=False)

    f_blk = w_up.shape[2]
    du, dh2, dh2b, g_mlp = _bwd_mlp(dh3, dh3b, act, w_up_t, w_down_f, h2, norm_mlp_g)
    gw_up = _wgrad(hn3, du, "wgrad_up", b_blk=4 * f_blk, stack=f_blk)
    gw_down = _wgrad(dh3b, act, "wgrad_down", b_blk=4 * f_blk, stack=f_blk)
    (dh1, dh1b, dxq, dxk, dxv, g_x), (r_up,) = _bwd_xattn(
        dh2, dh2b, h1, norm_x_g, xq, xk, xv, w_xq_f, w_xo_f, [scatter(gw_up)])
    gw_xq = _wgrad(hn2, dxq, "wgrad_xq")
    gw_xo = _wgrad(xo, dh2b, "wgrad_xo")
    gw_xk, gw_xv, g_mem = _bwd_mem(mem2, norm_mem_g, mn, dxk, dxv, w_xk_f, w_xv_f)
    head = jnp.arange(D_ATT) // HEAD_DIM
    head_ones = (head[:, None] == head[None, :]).astype(BF16)
    dcat, dsum, *strided = _bwd_mix_out(dh1b, w_out_f, cat, head_ones)
    n_str = len(DILATIONS) - 1
    do3, lt3, ds3 = [dcat[None]] + strided[:n_str], [ltot[None]] + lt_strided, [dsum[None]] + strided[n_str:]
    gw_out = _wgrad(cat, dh1b, "wgrad_out")
    (dag, g_cw, g_cb, g_lg, g_lb), (r_down,) = _bwd_conv(dcat, c1, ag, cw_f, conv_ln_g, conv_ln_b, [scatter(gw_down)])
    hosted = [[scatter(gw_out, gw_xq, gw_xk, gw_xv, gw_xo)], [], []]
    dqs, dkvs, landed = [], [], []
    for i, dil in enumerate(DILATIONS):
        (dq3, dkv3), got = _swa_bwd(qkv3[i], do3[i], lt3[i], ds3[i], band, f"swa_bwd_d{dil}", hosted[i])
        dqs.append(dq3)
        dkvs.append(dkv3)
        landed += got
    r_out, r_xq, r_xk, r_xv, r_xo = landed
    grad_x, dy, g_mix = _bwd_in(dqs, dkvs, dag, w_in_f, x2, norm_mix_g, dh1, rot)
    gw_in = _wgrad(xn, dy, "wgrad_in", b_blk=dy.shape[1] // 2)

    def widen(t):
        return jnp.pad(t, ((0, 0), (0, d - t.shape[1])))

    n_in = w_in.shape[2]
    small = jnp.concatenate([g_mix, g_x, g_mem, g_mlp, g_final, widen(g_cb), widen(g_lg), widen(g_lb),
                             g_cw.reshape(16, d)], axis=0)
    r_in, small_g = _exchange_call(
        [_Exchange([jnp.transpose(gw_in.reshape(d, N_DEV, n_in), (1, 0, 2))], gather=False),
         _Exchange([small], gather=True)], "scatter_w_in_gather_small")
    small_sum = _sum_slots(small_g, "sum_small_grads")
    loss = lax.psum(loss_part[0, 0], ("x", "y", "c"))

    res = {}

    def step(name, w, gsrc, m, v, transposed=False):
        shape = w.shape
        w2, m2, v2 = (t.reshape(-1, shape[-1]) for t in (w, m, v))
        res[name] = [t.reshape(shape) for t in _adamw(w2, gsrc, m2, v2, "adamw_" + name, transposed)]

    step("w_in", w_in, r_in, m_w_in, v_w_in)
    step("w_up", w_up, r_up, m_w_up, v_w_up)
    step("w_out", w_out, r_out, m_w_out, v_w_out)
    step("w_xq", w_xq, r_xq, m_w_xq, v_w_xq)
    step("w_xk", w_xk, r_xk, m_w_xk, v_w_xk)
    step("w_xv", w_xv, r_xv, m_w_xv, v_w_xv)
    step("w_xo", w_xo, r_xo, m_w_xo, v_w_xo)
    step("w_down", w_down, r_down, m_w_down, v_w_down, transposed=True)

    me = _dev_index((lax.axis_index("x"), lax.axis_index("y"), lax.axis_index("c")))
    n_cw = conv_w.shape[2]
    g_cw_full = small_sum[8:24].reshape(32, D_CONV)[:CONV_WIDTH]
    g_cw_mine = lax.dynamic_slice_in_dim(g_cw_full, me * n_cw, n_cw, axis=1)
    step("conv_w", conv_w, g_cw_mine, m_conv_w, v_conv_w)

    vec_names = ["norm_mix_g", "norm_x_g", "norm_mem_g", "norm_mlp_g", "norm_final_g", "conv_b", "conv_ln_g", "conv_ln_b"]
    vec_w = [norm_mix_g, norm_x_g, norm_mem_g, norm_mlp_g, gf, conv_b, conv_ln_g, conv_ln_b]
    vec_m = [m_norm_mix_g, m_norm_x_g, m_norm_mem_g, m_norm_mlp_g, m_norm_final_g[None, :], m_conv_b, m_conv_ln_g, m_conv_ln_b]
    vec_v = [v_norm_mix_g, v_norm_x_g, v_norm_mem_g, v_norm_mlp_g, v_norm_final_g[None, :], v_conv_b, v_conv_ln_g, v_conv_ln_b]

    def pack(ts):
        return jnp.concatenate([widen(t) for t in ts], axis=0)

    packed = _adamw(pack(vec_w), small_sum[0:8], pack(vec_m), pack(vec_v), "adamw_vectors")
    for i, name in enumerate(vec_names):
        width = vec_w[i].shape[1]
        shape = (width,) if name == "norm_final_g" else (1, width)
        res[name] = [t[i, :width].reshape(shape) for t in packed]

    order = ["norm_mix_g", "w_in", "conv_w", "conv_b", "conv_ln_g", "conv_ln_b", "w_out", "norm_x_g", "norm_mem_g",
             "w_xq", "w_xk", "w_xv", "w_xo", "norm_mlp_g", "w_up", "w_down", "norm_final_g"]
    out = [loss, grad_x[None]]
    for kind in range(4):
        out += [res[name][kind] for name in order]
    return tuple(out)
```

```python
import jax
import jax.numpy as jnp
from jax import lax
from jax.experimental import pallas as pl
from jax.experimental.pallas import tpu as pltpu

F32 = jnp.float32
BF16 = jnp.bfloat16

N_DEV = 8
EPS = 1e-6
NEG_INF = -1e30
ATT_HEADS = 8
HEAD_DIM = 64
D_ATT = ATT_HEADS * HEAD_DIM
D_CONV = 512
DILATIONS = (1, 4, 16)
HALF = 64
ROPE_THETA = 500000.0
ROT_DIM = HEAD_DIM // 4
CONV_WIDTH = 31
CONV_PAD = (CONV_WIDTH - 1) // 2
XATT_HEADS = 4
ADAM_LR = 0.001
ADAM_B1 = 0.9
ADAM_B2 = 0.999
ADAM_EPS = 1e-08
ADAM_WD = 0.01
ADAM_STEP = 10

LANES = 128
SUBLANES = 8
BF16_ROWS = 16
BQ = 128
WIN = BQ + 2 * HALF
MLP_SHARDS = 4
CONV_ROWS = 32
VMEM_LIMIT = 56 * 1024 * 1024
MESH = pl.DeviceIdType.MESH
ANY = pl.BlockSpec(memory_space=pl.ANY)

_NT = (((1,), (1,)), ((), ()))
_TN = (((0,), (0,)), ((), ()))


def _dot(a, b):
    return jnp.dot(a, b, preferred_element_type=F32)


def _dot_nt(a, b):
    return lax.dot_general(a, b, _NT, preferred_element_type=F32)


def _dot_tn(a, b):
    return lax.dot_general(a, b, _TN, preferred_element_type=F32)


def _params(*sem):
    return pltpu.CompilerParams(dimension_semantics=sem or None, vmem_limit_bytes=VMEM_LIMIT)


def _sigmoid(v):
    return 1.0 / (1.0 + jnp.exp(-v))


def _mean(v):
    return jnp.mean(v, axis=-1, keepdims=True)


def _rms_fwd(h, g):
    r = lax.rsqrt(_mean(h * h) + EPS)
    return h * r * g, r


def _rms_bwd(h, g, d_out):
    r = lax.rsqrt(_mean(h * h) + EPS)
    hn = h * r
    gd = d_out * g
    return r * (gd - hn * _mean(gd * hn)), d_out * hn


def _row_tile(s):
    return min(512, s)


def _full(shape):
    return pl.BlockSpec(shape, lambda *_: (0,) * len(shape))


def _mesh_pos():
    return lax.axis_index("x"), lax.axis_index("y"), lax.axis_index("c")


def _dev_index(p):
    return 4 * p[0] + 2 * p[1] + p[2]


class _Exchange:
    def __init__(self, arrays, gather):
        self.arrays, self.gather, self.n = list(arrays), gather, len(arrays)

    def out_shapes(self):
        return [jax.ShapeDtypeStruct(((N_DEV,) + a.shape) if self.gather else a.shape, a.dtype)
                for a in self.arrays]

    def sem_shapes(self):
        return [pltpu.SemaphoreType.DMA((7 * self.n,)), pltpu.SemaphoreType.DMA((7 * self.n,)),
                pltpu.SemaphoreType.DMA((self.n,))]

    def phases(self, x_refs, o_refs, send_sems, recv_sems, local_sems):
        n = self.n
        x, y, c = _mesh_pos()
        me, sibling = (x, y, c), (x, y, 1 - c)

        if self.gather:
            chips = [(1 - x, y), (x, 1 - y), (1 - x, 1 - y)]

            def copy(a, k, block, to, src=None):
                slot = o_refs[a].at[_dev_index(block)]
                return pltpu.make_async_remote_copy(
                    src_ref=slot if src is None else src, dst_ref=slot,
                    send_sem=send_sems.at[7 * a + k], recv_sem=recv_sems.at[7 * a + k],
                    device_id=to, device_id_type=MESH)

            def mine(a):
                return pltpu.make_async_copy(x_refs[a], o_refs[a].at[_dev_index(me)], local_sems.at[a])

            def first(a):
                return [copy(a, 0, me, sibling, src=x_refs[a])] + [
                    copy(a, 1 + j, me, (*chip, c), src=x_refs[a]) for j, chip in enumerate(chips)]

            def relayed(a, j):
                return copy(a, 4 + j, (*chips[j], c), sibling)

            def start():
                for a in range(n):
                    mine(a).start()
                    for cp in first(a):
                        cp.start()

            def relay():
                for j, chip in enumerate(chips):
                    for a in range(n):
                        copy(a, 1 + j, (*chip, c), me).wait_recv()
                        relayed(a, j).start()

            def finish():
                for a in range(n):
                    copy(a, 0, sibling, me).wait_recv()
                    for j, chip in enumerate(chips):
                        copy(a, 4 + j, (*chip, 1 - c), me).wait_recv()
                    for cp in first(a) + [relayed(a, j) for j in range(3)]:
                        cp.wait_send()
                    mine(a).wait()

            return start, relay, finish

        flips = [(dx, dy, dc) for dx in (0, 1) for dy in (0, 1) for dc in (0, 1)][1:]

        def peer(k):
            return tuple(1 - v if fl else v for v, fl in zip(me, flips[k]))

        def send(a, k):
            return pltpu.make_async_remote_copy(
                src_ref=x_refs[a].at[_dev_index(peer(k))], dst_ref=o_refs[a].at[_dev_index(me)],
                send_sem=send_sems.at[7 * a + k], recv_sem=recv_sems.at[7 * a + k],
                device_id=peer(k), device_id_type=MESH)

        def landed(a, k):
            slot = o_refs[a].at[_dev_index(peer(k))]
            return pltpu.make_async_remote_copy(
                src_ref=slot, dst_ref=slot, send_sem=send_sems.at[7 * a + k], recv_sem=recv_sems.at[7 * a + k],
                device_id=peer(k), device_id_type=MESH)

        def own(a):
            return pltpu.make_async_copy(x_refs[a].at[_dev_index(me)], o_refs[a].at[_dev_index(me)],
                                         local_sems.at[a])

        def start():
            for a in range(n):
                own(a).start()
            for k in range(7):
                for a in range(n):
                    send(a, k).start()

        def finish():
            for k in range(7):
                for a in range(n):
                    landed(a, k).wait_recv()
            for k in range(7):
                for a in range(n):
                    send(a, k).wait_send()
            for a in range(n):
                own(a).wait()

        return start, (lambda: None), finish


def _hosted_call(body, sides, *, name, grid, in_specs, out_specs, out_shape, scratch_shapes, args):
    n_in, n_out, ns = len(in_specs), len(out_specs), sum(s.n for s in sides)
    steps = 1
    for g in grid:
        steps *= g

    def wrapped(*refs):
        ins, s_ins = refs[:n_in], refs[n_in:n_in + ns]
        outs = refs[n_in + ns:n_in + ns + n_out]
        s_outs = refs[n_in + ns + n_out:n_in + 2 * ns + n_out]
        rest = refs[n_in + 2 * ns + n_out:]
        scratch, sems = rest[:len(rest) - 3 * len(sides)], rest[len(rest) - 3 * len(sides):]
        phases, off = [], 0
        for i, s in enumerate(sides):
            phases.append(s.phases(s_ins[off:off + s.n], s_outs[off:off + s.n], *sems[3 * i:3 * i + 3]))
            off += s.n
        lin = 0
        for ax, g in enumerate(grid):
            lin = lin * g + pl.program_id(ax)

        if sides:
            @pl.when(lin == 0)
            def _():
                for start, _, _ in phases:
                    start()

        body(*ins, *outs, *scratch)

        if sides:
            @pl.when(lin == min((3 * steps) // 4, steps - 1))
            def _():
                for _, relay, _ in phases:
                    relay()

            @pl.when(lin == steps - 1)
            def _():
                for _, _, finish in phases:
                    finish()

    res = pl.pallas_call(
        wrapped, name=name, grid=grid,
        in_specs=list(in_specs) + [ANY] * ns, out_specs=list(out_specs) + [ANY] * ns,
        out_shape=list(out_shape) + [sh for s in sides for sh in s.out_shapes()],
        scratch_shapes=list(scratch_shapes) + [sh for s in sides for sh in s.sem_shapes()],
        compiler_params=_params(*(("arbitrary",) * len(grid))),
    )(*args, *[a for s in sides for a in s.arrays])
    return res[:n_out], res[n_out:]


def _exchange_call(sides, name):
    ns = sum(s.n for s in sides)

    def body(*refs):
        x_refs, o_refs, sems = refs[:ns], refs[ns:2 * ns], refs[2 * ns:]
        phases, off = [], 0
        for i, s in enumerate(sides):
            phases.append(s.phases(x_refs[off:off + s.n], o_refs[off:off + s.n], *sems[3 * i:3 * i + 3]))
            off += s.n
        for step in range(3):
            for ph in phases:
                ph[step]()

    return pl.pallas_call(
        body, name=name,
        out_shape=[sh for s in sides for sh in s.out_shapes()],
        in_specs=[ANY] * ns, out_specs=[ANY] * ns,
        scratch_shapes=[sh for s in sides for sh in s.sem_shapes()],
    )(*[a for s in sides for a in s.arrays])


def _rotary_tables(s):
    half = ROT_DIM // 2
    freqs = ROPE_THETA ** (-jnp.arange(0, ROT_DIM, 2, dtype=F32) / ROT_DIM)
    ang = jnp.arange(s, dtype=F32)[:, None] * freqs[None, :]
    cos, sin = jnp.cos(ang), jnp.sin(ang)
    one = jnp.ones((s, HEAD_DIM - ROT_DIM), F32)
    zero = jnp.zeros((s, HEAD_DIM - ROT_DIM), F32)
    zh = jnp.zeros((s, half), F32)
    c64 = jnp.concatenate([cos, cos, one], axis=1)
    a64 = jnp.concatenate([-sin, zh, zero], axis=1)
    b64 = jnp.concatenate([zh, sin, zero], axis=1)
    return tuple(jnp.tile(t, (1, LANES // HEAD_DIM)) for t in (c64, a64, b64))


def _strided_spec(dil, tm, width):
    return pl.BlockSpec((dil, tm // dil, width), lambda i: (0, i, 0))


def _strided_shape(dil, s, width, dtype):
    return jax.ShapeDtypeStruct((dil, s // dil, width), dtype)


def _lane_scratch(tm, width):
    return pltpu.VMEM((width // LANES, tm, LANES), F32)


def _store_blocks(buf, v):
    for cb in range(buf.shape[0]):
        buf[cb] = v[:, cb * LANES:(cb + 1) * LANES]


def _load_blocks(buf):
    return jnp.concatenate([buf[cb] for cb in range(buf.shape[0])], axis=1)


def _write_strided(buf, dst_ref):
    dil, rows, _ = dst_ref.shape
    for r in range(dil):
        for cb in range(buf.shape[0]):
            dst_ref[r, :, cb * LANES:(cb + 1) * LANES] = buf[cb, pl.ds(r, rows, stride=dil), :].astype(dst_ref.dtype)


def _read_strided(src_ref, buf, add=False):
    dil, rows, _ = src_ref.shape
    for r in range(dil):
        for cb in range(buf.shape[0]):
            v = src_ref[r, :, cb * LANES:(cb + 1) * LANES].astype(F32)
            if add:
                v = v + buf[cb, pl.ds(r, rows, stride=dil), :]
            buf[cb, pl.ds(r, rows, stride=dil), :] = v


def _fwd_in(x, g, w_in, rot):
    s, d = x.shape
    n = w_in.shape[1]
    tm = _row_tile(s)
    dils = DILATIONS[1:]

    def body(x_ref, g_ref, w_ref, c_ref, a_ref, b_ref, xn_ref, qkv_ref, ag_ref, *rest):
        strided, ybuf = rest[:len(dils)], rest[len(dils)]
        xn = _rms_fwd(x_ref[...], g_ref[...])[0].astype(BF16)
        xn_ref[...] = xn
        y = _dot(xn, w_ref[...])
        reps = (1, D_ATT // LANES)
        cc, aa, bb = jnp.tile(c_ref[...], reps), jnp.tile(a_ref[...], reps), jnp.tile(b_ref[...], reps)
        parts = []
        for blk in range(2):
            t = y[:, blk * D_ATT:(blk + 1) * D_ATT]
            parts.append(t * cc + pltpu.roll(t, D_ATT - ROT_DIM // 2, 1) * aa + pltpu.roll(t, ROT_DIM // 2, 1) * bb)
        qkv = jnp.concatenate(parts + [y[:, 2 * D_ATT:3 * D_ATT]], axis=1)
        qkv_ref[...] = qkv.astype(BF16)
        _store_blocks(ybuf, qkv)
        for ref in strided:
            _write_strided(ybuf, ref)
        ag_ref[...] = y[:, 3 * D_ATT:].astype(BF16)

    row = lambda w: pl.BlockSpec((tm, w), lambda i: (i, 0))
    return pl.pallas_call(
        body, name="fwd_in", grid=(s // tm,),
        in_specs=[row(d), _full((1, d)), _full((d, n)), row(LANES), row(LANES), row(LANES)],
        out_specs=[row(d), row(3 * D_ATT), row(2 * D_CONV)] + [_strided_spec(dil, tm, 3 * D_ATT) for dil in dils],
        out_shape=[jax.ShapeDtypeStruct((s, d), BF16), jax.ShapeDtypeStruct((s, 3 * D_ATT), BF16),
                   jax.ShapeDtypeStruct((s, 2 * D_CONV), BF16)]
        + [_strided_shape(dil, s, 3 * D_ATT, BF16) for dil in dils],
        scratch_shapes=[_lane_scratch(tm, 3 * D_ATT)],
        compiler_params=_params("parallel"),
    )(x, g, w_in, *rot)


def _win_in_specs(length, col, width):
    per = BQ // HALF
    last = length // HALF - 1
    return [
        pl.BlockSpec((None, HALF, width), lambda r, j: (r, jnp.maximum(j * per - 1, 0), col)),
        pl.BlockSpec((None, BQ, width), lambda r, j: (r, j, col)),
        pl.BlockSpec((None, HALF, width), lambda r, j: (r, jnp.minimum(j * per + per, last), col)),
    ]


def _fill_window(win, prev_ref, main_ref, next_ref):
    win[0:HALF] = prev_ref[...]
    win[HALF:HALF + BQ] = main_ref[...]
    win[HALF + BQ:] = next_ref[...]


def _band_bias():
    blk = jnp.arange(2 * BQ)[:, None] & (BQ - 1)
    win = jnp.arange(WIN)[None, :]
    return jnp.where(jnp.abs(win - HALF - blk) <= HALF, 0.0, NEG_INF).astype(F32)


def _window_bias(band_ref, j, length):
    pos = j * BQ - HALF + lax.broadcasted_iota(jnp.int32, (1, WIN), 1)
    return band_ref[...] + jnp.where((pos >= 0) & (pos < length), 0.0, NEG_INF)


def _first_head():
    return lax.broadcasted_iota(jnp.int32, (1, LANES), 1) < HEAD_DIM


def _stack_heads(v, first):
    zero = jnp.zeros((), v.dtype)
    return jnp.concatenate([jnp.where(first, v, zero), jnp.where(first, zero, v)], axis=0)


def _unstack_heads(v, first):
    rows = v.shape[0] // 2
    return jnp.where(first, v[:rows], v[rows:])


def _stack_cols(v, first):
    top = jnp.max(jnp.where(first, v, -jnp.inf), axis=-1, keepdims=True)
    bot = jnp.max(jnp.where(first, -jnp.inf, v), axis=-1, keepdims=True)
    return jnp.concatenate([top, bot], axis=0)


def _swa_fwd(qkv3, band, name, sides=()):
    dil, length, _ = qkv3.shape
    scale = HEAD_DIM ** -0.5

    def body(q_ref, kp, km, kn, vp, vm, vn, band_ref, o_ref, lse_ref, kwin, vwin):
        j = pl.program_id(1)
        _fill_window(kwin, kp, km, kn)
        _fill_window(vwin, vp, vm, vn)
        bias = _window_bias(band_ref, j, length)
        first = _first_head()
        for pr in range(D_ATT // LANES):
            cols = slice(pr * LANES, (pr + 1) * LANES)
            qs = _stack_heads(q_ref[:, cols] * scale, first)
            sc = _dot_nt(qs, kwin[:, cols]) + bias
            m = jnp.max(sc, axis=-1, keepdims=True)
            p = jnp.exp(sc - m)
            den = jnp.sum(p, axis=-1, keepdims=True)
            pv = _dot(p.astype(BF16), vwin[:, cols]) * (1.0 / den)
            o_ref[:, cols] = _unstack_heads(pv, first).astype(BF16)
            lse_ref[:, cols] = _unstack_heads(jnp.broadcast_to(m + jnp.log(den), (2 * BQ, LANES)), first)

    blk = lambda w: pl.BlockSpec((None, BQ, w), lambda r, j: (r, j, 0))
    return _hosted_call(
        body, sides, name=name, grid=(dil, length // BQ),
        in_specs=[pl.BlockSpec((None, BQ, D_ATT), lambda r, j: (r, j, 0))]
        + _win_in_specs(length, 1, D_ATT) + _win_in_specs(length, 2, D_ATT) + [_full((2 * BQ, WIN))],
        out_specs=[blk(D_ATT), blk(D_ATT)],
        out_shape=[jax.ShapeDtypeStruct((dil, length, D_ATT), BF16),
                   jax.ShapeDtypeStruct((dil, length, D_ATT), F32)],
        scratch_shapes=[pltpu.VMEM((WIN, D_ATT), BF16), pltpu.VMEM((WIN, D_ATT), BF16)],
        args=[qkv3] * 7 + [band])


def _glu(v):
    return v[:, :D_CONV].astype(F32) * _sigmoid(v[:, D_CONV:].astype(F32))


def _halo_specs(tm, width, col=0):
    per = tm // BF16_ROWS
    return lambda nblk: [
        pl.BlockSpec((BF16_ROWS, width), lambda i: (jnp.maximum(i * per - 1, 0), col)),
        pl.BlockSpec((tm, width), lambda i: (i, col)),
        pl.BlockSpec((BF16_ROWS, width), lambda i: (jnp.minimum(i * per + per, nblk - 1), col)),
    ]


def _fill_halo(buf, i, ntiles, tm, prev, main, nxt):
    buf[0:BF16_ROWS] = jnp.where(i == 0, 0.0, prev)
    buf[BF16_ROWS:BF16_ROWS + tm] = main
    buf[BF16_ROWS + tm:] = jnp.where(i == ntiles - 1, 0.0, nxt)


def _halo_scratch(tm):
    return [pltpu.VMEM((tm + 2 * BF16_ROWS, D_CONV), F32),
            pltpu.VMEM((SUBLANES - 1, tm + 2 * BF16_ROWS - SUBLANES, D_CONV), F32)]


def _shift_copies(buf, shifted, tm):
    rows = tm + 2 * BF16_ROWS - SUBLANES
    for b in range(1, SUBLANES):
        shifted[b - 1] = buf[pl.ds(b, rows), :]


def _tap(buf, shifted, off, rows, base=0):
    a, b = divmod(off, SUBLANES)
    start = base + SUBLANES * a
    if not isinstance(start, int):
        start = pl.multiple_of(start, SUBLANES)
    if b == 0:
        return buf[pl.ds(start, rows), :]
    return shifted[b - 1, pl.ds(start, rows), :]


def _fwd_conv(ag, cw, cb, lg, lb):
    s = ag.shape[0]
    tm = _row_tile(s)
    nt = s // tm

    def body(agp, agm, agn, cw_ref, cb_ref, lg_ref, lb_ref, c1_ref, co_ref, ubuf, ush):
        i = pl.program_id(0)
        _fill_halo(ubuf, i, nt, tm, _glu(agp[...]), _glu(agm[...]), _glu(agn[...]))
        _shift_copies(ubuf, ush, tm)
        w, cb = cw_ref[...], cb_ref[...]

        def chunk(c, carry):
            base = pl.multiple_of(c * CONV_ROWS, CONV_ROWS)
            acc = jnp.zeros((CONV_ROWS, D_CONV), F32)
            for k in range(CONV_WIDTH):
                acc = acc + _tap(ubuf, ush, k + 1, CONV_ROWS, base) * w[k:k + 1, :]
            c1_ref[pl.ds(base, CONV_ROWS), :] = acc + cb
            return carry

        lax.fori_loop(0, tm // CONV_ROWS, chunk, 0)
        c1 = c1_ref[...]
        xc = c1 - _mean(c1)
        ln = xc * lax.rsqrt(_mean(xc * xc) + EPS) * lg_ref[...] + lb_ref[...]
        co_ref[...] = (ln * _sigmoid(ln)).astype(BF16)

    vec = _full((1, D_CONV))
    return pl.pallas_call(
        body, name="fwd_conv", grid=(nt,),
        in_specs=_halo_specs(tm, 2 * D_CONV)(s // BF16_ROWS) + [_full((32, D_CONV)), vec, vec, vec],
        out_specs=[pl.BlockSpec((tm, D_CONV), lambda i: (i, 0))] * 2,
        out_shape=[jax.ShapeDtypeStruct((s, D_CONV), F32), jax.ShapeDtypeStruct((s, D_CONV), BF16)],
        scratch_shapes=_halo_scratch(tm),
        compiler_params=_params("parallel"),
    )(ag, ag, ag, cw, cb, lg, lb)


def _fwd_mix_out(outs, lses, conv_out, x, w_out):
    s, d = x.shape
    tm = _row_tile(s)
    dils = DILATIONS[1:]
    nd = len(dils)

    def body(o1, *rest):
        o_str, l1, l_str = rest[:nd], rest[nd], rest[nd + 1:2 * nd + 1]
        co, x_ref, w_ref, h_ref, cat_ref, lt_ref = rest[2 * nd + 1:2 * nd + 7]
        lt_str = rest[2 * nd + 7:3 * nd + 7]
        obufs, lbufs, ltbuf = rest[3 * nd + 7:4 * nd + 7], rest[4 * nd + 7:5 * nd + 7], rest[5 * nd + 7]
        for ref, buf in zip(o_str + l_str, obufs + lbufs):
            _read_strided(ref, buf)
        lse = [l1[0]] + [_load_blocks(buf) for buf in lbufs]
        out = [o1[0].astype(F32)] + [_load_blocks(buf) for buf in obufs]
        m = lse[0]
        for v in lse[1:]:
            m = jnp.maximum(m, v)
        e = [jnp.exp(v - m) for v in lse]
        den = sum(e[1:], e[0])
        att = (sum((ev * ov for ev, ov in zip(e[1:], out[1:])), e[0] * out[0]) / den).astype(BF16)
        lt = m + jnp.log(den)
        lt_ref[...] = lt
        _store_blocks(ltbuf, lt)
        for ref in lt_str:
            _write_strided(ltbuf, ref)
        cat_ref[:, :D_ATT] = att
        cat_ref[:, D_ATT:] = co[...]
        h_ref[...] = x_ref[...] + _dot(att, w_ref[:D_ATT, :]) + _dot(co[...], w_ref[D_ATT:, :])

    row = lambda w: pl.BlockSpec((tm, w), lambda i: (i, 0))
    nat = pl.BlockSpec((1, tm, D_ATT), lambda i: (0, i, 0))
    strided = [_strided_spec(dil, tm, D_ATT) for dil in dils]
    return pl.pallas_call(
        body, name="fwd_mix_out", grid=(s // tm,),
        in_specs=[nat] + strided + [nat] + strided + [row(D_ATT), row(d), _full((d, d))],
        out_specs=[row(d), row(d), row(D_ATT)] + strided,
        out_shape=[jax.ShapeDtypeStruct((s, d), F32), jax.ShapeDtypeStruct((s, d), BF16),
                   jax.ShapeDtypeStruct((s, D_ATT), F32)] + [_strided_shape(dil, s, D_ATT, F32) for dil in dils],
        scratch_shapes=[_lane_scratch(tm, D_ATT)] * (2 * nd + 1),
        compiler_params=_params("parallel"),
    )(*outs, *lses, conv_out, x, w_out)


def _fwd_mem(mem, g, wk, wv):
    m, d = mem.shape

    def body(mem_ref, g_ref, wk_ref, wv_ref, mn_ref, xk_ref, xv_ref):
        mn = _rms_fwd(mem_ref[...], g_ref[...])[0].astype(BF16)
        mn_ref[...] = mn
        xk_ref[...] = _dot(mn, wk_ref[...]).astype(BF16)
        xv_ref[...] = _dot(mn, wv_ref[...]).astype(BF16)

    return pl.pallas_call(
        body, name="fwd_mem",
        out_shape=[jax.ShapeDtypeStruct((m, d), BF16)] * 3,
        compiler_params=_params(),
    )(mem, g, wk, wv)


def _xatt_probs(q, k):
    sc = _dot_nt(q, k) * (q.shape[1] ** -0.5)
    p = jnp.exp(sc - jnp.max(sc, axis=-1, keepdims=True))
    return p / jnp.sum(p, axis=-1, keepdims=True)


def _fwd_xattn(h1, g, wq, xk, xv, wo):
    s, d = h1.shape
    m = xk.shape[0]
    tm = _row_tile(s)
    hd = d // XATT_HEADS

    def body(h_ref, g_ref, wq_ref, xk_ref, xv_ref, wo_ref, h2_ref, hn_ref, xq_ref, xo_ref):
        h = h_ref[...]
        hn = _rms_fwd(h, g_ref[...])[0].astype(BF16)
        hn_ref[...] = hn
        xq = _dot(hn, wq_ref[...]).astype(BF16)
        xq_ref[...] = xq
        for i in range(XATT_HEADS):
            cols = slice(i * hd, (i + 1) * hd)
            pr = _xatt_probs(xq[:, cols], xk_ref[:, cols])
            xo_ref[:, cols] = _dot(pr.astype(BF16), xv_ref[:, cols]).astype(BF16)
        h2_ref[...] = h + _dot(xo_ref[...], wo_ref[...])

    row = pl.BlockSpec((tm, d), lambda i: (i, 0))
    return pl.pallas_call(
        body, name="fwd_xattn", grid=(s // tm,),
        in_specs=[row, _full((1, d)), _full((d, d)), _full((m, d)), _full((m, d)), _full((d, d))],
        out_specs=[row] * 4,
        out_shape=[jax.ShapeDtypeStruct((s, d), F32)] + [jax.ShapeDtypeStruct((s, d), BF16)] * 3,
        compiler_params=_params("parallel"),
    )(h1, g, wq, xk, xv, wo)


def _fwd_mlp_loss(h2, g, w_up, w_down, gf, target):
    s, d = h2.shape
    nsh, _, f = w_up.shape
    fb = MLP_SHARDS * f
    nb = nsh // MLP_SHARDS
    tm = _row_tile(s)

    def body(h_ref, g_ref, wu_ref, wd_ref, gf_ref, t_ref,
             hn_ref, act_ref, dh_ref, dhb_ref, loss_ref, ggf_ref, acc):
        i, k = pl.program_id(0), pl.program_id(1)

        @pl.when(k == 0)
        def _():
            hn_ref[...] = _rms_fwd(h_ref[...], g_ref[...])[0].astype(BF16)
            acc[...] = jnp.zeros_like(acc)

        @pl.when((i == 0) & (k == 0))
        def _():
            loss_ref[...] = jnp.zeros_like(loss_ref)
            ggf_ref[...] = jnp.zeros_like(ggf_ref)

        hn = hn_ref[...]
        for c in range(MLP_SHARDS):
            act_ref[:, c * f:(c + 1) * f] = jnp.square(jnp.maximum(_dot(hn, wu_ref[c]), 0.0)).astype(BF16)
        acc[...] += _dot(act_ref[...], wd_ref[...])

        @pl.when(k == nb - 1)
        def _():
            h3 = h_ref[...] + acc[...]
            gfv = gf_ref[...]
            y, _ = _rms_fwd(h3, gfv)
            err = y - t_ref[...]
            loss_ref[...] += 0.5 * jnp.sum(_mean(err * err))
            dh3, gg = _rms_bwd(h3, gfv, err * (1.0 / d))
            ggf_ref[...] += jnp.sum(gg, axis=0, keepdims=True)
            dh_ref[...] = dh3
            dhb_ref[...] = dh3.astype(BF16)

    row = pl.BlockSpec((tm, d), lambda i, k: (i, 0))
    return pl.pallas_call(
        body, name="fwd_mlp_loss", grid=(s // tm, nb),
        in_specs=[row, _full((1, d)),
                  pl.BlockSpec((MLP_SHARDS, d, f), lambda i, k: (k, 0, 0)),
                  pl.BlockSpec((fb, d), lambda i, k: (k, 0)),
                  _full((1, d)), row],
        out_specs=[row, pl.BlockSpec((tm, fb), lambda i, k: (i, k)), row, row,
                   _full((1, LANES)), _full((1, d))],
        out_shape=[jax.ShapeDtypeStruct((s, d), BF16), jax.ShapeDtypeStruct((s, nsh * f), BF16),
                   jax.ShapeDtypeStruct((s, d), F32), jax.ShapeDtypeStruct((s, d), BF16),
                   jax.ShapeDtypeStruct((1, LANES), F32), jax.ShapeDtypeStruct((1, d), F32)],
        scratch_shapes=[pltpu.VMEM((tm, d), F32)],
        compiler_params=_params("arbitrary", "arbitrary"),
    )(h2, g, w_up, w_down, gf, target)


def _bwd_mlp(dh3, dh3b, act, w_up_t, w_down, h2, g):
    s, d = h2.shape
    ff = w_down.shape[0]
    fb = MLP_SHARDS * (ff // N_DEV)
    nb = ff // fb
    tm = _row_tile(s)

    def body(dh_ref, dhb_ref, act_ref, wut_ref, wd_ref, h_ref, g_ref,
             du_ref, dh2_ref, dh2b_ref, gg_ref, acc):
        i, k = pl.program_id(0), pl.program_id(1)

        @pl.when(k == 0)
        def _():
            acc[...] = jnp.zeros_like(acc)

        @pl.when((i == 0) & (k == 0))
        def _():
            gg_ref[...] = jnp.zeros_like(gg_ref)

        dact = _dot_nt(dhb_ref[...], wd_ref[...])
        du_ref[...] = (dact * (2.0 * jnp.sqrt(act_ref[...].astype(F32)))).astype(BF16)
        acc[...] += _dot(du_ref[...], wut_ref[...])

        @pl.when(k == nb - 1)
        def _():
            dh, gg = _rms_bwd(h_ref[...], g_ref[...], acc[...])
            gg_ref[...] += jnp.sum(gg, axis=0, keepdims=True)
            dh2 = dh_ref[...] + dh
            dh2_ref[...] = dh2
            dh2b_ref[...] = dh2.astype(BF16)

    row = pl.BlockSpec((tm, d), lambda i, k: (i, 0))
    col = pl.BlockSpec((tm, fb), lambda i, k: (i, k))
    wblk = pl.BlockSpec((fb, d), lambda i, k: (k, 0))
    return pl.pallas_call(
        body, name="bwd_mlp", grid=(s // tm, nb),
        in_specs=[row, row, col, wblk, wblk, row, _full((1, d))],
        out_specs=[col, row, row, _full((1, d))],
        out_shape=[jax.ShapeDtypeStruct((s, ff), BF16), jax.ShapeDtypeStruct((s, d), F32),
                   jax.ShapeDtypeStruct((s, d), BF16), jax.ShapeDtypeStruct((1, d), F32)],
        scratch_shapes=[pltpu.VMEM((tm, d), F32)],
        compiler_params=_params("arbitrary", "arbitrary"),
    )(dh3, dh3b, act, w_up_t, w_down, h2, g)


def _bwd_xattn(dh2, dh2b, h1, g, xq, xk, xv, wq, wo, sides=()):
    s, d = h1.shape
    m = xk.shape[0]
    tm = _row_tile(s)
    hd = d // XATT_HEADS
    scale = hd ** -0.5

    def body(dh_ref, dhb_ref, h_ref, g_ref, xq_ref, xk_ref, xv_ref, wq_ref, wo_ref,
             dh1_ref, dh1b_ref, dxq_ref, dxk_ref, dxv_ref, gg_ref):
        @pl.when(pl.program_id(0) == 0)
        def _():
            dxk_ref[...] = jnp.zeros_like(dxk_ref)
            dxv_ref[...] = jnp.zeros_like(dxv_ref)
            gg_ref[...] = jnp.zeros_like(gg_ref)

        dxo = _dot_nt(dhb_ref[...], wo_ref[...])
        for i in range(XATT_HEADS):
            cols = slice(i * hd, (i + 1) * hd)
            q, k, v = xq_ref[:, cols], xk_ref[:, cols], xv_ref[:, cols]
            pr = _xatt_probs(q, k)
            dxo_h = dxo[:, cols].astype(BF16)
            dpr = _dot_nt(dxo_h, v)
            dsc = (pr * (dpr - jnp.sum(dpr * pr, axis=-1, keepdims=True)) * scale).astype(BF16)
            dxq_ref[:, cols] = _dot(dsc, k).astype(BF16)
            dxk_ref[:, cols] += _dot_tn(dsc, q)
            dxv_ref[:, cols] += _dot_tn(pr.astype(BF16), dxo_h)
        dh, gg = _rms_bwd(h_ref[...], g_ref[...], _dot_nt(dxq_ref[...], wq_ref[...]))
        gg_ref[...] += jnp.sum(gg, axis=0, keepdims=True)
        dh1 = dh_ref[...] + dh
        dh1_ref[...] = dh1
        dh1b_ref[...] = dh1.astype(BF16)

    row = pl.BlockSpec((tm, d), lambda i: (i, 0))
    return _hosted_call(
        body, sides, name="bwd_xattn", grid=(s // tm,),
        in_specs=[row, row, row, _full((1, d)), row, _full((m, d)), _full((m, d)), _full((d, d)), _full((d, d))],
        out_specs=[row, row, row, _full((m, d)), _full((m, d)), _full((1, d))],
        out_shape=[jax.ShapeDtypeStruct((s, d), F32), jax.ShapeDtypeStruct((s, d), BF16),
                   jax.ShapeDtypeStruct((s, d), BF16), jax.ShapeDtypeStruct((m, d), F32),
                   jax.ShapeDtypeStruct((m, d), F32), jax.ShapeDtypeStruct((1, d), F32)],
        scratch_shapes=[],
        args=[dh2, dh2b, h1, g, xq, xk, xv, wq, wo])


def _bwd_mem(mem, g, mn, dxk, dxv, wk, wv):
    m, d = mem.shape

    def body(mem_ref, g_ref, mn_ref, dxk_ref, dxv_ref, wk_ref, wv_ref, gk_ref, gv_ref, gg_ref):
        dk, dv = dxk_ref[...].astype(BF16), dxv_ref[...].astype(BF16)
        gk_ref[...] = _dot_tn(mn_ref[...], dk).astype(BF16)
        gv_ref[...] = _dot_tn(mn_ref[...], dv).astype(BF16)
        dmn = _dot_nt(dk, wk_ref[...]) + _dot_nt(dv, wv_ref[...])
        _, gg = _rms_bwd(mem_ref[...], g_ref[...], dmn)
        gg_ref[...] = jnp.sum(gg, axis=0, keepdims=True)

    return pl.pallas_call(
        body, name="bwd_mem",
        out_shape=[jax.ShapeDtypeStruct((d, d), BF16), jax.ShapeDtypeStruct((d, d), BF16),
                   jax.ShapeDtypeStruct((1, d), F32)],
        compiler_params=_params(),
    )(mem, g, mn, dxk, dxv, wk, wv)


def _bwd_mix_out(dh1b, w_out, cat, head_ones):
    s, d = dh1b.shape
    tm = _row_tile(s)
    dils = DILATIONS[1:]
    nd = len(dils)

    def body(dh_ref, w_ref, cat_ref, ones_ref, dcat_ref, dsum_ref, *rest):
        da_str, ds_str, dbuf, sbuf = rest[:nd], rest[nd:2 * nd], rest[2 * nd], rest[2 * nd + 1]
        dcat = _dot_nt(dh_ref[...], w_ref[...])
        dcat_ref[...] = dcat.astype(BF16)
        datt = dcat[:, :D_ATT]
        prod = datt * cat_ref[...].astype(F32)
        hi = prod.astype(BF16)
        lo = (prod - hi.astype(F32)).astype(BF16)
        dsum = _dot(hi, ones_ref[...]) + _dot(lo, ones_ref[...])
        dsum_ref[...] = dsum
        _store_blocks(dbuf, datt)
        _store_blocks(sbuf, dsum)
        for da_ref, ds_ref in zip(da_str, ds_str):
            _write_strided(dbuf, da_ref)
            _write_strided(sbuf, ds_ref)

    row = lambda w: pl.BlockSpec((tm, w), lambda i: (i, 0))
    strided = [_strided_spec(dil, tm, D_ATT) for dil in dils]
    return pl.pallas_call(
        body, name="bwd_mix_out", grid=(s // tm,),
        in_specs=[row(d), _full((d, d)), row(D_ATT), _full((D_ATT, D_ATT))],
        out_specs=[row(d), row(D_ATT)] + strided + strided,
        out_shape=[jax.ShapeDtypeStruct((s, d), BF16), jax.ShapeDtypeStruct((s, D_ATT), F32)]
        + [_strided_shape(dil, s, D_ATT, BF16) for dil in dils]
        + [_strided_shape(dil, s, D_ATT, F32) for dil in dils],
        scratch_shapes=[_lane_scratch(tm, D_ATT)] * 2,
        compiler_params=_params("parallel"),
    )(dh1b, w_out, cat, head_ones)


def _bwd_conv(dcat, c1, ag, cw, lg, lb, sides=()):
    s = ag.shape[0]
    tm = _row_tile(s)
    nt = s // tm

    def body(dp, dm, dn, cp, cm, cn, agp, agm, agn, cw_ref, lg_ref, lb_ref,
             dag_ref, gcw_ref, gcb_ref, glg_ref, glb_ref, ubuf, ush, dbuf, dsh, gacc):
        i = pl.program_id(0)

        @pl.when(i == 0)
        def _():
            gacc[...] = jnp.zeros_like(gacc)
            gcb_ref[...] = jnp.zeros_like(gcb_ref)
            glg_ref[...] = jnp.zeros_like(glg_ref)
            glb_ref[...] = jnp.zeros_like(glb_ref)

        lgv, lbv = lg_ref[...], lb_ref[...]

        def norm_bwd(dco, c1v):
            xc = c1v - _mean(c1v)
            rs = lax.rsqrt(_mean(xc * xc) + EPS)
            z = xc * rs
            ln = z * lgv + lbv
            sg = _sigmoid(ln)
            dln = dco.astype(F32) * (sg * (1.0 + ln * (1.0 - sg)))
            dz = dln * lgv
            return rs * (dz - _mean(dz) - z * _mean(dz * z)), dln, z

        dc_m, dln, z = norm_bwd(dm[...], cm[...])
        glg_ref[...] += jnp.sum(dln * z, axis=0, keepdims=True)
        glb_ref[...] += jnp.sum(dln, axis=0, keepdims=True)
        gcb_ref[...] += jnp.sum(dc_m, axis=0, keepdims=True)
        _fill_halo(dbuf, i, nt, tm, norm_bwd(dp[...], cp[...])[0], dc_m, norm_bwd(dn[...], cn[...])[0])
        _fill_halo(ubuf, i, nt, tm, _glu(agp[...]), _glu(agm[...]), _glu(agn[...]))
        _shift_copies(dbuf, dsh, tm)
        _shift_copies(ubuf, ush, tm)

        w = cw_ref[...]

        def chunk(c, carry):
            base = pl.multiple_of(c * CONV_ROWS, CONV_ROWS)
            rows = pl.ds(base, CONV_ROWS)
            dc = dbuf[pl.ds(pl.multiple_of(base + BF16_ROWS, SUBLANES), CONV_ROWS), :]
            du = jnp.zeros((CONV_ROWS, D_CONV), F32)
            for k in range(CONV_WIDTH):
                du = du + _tap(dbuf, dsh, CONV_WIDTH - k, CONV_ROWS, base) * w[k:k + 1, :]
                prod = dc * _tap(ubuf, ush, k + 1, CONV_ROWS, base)
                part = prod[0:SUBLANES]
                for r in range(SUBLANES, CONV_ROWS, SUBLANES):
                    part = part + prod[r:r + SUBLANES]
                gacc[k] += part
            a = agm[rows, :D_CONV].astype(F32)
            sg = _sigmoid(agm[rows, D_CONV:].astype(F32))
            dag_ref[rows, :D_CONV] = (du * sg).astype(BF16)
            dag_ref[rows, D_CONV:] = (du * a * sg * (1.0 - sg)).astype(BF16)
            return carry

        lax.fori_loop(0, tm // CONV_ROWS, chunk, 0)

        @pl.when(i == nt - 1)
        def _():
            tap = lax.broadcasted_iota(jnp.int32, (32, D_CONV), 0)
            gcw = jnp.zeros((32, D_CONV), F32)
            for k in range(CONV_WIDTH):
                gcw = jnp.where(tap == k, jnp.sum(gacc[k], axis=0, keepdims=True), gcw)
            gcw_ref[...] = gcw

    vec = _full((1, D_CONV))
    nblk = s // BF16_ROWS
    return _hosted_call(
        body, sides, name="bwd_conv", grid=(nt,),
        in_specs=_halo_specs(tm, D_CONV, 1)(nblk) + _halo_specs(tm, D_CONV)(nblk) + _halo_specs(tm, 2 * D_CONV)(nblk)
        + [_full((32, D_CONV)), vec, vec],
        out_specs=[pl.BlockSpec((tm, 2 * D_CONV), lambda i: (i, 0)), _full((32, D_CONV)), vec, vec, vec],
        out_shape=[jax.ShapeDtypeStruct((s, 2 * D_CONV), BF16), jax.ShapeDtypeStruct((32, D_CONV), F32)]
        + [jax.ShapeDtypeStruct((1, D_CONV), F32)] * 3,
        scratch_shapes=_halo_scratch(tm) + _halo_scratch(tm) + [pltpu.VMEM((32, SUBLANES, D_CONV), F32)],
        args=[dcat, dcat, dcat, c1, c1, c1, ag, ag, ag, cw, lg, lb])


def _swa_bwd(qkv3, do3, lt3, ds3, band, name, sides=()):
    dil, length, _ = qkv3.shape
    nb = length // BQ
    scale = HEAD_DIM ** -0.5
    assert WIN == 2 * BQ and BQ == 2 * HALF

    def body(q_ref, kp, km, kn, vp, vm, vn, do_ref, l_ref, s_ref, band_ref, dq_ref, dkv_ref, kwin, vwin, pend, keep):
        j = pl.program_id(1)

        @pl.when(j == 0)
        def _():
            pend[...] = jnp.zeros_like(pend)
            keep[...] = jnp.zeros_like(keep)

        @pl.when(j < nb)
        def _():
            _fill_window(kwin, kp, km, kn)
            _fill_window(vwin, vp, vm, vn)
            bias = _window_bias(band_ref, j, length)
            first = _first_head()
            for pr in range(D_ATT // LANES):
                cols = slice(pr * LANES, (pr + 1) * LANES)
                qs = _stack_heads(q_ref[:, cols] * scale, first)
                dos = _stack_heads(do_ref[:, cols], first)
                kw, vw = kwin[:, cols], vwin[:, cols]
                p = jnp.exp(_dot_nt(qs, kw) + (bias - _stack_cols(l_ref[:, cols], first)))
                dp = _dot_nt(dos, vw)
                dsc = (p * (dp - _stack_cols(s_ref[:, cols], first))).astype(BF16)
                dq_ref[:, cols] = _unstack_heads(_dot(dsc, kw * scale), first).astype(BF16)
                for part, at in ((_dot_tn(dsc, qs), pr * LANES), (_dot_tn(p.astype(BF16), dos), D_ATT + pr * LANES)):
                    at = slice(at, at + LANES)
                    dkv_ref[:HALF, at] = keep[:, at].astype(BF16)
                    dkv_ref[HALF:, at] = (pend[:HALF, at] + part[:HALF]).astype(BF16)
                    keep[:, at] = pend[HALF:, at] + part[HALF:BQ]
                    pend[:, at] = part[BQ:]

        @pl.when(j == nb)
        def _():
            dkv_ref[:HALF] = keep[...].astype(BF16)
            dkv_ref[HALF:] = pend[:HALF].astype(BF16)

    def clamp(idx):
        return lambda r, j: idx(r, jnp.minimum(j, nb - 1))

    main = pl.BlockSpec((None, BQ, D_ATT), clamp(lambda r, j: (r, j, 0)))
    wins = [pl.BlockSpec(sp.block_shape, clamp(sp.index_map))
            for c in (1, 2) for sp in _win_in_specs(length, c, D_ATT)]
    return _hosted_call(
        body, sides, name=name, grid=(dil, nb + 1),
        in_specs=[main] + wins + [main] * 3 + [_full((2 * BQ, WIN))],
        out_specs=[main, pl.BlockSpec((None, BQ, 2 * D_ATT), lambda r, j: (r, jnp.maximum(j - 1, 0), 0))],
        out_shape=[jax.ShapeDtypeStruct((dil, length, D_ATT), BF16),
                   jax.ShapeDtypeStruct((dil, length, 2 * D_ATT), BF16)],
        scratch_shapes=[pltpu.VMEM((WIN, D_ATT), BF16)] * 2
        + [pltpu.VMEM((BQ, 2 * D_ATT), F32), pltpu.VMEM((HALF, 2 * D_ATT), F32)],
        args=[qkv3] * 7 + [do3, lt3, ds3, band])


def _bwd_in(dqs, dkvs, dag, w_in, x, g, dh1, rot):
    s, d = x.shape
    n = w_in.shape[1]
    tm = _row_tile(s)
    dils = DILATIONS[1:]
    nd = len(dils)

    def body(q1, *rest):
        q_str, kv1, kv_str = rest[:nd], rest[nd], rest[nd + 1:2 * nd + 1]
        dag_ref, w_ref, x_ref, g_ref, dh_ref, c_ref, a_ref, b_ref, gx_ref, dy_ref, gg_ref, qbuf, kvbuf = rest[2 * nd + 1:]

        @pl.when(pl.program_id(0) == 0)
        def _():
            gg_ref[...] = jnp.zeros_like(gg_ref)

        _store_blocks(qbuf, q1[0].astype(F32))
        _store_blocks(kvbuf, kv1[0].astype(F32))
        for ref in q_str:
            _read_strided(ref, qbuf, add=True)
        for ref in kv_str:
            _read_strided(ref, kvbuf, add=True)
        dq, dkv = _load_blocks(qbuf), _load_blocks(kvbuf)
        reps = (1, D_ATT // LANES)
        cc, aa, bb = jnp.tile(c_ref[...], reps), jnp.tile(a_ref[...], reps), jnp.tile(b_ref[...], reps)
        for blk, t in enumerate((dq, dkv[:, :D_ATT])):
            dt = t * cc + pltpu.roll(t * aa, ROT_DIM // 2, 1) + pltpu.roll(t * bb, D_ATT - ROT_DIM // 2, 1)
            dy_ref[:, blk * D_ATT:(blk + 1) * D_ATT] = dt.astype(BF16)
        dy_ref[:, 2 * D_ATT:3 * D_ATT] = dkv[:, D_ATT:].astype(BF16)
        dy_ref[:, 3 * D_ATT:] = dag_ref[...]
        dx, gg = _rms_bwd(x_ref[...], g_ref[...], _dot_nt(dy_ref[...], w_ref[...]))
        gg_ref[...] += jnp.sum(gg, axis=0, keepdims=True)
        gx_ref[...] = dh_ref[...] + dx

    row = lambda w: pl.BlockSpec((tm, w), lambda i: (i, 0))
    def strided(width):
        return [pl.BlockSpec((1, tm, width), lambda i: (0, i, 0))] + [_strided_spec(dil, tm, width) for dil in dils]

    return pl.pallas_call(
        body, name="bwd_in", grid=(s // tm,),
        in_specs=strided(D_ATT) + strided(2 * D_ATT)
        + [row(2 * D_CONV), _full((d, n)), row(d), _full((1, d)), row(d)] + [row(LANES)] * 3,
        out_specs=[row(d), row(n), _full((1, d))],
        out_shape=[jax.ShapeDtypeStruct((s, d), F32), jax.ShapeDtypeStruct((s, n), BF16),
                   jax.ShapeDtypeStruct((1, d), F32)],
        scratch_shapes=[_lane_scratch(tm, D_ATT), _lane_scratch(tm, 2 * D_ATT)],
        compiler_params=_params("arbitrary"),
    )(*dqs, *dkvs, dag, w_in, x, g, dh1, *rot)


def _wgrad(a, b, name, a_blk=None, b_blk=None, stack=None, tm=1024):
    s, ka = a.shape
    nb = b.shape[1]
    a_blk, b_blk = a_blk or ka, b_blk or nb
    na, nbl = ka // a_blk, nb // b_blk
    assert na == 1 or nbl == 1
    tm = min(tm, s)
    nt = s // tm
    per = b_blk // stack if stack else 0

    def body(a_ref, b_ref, o_ref, acc):
        t = pl.program_id(1)

        @pl.when(t == 0)
        def _():
            acc[...] = jnp.zeros_like(acc)

        acc[...] += _dot_tn(a_ref[...], b_ref[...])

        @pl.when(t == nt - 1)
        def _():
            if stack:
                for c in range(per):
                    o_ref[c] = acc[:, c * stack:(c + 1) * stack].astype(BF16)
            else:
                o_ref[...] = acc[...].astype(BF16)

    if stack:
        out_spec = pl.BlockSpec((per, ka, stack), lambda k, t: (k, 0, 0))
        out_shape = jax.ShapeDtypeStruct((nb // stack, ka, stack), BF16)
    elif na > 1:
        out_spec = pl.BlockSpec((a_blk, nb), lambda k, t: (k, 0))
        out_shape = jax.ShapeDtypeStruct((ka, nb), BF16)
    else:
        out_spec = pl.BlockSpec((ka, b_blk), lambda k, t: (0, k))
        out_shape = jax.ShapeDtypeStruct((ka, nb), BF16)
    return pl.pallas_call(
        body, name=name, grid=(na * nbl, nt),
        in_specs=[pl.BlockSpec((tm, a_blk), (lambda k, t: (t, k)) if na > 1 else (lambda k, t: (t, 0))),
                  pl.BlockSpec((tm, b_blk), (lambda k, t: (t, k)) if nbl > 1 else (lambda k, t: (t, 0)))],
        out_specs=out_spec, out_shape=out_shape,
        scratch_shapes=[pltpu.VMEM((a_blk, b_blk), F32)],
        compiler_params=_params("parallel", "arbitrary"),
    )(a, b)


def _adamw(w, gsrc, m, v, name, transposed=False):
    summed = gsrc.ndim == w.ndim + 1
    rows, cols = w.shape
    assert gsrc.shape[-2:] == ((cols, rows) if transposed else (rows, cols)) and (summed or not transposed)
    tr = rows if rows <= 256 else 256
    assert rows % tr == 0
    c1 = 1.0 - ADAM_B1 ** ADAM_STEP
    c2 = 1.0 - ADAM_B2 ** ADAM_STEP

    def body(w_ref, g_ref, m_ref, v_ref, go_ref, d_ref, mo_ref, vo_ref):
        if summed:
            g = g_ref[0].astype(F32)
            for i in range(1, N_DEV):
                g = g + g_ref[i].astype(F32)
            if transposed:
                g = g.T
        else:
            g = g_ref[...]
        mn = ADAM_B1 * m_ref[...] + (1.0 - ADAM_B1) * g
        vn = ADAM_B2 * v_ref[...] + (1.0 - ADAM_B2) * jnp.square(g)
        go_ref[...] = g
        mo_ref[...] = mn
        vo_ref[...] = vn
        d_ref[...] = -ADAM_LR * ((mn / c1) / (jnp.sqrt(vn / c2) + ADAM_EPS) + ADAM_WD * w_ref[...])

    blk = pl.BlockSpec((tr, cols), lambda i: (i, 0))
    if transposed:
        gblk = pl.BlockSpec((N_DEV, cols, tr), lambda i: (0, 0, i))
    else:
        gblk = pl.BlockSpec((N_DEV, tr, cols), lambda i: (0, i, 0)) if summed else blk
    return pl.pallas_call(
        body, name=name, grid=(rows // tr,),
        in_specs=[blk, gblk, blk, blk], out_specs=[blk] * 4,
        out_shape=[jax.ShapeDtypeStruct(w.shape, F32)] * 4,
        compiler_params=_params("parallel"),
    )(w, gsrc, m, v)


def _sum_slots(g, name):
    _, rows, cols = g.shape

    def body(g_ref, o_ref):
        acc = g_ref[0]
        for i in range(1, N_DEV):
            acc = acc + g_ref[i]
        o_ref[...] = acc

    return pl.pallas_call(body, name=name, out_shape=jax.ShapeDtypeStruct((rows, cols), F32),
                          compiler_params=_params())(g)


def kernel(x, mem, norm_mix_g, w_in, conv_w, conv_b, conv_ln_g, conv_ln_b, w_out, norm_x_g, norm_mem_g, w_xq, w_xk, w_xv, w_xo, norm_mlp_g, w_up, w_down, norm_final_g, loss_target, m_norm_mix_g, m_w_in, m_conv_w, m_conv_b, m_conv_ln_g, m_conv_ln_b, m_w_out, m_norm_x_g, m_norm_mem_g, m_w_xq, m_w_xk, m_w_xv, m_w_xo, m_norm_mlp_g, m_w_up, m_w_down, m_norm_final_g, v_norm_mix_g, v_w_in, v_conv_w, v_conv_b, v_conv_ln_g, v_conv_ln_b, v_w_out, v_norm_x_g, v_norm_mem_g, v_w_xq, v_w_xk, v_w_xv, v_w_xo, v_norm_mlp_g, v_w_up, v_w_down, v_norm_final_g):
    x2, mem2, tgt = x[0], mem[0], loss_target[0]
    s, d = x2.shape
    gf = norm_final_g[None, :]

    cw_local = jnp.pad(conv_w[0], ((0, 1), (0, LANES - conv_w.shape[2])))
    win_g, cw_g = _exchange_call([_Exchange([w_in[0].astype(BF16), cw_local], gather=True)], "gather_w_in")
    w_in_f = jnp.transpose(win_g, (1, 0, 2)).reshape(d, -1)
    cw_f = jnp.transpose(cw_g[:, :, :conv_w.shape[2]], (1, 0, 2)).reshape(32, D_CONV)
    row_names = (w_out, w_xq, w_xk, w_xv, w_xo)
    rows_local = jnp.concatenate([w[0].astype(BF16) for w in row_names], axis=0)
    late = [_Exchange([rows_local], gather=True), _Exchange([w_up[0].astype(BF16)], gather=True),
            _Exchange([w_down[0].astype(BF16)], gather=True)]

    rot = _rotary_tables(s)
    band = _band_bias()
    xn, qkv, ag, *qkv_strided = _fwd_in(x2, norm_mix_g, w_in_f, rot)
    qkv3 = [qkv[None]] + qkv_strided
    outs, lses, gathered = [], [], []
    for dil, q3, side in zip(DILATIONS, qkv3, late):
        (o3, l3), got = _swa_fwd(q3, band, f"swa_fwd_d{dil}", [side])
        outs.append(o3)
        lses.append(l3)
        gathered += got
    rows_g, wup_g, wdown_g = gathered
    full_rows = []
    off = 0
    for w in row_names:
        full_rows.append(rows_g[:, off:off + w.shape[1], :].reshape(N_DEV * w.shape[1], d))
        off += w.shape[1]
    w_out_f, w_xq_f, w_xk_f, w_xv_f, w_xo_f = full_rows
    w_down_f = wdown_g.reshape(-1, d)
    w_up_t = jnp.swapaxes(wup_g, 1, 2).reshape(-1, d)
    c1, conv_out = _fwd_conv(ag, cw_f, conv_b, conv_ln_g, conv_ln_b)
    h1, cat, ltot, *lt_strided = _fwd_mix_out(outs, lses, conv_out, x2, w_out_f)
    mn, xk, xv = _fwd_mem(mem2, norm_mem_g, w_xk_f, w_xv_f)
    h2, hn2, xq, xo = _fwd_xattn(h1, norm_x_g, w_xq_f, xk, xv, w_xo_f)
    hn3, act, dh3, dh3b, loss_part, g_final = _fwd_mlp_loss(h2, norm_mlp_g, wup_g, w_down_f, gf, tgt)

    def scatter(*grads):
        return _Exchange([g.reshape(N_DEV, -1, g.shape[-1]) for g in grads], gather=False)

    f_blk = w_up.shape[2]
    du, dh2, dh2b, g_mlp = _bwd_mlp(dh3, dh3b, act, w_up_t, w_down_f, h2, norm_mlp_g)
    gw_up = _wgrad(hn3, du, "wgrad_up", b_blk=4 * f_blk, stack=f_blk)
    gw_down = _wgrad(dh3b, act, "wgrad_down", b_blk=4 * f_blk, stack=f_blk)
    (dh1, dh1b, dxq, dxk, dxv, g_x), (r_up,) = _bwd_xattn(
        dh2, dh2b, h1, norm_x_g, xq, xk, xv, w_xq_f, w_xo_f, [scatter(gw_up)])
    gw_xq = _wgrad(hn2, dxq, "wgrad_xq")
    gw_xo = _wgrad(xo, dh2b, "wgrad_xo")
    gw_xk, gw_xv, g_mem = _bwd_mem(mem2, norm_mem_g, mn, dxk, dxv, w_xk_f, w_xv_f)
    head = jnp.arange(D_ATT) // HEAD_DIM
    head_ones = (head[:, None] == head[None, :]).astype(BF16)
    dcat, dsum, *strided = _bwd_mix_out(dh1b, w_out_f, cat, head_ones)
    n_str = len(DILATIONS) - 1
    do3, lt3, ds3 = [dcat[None]] + strided[:n_str], [ltot[None]] + lt_strided, [dsum[None]] + strided[n_str:]
    gw_out = _wgrad(cat, dh1b, "wgrad_out")
    (dag, g_cw, g_cb, g_lg, g_lb), (r_down,) = _bwd_conv(dcat, c1, ag, cw_f, conv_ln_g, conv_ln_b, [scatter(gw_down)])
    hosted = [[scatter(gw_out, gw_xq, gw_xk, gw_xv, gw_xo)], [], []]
    dqs, dkvs, landed = [], [], []
    for i, dil in enumerate(DILATIONS):
        (dq3, dkv3), got = _swa_bwd(qkv3[i], do3[i], lt3[i], ds3[i], band, f"swa_bwd_d{dil}", hosted[i])
        dqs.append(dq3)
        dkvs.append(dkv3)
        landed += got
    r_out, r_xq, r_xk, r_xv, r_xo = landed
    grad_x, dy, g_mix = _bwd_in(dqs, dkvs, dag, w_in_f, x2, norm_mix_g, dh1, rot)
    gw_in = _wgrad(xn, dy, "wgrad_in", b_blk=dy.shape[1] // 2)

    def widen(t):
        return jnp.pad(t, ((0, 0), (0, d - t.shape[1])))

    n_in = w_in.shape[2]
    small = jnp.concatenate([g_mix, g_x, g_mem, g_mlp, g_final, widen(g_cb), widen(g_lg), widen(g_lb),
                             g_cw.reshape(16, d)], axis=0)
    r_in, small_g = _exchange_call(
        [_Exchange([jnp.transpose(gw_in.reshape(d, N_DEV, n_in), (1, 0, 2))], gather=False),
         _Exchange([small], gather=True)], "scatter_w_in_gather_small")
    small_sum = _sum_slots(small_g, "sum_small_grads")
    loss = lax.psum(loss_part[0, 0], ("x", "y", "c"))

    res = {}

    def step(name, w, gsrc, m, v, transposed=False):
        shape = w.shape
        w2, m2, v2 = (t.reshape(-1, shape[-1]) for t in (w, m, v))
        res[name] = [t.reshape(shape) for t in _adamw(w2, gsrc, m2, v2, "adamw_" + name, transposed)]

    step("w_in", w_in, r_in, m_w_in, v_w_in)
    step("w_up", w_up, r_up, m_w_up, v_w_up)
    step("w_out", w_out, r_out, m_w_out, v_w_out)
    step("w_xq", w_xq, r_xq, m_w_xq, v_w_xq)
    step("w_xk", w_xk, r_xk, m_w_xk, v_w_xk)
    step("w_xv", w_xv, r_xv, m_w_xv, v_w_xv)
    step("w_xo", w_xo, r_xo, m_w_xo, v_w_xo)
    step("w_down", w_down, r_down, m_w_down, v_w_down, transposed=True)

    me = _dev_index((lax.axis_index("x"), lax.axis_index("y"), lax.axis_index("c")))
    n_cw = conv_w.shape[2]
    g_cw_full = small_sum[8:24].reshape(32, D_CONV)[:CONV_WIDTH]
    g_cw_mine = lax.dynamic_slice_in_dim(g_cw_full, me * n_cw, n_cw, axis=1)
    step("conv_w", conv_w, g_cw_mine, m_conv_w, v_conv_w)

    vec_names = ["norm_mix_g", "norm_x_g", "norm_mem_g", "norm_mlp_g", "norm_final_g", "conv_b", "conv_ln_g", "conv_ln_b"]
    vec_w = [norm_mix_g, norm_x_g, norm_mem_g, norm_mlp_g, gf, conv_b, conv_ln_g, conv_ln_b]
    vec_m = [m_norm_mix_g, m_norm_x_g, m_norm_mem_g, m_norm_mlp_g, m_norm_final_g[None, :], m_conv_b, m_conv_ln_g, m_conv_ln_b]
    vec_v = [v_norm_mix_g, v_norm_x_g, v_norm_mem_g, v_norm_mlp_g, v_norm_final_g[None, :], v_conv_b, v_conv_ln_g, v_conv_ln_b]

    def pack(ts):
        return jnp.concatenate([widen(t) for t in ts], axis=0)

    packed = _adamw(pack(vec_w), small_sum[0:8], pack(vec_m), pack(vec_v), "adamw_vectors")
    for i, name in enumerate(vec_names):
        width = vec_w[i].shape[1]
        shape = (width,) if name == "norm_final_g" else (1, width)
        res[name] = [t[i, :width].reshape(shape) for t in packed]

    order = ["norm_mix_g", "w_in", "conv_w", "conv_b", "conv_ln_g", "conv_ln_b", "w_out", "norm_x_g", "norm_mem_g",
             "w_xq", "w_xk", "w_xv", "w_xo", "norm_mlp_g", "w_up", "w_down", "norm_final_g"]
    out = [loss, grad_x[None]]
    for kind in range(4):
        out += [res[name][kind] for name in order]
    return tuple(out)
```

```python
import jax
import jax.numpy as jnp
from jax import lax
from jax.experimental import pallas as pl
from jax.experimental.pallas import tpu as pltpu

F32 = jnp.float32
BF16 = jnp.bfloat16

N_DEV = 8
EPS = 1e-6
NEG_INF = -1e30
ATT_HEADS = 8
HEAD_DIM = 64
D_ATT = ATT_HEADS * HEAD_DIM
D_CONV = 512
DILATIONS = (1, 4, 16)
HALF = 64
ROPE_THETA = 500000.0
ROT_DIM = HEAD_DIM // 4
CONV_WIDTH = 31
CONV_PAD = (CONV_WIDTH - 1) // 2
XATT_HEADS = 4
ADAM_LR = 0.001
ADAM_B1 = 0.9
ADAM_B2 = 0.999
ADAM_EPS = 1e-08
ADAM_WD = 0.01
ADAM_STEP = 10

LANES = 128
SUBLANES = 8
BF16_ROWS = 16
BQ = 128
WIN = BQ + 2 * HALF
MLP_SHARDS = 4
CONV_ROWS = 32
VMEM_LIMIT = 56 * 1024 * 1024
MESH = pl.DeviceIdType.MESH
ANY = pl.BlockSpec(memory_space=pl.ANY)

_NT = (((1,), (1,)), ((), ()))
_TN = (((0,), (0,)), ((), ()))


def _dot(a, b):
    return jnp.dot(a, b, preferred_element_type=F32)


def _dot_nt(a, b):
    return lax.dot_general(a, b, _NT, preferred_element_type=F32)


def _dot_tn(a, b):
    return lax.dot_general(a, b, _TN, preferred_element_type=F32)


def _params(*sem):
    return pltpu.CompilerParams(dimension_semantics=sem or None, vmem_limit_bytes=VMEM_LIMIT)


def _sigmoid(v):
    return 1.0 / (1.0 + jnp.exp(-v))


def _mean(v):
    return jnp.mean(v, axis=-1, keepdims=True)


def _rms_fwd(h, g):
    r = lax.rsqrt(_mean(h * h) + EPS)
    return h * r * g, r


def _rms_bwd(h, g, d_out):
    r = lax.rsqrt(_mean(h * h) + EPS)
    hn = h * r
    gd = d_out * g
    return r * (gd - hn * _mean(gd * hn)), d_out * hn


def _row_tile(s):
    return min(512, s)


def _full(shape):
    return pl.BlockSpec(shape, lambda *_: (0,) * len(shape))


def _mesh_pos():
    return lax.axis_index("x"), lax.axis_index("y"), lax.axis_index("c")


def _dev_index(p):
    return 4 * p[0] + 2 * p[1] + p[2]


class _Exchange:
    def __init__(self, arrays, gather):
        self.arrays, self.gather, self.n = list(arrays), gather, len(arrays)

    def out_shapes(self):
        return [jax.ShapeDtypeStruct(((N_DEV,) + a.shape) if self.gather else a.shape, a.dtype)
                for a in self.arrays]

    def sem_shapes(self):
        return [pltpu.SemaphoreType.DMA((7 * self.n,)), pltpu.SemaphoreType.DMA((7 * self.n,)),
                pltpu.SemaphoreType.DMA((self.n,))]

    def phases(self, x_refs, o_refs, send_sems, recv_sems, local_sems):
        n = self.n
        x, y, c = _mesh_pos()
        me, sibling = (x, y, c), (x, y, 1 - c)

        if self.gather:
            chips = [(1 - x, y), (x, 1 - y), (1 - x, 1 - y)]

            def copy(a, k, block, to, src=None):
                slot = o_refs[a].at[_dev_index(block)]
                return pltpu.make_async_remote_copy(
                    src_ref=slot if src is None else src, dst_ref=slot,
                    send_sem=send_sems.at[7 * a + k], recv_sem=recv_sems.at[7 * a + k],
                    device_id=to, device_id_type=MESH)

            def mine(a):
                return pltpu.make_async_copy(x_refs[a], o_refs[a].at[_dev_index(me)], local_sems.at[a])

            def first(a):
                return [copy(a, 0, me, sibling, src=x_refs[a])] + [
                    copy(a, 1 + j, me, (*chip, c), src=x_refs[a]) for j, chip in enumerate(chips)]

            def relayed(a, j):
                return copy(a, 4 + j, (*chips[j], c), sibling)

            def start():
                for a in range(n):
                    mine(a).start()
                    for cp in first(a):
                        cp.start()

            def relay():
                for j, chip in enumerate(chips):
                    for a in range(n):
                        copy(a, 1 + j, (*chip, c), me).wait_recv()
                        relayed(a, j).start()

            def finish():
                for a in range(n):
                    copy(a, 0, sibling, me).wait_recv()
                    for j, chip in enumerate(chips):
                        copy(a, 4 + j, (*chip, 1 - c), me).wait_recv()
                    for cp in first(a) + [relayed(a, j) for j in range(3)]:
                        cp.wait_send()
                    mine(a).wait()

            return start, relay, finish

        flips = [(dx, dy, dc) for dx in (0, 1) for dy in (0, 1) for dc in (0, 1)][1:]

        def peer(k):
            return tuple(1 - v if fl else v for v, fl in zip(me, flips[k]))

        def send(a, k):
            return pltpu.make_async_remote_copy(
                src_ref=x_refs[a].at[_dev_index(peer(k))], dst_ref=o_refs[a].at[_dev_index(me)],
                send_sem=send_sems.at[7 * a + k], recv_sem=recv_sems.at[7 * a + k],
                device_id=peer(k), device_id_type=MESH)

        def landed(a, k):
            slot = o_refs[a].at[_dev_index(peer(k))]
            return pltpu.make_async_remote_copy(
                src_ref=slot, dst_ref=slot, send_sem=send_sems.at[7 * a + k], recv_sem=recv_sems.at[7 * a + k],
                device_id=peer(k), device_id_type=MESH)

        def own(a):
            return pltpu.make_async_copy(x_refs[a].at[_dev_index(me)], o_refs[a].at[_dev_index(me)],
                                         local_sems.at[a])

        def start():
            for a in range(n):
                own(a).start()
            for k in range(7):
                for a in range(n):
                    send(a, k).start()

        def finish():
            for k in range(7):
                for a in range(n):
                    landed(a, k).wait_recv()
            for k in range(7):
                for a in range(n):
                    send(a, k).wait_send()
            for a in range(n):
                own(a).wait()

        return start, (lambda: None), finish


def _hosted_call(body, sides, *, name, grid, in_specs, out_specs, out_shape, scratch_shapes, args):
    n_in, n_out, ns = len(in_specs), len(out_specs), sum(s.n for s in sides)
    steps = 1
    for g in grid:
        steps *= g

    def wrapped(*refs):
        ins, s_ins = refs[:n_in], refs[n_in:n_in + ns]
        outs = refs[n_in + ns:n_in + ns + n_out]
        s_outs = refs[n_in + ns + n_out:n_in + 2 * ns + n_out]
        rest = refs[n_in + 2 * ns + n_out:]
        scratch, sems = rest[:len(rest) - 3 * len(sides)], rest[len(rest) - 3 * len(sides):]
        phases, off = [], 0
        for i, s in enumerate(sides):
            phases.append(s.phases(s_ins[off:off + s.n], s_outs[off:off + s.n], *sems[3 * i:3 * i + 3]))
            off += s.n
        lin = 0
        for ax, g in enumerate(grid):
            lin = lin * g + pl.program_id(ax)

        if sides:
            @pl.when(lin == 0)
            def _():
                for start, _, _ in phases:
                    start()

        body(*ins, *outs, *scratch)

        if sides:
            @pl.when(lin == min((3 * steps) // 4, steps - 1))
            def _():
                for _, relay, _ in phases:
                    relay()

            @pl.when(lin == steps - 1)
            def _():
                for _, _, finish in phases:
                    finish()

    res = pl.pallas_call(
        wrapped, name=name, grid=grid,
        in_specs=list(in_specs) + [ANY] * ns, out_specs=list(out_specs) + [ANY] * ns,
        out_shape=list(out_shape) + [sh for s in sides for sh in s.out_shapes()],
        scratch_shapes=list(scratch_shapes) + [sh for s in sides for sh in s.sem_shapes()],
        compiler_params=_params(*(("arbitrary",) * len(grid))),
    )(*args, *[a for s in sides for a in s.arrays])
    return res[:n_out], res[n_out:]


def _exchange_call(sides, name):
    ns = sum(s.n for s in sides)

    def body(*refs):
        x_refs, o_refs, sems = refs[:ns], refs[ns:2 * ns], refs[2 * ns:]
        phases, off = [], 0
        for i, s in enumerate(sides):
            phases.append(s.phases(x_refs[off:off + s.n], o_refs[off:off + s.n], *sems[3 * i:3 * i + 3]))
            off += s.n
        for step in range(3):
            for ph in phases:
                ph[step]()

    return pl.pallas_call(
        body, name=name,
        out_shape=[sh for s in sides for sh in s.out_shapes()],
        in_specs=[ANY] * ns, out_specs=[ANY] * ns,
        scratch_shapes=[sh for s in sides for sh in s.sem_shapes()],
    )(*[a for s in sides for a in s.arrays])


def _rotary_tables(s):
    half = ROT_DIM // 2
    freqs = ROPE_THETA ** (-jnp.arange(0, ROT_DIM, 2, dtype=F32) / ROT_DIM)
    ang = jnp.arange(s, dtype=F32)[:, None] * freqs[None, :]
    cos, sin = jnp.cos(ang), jnp.sin(ang)
    one = jnp.ones((s, HEAD_DIM - ROT_DIM), F32)
    zero = jnp.zeros((s, HEAD_DIM - ROT_DIM), F32)
    zh = jnp.zeros((s, half), F32)
    c64 = jnp.concatenate([cos, cos, one], axis=1)
    a64 = jnp.concatenate([-sin, zh, zero], axis=1)
    b64 = jnp.concatenate([zh, sin, zero], axis=1)
    return tuple(jnp.tile(t, (1, LANES // HEAD_DIM)) for t in (c64, a64, b64))


def _strided_spec(dil, tm, width):
    return pl.BlockSpec((dil, tm // dil, width), lambda i: (0, i, 0))


def _strided_shape(dil, s, width, dtype):
    return jax.ShapeDtypeStruct((dil, s // dil, width), dtype)


def _lane_scratch(tm, width):
    return pltpu.VMEM((width // LANES, tm, LANES), F32)


def _store_blocks(buf, v):
    for cb in range(buf.shape[0]):
        buf[cb] = v[:, cb * LANES:(cb + 1) * LANES]


def _load_blocks(buf):
    return jnp.concatenate([buf[cb] for cb in range(buf.shape[0])], axis=1)


def _write_strided(buf, dst_ref):
    dil, rows, _ = dst_ref.shape
    for r in range(dil):
        for cb in range(buf.shape[0]):
            dst_ref[r, :, cb * LANES:(cb + 1) * LANES] = buf[cb, pl.ds(r, rows, stride=dil), :].astype(dst_ref.dtype)


def _read_strided(src_ref, buf, add=False):
    dil, rows, _ = src_ref.shape
    for r in range(dil):
        for cb in range(buf.shape[0]):
            v = src_ref[r, :, cb * LANES:(cb + 1) * LANES].astype(F32)
            if add:
                v = v + buf[cb, pl.ds(r, rows, stride=dil), :]
            buf[cb, pl.ds(r, rows, stride=dil), :] = v


def _fwd_in(x, g, w_in, rot):
    s, d = x.shape
    n = w_in.shape[1]
    tm = _row_tile(s)
    dils = DILATIONS[1:]

    def body(x_ref, g_ref, w_ref, c_ref, a_ref, b_ref, xn_ref, qkv_ref, ag_ref, *rest):
        strided, ybuf = rest[:len(dils)], rest[len(dils)]
        xn = _rms_fwd(x_ref[...], g_ref[...])[0].astype(BF16)
        xn_ref[...] = xn
        y = _dot(xn, w_ref[...])
        reps = (1, D_ATT // LANES)
        cc, aa, bb = jnp.tile(c_ref[...], reps), jnp.tile(a_ref[...], reps), jnp.tile(b_ref[...], reps)
        parts = []
        for blk in range(2):
            t = y[:, blk * D_ATT:(blk + 1) * D_ATT]
            parts.append(t * cc + pltpu.roll(t, D_ATT - ROT_DIM // 2, 1) * aa + pltpu.roll(t, ROT_DIM // 2, 1) * bb)
        qkv = jnp.concatenate(parts + [y[:, 2 * D_ATT:3 * D_ATT]], axis=1)
        qkv_ref[...] = qkv.astype(BF16)
        _store_blocks(ybuf, qkv)
        for ref in strided:
            _write_strided(ybuf, ref)
        ag_ref[...] = y[:, 3 * D_ATT:].astype(BF16)

    row = lambda w: pl.BlockSpec((tm, w), lambda i: (i, 0))
    return pl.pallas_call(
        body, name="fwd_in", grid=(s // tm,),
        in_specs=[row(d), _full((1, d)), _full((d, n)), row(LANES), row(LANES), row(LANES)],
        out_specs=[row(d), row(3 * D_ATT), row(2 * D_CONV)] + [_strided_spec(dil, tm, 3 * D_ATT) for dil in dils],
        out_shape=[jax.ShapeDtypeStruct((s, d), BF16), jax.ShapeDtypeStruct((s, 3 * D_ATT), BF16),
                   jax.ShapeDtypeStruct((s, 2 * D_CONV), BF16)]
        + [_strided_shape(dil, s, 3 * D_ATT, BF16) for dil in dils],
        scratch_shapes=[_lane_scratch(tm, 3 * D_ATT)],
        compiler_params=_params("parallel"),
    )(x, g, w_in, *rot)


def _win_in_specs(length, col, width):
    per = BQ // HALF
    last = length // HALF - 1
    return [
        pl.BlockSpec((None, HALF, width), lambda r, j: (r, jnp.maximum(j * per - 1, 0), col)),
        pl.BlockSpec((None, BQ, width), lambda r, j: (r, j, col)),
        pl.BlockSpec((None, HALF, width), lambda r, j: (r, jnp.minimum(j * per + per, last), col)),
    ]


def _fill_window(win, prev_ref, main_ref, next_ref):
    win[0:HALF] = prev_ref[...]
    win[HALF:HALF + BQ] = main_ref[...]
    win[HALF + BQ:] = next_ref[...]


def _band_bias():
    blk = jnp.arange(2 * BQ)[:, None] & (BQ - 1)
    win = jnp.arange(WIN)[None, :]
    return jnp.where(jnp.abs(win - HALF - blk) <= HALF, 0.0, NEG_INF).astype(F32)


def _window_bias(band_ref, j, length):
    pos = j * BQ - HALF + lax.broadcasted_iota(jnp.int32, (1, WIN), 1)
    return band_ref[...] + jnp.where((pos >= 0) & (pos < length), 0.0, NEG_INF)


def _first_head():
    return lax.broadcasted_iota(jnp.int32, (1, LANES), 1) < HEAD_DIM


def _stack_heads(v, first):
    zero = jnp.zeros((), v.dtype)
    return jnp.concatenate([jnp.where(first, v, zero), jnp.where(first, zero, v)], axis=0)


def _unstack_heads(v, first):
    rows = v.shape[0] // 2
    return jnp.where(first, v[:rows], v[rows:])


def _stack_cols(v, first):
    other = pltpu.roll(v, HEAD_DIM, 1)
    stacked = jnp.concatenate([jnp.where(first, v, other), jnp.where(first, other, v)], axis=0)
    return jnp.tile(stacked, (1, WIN // LANES))


def _swa_fwd(qkv3, band, name, sides=()):
    dil, length, _ = qkv3.shape
    scale = HEAD_DIM ** -0.5

    def body(q_ref, kp, km, kn, vp, vm, vn, band_ref, o_ref, lse_ref, kwin, vwin):
        j = pl.program_id(1)
        _fill_window(kwin, kp, km, kn)
        _fill_window(vwin, vp, vm, vn)
        bias = _window_bias(band_ref, j, length)
        first = _first_head()
        pairs = [slice(pr * LANES, (pr + 1) * LANES) for pr in range(D_ATT // LANES)]
        scores = [_dot_nt(_stack_heads(q_ref[:, cols] * scale, first), kwin[:, cols]) + bias for cols in pairs]
        stats, probs = [], []
        for sc in scores:
            m = jnp.max(sc, axis=-1, keepdims=True)
            p = jnp.exp(sc - m)
            stats.append((m, jnp.sum(p, axis=-1, keepdims=True)))
            probs.append(p.astype(BF16))
        for cols, p, (m, den) in zip(pairs, probs, stats):
            pv = _dot(p, vwin[:, cols]) * (1.0 / den)
            o_ref[:, cols] = _unstack_heads(pv, first).astype(BF16)
            lse_ref[:, cols] = _unstack_heads(jnp.broadcast_to(m + jnp.log(den), (2 * BQ, LANES)), first)

    blk = lambda w: pl.BlockSpec((None, BQ, w), lambda r, j: (r, j, 0))
    return _hosted_call(
        body, sides, name=name, grid=(dil, length // BQ),
        in_specs=[pl.BlockSpec((None, BQ, D_ATT), lambda r, j: (r, j, 0))]
        + _win_in_specs(length, 1, D_ATT) + _win_in_specs(length, 2, D_ATT) + [_full((2 * BQ, WIN))],
        out_specs=[blk(D_ATT), blk(D_ATT)],
        out_shape=[jax.ShapeDtypeStruct((dil, length, D_ATT), BF16),
                   jax.ShapeDtypeStruct((dil, length, D_ATT), F32)],
        scratch_shapes=[pltpu.VMEM((WIN, D_ATT), BF16), pltpu.VMEM((WIN, D_ATT), BF16)],
        args=[qkv3] * 7 + [band])


def _glu(v):
    return v[:, :D_CONV].astype(F32) * _sigmoid(v[:, D_CONV:].astype(F32))


def _halo_specs(tm, width, col=0):
    per = tm // BF16_ROWS
    return lambda nblk: [
        pl.BlockSpec((BF16_ROWS, width), lambda i: (jnp.maximum(i * per - 1, 0), col)),
        pl.BlockSpec((tm, width), lambda i: (i, col)),
        pl.BlockSpec((BF16_ROWS, width), lambda i: (jnp.minimum(i * per + per, nblk - 1), col)),
    ]


def _fill_halo(buf, i, ntiles, tm, prev, main, nxt):
    buf[0:BF16_ROWS] = jnp.where(i == 0, 0.0, prev)
    buf[BF16_ROWS:BF16_ROWS + tm] = main
    buf[BF16_ROWS + tm:] = jnp.where(i == ntiles - 1, 0.0, nxt)


def _halo_scratch(tm):
    return [pltpu.VMEM((tm + 2 * BF16_ROWS, D_CONV), F32),
            pltpu.VMEM((SUBLANES - 1, tm + 2 * BF16_ROWS - SUBLANES, D_CONV), F32)]


def _shift_copies(buf, shifted, tm):
    rows = tm + 2 * BF16_ROWS - SUBLANES
    for b in range(1, SUBLANES):
        shifted[b - 1] = buf[pl.ds(b, rows), :]


def _tap(buf, shifted, off, rows, base=0):
    a, b = divmod(off, SUBLANES)
    start = base + SUBLANES * a
    if not isinstance(start, int):
        start = pl.multiple_of(start, SUBLANES)
    if b == 0:
        return buf[pl.ds(start, rows), :]
    return shifted[b - 1, pl.ds(start, rows), :]


def _fwd_conv(ag, cw, cb, lg, lb):
    s = ag.shape[0]
    tm = _row_tile(s)
    nt = s // tm

    def body(agp, agm, agn, cw_ref, cb_ref, lg_ref, lb_ref, c1_ref, co_ref, ubuf, ush):
        i = pl.program_id(0)
        _fill_halo(ubuf, i, nt, tm, _glu(agp[...]), _glu(agm[...]), _glu(agn[...]))
        _shift_copies(ubuf, ush, tm)
        w, cb = cw_ref[...], cb_ref[...]

        def chunk(c, carry):
            base = pl.multiple_of(c * CONV_ROWS, CONV_ROWS)
            acc = jnp.zeros((CONV_ROWS, D_CONV), F32)
            for k in range(CONV_WIDTH):
                acc = acc + _tap(ubuf, ush, k + 1, CONV_ROWS, base) * w[k:k + 1, :]
            c1_ref[pl.ds(base, CONV_ROWS), :] = acc + cb
            return carry

        lax.fori_loop(0, tm // CONV_ROWS, chunk, 0)
        c1 = c1_ref[...]
        xc = c1 - _mean(c1)
        ln = xc * lax.rsqrt(_mean(xc * xc) + EPS) * lg_ref[...] + lb_ref[...]
        co_ref[...] = (ln * _sigmoid(ln)).astype(BF16)

    vec = _full((1, D_CONV))
    return pl.pallas_call(
        body, name="fwd_conv", grid=(nt,),
        in_specs=_halo_specs(tm, 2 * D_CONV)(s // BF16_ROWS) + [_full((32, D_CONV)), vec, vec, vec],
        out_specs=[pl.BlockSpec((tm, D_CONV), lambda i: (i, 0))] * 2,
        out_shape=[jax.ShapeDtypeStruct((s, D_CONV), F32), jax.ShapeDtypeStruct((s, D_CONV), BF16)],
        scratch_shapes=_halo_scratch(tm),
        compiler_params=_params("parallel"),
    )(ag, ag, ag, cw, cb, lg, lb)


def _fwd_mix_out(outs, lses, conv_out, x, w_out):
    s, d = x.shape
    tm = _row_tile(s)
    dils = DILATIONS[1:]
    nd = len(dils)

    def body(o1, *rest):
        o_str, l1, l_str = rest[:nd], rest[nd], rest[nd + 1:2 * nd + 1]
        co, x_ref, w_ref, h_ref, cat_ref, lt_ref = rest[2 * nd + 1:2 * nd + 7]
        lt_str = rest[2 * nd + 7:3 * nd + 7]
        obufs, lbufs, ltbuf = rest[3 * nd + 7:4 * nd + 7], rest[4 * nd + 7:5 * nd + 7], rest[5 * nd + 7]
        for ref, buf in zip(o_str + l_str, obufs + lbufs):
            _read_strided(ref, buf)
        lse = [l1[0]] + [_load_blocks(buf) for buf in lbufs]
        out = [o1[0].astype(F32)] + [_load_blocks(buf) for buf in obufs]
        m = lse[0]
        for v in lse[1:]:
            m = jnp.maximum(m, v)
        e = [jnp.exp(v - m) for v in lse]
        den = sum(e[1:], e[0])
        att = (sum((ev * ov for ev, ov in zip(e[1:], out[1:])), e[0] * out[0]) / den).astype(BF16)
        lt = m + jnp.log(den)
        lt_ref[...] = lt
        _store_blocks(ltbuf, lt)
        for ref in lt_str:
            _write_strided(ltbuf, ref)
        cat_ref[:, :D_ATT] = att
        cat_ref[:, D_ATT:] = co[...]
        h_ref[...] = x_ref[...] + _dot(att, w_ref[:D_ATT, :]) + _dot(co[...], w_ref[D_ATT:, :])

    row = lambda w: pl.BlockSpec((tm, w), lambda i: (i, 0))
    nat = pl.BlockSpec((1, tm, D_ATT), lambda i: (0, i, 0))
    strided = [_strided_spec(dil, tm, D_ATT) for dil in dils]
    return pl.pallas_call(
        body, name="fwd_mix_out", grid=(s // tm,),
        in_specs=[nat] + strided + [nat] + strided + [row(D_ATT), row(d), _full((d, d))],
        out_specs=[row(d), row(d), row(D_ATT)] + strided,
        out_shape=[jax.ShapeDtypeStruct((s, d), F32), jax.ShapeDtypeStruct((s, d), BF16),
                   jax.ShapeDtypeStruct((s, D_ATT), F32)] + [_strided_shape(dil, s, D_ATT, F32) for dil in dils],
        scratch_shapes=[_lane_scratch(tm, D_ATT)] * (2 * nd + 1),
        compiler_params=_params("parallel"),
    )(*outs, *lses, conv_out, x, w_out)


def _fwd_mem(mem, g, wk, wv):
    m, d = mem.shape

    def body(mem_ref, g_ref, wk_ref, wv_ref, mn_ref, xk_ref, xv_ref):
        mn = _rms_fwd(mem_ref[...], g_ref[...])[0].astype(BF16)
        mn_ref[...] = mn
        xk_ref[...] = _dot(mn, wk_ref[...]).astype(BF16)
        xv_ref[...] = _dot(mn, wv_ref[...]).astype(BF16)

    return pl.pallas_call(
        body, name="fwd_mem",
        out_shape=[jax.ShapeDtypeStruct((m, d), BF16)] * 3,
        compiler_params=_params(),
    )(mem, g, wk, wv)


def _softmax(sc):
    p = jnp.exp(sc - jnp.max(sc, axis=-1, keepdims=True))
    return p / jnp.sum(p, axis=-1, keepdims=True)


def _fwd_xattn(h1, g, wq, xk, xv, wo):
    s, d = h1.shape
    m = xk.shape[0]
    tm = _row_tile(s)
    hd = d // XATT_HEADS

    def body(h_ref, g_ref, wq_ref, xk_ref, xv_ref, wo_ref, h2_ref, hn_ref, xq_ref, xo_ref):
        h = h_ref[...]
        hn = _rms_fwd(h, g_ref[...])[0].astype(BF16)
        hn_ref[...] = hn
        xq = _dot(hn, wq_ref[...]).astype(BF16)
        xq_ref[...] = xq
        heads = [slice(i * hd, (i + 1) * hd) for i in range(XATT_HEADS)]
        scores = [_dot_nt(xq[:, cols], xk_ref[:, cols]) * hd ** -0.5 for cols in heads]
        probs = [_softmax(sc).astype(BF16) for sc in scores]
        for cols, pr in zip(heads, probs):
            xo_ref[:, cols] = _dot(pr, xv_ref[:, cols]).astype(BF16)
        h2_ref[...] = h + _dot(xo_ref[...], wo_ref[...])

    row = pl.BlockSpec((tm, d), lambda i: (i, 0))
    return pl.pallas_call(
        body, name="fwd_xattn", grid=(s // tm,),
        in_specs=[row, _full((1, d)), _full((d, d)), _full((m, d)), _full((m, d)), _full((d, d))],
        out_specs=[row] * 4,
        out_shape=[jax.ShapeDtypeStruct((s, d), F32)] + [jax.ShapeDtypeStruct((s, d), BF16)] * 3,
        compiler_params=_params("parallel"),
    )(h1, g, wq, xk, xv, wo)


def _fwd_mlp_loss(h2, g, w_up, w_down, gf, target):
    s, d = h2.shape
    nsh, _, f = w_up.shape
    fb = MLP_SHARDS * f
    nb = nsh // MLP_SHARDS
    tm = _row_tile(s)

    def body(h_ref, g_ref, wu_ref, wd_ref, gf_ref, t_ref,
             hn_ref, act_ref, dh_ref, dhb_ref, loss_ref, ggf_ref, acc):
        i, k = pl.program_id(0), pl.program_id(1)

        @pl.when(k == 0)
        def _():
            hn_ref[...] = _rms_fwd(h_ref[...], g_ref[...])[0].astype(BF16)
            acc[...] = jnp.zeros_like(acc)

        @pl.when((i == 0) & (k == 0))
        def _():
            loss_ref[...] = jnp.zeros_like(loss_ref)
            ggf_ref[...] = jnp.zeros_like(ggf_ref)

        hn = hn_ref[...]
        ups = [_dot(hn, wu_ref[c]) for c in range(MLP_SHARDS)]
        for c, u in enumerate(ups):
            act_ref[:, c * f:(c + 1) * f] = jnp.square(jnp.maximum(u, 0.0)).astype(BF16)
        acc[...] += _dot(act_ref[...], wd_ref[...])

        @pl.when(k == nb - 1)
        def _():
            h3 = h_ref[...] + acc[...]
            gfv = gf_ref[...]
            y, _ = _rms_fwd(h3, gfv)
            err = y - t_ref[...]
            loss_ref[...] += 0.5 * jnp.sum(_mean(err * err))
            dh3, gg = _rms_bwd(h3, gfv, err * (1.0 / d))
            ggf_ref[...] += jnp.sum(gg, axis=0, keepdims=True)
            dh_ref[...] = dh3
            dhb_ref[...] = dh3.astype(BF16)

    row = pl.BlockSpec((tm, d), lambda i, k: (i, 0))
    return pl.pallas_call(
        body, name="fwd_mlp_loss", grid=(s // tm, nb),
        in_specs=[row, _full((1, d)),
                  pl.BlockSpec((MLP_SHARDS, d, f), lambda i, k: (k, 0, 0)),
                  pl.BlockSpec((fb, d), lambda i, k: (k, 0)),
                  _full((1, d)), row],
        out_specs=[row, pl.BlockSpec((tm, fb), lambda i, k: (i, k)), row, row,
                   _full((1, LANES)), _full((1, d))],
        out_shape=[jax.ShapeDtypeStruct((s, d), BF16), jax.ShapeDtypeStruct((s, nsh * f), BF16),
                   jax.ShapeDtypeStruct((s, d), F32), jax.ShapeDtypeStruct((s, d), BF16),
                   jax.ShapeDtypeStruct((1, LANES), F32), jax.ShapeDtypeStruct((1, d), F32)],
        scratch_shapes=[pltpu.VMEM((tm, d), F32)],
        compiler_params=_params("arbitrary", "arbitrary"),
    )(h2, g, w_up, w_down, gf, target)


def _bwd_mlp(dh3, dh3b, act, w_up_t, w_down, h2, g):
    s, d = h2.shape
    ff = w_down.shape[0]
    fb = MLP_SHARDS * (ff // N_DEV)
    nb = ff // fb
    tm = _row_tile(s)

    def body(dh_ref, dhb_ref, act_ref, wut_ref, wd_ref, h_ref, g_ref,
             du_ref, dh2_ref, dh2b_ref, gg_ref, acc):
        i, k = pl.program_id(0), pl.program_id(1)

        @pl.when(k == 0)
        def _():
            acc[...] = jnp.zeros_like(acc)

        @pl.when((i == 0) & (k == 0))
        def _():
            gg_ref[...] = jnp.zeros_like(gg_ref)

        dhb = dhb_ref[...]
        f = fb // MLP_SHARDS
        shards = [slice(c * f, (c + 1) * f) for c in range(MLP_SHARDS)]
        dacts = [_dot_nt(dhb, wd_ref[cols, :]) for cols in shards]
        for cols, dact in zip(shards, dacts):
            du_ref[:, cols] = (dact * (2.0 * jnp.sqrt(act_ref[:, cols].astype(F32)))).astype(BF16)
        acc[...] += _dot(du_ref[...], wut_ref[...])

        @pl.when(k == nb - 1)
        def _():
            dh, gg = _rms_bwd(h_ref[...], g_ref[...], acc[...])
            gg_ref[...] += jnp.sum(gg, axis=0, keepdims=True)
            dh2 = dh_ref[...] + dh
            dh2_ref[...] = dh2
            dh2b_ref[...] = dh2.astype(BF16)

    row = pl.BlockSpec((tm, d), lambda i, k: (i, 0))
    col = pl.BlockSpec((tm, fb), lambda i, k: (i, k))
    wblk = pl.BlockSpec((fb, d), lambda i, k: (k, 0))
    return pl.pallas_call(
        body, name="bwd_mlp", grid=(s // tm, nb),
        in_specs=[row, row, col, wblk, wblk, row, _full((1, d))],
        out_specs=[col, row, row, _full((1, d))],
        out_shape=[jax.ShapeDtypeStruct((s, ff), BF16), jax.ShapeDtypeStruct((s, d), F32),
                   jax.ShapeDtypeStruct((s, d), BF16), jax.ShapeDtypeStruct((1, d), F32)],
        scratch_shapes=[pltpu.VMEM((tm, d), F32)],
        compiler_params=_params("arbitrary", "arbitrary"),
    )(dh3, dh3b, act, w_up_t, w_down, h2, g)


def _bwd_xattn(dh2, dh2b, h1, g, xq, xk, xv, wq, wo, sides=()):
    s, d = h1.shape
    m = xk.shape[0]
    tm = _row_tile(s)
    hd = d // XATT_HEADS
    scale = hd ** -0.5

    def body(dh_ref, dhb_ref, h_ref, g_ref, xq_ref, xk_ref, xv_ref, wq_ref, wo_ref,
             dh1_ref, dh1b_ref, dxq_ref, dxk_ref, dxv_ref, gg_ref):
        @pl.when(pl.program_id(0) == 0)
        def _():
            dxk_ref[...] = jnp.zeros_like(dxk_ref)
            dxv_ref[...] = jnp.zeros_like(dxv_ref)
            gg_ref[...] = jnp.zeros_like(gg_ref)

        dxo = _dot_nt(dhb_ref[...], wo_ref[...])
        heads = [slice(i * hd, (i + 1) * hd) for i in range(XATT_HEADS)]
        dxos = [dxo[:, cols].astype(BF16) for cols in heads]
        scores = [_dot_nt(xq_ref[:, cols], xk_ref[:, cols]) * scale for cols in heads]
        dprs = [_dot_nt(dxo_h, xv_ref[:, cols]) for dxo_h, cols in zip(dxos, heads)]
        probs, dscs = [], []
        for sc, dpr in zip(scores, dprs):
            pr = _softmax(sc)
            dscs.append((pr * (dpr - jnp.sum(dpr * pr, axis=-1, keepdims=True)) * scale).astype(BF16))
            probs.append(pr.astype(BF16))
        for cols, dsc, pr, dxo_h in zip(heads, dscs, probs, dxos):
            dxq_ref[:, cols] = _dot(dsc, xk_ref[:, cols]).astype(BF16)
            dxk_ref[:, cols] += _dot_tn(dsc, xq_ref[:, cols])
            dxv_ref[:, cols] += _dot_tn(pr, dxo_h)
        dh, gg = _rms_bwd(h_ref[...], g_ref[...], _dot_nt(dxq_ref[...], wq_ref[...]))
        gg_ref[...] += jnp.sum(gg, axis=0, keepdims=True)
        dh1 = dh_ref[...] + dh
        dh1_ref[...] = dh1
        dh1b_ref[...] = dh1.astype(BF16)

    row = pl.BlockSpec((tm, d), lambda i: (i, 0))
    return _hosted_call(
        body, sides, name="bwd_xattn", grid=(s // tm,),
        in_specs=[row, row, row, _full((1, d)), row, _full((m, d)), _full((m, d)), _full((d, d)), _full((d, d))],
        out_specs=[row, row, row, _full((m, d)), _full((m, d)), _full((1, d))],
        out_shape=[jax.ShapeDtypeStruct((s, d), F32), jax.ShapeDtypeStruct((s, d), BF16),
                   jax.ShapeDtypeStruct((s, d), BF16), jax.ShapeDtypeStruct((m, d), F32),
                   jax.ShapeDtypeStruct((m, d), F32), jax.ShapeDtypeStruct((1, d), F32)],
        scratch_shapes=[],
        args=[dh2, dh2b, h1, g, xq, xk, xv, wq, wo])


def _bwd_mem(mem, g, mn, dxk, dxv, wk, wv):
    m, d = mem.shape

    def body(mem_ref, g_ref, mn_ref, dxk_ref, dxv_ref, wk_ref, wv_ref, gk_ref, gv_ref, gg_ref):
        dk, dv = dxk_ref[...].astype(BF16), dxv_ref[...].astype(BF16)
        gk_ref[...] = _dot_tn(mn_ref[...], dk).astype(BF16)
        gv_ref[...] = _dot_tn(mn_ref[...], dv).astype(BF16)
        dmn = _dot_nt(dk, wk_ref[...]) + _dot_nt(dv, wv_ref[...])
        _, gg = _rms_bwd(mem_ref[...], g_ref[...], dmn)
        gg_ref[...] = jnp.sum(gg, axis=0, keepdims=True)

    return pl.pallas_call(
        body, name="bwd_mem",
        out_shape=[jax.ShapeDtypeStruct((d, d), BF16), jax.ShapeDtypeStruct((d, d), BF16),
                   jax.ShapeDtypeStruct((1, d), F32)],
        compiler_params=_params(),
    )(mem, g, mn, dxk, dxv, wk, wv)


def _bwd_mix_out(dh1b, w_out, cat, head_ones):
    s, d = dh1b.shape
    tm = _row_tile(s)
    dils = DILATIONS[1:]
    nd = len(dils)

    def body(dh_ref, w_ref, cat_ref, ones_ref, dcat_ref, dsum_ref, *rest):
        da_str, ds_str, dbuf, sbuf = rest[:nd], rest[nd:2 * nd], rest[2 * nd], rest[2 * nd + 1]
        dcat = _dot_nt(dh_ref[...], w_ref[...])
        dcat_ref[...] = dcat.astype(BF16)
        datt = dcat[:, :D_ATT]
        prod = datt * cat_ref[...].astype(F32)
        hi = prod.astype(BF16)
        lo = (prod - hi.astype(F32)).astype(BF16)
        dsum = _dot(hi, ones_ref[...]) + _dot(lo, ones_ref[...])
        dsum_ref[...] = dsum
        _store_blocks(dbuf, datt)
        _store_blocks(sbuf, dsum)
        for da_ref, ds_ref in zip(da_str, ds_str):
            _write_strided(dbuf, da_ref)
            _write_strided(sbuf, ds_ref)

    row = lambda w: pl.BlockSpec((tm, w), lambda i: (i, 0))
    strided = [_strided_spec(dil, tm, D_ATT) for dil in dils]
    return pl.pallas_call(
        body, name="bwd_mix_out", grid=(s // tm,),
        in_specs=[row(d), _full((d, d)), row(D_ATT), _full((D_ATT, D_ATT))],
        out_specs=[row(d), row(D_ATT)] + strided + strided,
        out_shape=[jax.ShapeDtypeStruct((s, d), BF16), jax.ShapeDtypeStruct((s, D_ATT), F32)]
        + [_strided_shape(dil, s, D_ATT, BF16) for dil in dils]
        + [_strided_shape(dil, s, D_ATT, F32) for dil in dils],
        scratch_shapes=[_lane_scratch(tm, D_ATT)] * 2,
        compiler_params=_params("parallel"),
    )(dh1b, w_out, cat, head_ones)


def _bwd_conv(dcat, c1, ag, cw, lg, lb, sides=()):
    s = ag.shape[0]
    tm = _row_tile(s)
    nt = s // tm

    def body(dp, dm, dn, cp, cm, cn, agp, agm, agn, cw_ref, lg_ref, lb_ref,
             dag_ref, gcw_ref, gcb_ref, glg_ref, glb_ref, ubuf, ush, dbuf, dsh, gacc):
        i = pl.program_id(0)

        @pl.when(i == 0)
        def _():
            gacc[...] = jnp.zeros_like(gacc)
            gcb_ref[...] = jnp.zeros_like(gcb_ref)
            glg_ref[...] = jnp.zeros_like(glg_ref)
            glb_ref[...] = jnp.zeros_like(glb_ref)

        lgv, lbv = lg_ref[...], lb_ref[...]

        def norm_bwd(dco, c1v):
            xc = c1v - _mean(c1v)
            rs = lax.rsqrt(_mean(xc * xc) + EPS)
            z = xc * rs
            ln = z * lgv + lbv
            sg = _sigmoid(ln)
            dln = dco.astype(F32) * (sg * (1.0 + ln * (1.0 - sg)))
            dz = dln * lgv
            return rs * (dz - _mean(dz) - z * _mean(dz * z)), dln, z

        dc_m, dln, z = norm_bwd(dm[...], cm[...])
        glg_ref[...] += jnp.sum(dln * z, axis=0, keepdims=True)
        glb_ref[...] += jnp.sum(dln, axis=0, keepdims=True)
        gcb_ref[...] += jnp.sum(dc_m, axis=0, keepdims=True)
        _fill_halo(dbuf, i, nt, tm, norm_bwd(dp[...], cp[...])[0], dc_m, norm_bwd(dn[...], cn[...])[0])
        _fill_halo(ubuf, i, nt, tm, _glu(agp[...]), _glu(agm[...]), _glu(agn[...]))
        _shift_copies(dbuf, dsh, tm)
        _shift_copies(ubuf, ush, tm)

        w = cw_ref[...]

        def chunk(c, carry):
            base = pl.multiple_of(c * CONV_ROWS, CONV_ROWS)
            rows = pl.ds(base, CONV_ROWS)
            dc = dbuf[pl.ds(pl.multiple_of(base + BF16_ROWS, SUBLANES), CONV_ROWS), :]
            du = jnp.zeros((CONV_ROWS, D_CONV), F32)
            for k in range(CONV_WIDTH):
                du = du + _tap(dbuf, dsh, CONV_WIDTH - k, CONV_ROWS, base) * w[k:k + 1, :]
                prod = dc * _tap(ubuf, ush, k + 1, CONV_ROWS, base)
                part = prod[0:SUBLANES]
                for r in range(SUBLANES, CONV_ROWS, SUBLANES):
                    part = part + prod[r:r + SUBLANES]
                gacc[k] += part
            a = agm[rows, :D_CONV].astype(F32)
            sg = _sigmoid(agm[rows, D_CONV:].astype(F32))
            dag_ref[rows, :D_CONV] = (du * sg).astype(BF16)
            dag_ref[rows, D_CONV:] = (du * a * sg * (1.0 - sg)).astype(BF16)
            return carry

        lax.fori_loop(0, tm // CONV_ROWS, chunk, 0)

        @pl.when(i == nt - 1)
        def _():
            tap = lax.broadcasted_iota(jnp.int32, (32, D_CONV), 0)
            gcw = jnp.zeros((32, D_CONV), F32)
            for k in range(CONV_WIDTH):
                gcw = jnp.where(tap == k, jnp.sum(gacc[k], axis=0, keepdims=True), gcw)
            gcw_ref[...] = gcw

    vec = _full((1, D_CONV))
    nblk = s // BF16_ROWS
    return _hosted_call(
        body, sides, name="bwd_conv", grid=(nt,),
        in_specs=_halo_specs(tm, D_CONV, 1)(nblk) + _halo_specs(tm, D_CONV)(nblk) + _halo_specs(tm, 2 * D_CONV)(nblk)
        + [_full((32, D_CONV)), vec, vec],
        out_specs=[pl.BlockSpec((tm, 2 * D_CONV), lambda i: (i, 0)), _full((32, D_CONV)), vec, vec, vec],
        out_shape=[jax.ShapeDtypeStruct((s, 2 * D_CONV), BF16), jax.ShapeDtypeStruct((32, D_CONV), F32)]
        + [jax.ShapeDtypeStruct((1, D_CONV), F32)] * 3,
        scratch_shapes=_halo_scratch(tm) + _halo_scratch(tm) + [pltpu.VMEM((32, SUBLANES, D_CONV), F32)],
        args=[dcat, dcat, dcat, c1, c1, c1, ag, ag, ag, cw, lg, lb])


def _swa_bwd(qkv3, do3, lt3, ds3, band, name, sides=()):
    dil, length, _ = qkv3.shape
    nb = length // BQ
    scale = HEAD_DIM ** -0.5
    assert WIN == 2 * BQ and BQ == 2 * HALF

    def body(q_ref, kp, km, kn, vp, vm, vn, do_ref, l_ref, s_ref, band_ref, dq_ref, dkv_ref, kwin, vwin, pend, keep):
        j = pl.program_id(1)

        @pl.when(j == 0)
        def _():
            pend[...] = jnp.zeros_like(pend)
            keep[...] = jnp.zeros_like(keep)

        @pl.when(j < nb)
        def _():
            _fill_window(kwin, kp, km, kn)
            _fill_window(vwin, vp, vm, vn)
            bias = _window_bias(band_ref, j, length)
            first = _first_head()
            pairs = [slice(pr * LANES, (pr + 1) * LANES) for pr in range(D_ATT // LANES)]
            qs = [_stack_heads(q_ref[:, cols] * scale, first) for cols in pairs]
            dos = [_stack_heads(do_ref[:, cols], first) for cols in pairs]
            scores = [_dot_nt(q, kwin[:, cols]) for q, cols in zip(qs, pairs)]
            dps = [_dot_nt(do, vwin[:, cols]) for do, cols in zip(dos, pairs)]
            probs, dscs = [], []
            for cols, sc, dp in zip(pairs, scores, dps):
                p = jnp.exp(sc + (bias - _stack_cols(l_ref[:, cols], first)))
                dscs.append((p * (dp - _stack_cols(s_ref[:, cols], first))).astype(BF16))
                probs.append(p.astype(BF16))
            dqs = [_dot(dsc, kwin[:, cols] * scale) for dsc, cols in zip(dscs, pairs)]
            dks = [_dot_tn(dsc, q) for dsc, q in zip(dscs, qs)]
            dvs = [_dot_tn(p, do) for p, do in zip(probs, dos)]
            for pr, cols in enumerate(pairs):
                dq_ref[:, cols] = _unstack_heads(dqs[pr], first).astype(BF16)
                for part, at in ((dks[pr], pr * LANES), (dvs[pr], D_ATT + pr * LANES)):
                    at = slice(at, at + LANES)
                    dkv_ref[:HALF, at] = keep[:, at].astype(BF16)
                    dkv_ref[HALF:, at] = (pend[:HALF, at] + part[:HALF]).astype(BF16)
                    keep[:, at] = pend[HALF:, at] + part[HALF:BQ]
                    pend[:, at] = part[BQ:]

        @pl.when(j == nb)
        def _():
            dkv_ref[:HALF] = keep[...].astype(BF16)
            dkv_ref[HALF:] = pend[:HALF].astype(BF16)

    def clamp(idx):
        return lambda r, j: idx(r, jnp.minimum(j, nb - 1))

    main = pl.BlockSpec((None, BQ, D_ATT), clamp(lambda r, j: (r, j, 0)))
    wins = [pl.BlockSpec(sp.block_shape, clamp(sp.index_map))
            for c in (1, 2) for sp in _win_in_specs(length, c, D_ATT)]
    return _hosted_call(
        body, sides, name=name, grid=(dil, nb + 1),
        in_specs=[main] + wins + [main] * 3 + [_full((2 * BQ, WIN))],
        out_specs=[main, pl.BlockSpec((None, BQ, 2 * D_ATT), lambda r, j: (r, jnp.maximum(j - 1, 0), 0))],
        out_shape=[jax.ShapeDtypeStruct((dil, length, D_ATT), BF16),
                   jax.ShapeDtypeStruct((dil, length, 2 * D_ATT), BF16)],
        scratch_shapes=[pltpu.VMEM((WIN, D_ATT), BF16)] * 2
        + [pltpu.VMEM((BQ, 2 * D_ATT), F32), pltpu.VMEM((HALF, 2 * D_ATT), F32)],
        args=[qkv3] * 7 + [do3, lt3, ds3, band])


def _bwd_in(dqs, dkvs, dag, w_in, x, g, dh1, rot):
    s, d = x.shape
    n = w_in.shape[1]
    tm = _row_tile(s)
    dils = DILATIONS[1:]
    nd = len(dils)

    def body(q1, *rest):
        q_str, kv1, kv_str = rest[:nd], rest[nd], rest[nd + 1:2 * nd + 1]
        dag_ref, w_ref, x_ref, g_ref, dh_ref, c_ref, a_ref, b_ref, gx_ref, dy_ref, gg_ref, qbuf, kvbuf = rest[2 * nd + 1:]

        @pl.when(pl.program_id(0) == 0)
        def _():
            gg_ref[...] = jnp.zeros_like(gg_ref)

        _store_blocks(qbuf, q1[0].astype(F32))
        _store_blocks(kvbuf, kv1[0].astype(F32))
        for ref in q_str:
            _read_strided(ref, qbuf, add=True)
        for ref in kv_str:
            _read_strided(ref, kvbuf, add=True)
        dq, dkv = _load_blocks(qbuf), _load_blocks(kvbuf)
        reps = (1, D_ATT // LANES)
        cc, aa, bb = jnp.tile(c_ref[...], reps), jnp.tile(a_ref[...], reps), jnp.tile(b_ref[...], reps)
        for blk, t in enumerate((dq, dkv[:, :D_ATT])):
            dt = t * cc + pltpu.roll(t * aa, ROT_DIM // 2, 1) + pltpu.roll(t * bb, D_ATT - ROT_DIM // 2, 1)
            dy_ref[:, blk * D_ATT:(blk + 1) * D_ATT] = dt.astype(BF16)
        dy_ref[:, 2 * D_ATT:3 * D_ATT] = dkv[:, D_ATT:].astype(BF16)
        dy_ref[:, 3 * D_ATT:] = dag_ref[...]
        dx, gg = _rms_bwd(x_ref[...], g_ref[...], _dot_nt(dy_ref[...], w_ref[...]))
        gg_ref[...] += jnp.sum(gg, axis=0, keepdims=True)
        gx_ref[...] = dh_ref[...] + dx

    row = lambda w: pl.BlockSpec((tm, w), lambda i: (i, 0))
    def strided(width):
        return [pl.BlockSpec((1, tm, width), lambda i: (0, i, 0))] + [_strided_spec(dil, tm, width) for dil in dils]

    return pl.pallas_call(
        body, name="bwd_in", grid=(s // tm,),
        in_specs=strided(D_ATT) + strided(2 * D_ATT)
        + [row(2 * D_CONV), _full((d, n)), row(d), _full((1, d)), row(d)] + [row(LANES)] * 3,
        out_specs=[row(d), row(n), _full((1, d))],
        out_shape=[jax.ShapeDtypeStruct((s, d), F32), jax.ShapeDtypeStruct((s, n), BF16),
                   jax.ShapeDtypeStruct((1, d), F32)],
        scratch_shapes=[_lane_scratch(tm, D_ATT), _lane_scratch(tm, 2 * D_ATT)],
        compiler_params=_params("arbitrary"),
    )(*dqs, *dkvs, dag, w_in, x, g, dh1, *rot)


def _wgrad(a, b, name, a_blk=None, b_blk=None, stack=None, tm=1024):
    s, ka = a.shape
    nb = b.shape[1]
    a_blk, b_blk = a_blk or ka, b_blk or nb
    na, nbl = ka // a_blk, nb // b_blk
    assert na == 1 or nbl == 1
    tm = min(tm, s)
    nt = s // tm
    per = b_blk // stack if stack else 0

    def body(a_ref, b_ref, o_ref, acc):
        t = pl.program_id(1)

        @pl.when(t == 0)
        def _():
            acc[...] = jnp.zeros_like(acc)

        acc[...] += _dot_tn(a_ref[...], b_ref[...])

        @pl.when(t == nt - 1)
        def _():
            if stack:
                for c in range(per):
                    o_ref[c] = acc[:, c * stack:(c + 1) * stack].astype(BF16)
            else:
                o_ref[...] = acc[...].astype(BF16)

    if stack:
        out_spec = pl.BlockSpec((per, ka, stack), lambda k, t: (k, 0, 0))
        out_shape = jax.ShapeDtypeStruct((nb // stack, ka, stack), BF16)
    elif na > 1:
        out_spec = pl.BlockSpec((a_blk, nb), lambda k, t: (k, 0))
        out_shape = jax.ShapeDtypeStruct((ka, nb), BF16)
    else:
        out_spec = pl.BlockSpec((ka, b_blk), lambda k, t: (0, k))
        out_shape = jax.ShapeDtypeStruct((ka, nb), BF16)
    return pl.pallas_call(
        body, name=name, grid=(na * nbl, nt),
        in_specs=[pl.BlockSpec((tm, a_blk), (lambda k, t: (t, k)) if na > 1 else (lambda k, t: (t, 0))),
                  pl.BlockSpec((tm, b_blk), (lambda k, t: (t, k)) if nbl > 1 else (lambda k, t: (t, 0)))],
        out_specs=out_spec, out_shape=out_shape,
        scratch_shapes=[pltpu.VMEM((a_blk, b_blk), F32)],
        compiler_params=_params("parallel", "arbitrary"),
    )(a, b)


def _adamw(w, gsrc, m, v, name, transposed=False):
    summed = gsrc.ndim == w.ndim + 1
    rows, cols = w.shape
    assert gsrc.shape[-2:] == ((cols, rows) if transposed else (rows, cols)) and (summed or not transposed)
    tr = rows if rows <= 256 else 256
    assert rows % tr == 0
    c1 = 1.0 - ADAM_B1 ** ADAM_STEP
    c2 = 1.0 - ADAM_B2 ** ADAM_STEP

    def body(w_ref, g_ref, m_ref, v_ref, go_ref, d_ref, mo_ref, vo_ref):
        if summed:
            g = g_ref[0].astype(F32)
            for i in range(1, N_DEV):
                g = g + g_ref[i].astype(F32)
            if transposed:
                g = g.T
        else:
            g = g_ref[...]
        mn = ADAM_B1 * m_ref[...] + (1.0 - ADAM_B1) * g
        vn = ADAM_B2 * v_ref[...] + (1.0 - ADAM_B2) * jnp.square(g)
        go_ref[...] = g
        mo_ref[...] = mn
        vo_ref[...] = vn
        d_ref[...] = -ADAM_LR * ((mn / c1) / (jnp.sqrt(vn / c2) + ADAM_EPS) + ADAM_WD * w_ref[...])

    blk = pl.BlockSpec((tr, cols), lambda i: (i, 0))
    if transposed:
        gblk = pl.BlockSpec((N_DEV, cols, tr), lambda i: (0, 0, i))
    else:
        gblk = pl.BlockSpec((N_DEV, tr, cols), lambda i: (0, i, 0)) if summed else blk
    return pl.pallas_call(
        body, name=name, grid=(rows // tr,),
        in_specs=[blk, gblk, blk, blk], out_specs=[blk] * 4,
        out_shape=[jax.ShapeDtypeStruct(w.shape, F32)] * 4,
        compiler_params=_params("parallel"),
    )(w, gsrc, m, v)


def _sum_slots(g, name):
    _, rows, cols = g.shape

    def body(g_ref, o_ref):
        acc = g_ref[0]
        for i in range(1, N_DEV):
            acc = acc + g_ref[i]
        o_ref[...] = acc

    return pl.pallas_call(body, name=name, out_shape=jax.ShapeDtypeStruct((rows, cols), F32),
                          compiler_params=_params())(g)


def kernel(x, mem, norm_mix_g, w_in, conv_w, conv_b, conv_ln_g, conv_ln_b, w_out, norm_x_g, norm_mem_g, w_xq, w_xk, w_xv, w_xo, norm_mlp_g, w_up, w_down, norm_final_g, loss_target, m_norm_mix_g, m_w_in, m_conv_w, m_conv_b, m_conv_ln_g, m_conv_ln_b, m_w_out, m_norm_x_g, m_norm_mem_g, m_w_xq, m_w_xk, m_w_xv, m_w_xo, m_norm_mlp_g, m_w_up, m_w_down, m_norm_final_g, v_norm_mix_g, v_w_in, v_conv_w, v_conv_b, v_conv_ln_g, v_conv_ln_b, v_w_out, v_norm_x_g, v_norm_mem_g, v_w_xq, v_w_xk, v_w_xv, v_w_xo, v_norm_mlp_g, v_w_up, v_w_down, v_norm_final_g):
    x2, mem2, tgt = x[0], mem[0], loss_target[0]
    s, d = x2.shape
    gf = norm_final_g[None, :]

    cw_local = jnp.pad(conv_w[0], ((0, 1), (0, LANES - conv_w.shape[2])))
    win_g, cw_g = _exchange_call([_Exchange([w_in[0].astype(BF16), cw_local], gather=True)], "gather_w_in")
    w_in_f = jnp.transpose(win_g, (1, 0, 2)).reshape(d, -1)
    cw_f = jnp.transpose(cw_g[:, :, :conv_w.shape[2]], (1, 0, 2)).reshape(32, D_CONV)
    row_names = (w_out, w_xq, w_xk, w_xv, w_xo)
    rows_local = jnp.concatenate([w[0].astype(BF16) for w in row_names], axis=0)
    late = [_Exchange([rows_local], gather=True), _Exchange([w_up[0].astype(BF16)], gather=True),
            _Exchange([w_down[0].astype(BF16)], gather=True)]

    rot = _rotary_tables(s)
    band = _band_bias()
    xn, qkv, ag, *qkv_strided = _fwd_in(x2, norm_mix_g, w_in_f, rot)
    qkv3 = [qkv[None]] + qkv_strided
    outs, lses, gathered = [], [], []
    for dil, q3, side in zip(DILATIONS, qkv3, late):
        (o3, l3), got = _swa_fwd(q3, band, f"swa_fwd_d{dil}", [side])
        outs.append(o3)
        lses.append(l3)
        gathered += got
    rows_g, wup_g, wdown_g = gathered
    full_rows = []
    off = 0
    for w in row_names:
        full_rows.append(rows_g[:, off:off + w.shape[1], :].reshape(N_DEV * w.shape[1], d))
        off += w.shape[1]
    w_out_f, w_xq_f, w_xk_f, w_xv_f, w_xo_f = full_rows
    w_down_f = wdown_g.reshape(-1, d)
    w_up_t = jnp.swapaxes(wup_g, 1, 2).reshape(-1, d)
    c1, conv_out = _fwd_conv(ag, cw_f, conv_b, conv_ln_g, conv_ln_b)
    h1, cat, ltot, *lt_strided = _fwd_mix_out(outs, lses, conv_out, x2, w_out_f)
    mn, xk, xv = _fwd_mem(mem2, norm_mem_g, w_xk_f, w_xv_f)
    h2, hn2, xq, xo = _fwd_xattn(h1, norm_x_g, w_xq_f, xk, xv, w_xo_f)
    hn3, act, dh3, dh3b, loss_part, g_final = _fwd_mlp_loss(h2, norm_mlp_g, wup_g, w_down_f, gf, tgt)

    def scatter(*grads):
        return _Exchange([g.reshape(N_DEV, -1, g.shape[-1]) for g in grads], gather=False)

    f_blk = w_up.shape[2]
    du, dh2, dh2b, g_mlp = _bwd_mlp(dh3, dh3b, act, w_up_t, w_down_f, h2, norm_mlp_g)
    gw_up = _wgrad(hn3, du, "wgrad_up", b_blk=4 * f_blk, stack=f_blk)
    gw_down = _wgrad(dh3b, act, "wgrad_down", b_blk=4 * f_blk, stack=f_blk)
    (dh1, dh1b, dxq, dxk, dxv, g_x), (r_up,) = _bwd_xattn(
        dh2, dh2b, h1, norm_x_g, xq, xk, xv, w_xq_f, w_xo_f, [scatter(gw_up)])
    gw_xq = _wgrad(hn2, dxq, "wgrad_xq")
    gw_xo = _wgrad(xo, dh2b, "wgrad_xo")
    gw_xk, gw_xv, g_mem = _bwd_mem(mem2, norm_mem_g, mn, dxk, dxv, w_xk_f, w_xv_f)
    head = jnp.arange(D_ATT) // HEAD_DIM
    head_ones = (head[:, None] == head[None, :]).astype(BF16)
    dcat, dsum, *strided = _bwd_mix_out(dh1b, w_out_f, cat, head_ones)
    n_str = len(DILATIONS) - 1
    do3, lt3, ds3 = [dcat[None]] + strided[:n_str], [ltot[None]] + lt_strided, [dsum[None]] + strided[n_str:]
    gw_out = _wgrad(cat, dh1b, "wgrad_out")
    (dag, g_cw, g_cb, g_lg, g_lb), (r_down,) = _bwd_conv(dcat, c1, ag, cw_f, conv_ln_g, conv_ln_b, [scatter(gw_down)])
    hosted = [[scatter(gw_out, gw_xq, gw_xk, gw_xv, gw_xo)], [], []]
    dqs, dkvs, landed = [], [], []
    for i, dil in enumerate(DILATIONS):
        (dq3, dkv3), got = _swa_bwd(qkv3[i], do3[i], lt3[i], ds3[i], band, f"swa_bwd_d{dil}", hosted[i])
        dqs.append(dq3)
        dkvs.append(dkv3)
        landed += got
    r_out, r_xq, r_xk, r_xv, r_xo = landed
    grad_x, dy, g_mix = _bwd_in(dqs, dkvs, dag, w_in_f, x2, norm_mix_g, dh1, rot)
    gw_in = _wgrad(xn, dy, "wgrad_in", b_blk=dy.shape[1] // 2)

    def widen(t):
        return jnp.pad(t, ((0, 0), (0, d - t.shape[1])))

    n_in = w_in.shape[2]
    small = jnp.concatenate([g_mix, g_x, g_mem, g_mlp, g_final, widen(g_cb), widen(g_lg), widen(g_lb),
                             g_cw.reshape(16, d), widen(loss_part), jnp.zeros((7, d), F32)], axis=0)
    r_in, small_g = _exchange_call(
        [_Exchange([jnp.transpose(gw_in.reshape(d, N_DEV, n_in), (1, 0, 2))], gather=False),
         _Exchange([small], gather=True)], "scatter_w_in_gather_small")
    small_sum = _sum_slots(small_g, "sum_small_grads")
    loss = small_sum[24, 0]

    res = {}

    def step(name, w, gsrc, m, v, transposed=False):
        shape = w.shape
        w2, m2, v2 = (t.reshape(-1, shape[-1]) for t in (w, m, v))
        res[name] = [t.reshape(shape) for t in _adamw(w2, gsrc, m2, v2, "adamw_" + name, transposed)]

    step("w_in", w_in, r_in, m_w_in, v_w_in)
    step("w_up", w_up, r_up, m_w_up, v_w_up)
    step("w_out", w_out, r_out, m_w_out, v_w_out)
    step("w_xq", w_xq, r_xq, m_w_xq, v_w_xq)
    step("w_xk", w_xk, r_xk, m_w_xk, v_w_xk)
    step("w_xv", w_xv, r_xv, m_w_xv, v_w_xv)
    step("w_xo", w_xo, r_xo, m_w_xo, v_w_xo)
    step("w_down", w_down, r_down, m_w_down, v_w_down, transposed=True)

    me = _dev_index((lax.axis_index("x"), lax.axis_index("y"), lax.axis_index("c")))
    n_cw = conv_w.shape[2]
    g_cw_full = small_sum[8:24].reshape(32, D_CONV)[:CONV_WIDTH]
    g_cw_mine = lax.dynamic_slice_in_dim(g_cw_full, me * n_cw, n_cw, axis=1)
    step("conv_w", conv_w, g_cw_mine, m_conv_w, v_conv_w)

    vec_names = ["norm_mix_g", "norm_x_g", "norm_mem_g", "norm_mlp_g", "norm_final_g", "conv_b", "conv_ln_g", "conv_ln_b"]
    vec_w = [norm_mix_g, norm_x_g, norm_mem_g, norm_mlp_g, gf, conv_b, conv_ln_g, conv_ln_b]
    vec_m = [m_norm_mix_g, m_norm_x_g, m_norm_mem_g, m_norm_mlp_g, m_norm_final_g[None, :], m_conv_b, m_conv_ln_g, m_conv_ln_b]
    vec_v = [v_norm_mix_g, v_norm_x_g, v_norm_mem_g, v_norm_mlp_g, v_norm_final_g[None, :], v_conv_b, v_conv_ln_g, v_conv_ln_b]

    def pack(ts):
        return jnp.concatenate([widen(t) for t in ts], axis=0)

    packed = _adamw(pack(vec_w), small_sum[0:8], pack(vec_m), pack(vec_v), "adamw_vectors")
    for i, name in enumerate(vec_names):
        width = vec_w[i].shape[1]
        shape = (width,) if name == "norm_final_g" else (1, width)
        res[name] = [t[i, :width].reshape(shape) for t in packed]

    order = ["norm_mix_g", "w_in", "conv_w", "conv_b", "conv_ln_g", "conv_ln_b", "w_out", "norm_x_g", "norm_mem_g",
             "w_xq", "w_xk", "w_xv", "w_xo", "norm_mlp_g", "w_up", "w_down", "norm_final_g"]
    out = [loss, grad_x[None]]
    for kind in range(4):
        out += [res[name][kind] for name in order]
    return tuple(out)
```

```python
import jax
import jax.numpy as jnp
from jax import lax
from jax.experimental import pallas as pl
from jax.experimental.pallas import tpu as pltpu

F32 = jnp.float32
BF16 = jnp.bfloat16

N_DEV = 8
EPS = 1e-6
NEG_INF = -1e30
ATT_HEADS = 8
HEAD_DIM = 64
D_ATT = ATT_HEADS * HEAD_DIM
D_CONV = 512
DILATIONS = (1, 4, 16)
HALF = 64
ROPE_THETA = 500000.0
ROT_DIM = HEAD_DIM // 4
CONV_WIDTH = 31
CONV_PAD = (CONV_WIDTH - 1) // 2
XATT_HEADS = 4
ADAM_LR = 0.001
ADAM_B1 = 0.9
ADAM_B2 = 0.999
ADAM_EPS = 1e-08
ADAM_WD = 0.01
ADAM_STEP = 10

LANES = 128
SUBLANES = 8
BF16_ROWS = 16
BQ = 128
WIN = BQ + 2 * HALF
MLP_SHARDS = 4
CONV_ROWS = 32
VMEM_LIMIT = 56 * 1024 * 1024
MESH = pl.DeviceIdType.MESH
ANY = pl.BlockSpec(memory_space=pl.ANY)

_NT = (((1,), (1,)), ((), ()))
_TN = (((0,), (0,)), ((), ()))


def _dot(a, b):
    return jnp.dot(a, b, preferred_element_type=F32)


def _dot_nt(a, b):
    return lax.dot_general(a, b, _NT, preferred_element_type=F32)


def _dot_tn(a, b):
    return lax.dot_general(a, b, _TN, preferred_element_type=F32)


def _params(*sem):
    return pltpu.CompilerParams(dimension_semantics=sem or None, vmem_limit_bytes=VMEM_LIMIT)


def _sigmoid(v):
    return 1.0 / (1.0 + jnp.exp(-v))


def _mean(v):
    return jnp.mean(v, axis=-1, keepdims=True)


def _rms_fwd(h, g):
    r = lax.rsqrt(_mean(h * h) + EPS)
    return h * r * g, r


def _rms_bwd(h, g, d_out):
    r = lax.rsqrt(_mean(h * h) + EPS)
    hn = h * r
    gd = d_out * g
    return r * (gd - hn * _mean(gd * hn)), d_out * hn


def _row_tile(s):
    return min(512, s)


def _full(shape):
    return pl.BlockSpec(shape, lambda *_: (0,) * len(shape))


def _mesh_pos():
    return lax.axis_index("x"), lax.axis_index("y"), lax.axis_index("c")


def _dev_index(p):
    return 4 * p[0] + 2 * p[1] + p[2]


class _Exchange:
    def __init__(self, arrays, gather):
        self.arrays, self.gather, self.n = list(arrays), gather, len(arrays)

    def out_shapes(self):
        return [jax.ShapeDtypeStruct(((N_DEV,) + a.shape) if self.gather else a.shape, a.dtype)
                for a in self.arrays]

    def sem_shapes(self):
        return [pltpu.SemaphoreType.DMA((7 * self.n,)), pltpu.SemaphoreType.DMA((7 * self.n,)),
                pltpu.SemaphoreType.DMA((self.n,))]

    def phases(self, x_refs, o_refs, send_sems, recv_sems, local_sems):
        n = self.n
        x, y, c = _mesh_pos()
        me, sibling = (x, y, c), (x, y, 1 - c)

        if self.gather:
            chips = [(1 - x, y), (x, 1 - y), (1 - x, 1 - y)]

            def copy(a, k, block, to, src=None):
                slot = o_refs[a].at[_dev_index(block)]
                return pltpu.make_async_remote_copy(
                    src_ref=slot if src is None else src, dst_ref=slot,
                    send_sem=send_sems.at[7 * a + k], recv_sem=recv_sems.at[7 * a + k],
                    device_id=to, device_id_type=MESH)

            def mine(a):
                return pltpu.make_async_copy(x_refs[a], o_refs[a].at[_dev_index(me)], local_sems.at[a])

            def first(a):
                return [copy(a, 0, me, sibling, src=x_refs[a])] + [
                    copy(a, 1 + j, me, (*chip, c), src=x_refs[a]) for j, chip in enumerate(chips)]

            def relayed(a, j):
                return copy(a, 4 + j, (*chips[j], c), sibling)

            def start():
                for a in range(n):
                    mine(a).start()
                    for cp in first(a):
                        cp.start()

            def relay():
                for j, chip in enumerate(chips):
                    for a in range(n):
                        copy(a, 1 + j, (*chip, c), me).wait_recv()
                        relayed(a, j).start()

            def finish():
                for a in range(n):
                    copy(a, 0, sibling, me).wait_recv()
                    for j, chip in enumerate(chips):
                        copy(a, 4 + j, (*chip, 1 - c), me).wait_recv()
                    for cp in first(a) + [relayed(a, j) for j in range(3)]:
                        cp.wait_send()
                    mine(a).wait()

            return start, relay, finish

        flips = [(dx, dy, dc) for dx in (0, 1) for dy in (0, 1) for dc in (0, 1)][1:]

        def peer(k):
            return tuple(1 - v if fl else v for v, fl in zip(me, flips[k]))

        def send(a, k):
            return pltpu.make_async_remote_copy(
                src_ref=x_refs[a].at[_dev_index(peer(k))], dst_ref=o_refs[a].at[_dev_index(me)],
                send_sem=send_sems.at[7 * a + k], recv_sem=recv_sems.at[7 * a + k],
                device_id=peer(k), device_id_type=MESH)

        def landed(a, k):
            slot = o_refs[a].at[_dev_index(peer(k))]
            return pltpu.make_async_remote_copy(
                src_ref=slot, dst_ref=slot, send_sem=send_sems.at[7 * a + k], recv_sem=recv_sems.at[7 * a + k],
                device_id=peer(k), device_id_type=MESH)

        def own(a):
            return pltpu.make_async_copy(x_refs[a].at[_dev_index(me)], o_refs[a].at[_dev_index(me)],
                                         local_sems.at[a])

        def start():
            for a in range(n):
                own(a).start()
            for k in range(7):
                for a in range(n):
                    send(a, k).start()

        def finish():
            for k in range(7):
                for a in range(n):
                    landed(a, k).wait_recv()
            for k in range(7):
                for a in range(n):
                    send(a, k).wait_send()
            for a in range(n):
                own(a).wait()

        return start, (lambda: None), finish


def _hosted_call(body, sides, *, name, grid, in_specs, out_specs, out_shape, scratch_shapes, args):
    n_in, n_out, ns = len(in_specs), len(out_specs), sum(s.n for s in sides)
    steps = 1
    for g in grid:
        steps *= g

    def wrapped(*refs):
        ins, s_ins = refs[:n_in], refs[n_in:n_in + ns]
        outs = refs[n_in + ns:n_in + ns + n_out]
        s_outs = refs[n_in + ns + n_out:n_in + 2 * ns + n_out]
        rest = refs[n_in + 2 * ns + n_out:]
        scratch, sems = rest[:len(rest) - 3 * len(sides)], rest[len(rest) - 3 * len(sides):]
        phases, off = [], 0
        for i, s in enumerate(sides):
            phases.append(s.phases(s_ins[off:off + s.n], s_outs[off:off + s.n], *sems[3 * i:3 * i + 3]))
            off += s.n
        lin = 0
        for ax, g in enumerate(grid):
            lin = lin * g + pl.program_id(ax)

        if sides:
            @pl.when(lin == 0)
            def _():
                for start, _, _ in phases:
                    start()

        body(*ins, *outs, *scratch)

        if sides:
            @pl.when(lin == min((3 * steps) // 4, steps - 1))
            def _():
                for _, relay, _ in phases:
                    relay()

            @pl.when(lin == steps - 1)
            def _():
                for _, _, finish in phases:
                    finish()

    res = pl.pallas_call(
        wrapped, name=name, grid=grid,
        in_specs=list(in_specs) + [ANY] * ns, out_specs=list(out_specs) + [ANY] * ns,
        out_shape=list(out_shape) + [sh for s in sides for sh in s.out_shapes()],
        scratch_shapes=list(scratch_shapes) + [sh for s in sides for sh in s.sem_shapes()],
        compiler_params=_params(*(("arbitrary",) * len(grid))),
    )(*args, *[a for s in sides for a in s.arrays])
    return res[:n_out], res[n_out:]


def _exchange_call(sides, name):
    ns = sum(s.n for s in sides)

    def body(*refs):
        x_refs, o_refs, sems = refs[:ns], refs[ns:2 * ns], refs[2 * ns:]
        phases, off = [], 0
        for i, s in enumerate(sides):
            phases.append(s.phases(x_refs[off:off + s.n], o_refs[off:off + s.n], *sems[3 * i:3 * i + 3]))
            off += s.n
        for step in range(3):
            for ph in phases:
                ph[step]()

    return pl.pallas_call(
        body, name=name,
        out_shape=[sh for s in sides for sh in s.out_shapes()],
        in_specs=[ANY] * ns, out_specs=[ANY] * ns,
        scratch_shapes=[sh for s in sides for sh in s.sem_shapes()],
    )(*[a for s in sides for a in s.arrays])


def _rotary_tables(s):
    half = ROT_DIM // 2
    freqs = ROPE_THETA ** (-jnp.arange(0, ROT_DIM, 2, dtype=F32) / ROT_DIM)
    ang = jnp.arange(s, dtype=F32)[:, None] * freqs[None, :]
    cos, sin = jnp.cos(ang), jnp.sin(ang)
    one = jnp.ones((s, HEAD_DIM - ROT_DIM), F32)
    zero = jnp.zeros((s, HEAD_DIM - ROT_DIM), F32)
    zh = jnp.zeros((s, half), F32)
    c64 = jnp.concatenate([cos, cos, one], axis=1)
    a64 = jnp.concatenate([-sin, zh, zero], axis=1)
    b64 = jnp.concatenate([zh, sin, zero], axis=1)
    return tuple(jnp.tile(t, (1, LANES // HEAD_DIM)) for t in (c64, a64, b64))


def _strided_spec(dil, tm, width):
    return pl.BlockSpec((dil, tm // dil, width), lambda i: (0, i, 0))


def _strided_shape(dil, s, width, dtype):
    return jax.ShapeDtypeStruct((dil, s // dil, width), dtype)


def _lane_scratch(tm, width):
    return pltpu.VMEM((width // LANES, tm, LANES), F32)


def _store_blocks(buf, v):
    for cb in range(buf.shape[0]):
        buf[cb] = v[:, cb * LANES:(cb + 1) * LANES]


def _load_blocks(buf):
    return jnp.concatenate([buf[cb] for cb in range(buf.shape[0])], axis=1)


def _write_strided(buf, dst_ref):
    dil, rows, _ = dst_ref.shape
    for r in range(dil):
        for cb in range(buf.shape[0]):
            dst_ref[r, :, cb * LANES:(cb + 1) * LANES] = buf[cb, pl.ds(r, rows, stride=dil), :].astype(dst_ref.dtype)


def _read_strided(src_ref, buf, add=False):
    dil, rows, _ = src_ref.shape
    for r in range(dil):
        for cb in range(buf.shape[0]):
            v = src_ref[r, :, cb * LANES:(cb + 1) * LANES].astype(F32)
            if add:
                v = v + buf[cb, pl.ds(r, rows, stride=dil), :]
            buf[cb, pl.ds(r, rows, stride=dil), :] = v


def _fwd_in(x, g, w_in, rot):
    s, d = x.shape
    n = w_in.shape[1]
    tm = _row_tile(s)
    dils = DILATIONS[1:]

    def body(x_ref, g_ref, w_ref, c_ref, a_ref, b_ref, xn_ref, qkv_ref, ag_ref, *rest):
        strided, ybuf = rest[:len(dils)], rest[len(dils)]
        xn = _rms_fwd(x_ref[...], g_ref[...])[0].astype(BF16)
        xn_ref[...] = xn
        y = _dot(xn, w_ref[...])
        reps = (1, D_ATT // LANES)
        cc, aa, bb = jnp.tile(c_ref[...], reps), jnp.tile(a_ref[...], reps), jnp.tile(b_ref[...], reps)
        parts = []
        for blk in range(2):
            t = y[:, blk * D_ATT:(blk + 1) * D_ATT]
            parts.append(t * cc + pltpu.roll(t, D_ATT - ROT_DIM // 2, 1) * aa + pltpu.roll(t, ROT_DIM // 2, 1) * bb)
        qkv = jnp.concatenate(parts + [y[:, 2 * D_ATT:3 * D_ATT]], axis=1)
        qkv_ref[...] = qkv.astype(BF16)
        _store_blocks(ybuf, qkv)
        for ref in strided:
            _write_strided(ybuf, ref)
        ag_ref[...] = y[:, 3 * D_ATT:].astype(BF16)

    row = lambda w: pl.BlockSpec((tm, w), lambda i: (i, 0))
    return pl.pallas_call(
        body, name="fwd_in", grid=(s // tm,),
        in_specs=[row(d), _full((1, d)), _full((d, n)), row(LANES), row(LANES), row(LANES)],
        out_specs=[row(d), row(3 * D_ATT), row(2 * D_CONV)] + [_strided_spec(dil, tm, 3 * D_ATT) for dil in dils],
        out_shape=[jax.ShapeDtypeStruct((s, d), BF16), jax.ShapeDtypeStruct((s, 3 * D_ATT), BF16),
                   jax.ShapeDtypeStruct((s, 2 * D_CONV), BF16)]
        + [_strided_shape(dil, s, 3 * D_ATT, BF16) for dil in dils],
        scratch_shapes=[_lane_scratch(tm, 3 * D_ATT)],
        compiler_params=_params("parallel"),
    )(x, g, w_in, *rot)


def _win_in_specs(length, col, width):
    per = BQ // HALF
    last = length // HALF - 1
    return [
        pl.BlockSpec((None, HALF, width), lambda r, j: (r, jnp.maximum(j * per - 1, 0), col)),
        pl.BlockSpec((None, BQ, width), lambda r, j: (r, j, col)),
        pl.BlockSpec((None, HALF, width), lambda r, j: (r, jnp.minimum(j * per + per, last), col)),
    ]


def _fill_window(win, prev_ref, main_ref, next_ref):
    win[0:HALF] = prev_ref[...]
    win[HALF:HALF + BQ] = main_ref[...]
    win[HALF + BQ:] = next_ref[...]


def _band_bias():
    blk = jnp.arange(2 * BQ)[:, None] & (BQ - 1)
    win = jnp.arange(WIN)[None, :]
    return jnp.where(jnp.abs(win - HALF - blk) <= HALF, 0.0, NEG_INF).astype(F32)


def _window_bias(band_ref, j, length):
    pos = j * BQ - HALF + lax.broadcasted_iota(jnp.int32, (1, WIN), 1)
    return band_ref[...] + jnp.where((pos >= 0) & (pos < length), 0.0, NEG_INF)


def _first_head():
    return lax.broadcasted_iota(jnp.int32, (1, LANES), 1) < HEAD_DIM


def _stack_heads(v, first):
    zero = jnp.zeros((), v.dtype)
    return jnp.concatenate([jnp.where(first, v, zero), jnp.where(first, zero, v)], axis=0)


def _unstack_heads(v, first):
    rows = v.shape[0] // 2
    return jnp.where(first, v[:rows], v[rows:])


def _stack_cols(v, first):
    other = pltpu.roll(v, HEAD_DIM, 1)
    stacked = jnp.concatenate([jnp.where(first, v, other), jnp.where(first, other, v)], axis=0)
    return jnp.tile(stacked, (1, WIN // LANES))


def _swa_fwd(qkv3, band, name, sides=()):
    dil, length, _ = qkv3.shape
    scale = HEAD_DIM ** -0.5

    def body(q_ref, kp, km, kn, vp, vm, vn, band_ref, o_ref, lse_ref, kwin, vwin):
        j = pl.program_id(1)
        _fill_window(kwin, kp, km, kn)
        _fill_window(vwin, vp, vm, vn)
        bias = _window_bias(band_ref, j, length)
        first = _first_head()
        pairs = [slice(pr * LANES, (pr + 1) * LANES) for pr in range(D_ATT // LANES)]
        scores = [_dot_nt(_stack_heads(q_ref[:, cols] * scale, first), kwin[:, cols]) + bias for cols in pairs]
        stats, probs = [], []
        for sc in scores:
            m = jnp.max(sc, axis=-1, keepdims=True)
            p = jnp.exp(sc - m)
            stats.append((m, jnp.sum(p, axis=-1, keepdims=True)))
            probs.append(p.astype(BF16))
        for cols, p, (m, den) in zip(pairs, probs, stats):
            pv = _dot(p, vwin[:, cols]) * (1.0 / den)
            o_ref[:, cols] = _unstack_heads(pv, first).astype(BF16)
            lse_ref[:, cols] = _unstack_heads(jnp.broadcast_to(m + jnp.log(den), (2 * BQ, LANES)), first)

    blk = lambda w: pl.BlockSpec((None, BQ, w), lambda r, j: (r, j, 0))
    return _hosted_call(
        body, sides, name=name, grid=(dil, length // BQ),
        in_specs=[pl.BlockSpec((None, BQ, D_ATT), lambda r, j: (r, j, 0))]
        + _win_in_specs(length, 1, D_ATT) + _win_in_specs(length, 2, D_ATT) + [_full((2 * BQ, WIN))],
        out_specs=[blk(D_ATT), blk(D_ATT)],
        out_shape=[jax.ShapeDtypeStruct((dil, length, D_ATT), BF16),
                   jax.ShapeDtypeStruct((dil, length, D_ATT), F32)],
        scratch_shapes=[pltpu.VMEM((WIN, D_ATT), BF16), pltpu.VMEM((WIN, D_ATT), BF16)],
        args=[qkv3] * 7 + [band])


def _glu(v):
    return v[:, :D_CONV].astype(F32) * _sigmoid(v[:, D_CONV:].astype(F32))


def _halo_specs(tm, width, col=0):
    per = tm // BF16_ROWS
    return lambda nblk: [
        pl.BlockSpec((BF16_ROWS, width), lambda i: (jnp.maximum(i * per - 1, 0), col)),
        pl.BlockSpec((tm, width), lambda i: (i, col)),
        pl.BlockSpec((BF16_ROWS, width), lambda i: (jnp.minimum(i * per + per, nblk - 1), col)),
    ]


def _fill_halo(buf, i, ntiles, tm, prev, main, nxt):
    buf[0:BF16_ROWS] = jnp.where(i == 0, 0.0, prev)
    buf[BF16_ROWS:BF16_ROWS + tm] = main
    buf[BF16_ROWS + tm:] = jnp.where(i == ntiles - 1, 0.0, nxt)


def _halo_scratch(tm):
    return [pltpu.VMEM((tm + 2 * BF16_ROWS, D_CONV), F32),
            pltpu.VMEM((SUBLANES - 1, tm + 2 * BF16_ROWS - SUBLANES, D_CONV), F32)]


def _shift_copies(buf, shifted, tm):
    rows = tm + 2 * BF16_ROWS - SUBLANES
    for b in range(1, SUBLANES):
        shifted[b - 1] = buf[pl.ds(b, rows), :]


def _tap(buf, shifted, off, rows, base=0):
    a, b = divmod(off, SUBLANES)
    start = base + SUBLANES * a
    if not isinstance(start, int):
        start = pl.multiple_of(start, SUBLANES)
    if b == 0:
        return buf[pl.ds(start, rows), :]
    return shifted[b - 1, pl.ds(start, rows), :]


def _fwd_conv(ag, cw, cb, lg, lb, sides=()):
    s = ag.shape[0]
    tm = _row_tile(s)
    nt = s // tm

    def body(agp, agm, agn, cw_ref, cb_ref, lg_ref, lb_ref, c1_ref, co_ref, ubuf, ush):
        i = pl.program_id(0)
        _fill_halo(ubuf, i, nt, tm, _glu(agp[...]), _glu(agm[...]), _glu(agn[...]))
        _shift_copies(ubuf, ush, tm)
        w, cb = cw_ref[...], cb_ref[...]

        def chunk(c, carry):
            base = pl.multiple_of(c * CONV_ROWS, CONV_ROWS)
            acc = jnp.zeros((CONV_ROWS, D_CONV), F32)
            for k in range(CONV_WIDTH):
                acc = acc + _tap(ubuf, ush, k + 1, CONV_ROWS, base) * w[k:k + 1, :]
            c1_ref[pl.ds(base, CONV_ROWS), :] = acc + cb
            return carry

        lax.fori_loop(0, tm // CONV_ROWS, chunk, 0)
        c1 = c1_ref[...]
        xc = c1 - _mean(c1)
        ln = xc * lax.rsqrt(_mean(xc * xc) + EPS) * lg_ref[...] + lb_ref[...]
        co_ref[...] = (ln * _sigmoid(ln)).astype(BF16)

    vec = _full((1, D_CONV))
    return _hosted_call(
        body, sides, name="fwd_conv", grid=(nt,),
        in_specs=_halo_specs(tm, 2 * D_CONV)(s // BF16_ROWS) + [_full((32, D_CONV)), vec, vec, vec],
        out_specs=[pl.BlockSpec((tm, D_CONV), lambda i: (i, 0))] * 2,
        out_shape=[jax.ShapeDtypeStruct((s, D_CONV), F32), jax.ShapeDtypeStruct((s, D_CONV), BF16)],
        scratch_shapes=_halo_scratch(tm),
        args=[ag, ag, ag, cw, cb, lg, lb])


def _fwd_mix_out(outs, lses, conv_out, x, w_out):
    s, d = x.shape
    tm = _row_tile(s)
    dils = DILATIONS[1:]
    nd = len(dils)

    def body(o1, *rest):
        o_str, l1, l_str = rest[:nd], rest[nd], rest[nd + 1:2 * nd + 1]
        co, x_ref, w_ref, h_ref, cat_ref, lt_ref = rest[2 * nd + 1:2 * nd + 7]
        lt_str = rest[2 * nd + 7:3 * nd + 7]
        obufs, lbufs, ltbuf = rest[3 * nd + 7:4 * nd + 7], rest[4 * nd + 7:5 * nd + 7], rest[5 * nd + 7]
        for ref, buf in zip(o_str + l_str, obufs + lbufs):
            _read_strided(ref, buf)
        lse = [l1[0]] + [_load_blocks(buf) for buf in lbufs]
        out = [o1[0].astype(F32)] + [_load_blocks(buf) for buf in obufs]
        m = lse[0]
        for v in lse[1:]:
            m = jnp.maximum(m, v)
        e = [jnp.exp(v - m) for v in lse]
        den = sum(e[1:], e[0])
        att = (sum((ev * ov for ev, ov in zip(e[1:], out[1:])), e[0] * out[0]) / den).astype(BF16)
        lt = m + jnp.log(den)
        lt_ref[...] = lt
        _store_blocks(ltbuf, lt)
        for ref in lt_str:
            _write_strided(ltbuf, ref)
        cat_ref[:, :D_ATT] = att
        cat_ref[:, D_ATT:] = co[...]
        h_ref[...] = x_ref[...] + _dot(att, w_ref[:D_ATT, :]) + _dot(co[...], w_ref[D_ATT:, :])

    row = lambda w: pl.BlockSpec((tm, w), lambda i: (i, 0))
    nat = pl.BlockSpec((1, tm, D_ATT), lambda i: (0, i, 0))
    strided = [_strided_spec(dil, tm, D_ATT) for dil in dils]
    return pl.pallas_call(
        body, name="fwd_mix_out", grid=(s // tm,),
        in_specs=[nat] + strided + [nat] + strided + [row(D_ATT), row(d), _full((d, d))],
        out_specs=[row(d), row(d), row(D_ATT)] + strided,
        out_shape=[jax.ShapeDtypeStruct((s, d), F32), jax.ShapeDtypeStruct((s, d), BF16),
                   jax.ShapeDtypeStruct((s, D_ATT), F32)] + [_strided_shape(dil, s, D_ATT, F32) for dil in dils],
        scratch_shapes=[_lane_scratch(tm, D_ATT)] * (2 * nd + 1),
        compiler_params=_params("parallel"),
    )(*outs, *lses, conv_out, x, w_out)


def _fwd_mem(mem, g, wk, wv):
    m, d = mem.shape

    def body(mem_ref, g_ref, wk_ref, wv_ref, mn_ref, xk_ref, xv_ref):
        mn = _rms_fwd(mem_ref[...], g_ref[...])[0].astype(BF16)
        mn_ref[...] = mn
        xk_ref[...] = _dot(mn, wk_ref[...]).astype(BF16)
        xv_ref[...] = _dot(mn, wv_ref[...]).astype(BF16)

    return pl.pallas_call(
        body, name="fwd_mem",
        out_shape=[jax.ShapeDtypeStruct((m, d), BF16)] * 3,
        compiler_params=_params(),
    )(mem, g, wk, wv)


def _softmax(sc):
    p = jnp.exp(sc - jnp.max(sc, axis=-1, keepdims=True))
    return p / jnp.sum(p, axis=-1, keepdims=True)


def _fwd_xattn(h1, g, wq, xk, xv, wo):
    s, d = h1.shape
    m = xk.shape[0]
    tm = _row_tile(s)
    hd = d // XATT_HEADS

    def body(h_ref, g_ref, wq_ref, xk_ref, xv_ref, wo_ref, h2_ref, hn_ref, xq_ref, xo_ref):
        h = h_ref[...]
        hn = _rms_fwd(h, g_ref[...])[0].astype(BF16)
        hn_ref[...] = hn
        xq = _dot(hn, wq_ref[...]).astype(BF16)
        xq_ref[...] = xq
        heads = [slice(i * hd, (i + 1) * hd) for i in range(XATT_HEADS)]
        scores = [_dot_nt(xq[:, cols], xk_ref[:, cols]) * hd ** -0.5 for cols in heads]
        probs = [_softmax(sc).astype(BF16) for sc in scores]
        for cols, pr in zip(heads, probs):
            xo_ref[:, cols] = _dot(pr, xv_ref[:, cols]).astype(BF16)
        h2_ref[...] = h + _dot(xo_ref[...], wo_ref[...])

    row = pl.BlockSpec((tm, d), lambda i: (i, 0))
    return pl.pallas_call(
        body, name="fwd_xattn", grid=(s // tm,),
        in_specs=[row, _full((1, d)), _full((d, d)), _full((m, d)), _full((m, d)), _full((d, d))],
        out_specs=[row] * 4,
        out_shape=[jax.ShapeDtypeStruct((s, d), F32)] + [jax.ShapeDtypeStruct((s, d), BF16)] * 3,
        compiler_params=_params("parallel"),
    )(h1, g, wq, xk, xv, wo)


def _fwd_mlp_loss(h2, g, w_up, w_down, gf, target):
    s, d = h2.shape
    nsh, _, f = w_up.shape
    fb = MLP_SHARDS * f
    nb = nsh // MLP_SHARDS
    tm = _row_tile(s)

    def body(h_ref, g_ref, wu_ref, wd_ref, gf_ref, t_ref,
             hn_ref, act_ref, dh_ref, dhb_ref, loss_ref, ggf_ref, acc):
        i, k = pl.program_id(0), pl.program_id(1)

        @pl.when(k == 0)
        def _():
            hn_ref[...] = _rms_fwd(h_ref[...], g_ref[...])[0].astype(BF16)
            acc[...] = jnp.zeros_like(acc)

        @pl.when((i == 0) & (k == 0))
        def _():
            loss_ref[...] = jnp.zeros_like(loss_ref)
            ggf_ref[...] = jnp.zeros_like(ggf_ref)

        hn = hn_ref[...]
        ups = [_dot(hn, wu_ref[c]) for c in range(MLP_SHARDS)]
        for c, u in enumerate(ups):
            act_ref[:, c * f:(c + 1) * f] = jnp.square(jnp.maximum(u, 0.0)).astype(BF16)
        acc[...] += _dot(act_ref[...], wd_ref[...])

        @pl.when(k == nb - 1)
        def _():
            h3 = h_ref[...] + acc[...]
            gfv = gf_ref[...]
            y, _ = _rms_fwd(h3, gfv)
            err = y - t_ref[...]
            loss_ref[...] += 0.5 * jnp.sum(_mean(err * err))
            dh3, gg = _rms_bwd(h3, gfv, err * (1.0 / d))
            ggf_ref[...] += jnp.sum(gg, axis=0, keepdims=True)
            dh_ref[...] = dh3
            dhb_ref[...] = dh3.astype(BF16)

    row = pl.BlockSpec((tm, d), lambda i, k: (i, 0))
    return pl.pallas_call(
        body, name="fwd_mlp_loss", grid=(s // tm, nb),
        in_specs=[row, _full((1, d)),
                  pl.BlockSpec((MLP_SHARDS, d, f), lambda i, k: (k, 0, 0)),
                  pl.BlockSpec((fb, d), lambda i, k: (k, 0)),
                  _full((1, d)), row],
        out_specs=[row, pl.BlockSpec((tm, fb), lambda i, k: (i, k)), row, row,
                   _full((1, LANES)), _full((1, d))],
        out_shape=[jax.ShapeDtypeStruct((s, d), BF16), jax.ShapeDtypeStruct((s, nsh * f), BF16),
                   jax.ShapeDtypeStruct((s, d), F32), jax.ShapeDtypeStruct((s, d), BF16),
                   jax.ShapeDtypeStruct((1, LANES), F32), jax.ShapeDtypeStruct((1, d), F32)],
        scratch_shapes=[pltpu.VMEM((tm, d), F32)],
        compiler_params=_params("arbitrary", "arbitrary"),
    )(h2, g, w_up, w_down, gf, target)


def _bwd_mlp(dh3, dh3b, act, w_up_t, w_down, h2, g):
    s, d = h2.shape
    ff = w_down.shape[0]
    fb = MLP_SHARDS * (ff // N_DEV)
    nb = ff // fb
    tm = _row_tile(s)

    def body(dh_ref, dhb_ref, act_ref, wut_ref, wd_ref, h_ref, g_ref,
             du_ref, dh2_ref, dh2b_ref, gg_ref, acc):
        i, k = pl.program_id(0), pl.program_id(1)

        @pl.when(k == 0)
        def _():
            acc[...] = jnp.zeros_like(acc)

        @pl.when((i == 0) & (k == 0))
        def _():
            gg_ref[...] = jnp.zeros_like(gg_ref)

        dhb = dhb_ref[...]
        f = fb // MLP_SHARDS
        shards = [slice(c * f, (c + 1) * f) for c in range(MLP_SHARDS)]
        dacts = [_dot_nt(dhb, wd_ref[cols, :]) for cols in shards]
        for cols, dact in zip(shards, dacts):
            du_ref[:, cols] = (dact * (2.0 * jnp.sqrt(act_ref[:, cols].astype(F32)))).astype(BF16)
        acc[...] += _dot(du_ref[...], wut_ref[...])

        @pl.when(k == nb - 1)
        def _():
            dh, gg = _rms_bwd(h_ref[...], g_ref[...], acc[...])
            gg_ref[...] += jnp.sum(gg, axis=0, keepdims=True)
            dh2 = dh_ref[...] + dh
            dh2_ref[...] = dh2
            dh2b_ref[...] = dh2.astype(BF16)

    row = pl.BlockSpec((tm, d), lambda i, k: (i, 0))
    col = pl.BlockSpec((tm, fb), lambda i, k: (i, k))
    wblk = pl.BlockSpec((fb, d), lambda i, k: (k, 0))
    return pl.pallas_call(
        body, name="bwd_mlp", grid=(s // tm, nb),
        in_specs=[row, row, col, wblk, wblk, row, _full((1, d))],
        out_specs=[col, row, row, _full((1, d))],
        out_shape=[jax.ShapeDtypeStruct((s, ff), BF16), jax.ShapeDtypeStruct((s, d), F32),
                   jax.ShapeDtypeStruct((s, d), BF16), jax.ShapeDtypeStruct((1, d), F32)],
        scratch_shapes=[pltpu.VMEM((tm, d), F32)],
        compiler_params=_params("arbitrary", "arbitrary"),
    )(dh3, dh3b, act, w_up_t, w_down, h2, g)


def _bwd_xattn(dh2, dh2b, h1, g, xq, xk, xv, wq, wo, sides=()):
    s, d = h1.shape
    m = xk.shape[0]
    tm = _row_tile(s)
    hd = d // XATT_HEADS
    scale = hd ** -0.5

    def body(dh_ref, dhb_ref, h_ref, g_ref, xq_ref, xk_ref, xv_ref, wq_ref, wo_ref,
             dh1_ref, dh1b_ref, dxq_ref, dxk_ref, dxv_ref, gg_ref):
        @pl.when(pl.program_id(0) == 0)
        def _():
            dxk_ref[...] = jnp.zeros_like(dxk_ref)
            dxv_ref[...] = jnp.zeros_like(dxv_ref)
            gg_ref[...] = jnp.zeros_like(gg_ref)

        dxo = _dot_nt(dhb_ref[...], wo_ref[...])
        heads = [slice(i * hd, (i + 1) * hd) for i in range(XATT_HEADS)]
        dxos = [dxo[:, cols].astype(BF16) for cols in heads]
        scores = [_dot_nt(xq_ref[:, cols], xk_ref[:, cols]) * scale for cols in heads]
        dprs = [_dot_nt(dxo_h, xv_ref[:, cols]) for dxo_h, cols in zip(dxos, heads)]
        probs, dscs = [], []
        for sc, dpr in zip(scores, dprs):
            pr = _softmax(sc)
            dscs.append((pr * (dpr - jnp.sum(dpr * pr, axis=-1, keepdims=True)) * scale).astype(BF16))
            probs.append(pr.astype(BF16))
        for cols, dsc, pr, dxo_h in zip(heads, dscs, probs, dxos):
            dxq_ref[:, cols] = _dot(dsc, xk_ref[:, cols]).astype(BF16)
            dxk_ref[:, cols] += _dot_tn(dsc, xq_ref[:, cols])
            dxv_ref[:, cols] += _dot_tn(pr, dxo_h)
        dh, gg = _rms_bwd(h_ref[...], g_ref[...], _dot_nt(dxq_ref[...], wq_ref[...]))
        gg_ref[...] += jnp.sum(gg, axis=0, keepdims=True)
        dh1 = dh_ref[...] + dh
        dh1_ref[...] = dh1
        dh1b_ref[...] = dh1.astype(BF16)

    row = pl.BlockSpec((tm, d), lambda i: (i, 0))
    return _hosted_call(
        body, sides, name="bwd_xattn", grid=(s // tm,),
        in_specs=[row, row, row, _full((1, d)), row, _full((m, d)), _full((m, d)), _full((d, d)), _full((d, d))],
        out_specs=[row, row, row, _full((m, d)), _full((m, d)), _full((1, d))],
        out_shape=[jax.ShapeDtypeStruct((s, d), F32), jax.ShapeDtypeStruct((s, d), BF16),
                   jax.ShapeDtypeStruct((s, d), BF16), jax.ShapeDtypeStruct((m, d), F32),
                   jax.ShapeDtypeStruct((m, d), F32), jax.ShapeDtypeStruct((1, d), F32)],
        scratch_shapes=[],
        args=[dh2, dh2b, h1, g, xq, xk, xv, wq, wo])


def _bwd_mem(mem, g, mn, dxk, dxv, wk, wv):
    m, d = mem.shape

    def body(mem_ref, g_ref, mn_ref, dxk_ref, dxv_ref, wk_ref, wv_ref, gk_ref, gv_ref, gg_ref):
        dk, dv = dxk_ref[...].astype(BF16), dxv_ref[...].astype(BF16)
        gk_ref[...] = _dot_tn(mn_ref[...], dk).astype(BF16)
        gv_ref[...] = _dot_tn(mn_ref[...], dv).astype(BF16)
        dmn = _dot_nt(dk, wk_ref[...]) + _dot_nt(dv, wv_ref[...])
        _, gg = _rms_bwd(mem_ref[...], g_ref[...], dmn)
        gg_ref[...] = jnp.sum(gg, axis=0, keepdims=True)

    return pl.pallas_call(
        body, name="bwd_mem",
        out_shape=[jax.ShapeDtypeStruct((d, d), BF16), jax.ShapeDtypeStruct((d, d), BF16),
                   jax.ShapeDtypeStruct((1, d), F32)],
        compiler_params=_params(),
    )(mem, g, mn, dxk, dxv, wk, wv)


def _bwd_mix_out(dh1b, w_out, cat, head_ones):
    s, d = dh1b.shape
    tm = _row_tile(s)
    dils = DILATIONS[1:]
    nd = len(dils)

    def body(dh_ref, w_ref, cat_ref, ones_ref, dcat_ref, dsum_ref, *rest):
        da_str, ds_str, dbuf, sbuf = rest[:nd], rest[nd:2 * nd], rest[2 * nd], rest[2 * nd + 1]
        dcat = _dot_nt(dh_ref[...], w_ref[...])
        dcat_ref[...] = dcat.astype(BF16)
        datt = dcat[:, :D_ATT]
        prod = datt * cat_ref[...].astype(F32)
        hi = prod.astype(BF16)
        lo = (prod - hi.astype(F32)).astype(BF16)
        dsum = _dot(hi, ones_ref[...]) + _dot(lo, ones_ref[...])
        dsum_ref[...] = dsum
        _store_blocks(dbuf, datt)
        _store_blocks(sbuf, dsum)
        for da_ref, ds_ref in zip(da_str, ds_str):
            _write_strided(dbuf, da_ref)
            _write_strided(sbuf, ds_ref)

    row = lambda w: pl.BlockSpec((tm, w), lambda i: (i, 0))
    strided = [_strided_spec(dil, tm, D_ATT) for dil in dils]
    return pl.pallas_call(
        body, name="bwd_mix_out", grid=(s // tm,),
        in_specs=[row(d), _full((d, d)), row(D_ATT), _full((D_ATT, D_ATT))],
        out_specs=[row(d), row(D_ATT)] + strided + strided,
        out_shape=[jax.ShapeDtypeStruct((s, d), BF16), jax.ShapeDtypeStruct((s, D_ATT), F32)]
        + [_strided_shape(dil, s, D_ATT, BF16) for dil in dils]
        + [_strided_shape(dil, s, D_ATT, F32) for dil in dils],
        scratch_shapes=[_lane_scratch(tm, D_ATT)] * 2,
        compiler_params=_params("parallel"),
    )(dh1b, w_out, cat, head_ones)


def _bwd_conv(dcat, c1, ag, cw, lg, lb, sides=()):
    s = ag.shape[0]
    tm = _row_tile(s)
    nt = s // tm

    def body(dp, dm, dn, cp, cm, cn, agp, agm, agn, cw_ref, lg_ref, lb_ref,
             dag_ref, gcw_ref, gcb_ref, glg_ref, glb_ref, ubuf, ush, dbuf, dsh, gacc):
        i = pl.program_id(0)

        @pl.when(i == 0)
        def _():
            gacc[...] = jnp.zeros_like(gacc)
            gcb_ref[...] = jnp.zeros_like(gcb_ref)
            glg_ref[...] = jnp.zeros_like(glg_ref)
            glb_ref[...] = jnp.zeros_like(glb_ref)

        lgv, lbv = lg_ref[...], lb_ref[...]

        def norm_bwd(dco, c1v):
            xc = c1v - _mean(c1v)
            rs = lax.rsqrt(_mean(xc * xc) + EPS)
            z = xc * rs
            ln = z * lgv + lbv
            sg = _sigmoid(ln)
            dln = dco.astype(F32) * (sg * (1.0 + ln * (1.0 - sg)))
            dz = dln * lgv
            return rs * (dz - _mean(dz) - z * _mean(dz * z)), dln, z

        dc_m, dln, z = norm_bwd(dm[...], cm[...])
        glg_ref[...] += jnp.sum(dln * z, axis=0, keepdims=True)
        glb_ref[...] += jnp.sum(dln, axis=0, keepdims=True)
        gcb_ref[...] += jnp.sum(dc_m, axis=0, keepdims=True)
        _fill_halo(dbuf, i, nt, tm, norm_bwd(dp[...], cp[...])[0], dc_m, norm_bwd(dn[...], cn[...])[0])
        _fill_halo(ubuf, i, nt, tm, _glu(agp[...]), _glu(agm[...]), _glu(agn[...]))
        _shift_copies(dbuf, dsh, tm)
        _shift_copies(ubuf, ush, tm)

        w = cw_ref[...]

        def chunk(c, carry):
            base = pl.multiple_of(c * CONV_ROWS, CONV_ROWS)
            rows = pl.ds(base, CONV_ROWS)
            dc = dbuf[pl.ds(pl.multiple_of(base + BF16_ROWS, SUBLANES), CONV_ROWS), :]
            du = jnp.zeros((CONV_ROWS, D_CONV), F32)
            for k in range(CONV_WIDTH):
                du = du + _tap(dbuf, dsh, CONV_WIDTH - k, CONV_ROWS, base) * w[k:k + 1, :]
                prod = dc * _tap(ubuf, ush, k + 1, CONV_ROWS, base)
                part = prod[0:SUBLANES]
                for r in range(SUBLANES, CONV_ROWS, SUBLANES):
                    part = part + prod[r:r + SUBLANES]
                gacc[k] += part
            a = agm[rows, :D_CONV].astype(F32)
            sg = _sigmoid(agm[rows, D_CONV:].astype(F32))
            dag_ref[rows, :D_CONV] = (du * sg).astype(BF16)
            dag_ref[rows, D_CONV:] = (du * a * sg * (1.0 - sg)).astype(BF16)
            return carry

        lax.fori_loop(0, tm // CONV_ROWS, chunk, 0)

        @pl.when(i == nt - 1)
        def _():
            tap = lax.broadcasted_iota(jnp.int32, (32, D_CONV), 0)
            gcw = jnp.zeros((32, D_CONV), F32)
            for k in range(CONV_WIDTH):
                gcw = jnp.where(tap == k, jnp.sum(gacc[k], axis=0, keepdims=True), gcw)
            gcw_ref[...] = gcw

    vec = _full((1, D_CONV))
    nblk = s // BF16_ROWS
    return _hosted_call(
        body, sides, name="bwd_conv", grid=(nt,),
        in_specs=_halo_specs(tm, D_CONV, 1)(nblk) + _halo_specs(tm, D_CONV)(nblk) + _halo_specs(tm, 2 * D_CONV)(nblk)
        + [_full((32, D_CONV)), vec, vec],
        out_specs=[pl.BlockSpec((tm, 2 * D_CONV), lambda i: (i, 0)), _full((32, D_CONV)), vec, vec, vec],
        out_shape=[jax.ShapeDtypeStruct((s, 2 * D_CONV), BF16), jax.ShapeDtypeStruct((32, D_CONV), F32)]
        + [jax.ShapeDtypeStruct((1, D_CONV), F32)] * 3,
        scratch_shapes=_halo_scratch(tm) + _halo_scratch(tm) + [pltpu.VMEM((32, SUBLANES, D_CONV), F32)],
        args=[dcat, dcat, dcat, c1, c1, c1, ag, ag, ag, cw, lg, lb])


def _swa_bwd(qkv3, do3, lt3, ds3, band, name, sides=()):
    dil, length, _ = qkv3.shape
    nb = length // BQ
    scale = HEAD_DIM ** -0.5
    assert WIN == 2 * BQ and BQ == 2 * HALF

    def body(q_ref, kp, km, kn, vp, vm, vn, do_ref, l_ref, s_ref, band_ref, dq_ref, dkv_ref, kwin, vwin, pend, keep):
        j = pl.program_id(1)

        @pl.when(j == 0)
        def _():
            pend[...] = jnp.zeros_like(pend)
            keep[...] = jnp.zeros_like(keep)

        @pl.when(j < nb)
        def _():
            _fill_window(kwin, kp, km, kn)
            _fill_window(vwin, vp, vm, vn)
            bias = _window_bias(band_ref, j, length)
            first = _first_head()
            pairs = [slice(pr * LANES, (pr + 1) * LANES) for pr in range(D_ATT // LANES)]
            qs = [_stack_heads(q_ref[:, cols] * scale, first) for cols in pairs]
            dos = [_stack_heads(do_ref[:, cols], first) for cols in pairs]
            scores = [_dot_nt(q, kwin[:, cols]) for q, cols in zip(qs, pairs)]
            dps = [_dot_nt(do, vwin[:, cols]) for do, cols in zip(dos, pairs)]
            probs, dscs = [], []
            for cols, sc, dp in zip(pairs, scores, dps):
                p = jnp.exp(sc + (bias - _stack_cols(l_ref[:, cols], first)))
                dscs.append((p * (dp - _stack_cols(s_ref[:, cols], first))).astype(BF16))
                probs.append(p.astype(BF16))
            dqs = [_dot(dsc, kwin[:, cols] * scale) for dsc, cols in zip(dscs, pairs)]
            dks = [_dot_tn(dsc, q) for dsc, q in zip(dscs, qs)]
            dvs = [_dot_tn(p, do) for p, do in zip(probs, dos)]
            for pr, cols in enumerate(pairs):
                dq_ref[:, cols] = _unstack_heads(dqs[pr], first).astype(BF16)
                for part, at in ((dks[pr], pr * LANES), (dvs[pr], D_ATT + pr * LANES)):
                    at = slice(at, at + LANES)
                    dkv_ref[:HALF, at] = keep[:, at].astype(BF16)
                    dkv_ref[HALF:, at] = (pend[:HALF, at] + part[:HALF]).astype(BF16)
                    keep[:, at] = pend[HALF:, at] + part[HALF:BQ]
                    pend[:, at] = part[BQ:]

        @pl.when(j == nb)
        def _():
            dkv_ref[:HALF] = keep[...].astype(BF16)
            dkv_ref[HALF:] = pend[:HALF].astype(BF16)

    def clamp(idx):
        return lambda r, j: idx(r, jnp.minimum(j, nb - 1))

    main = pl.BlockSpec((None, BQ, D_ATT), clamp(lambda r, j: (r, j, 0)))
    wins = [pl.BlockSpec(sp.block_shape, clamp(sp.index_map))
            for c in (1, 2) for sp in _win_in_specs(length, c, D_ATT)]
    return _hosted_call(
        body, sides, name=name, grid=(dil, nb + 1),
        in_specs=[main] + wins + [main] * 3 + [_full((2 * BQ, WIN))],
        out_specs=[main, pl.BlockSpec((None, BQ, 2 * D_ATT), lambda r, j: (r, jnp.maximum(j - 1, 0), 0))],
        out_shape=[jax.ShapeDtypeStruct((dil, length, D_ATT), BF16),
                   jax.ShapeDtypeStruct((dil, length, 2 * D_ATT), BF16)],
        scratch_shapes=[pltpu.VMEM((WIN, D_ATT), BF16)] * 2
        + [pltpu.VMEM((BQ, 2 * D_ATT), F32), pltpu.VMEM((HALF, 2 * D_ATT), F32)],
        args=[qkv3] * 7 + [do3, lt3, ds3, band])


def _bwd_in(dqs, dkvs, dag, w_in, x, g, dh1, rot):
    s, d = x.shape
    n = w_in.shape[1]
    tm = _row_tile(s)
    dils = DILATIONS[1:]
    nd = len(dils)

    def body(q1, *rest):
        q_str, kv1, kv_str = rest[:nd], rest[nd], rest[nd + 1:2 * nd + 1]
        dag_ref, w_ref, x_ref, g_ref, dh_ref, c_ref, a_ref, b_ref, gx_ref, dy_ref, gg_ref, qbuf, kvbuf = rest[2 * nd + 1:]

        @pl.when(pl.program_id(0) == 0)
        def _():
            gg_ref[...] = jnp.zeros_like(gg_ref)

        _store_blocks(qbuf, q1[0].astype(F32))
        _store_blocks(kvbuf, kv1[0].astype(F32))
        for ref in q_str:
            _read_strided(ref, qbuf, add=True)
        for ref in kv_str:
            _read_strided(ref, kvbuf, add=True)
        dq, dkv = _load_blocks(qbuf), _load_blocks(kvbuf)
        reps = (1, D_ATT // LANES)
        cc, aa, bb = jnp.tile(c_ref[...], reps), jnp.tile(a_ref[...], reps), jnp.tile(b_ref[...], reps)
        for blk, t in enumerate((dq, dkv[:, :D_ATT])):
            dt = t * cc + pltpu.roll(t * aa, ROT_DIM // 2, 1) + pltpu.roll(t * bb, D_ATT - ROT_DIM // 2, 1)
            dy_ref[:, blk * D_ATT:(blk + 1) * D_ATT] = dt.astype(BF16)
        dy_ref[:, 2 * D_ATT:3 * D_ATT] = dkv[:, D_ATT:].astype(BF16)
        dy_ref[:, 3 * D_ATT:] = dag_ref[...]
        dx, gg = _rms_bwd(x_ref[...], g_ref[...], _dot_nt(dy_ref[...], w_ref[...]))
        gg_ref[...] += jnp.sum(gg, axis=0, keepdims=True)
        gx_ref[...] = dh_ref[...] + dx

    row = lambda w: pl.BlockSpec((tm, w), lambda i: (i, 0))
    def strided(width):
        return [pl.BlockSpec((1, tm, width), lambda i: (0, i, 0))] + [_strided_spec(dil, tm, width) for dil in dils]

    return pl.pallas_call(
        body, name="bwd_in", grid=(s // tm,),
        in_specs=strided(D_ATT) + strided(2 * D_ATT)
        + [row(2 * D_CONV), _full((d, n)), row(d), _full((1, d)), row(d)] + [row(LANES)] * 3,
        out_specs=[row(d), row(n), _full((1, d))],
        out_shape=[jax.ShapeDtypeStruct((s, d), F32), jax.ShapeDtypeStruct((s, n), BF16),
                   jax.ShapeDtypeStruct((1, d), F32)],
        scratch_shapes=[_lane_scratch(tm, D_ATT), _lane_scratch(tm, 2 * D_ATT)],
        compiler_params=_params("arbitrary"),
    )(*dqs, *dkvs, dag, w_in, x, g, dh1, *rot)


def _wgrad(a, b, name, a_blk=None, b_blk=None, stack=None, tm=1024):
    s, ka = a.shape
    nb = b.shape[1]
    a_blk, b_blk = a_blk or ka, b_blk or nb
    na, nbl = ka // a_blk, nb // b_blk
    assert na == 1 or nbl == 1
    tm = min(tm, s)
    nt = s // tm
    per = b_blk // stack if stack else 0

    def body(a_ref, b_ref, o_ref, acc):
        t = pl.program_id(1)

        @pl.when(t == 0)
        def _():
            acc[...] = jnp.zeros_like(acc)

        acc[...] += _dot_tn(a_ref[...], b_ref[...])

        @pl.when(t == nt - 1)
        def _():
            if stack:
                for c in range(per):
                    o_ref[c] = acc[:, c * stack:(c + 1) * stack].astype(BF16)
            else:
                o_ref[...] = acc[...].astype(BF16)

    if stack:
        out_spec = pl.BlockSpec((per, ka, stack), lambda k, t: (k, 0, 0))
        out_shape = jax.ShapeDtypeStruct((nb // stack, ka, stack), BF16)
    elif na > 1:
        out_spec = pl.BlockSpec((a_blk, nb), lambda k, t: (k, 0))
        out_shape = jax.ShapeDtypeStruct((ka, nb), BF16)
    else:
        out_spec = pl.BlockSpec((ka, b_blk), lambda k, t: (0, k))
        out_shape = jax.ShapeDtypeStruct((ka, nb), BF16)
    return pl.pallas_call(
        body, name=name, grid=(na * nbl, nt),
        in_specs=[pl.BlockSpec((tm, a_blk), (lambda k, t: (t, k)) if na > 1 else (lambda k, t: (t, 0))),
                  pl.BlockSpec((tm, b_blk), (lambda k, t: (t, k)) if nbl > 1 else (lambda k, t: (t, 0)))],
        out_specs=out_spec, out_shape=out_shape,
        scratch_shapes=[pltpu.VMEM((a_blk, b_blk), F32)],
        compiler_params=_params("parallel", "arbitrary"),
    )(a, b)


def _adamw(w, gsrc, m, v, name, transposed=False):
    summed = gsrc.ndim == w.ndim + 1
    rows, cols = w.shape
    assert gsrc.shape[-2:] == ((cols, rows) if transposed else (rows, cols)) and (summed or not transposed)
    tr = rows if rows <= 256 else 256
    assert rows % tr == 0
    c1 = 1.0 - ADAM_B1 ** ADAM_STEP
    c2 = 1.0 - ADAM_B2 ** ADAM_STEP

    def body(w_ref, g_ref, m_ref, v_ref, go_ref, d_ref, mo_ref, vo_ref):
        if summed:
            g = g_ref[0].astype(F32)
            for i in range(1, N_DEV):
                g = g + g_ref[i].astype(F32)
            if transposed:
                g = g.T
        else:
            g = g_ref[...]
        mn = ADAM_B1 * m_ref[...] + (1.0 - ADAM_B1) * g
        vn = ADAM_B2 * v_ref[...] + (1.0 - ADAM_B2) * jnp.square(g)
        go_ref[...] = g
        mo_ref[...] = mn
        vo_ref[...] = vn
        d_ref[...] = -ADAM_LR * ((mn / c1) / (jnp.sqrt(vn / c2) + ADAM_EPS) + ADAM_WD * w_ref[...])

    blk = pl.BlockSpec((tr, cols), lambda i: (i, 0))
    if transposed:
        gblk = pl.BlockSpec((N_DEV, cols, tr), lambda i: (0, 0, i))
    else:
        gblk = pl.BlockSpec((N_DEV, tr, cols), lambda i: (0, i, 0)) if summed else blk
    return pl.pallas_call(
        body, name=name, grid=(rows // tr,),
        in_specs=[blk, gblk, blk, blk], out_specs=[blk] * 4,
        out_shape=[jax.ShapeDtypeStruct(w.shape, F32)] * 4,
        compiler_params=_params("parallel"),
    )(w, gsrc, m, v)


def _sum_slots(g, name):
    _, rows, cols = g.shape

    def body(g_ref, o_ref):
        acc = g_ref[0]
        for i in range(1, N_DEV):
            acc = acc + g_ref[i]
        o_ref[...] = acc

    return pl.pallas_call(body, name=name, out_shape=jax.ShapeDtypeStruct((rows, cols), F32),
                          compiler_params=_params())(g)


def kernel(x, mem, norm_mix_g, w_in, conv_w, conv_b, conv_ln_g, conv_ln_b, w_out, norm_x_g, norm_mem_g, w_xq, w_xk, w_xv, w_xo, norm_mlp_g, w_up, w_down, norm_final_g, loss_target, m_norm_mix_g, m_w_in, m_conv_w, m_conv_b, m_conv_ln_g, m_conv_ln_b, m_w_out, m_norm_x_g, m_norm_mem_g, m_w_xq, m_w_xk, m_w_xv, m_w_xo, m_norm_mlp_g, m_w_up, m_w_down, m_norm_final_g, v_norm_mix_g, v_w_in, v_conv_w, v_conv_b, v_conv_ln_g, v_conv_ln_b, v_w_out, v_norm_x_g, v_norm_mem_g, v_w_xq, v_w_xk, v_w_xv, v_w_xo, v_norm_mlp_g, v_w_up, v_w_down, v_norm_final_g):
    x2, mem2, tgt = x[0], mem[0], loss_target[0]
    s, d = x2.shape
    gf = norm_final_g[None, :]

    cw_local = jnp.pad(conv_w[0], ((0, 1), (0, LANES - conv_w.shape[2])))
    win_g, cw_g = _exchange_call([_Exchange([w_in[0].astype(BF16), cw_local], gather=True)], "gather_w_in")
    w_in_f = jnp.transpose(win_g, (1, 0, 2)).reshape(d, -1)
    cw_f = jnp.transpose(cw_g[:, :, :conv_w.shape[2]], (1, 0, 2)).reshape(32, D_CONV)
    def stacked(*ws):
        return jnp.concatenate([w[0].astype(BF16) for w in ws], axis=0)

    def unstacked(g, *ws):
        offs = [0]
        for w in ws:
            offs.append(offs[-1] + w.shape[1])
        return [g[:, a:b, :].reshape(N_DEV * (b - a), d) for a, b in zip(offs, offs[1:])]

    late = [_Exchange([stacked(w_out, w_xq)], gather=True), _Exchange([w_up[0].astype(BF16)], gather=True),
            _Exchange([w_down[0].astype(BF16)], gather=True)]

    rot = _rotary_tables(s)
    band = _band_bias()
    xn, qkv, ag, *qkv_strided = _fwd_in(x2, norm_mix_g, w_in_f, rot)
    qkv3 = [qkv[None]] + qkv_strided
    outs, lses, gathered = [], [], []
    for dil, q3, side in zip(DILATIONS, qkv3, late):
        (o3, l3), got = _swa_fwd(q3, band, f"swa_fwd_d{dil}", [side])
        outs.append(o3)
        lses.append(l3)
        gathered += got
    rows_g, wup_g, wdown_g = gathered
    w_out_f, w_xq_f = unstacked(rows_g, w_out, w_xq)
    w_down_f = wdown_g.reshape(-1, d)
    w_up_t = jnp.swapaxes(wup_g, 1, 2).reshape(-1, d)
    (c1, conv_out), (rows_g,) = _fwd_conv(ag, cw_f, conv_b, conv_ln_g, conv_ln_b,
                                          [_Exchange([stacked(w_xk, w_xv, w_xo)], gather=True)])
    w_xk_f, w_xv_f, w_xo_f = unstacked(rows_g, w_xk, w_xv, w_xo)
    h1, cat, ltot, *lt_strided = _fwd_mix_out(outs, lses, conv_out, x2, w_out_f)
    mn, xk, xv = _fwd_mem(mem2, norm_mem_g, w_xk_f, w_xv_f)
    h2, hn2, xq, xo = _fwd_xattn(h1, norm_x_g, w_xq_f, xk, xv, w_xo_f)
    hn3, act, dh3, dh3b, loss_part, g_final = _fwd_mlp_loss(h2, norm_mlp_g, wup_g, w_down_f, gf, tgt)

    def scatter(*grads):
        return _Exchange([g.reshape(N_DEV, -1, g.shape[-1]) for g in grads], gather=False)

    f_blk = w_up.shape[2]
    du, dh2, dh2b, g_mlp = _bwd_mlp(dh3, dh3b, act, w_up_t, w_down_f, h2, norm_mlp_g)
    gw_up = _wgrad(hn3, du, "wgrad_up", b_blk=4 * f_blk, stack=f_blk)
    gw_down = _wgrad(dh3b, act, "wgrad_down", b_blk=4 * f_blk, stack=f_blk)
    (dh1, dh1b, dxq, dxk, dxv, g_x), (r_up,) = _bwd_xattn(
        dh2, dh2b, h1, norm_x_g, xq, xk, xv, w_xq_f, w_xo_f, [scatter(gw_up)])
    gw_xq = _wgrad(hn2, dxq, "wgrad_xq")
    gw_xo = _wgrad(xo, dh2b, "wgrad_xo")
    gw_xk, gw_xv, g_mem = _bwd_mem(mem2, norm_mem_g, mn, dxk, dxv, w_xk_f, w_xv_f)
    head = jnp.arange(D_ATT) // HEAD_DIM
    head_ones = (head[:, None] == head[None, :]).astype(BF16)
    dcat, dsum, *strided = _bwd_mix_out(dh1b, w_out_f, cat, head_ones)
    n_str = len(DILATIONS) - 1
    do3, lt3, ds3 = [dcat[None]] + strided[:n_str], [ltot[None]] + lt_strided, [dsum[None]] + strided[n_str:]
    gw_out = _wgrad(cat, dh1b, "wgrad_out")
    (dag, g_cw, g_cb, g_lg, g_lb), (r_down,) = _bwd_conv(dcat, c1, ag, cw_f, conv_ln_g, conv_ln_b, [scatter(gw_down)])
    hosted = [[scatter(gw_out, gw_xq)], [scatter(gw_xk, gw_xv)], [scatter(gw_xo)]]
    dqs, dkvs, landed = [], [], []
    for i, dil in enumerate(DILATIONS):
        (dq3, dkv3), got = _swa_bwd(qkv3[i], do3[i], lt3[i], ds3[i], band, f"swa_bwd_d{dil}", hosted[i])
        dqs.append(dq3)
        dkvs.append(dkv3)
        landed += got
    r_out, r_xq, r_xk, r_xv, r_xo = landed
    grad_x, dy, g_mix = _bwd_in(dqs, dkvs, dag, w_in_f, x2, norm_mix_g, dh1, rot)
    gw_in = _wgrad(xn, dy, "wgrad_in", b_blk=dy.shape[1] // 2)

    def widen(t):
        return jnp.pad(t, ((0, 0), (0, d - t.shape[1])))

    n_in = w_in.shape[2]
    small = jnp.concatenate([g_mix, g_x, g_mem, g_mlp, g_final, widen(g_cb), widen(g_lg), widen(g_lb),
                             g_cw.reshape(16, d), widen(loss_part), jnp.zeros((7, d), F32)], axis=0)
    r_in, small_g = _exchange_call(
        [_Exchange([jnp.transpose(gw_in.reshape(d, N_DEV, n_in), (1, 0, 2))], gather=False),
         _Exchange([small], gather=True)], "scatter_w_in_gather_small")
    small_sum = _sum_slots(small_g, "sum_small_grads")
    loss = small_sum[24, 0]

    res = {}

    def step(name, w, gsrc, m, v, transposed=False):
        shape = w.shape
        w2, m2, v2 = (t.reshape(-1, shape[-1]) for t in (w, m, v))
        res[name] = [t.reshape(shape) for t in _adamw(w2, gsrc, m2, v2, "adamw_" + name, transposed)]

    step("w_in", w_in, r_in, m_w_in, v_w_in)
    step("w_up", w_up, r_up, m_w_up, v_w_up)
    step("w_out", w_out, r_out, m_w_out, v_w_out)
    step("w_xq", w_xq, r_xq, m_w_xq, v_w_xq)
    step("w_xk", w_xk, r_xk, m_w_xk, v_w_xk)
    step("w_xv", w_xv, r_xv, m_w_xv, v_w_xv)
    step("w_xo", w_xo, r_xo, m_w_xo, v_w_xo)
    step("w_down", w_down, r_down, m_w_down, v_w_down, transposed=True)

    me = _dev_index((lax.axis_index("x"), lax.axis_index("y"), lax.axis_index("c")))
    n_cw = conv_w.shape[2]
    g_cw_full = small_sum[8:24].reshape(32, D_CONV)[:CONV_WIDTH]
    g_cw_mine = lax.dynamic_slice_in_dim(g_cw_full, me * n_cw, n_cw, axis=1)
    step("conv_w", conv_w, g_cw_mine, m_conv_w, v_conv_w)

    vec_names = ["norm_mix_g", "norm_x_g", "norm_mem_g", "norm_mlp_g", "norm_final_g", "conv_b", "conv_ln_g", "conv_ln_b"]
    vec_w = [norm_mix_g, norm_x_g, norm_mem_g, norm_mlp_g, gf, conv_b, conv_ln_g, conv_ln_b]
    vec_m = [m_norm_mix_g, m_norm_x_g, m_norm_mem_g, m_norm_mlp_g, m_norm_final_g[None, :], m_conv_b, m_conv_ln_g, m_conv_ln_b]
    vec_v = [v_norm_mix_g, v_norm_x_g, v_norm_mem_g, v_norm_mlp_g, v_norm_final_g[None, :], v_conv_b, v_conv_ln_g, v_conv_ln_b]

    def pack(ts):
        return jnp.concatenate([widen(t) for t in ts], axis=0)

    packed = _adamw(pack(vec_w), small_sum[0:8], pack(vec_m), pack(vec_v), "adamw_vectors")
    for i, name in enumerate(vec_names):
        width = vec_w[i].shape[1]
        shape = (width,) if name == "norm_final_g" else (1, width)
        res[name] = [t[i, :width].reshape(shape) for t in packed]

    order = ["norm_mix_g", "w_in", "conv_w", "conv_b", "conv_ln_g", "conv_ln_b", "w_out", "norm_x_g", "norm_mem_g",
             "w_xq", "w_xk", "w_xv", "w_xo", "norm_mlp_g", "w_up", "w_down", "norm_final_g"]
    out = [loss, grad_x[None]]
    for kind in range(4):
        out += [res[name][kind] for name in order]
    return tuple(out)
```

```python
import jax
import jax.numpy as jnp
from jax import lax
from jax.experimental import pallas as pl
from jax.experimental.pallas import tpu as pltpu

F32 = jnp.float32
BF16 = jnp.bfloat16

N_DEV = 8
EPS = 1e-6
NEG_INF = -1e30
ATT_HEADS = 8
HEAD_DIM = 64
D_ATT = ATT_HEADS * HEAD_DIM
D_CONV = 512
DILATIONS = (1, 4, 16)
HALF = 64
ROPE_THETA = 500000.0
ROT_DIM = HEAD_DIM // 4
CONV_WIDTH = 31
CONV_PAD = (CONV_WIDTH - 1) // 2
XATT_HEADS = 4
ADAM_LR = 0.001
ADAM_B1 = 0.9
ADAM_B2 = 0.999
ADAM_EPS = 1e-08
ADAM_WD = 0.01
ADAM_STEP = 10

LANES = 128
SUBLANES = 8
BF16_ROWS = 16
BQ = 128
WIN = BQ + 2 * HALF
MLP_SHARDS = 4
CONV_ROWS = 32
VMEM_LIMIT = 56 * 1024 * 1024
MESH = pl.DeviceIdType.MESH
ANY = pl.BlockSpec(memory_space=pl.ANY)

_NT = (((1,), (1,)), ((), ()))
_TN = (((0,), (0,)), ((), ()))


def _dot(a, b):
    return jnp.dot(a, b, preferred_element_type=F32)


def _dot_nt(a, b):
    return lax.dot_general(a, b, _NT, preferred_element_type=F32)


def _dot_tn(a, b):
    return lax.dot_general(a, b, _TN, preferred_element_type=F32)


def _params(*sem):
    return pltpu.CompilerParams(dimension_semantics=sem or None, vmem_limit_bytes=VMEM_LIMIT)


def _sigmoid(v):
    return 1.0 / (1.0 + jnp.exp(-v))


def _mean(v):
    return jnp.mean(v, axis=-1, keepdims=True)


def _rms_fwd(h, g):
    r = lax.rsqrt(_mean(h * h) + EPS)
    return h * r * g, r


def _rms_bwd(h, g, d_out):
    r = lax.rsqrt(_mean(h * h) + EPS)
    hn = h * r
    gd = d_out * g
    return r * (gd - hn * _mean(gd * hn)), d_out * hn


def _row_tile(s):
    return min(512, s)


def _full(shape):
    return pl.BlockSpec(shape, lambda *_: (0,) * len(shape))


def _mesh_pos():
    return lax.axis_index("x"), lax.axis_index("y"), lax.axis_index("c")


def _dev_index(p):
    return 4 * p[0] + 2 * p[1] + p[2]


class _Exchange:
    def __init__(self, arrays, gather):
        self.arrays, self.gather, self.n = list(arrays), gather, len(arrays)

    def out_shapes(self):
        return [jax.ShapeDtypeStruct(((N_DEV,) + a.shape) if self.gather else a.shape, a.dtype)
                for a in self.arrays]

    def sem_shapes(self):
        return [pltpu.SemaphoreType.DMA((7 * self.n,)), pltpu.SemaphoreType.DMA((7 * self.n,)),
                pltpu.SemaphoreType.DMA((self.n,))]

    def phases(self, x_refs, o_refs, send_sems, recv_sems, local_sems):
        n = self.n
        x, y, c = _mesh_pos()
        me, sibling = (x, y, c), (x, y, 1 - c)

        if self.gather:
            chips = [(1 - x, y), (x, 1 - y), (1 - x, 1 - y)]

            def copy(a, k, block, to, src=None):
                slot = o_refs[a].at[_dev_index(block)]
                return pltpu.make_async_remote_copy(
                    src_ref=slot if src is None else src, dst_ref=slot,
                    send_sem=send_sems.at[7 * a + k], recv_sem=recv_sems.at[7 * a + k],
                    device_id=to, device_id_type=MESH)

            def mine(a):
                return pltpu.make_async_copy(x_refs[a], o_refs[a].at[_dev_index(me)], local_sems.at[a])

            def first(a):
                return [copy(a, 0, me, sibling, src=x_refs[a])] + [
                    copy(a, 1 + j, me, (*chip, c), src=x_refs[a]) for j, chip in enumerate(chips)]

            def relayed(a, j):
                return copy(a, 4 + j, (*chips[j], c), sibling)

            def start():
                for a in range(n):
                    mine(a).start()
                    for cp in first(a):
                        cp.start()

            def relay():
                for j, chip in enumerate(chips):
                    for a in range(n):
                        copy(a, 1 + j, (*chip, c), me).wait_recv()
                        relayed(a, j).start()

            def finish():
                for a in range(n):
                    copy(a, 0, sibling, me).wait_recv()
                    for j, chip in enumerate(chips):
                        copy(a, 4 + j, (*chip, 1 - c), me).wait_recv()
                    for cp in first(a) + [relayed(a, j) for j in range(3)]:
                        cp.wait_send()
                    mine(a).wait()

            return start, relay, finish

        flips = [(dx, dy, dc) for dx in (0, 1) for dy in (0, 1) for dc in (0, 1)][1:]

        def peer(k):
            return tuple(1 - v if fl else v for v, fl in zip(me, flips[k]))

        def send(a, k):
            return pltpu.make_async_remote_copy(
                src_ref=x_refs[a].at[_dev_index(peer(k))], dst_ref=o_refs[a].at[_dev_index(me)],
                send_sem=send_sems.at[7 * a + k], recv_sem=recv_sems.at[7 * a + k],
                device_id=peer(k), device_id_type=MESH)

        def landed(a, k):
            slot = o_refs[a].at[_dev_index(peer(k))]
            return pltpu.make_async_remote_copy(
                src_ref=slot, dst_ref=slot, send_sem=send_sems.at[7 * a + k], recv_sem=recv_sems.at[7 * a + k],
                device_id=peer(k), device_id_type=MESH)

        def own(a):
            return pltpu.make_async_copy(x_refs[a].at[_dev_index(me)], o_refs[a].at[_dev_index(me)],
                                         local_sems.at[a])

        def start():
            for a in range(n):
                own(a).start()
            for k in range(7):
                for a in range(n):
                    send(a, k).start()

        def finish():
            for k in range(7):
                for a in range(n):
                    landed(a, k).wait_recv()
            for k in range(7):
                for a in range(n):
                    send(a, k).wait_send()
            for a in range(n):
                own(a).wait()

        return start, (lambda: None), finish


def _hosted_call(body, sides, *, name, grid, in_specs, out_specs, out_shape, scratch_shapes, args):
    n_in, n_out, ns = len(in_specs), len(out_specs), sum(s.n for s in sides)
    steps = 1
    for g in grid:
        steps *= g

    def wrapped(*refs):
        ins, s_ins = refs[:n_in], refs[n_in:n_in + ns]
        outs = refs[n_in + ns:n_in + ns + n_out]
        s_outs = refs[n_in + ns + n_out:n_in + 2 * ns + n_out]
        rest = refs[n_in + 2 * ns + n_out:]
        scratch, sems = rest[:len(rest) - 3 * len(sides)], rest[len(rest) - 3 * len(sides):]
        phases, off = [], 0
        for i, s in enumerate(sides):
            phases.append(s.phases(s_ins[off:off + s.n], s_outs[off:off + s.n], *sems[3 * i:3 * i + 3]))
            off += s.n
        lin = 0
        for ax, g in enumerate(grid):
            lin = lin * g + pl.program_id(ax)

        if sides:
            @pl.when(lin == 0)
            def _():
                for start, _, _ in phases:
                    start()

        body(*ins, *outs, *scratch)

        if sides:
            @pl.when(lin == min((3 * steps) // 4, steps - 1))
            def _():
                for _, relay, _ in phases:
                    relay()

            @pl.when(lin == steps - 1)
            def _():
                for _, _, finish in phases:
                    finish()

    res = pl.pallas_call(
        wrapped, name=name, grid=grid,
        in_specs=list(in_specs) + [ANY] * ns, out_specs=list(out_specs) + [ANY] * ns,
        out_shape=list(out_shape) + [sh for s in sides for sh in s.out_shapes()],
        scratch_shapes=list(scratch_shapes) + [sh for s in sides for sh in s.sem_shapes()],
        compiler_params=_params(*(("arbitrary",) * len(grid))),
    )(*args, *[a for s in sides for a in s.arrays])
    return res[:n_out], res[n_out:]


def _exchange_call(sides, name):
    ns = sum(s.n for s in sides)

    def body(*refs):
        x_refs, o_refs, sems = refs[:ns], refs[ns:2 * ns], refs[2 * ns:]
        phases, off = [], 0
        for i, s in enumerate(sides):
            phases.append(s.phases(x_refs[off:off + s.n], o_refs[off:off + s.n], *sems[3 * i:3 * i + 3]))
            off += s.n
        for step in range(3):
            for ph in phases:
                ph[step]()

    return pl.pallas_call(
        body, name=name,
        out_shape=[sh for s in sides for sh in s.out_shapes()],
        in_specs=[ANY] * ns, out_specs=[ANY] * ns,
        scratch_shapes=[sh for s in sides for sh in s.sem_shapes()],
    )(*[a for s in sides for a in s.arrays])


def _rotary_tables(s):
    half = ROT_DIM // 2
    freqs = ROPE_THETA ** (-jnp.arange(0, ROT_DIM, 2, dtype=F32) / ROT_DIM)
    ang = jnp.arange(s, dtype=F32)[:, None] * freqs[None, :]
    cos, sin = jnp.cos(ang), jnp.sin(ang)
    one = jnp.ones((s, HEAD_DIM - ROT_DIM), F32)
    zero = jnp.zeros((s, HEAD_DIM - ROT_DIM), F32)
    zh = jnp.zeros((s, half), F32)
    c64 = jnp.concatenate([cos, cos, one], axis=1)
    a64 = jnp.concatenate([-sin, zh, zero], axis=1)
    b64 = jnp.concatenate([zh, sin, zero], axis=1)
    return tuple(jnp.tile(t, (1, LANES // HEAD_DIM)) for t in (c64, a64, b64))


def _strided_spec(dil, tm, width):
    return pl.BlockSpec((dil, tm // dil, width), lambda i: (0, i, 0))


def _strided_shape(dil, s, width, dtype):
    return jax.ShapeDtypeStruct((dil, s // dil, width), dtype)


def _lane_scratch(tm, width):
    return pltpu.VMEM((width // LANES, tm, LANES), F32)


def _store_blocks(buf, v):
    for cb in range(buf.shape[0]):
        buf[cb] = v[:, cb * LANES:(cb + 1) * LANES]


def _load_blocks(buf):
    return jnp.concatenate([buf[cb] for cb in range(buf.shape[0])], axis=1)


def _write_strided(buf, dst_ref):
    dil, rows, _ = dst_ref.shape
    for r in range(dil):
        for cb in range(buf.shape[0]):
            dst_ref[r, :, cb * LANES:(cb + 1) * LANES] = buf[cb, pl.ds(r, rows, stride=dil), :].astype(dst_ref.dtype)


def _read_strided(src_ref, buf, add=False):
    dil, rows, _ = src_ref.shape
    for r in range(dil):
        for cb in range(buf.shape[0]):
            v = src_ref[r, :, cb * LANES:(cb + 1) * LANES].astype(F32)
            if add:
                v = v + buf[cb, pl.ds(r, rows, stride=dil), :]
            buf[cb, pl.ds(r, rows, stride=dil), :] = v


def _fwd_in(x, g, w_in, rot):
    s, d = x.shape
    n = w_in.shape[1]
    tm = _row_tile(s)
    dils = DILATIONS[1:]

    def body(x_ref, g_ref, w_ref, c_ref, a_ref, b_ref, xn_ref, qkv_ref, ag_ref, *rest):
        strided, ybuf = rest[:len(dils)], rest[len(dils)]
        xn = _rms_fwd(x_ref[...], g_ref[...])[0].astype(BF16)
        xn_ref[...] = xn
        y = _dot(xn, w_ref[...])
        reps = (1, D_ATT // LANES)
        cc, aa, bb = jnp.tile(c_ref[...], reps), jnp.tile(a_ref[...], reps), jnp.tile(b_ref[...], reps)
        parts = []
        for blk in range(2):
            t = y[:, blk * D_ATT:(blk + 1) * D_ATT]
            parts.append(t * cc + pltpu.roll(t, D_ATT - ROT_DIM // 2, 1) * aa + pltpu.roll(t, ROT_DIM // 2, 1) * bb)
        qkv = jnp.concatenate(parts + [y[:, 2 * D_ATT:3 * D_ATT]], axis=1)
        qkv_ref[...] = qkv.astype(BF16)
        _store_blocks(ybuf, qkv)
        for ref in strided:
            _write_strided(ybuf, ref)
        ag_ref[...] = y[:, 3 * D_ATT:].astype(BF16)

    row = lambda w: pl.BlockSpec((tm, w), lambda i: (i, 0))
    return pl.pallas_call(
        body, name="fwd_in", grid=(s // tm,),
        in_specs=[row(d), _full((1, d)), _full((d, n)), row(LANES), row(LANES), row(LANES)],
        out_specs=[row(d), row(3 * D_ATT), row(2 * D_CONV)] + [_strided_spec(dil, tm, 3 * D_ATT) for dil in dils],
        out_shape=[jax.ShapeDtypeStruct((s, d), BF16), jax.ShapeDtypeStruct((s, 3 * D_ATT), BF16),
                   jax.ShapeDtypeStruct((s, 2 * D_CONV), BF16)]
        + [_strided_shape(dil, s, 3 * D_ATT, BF16) for dil in dils],
        scratch_shapes=[_lane_scratch(tm, 3 * D_ATT)],
        compiler_params=_params("parallel"),
    )(x, g, w_in, *rot)


def _residues_per_step(dil):
    return 2 if dil % 2 == 0 else 1


def _win_in_specs(length, col, width, group):
    per = BQ // HALF
    last = length // HALF - 1
    return [
        pl.BlockSpec((group, HALF, width), lambda r, j: (r, jnp.maximum(j * per - 1, 0), col)),
        pl.BlockSpec((group, BQ, width), lambda r, j: (r, j, col)),
        pl.BlockSpec((group, HALF, width), lambda r, j: (r, jnp.minimum(j * per + per, last), col)),
    ]


def _fill_window(win, prev_ref, main_ref, next_ref):
    for g in range(win.shape[0]):
        win[g, 0:HALF] = prev_ref[g]
        win[g, HALF:HALF + BQ] = main_ref[g]
        win[g, HALF + BQ:] = next_ref[g]


def _band_bias():
    blk = jnp.arange(2 * BQ)[:, None] & (BQ - 1)
    win = jnp.arange(WIN)[None, :]
    return jnp.where(jnp.abs(win - HALF - blk) <= HALF, 0.0, NEG_INF).astype(F32)


def _window_bias(band_ref, j, length):
    pos = j * BQ - HALF + lax.broadcasted_iota(jnp.int32, (1, WIN), 1)
    return band_ref[...] + jnp.where((pos >= 0) & (pos < length), 0.0, NEG_INF)


def _first_head():
    return lax.broadcasted_iota(jnp.int32, (1, LANES), 1) < HEAD_DIM


def _stack_heads(v, first):
    zero = jnp.zeros((), v.dtype)
    return jnp.concatenate([jnp.where(first, v, zero), jnp.where(first, zero, v)], axis=0)


def _unstack_heads(v, first):
    rows = v.shape[0] // 2
    return jnp.where(first, v[:rows], v[rows:])


def _stack_cols(v, first):
    other = pltpu.roll(v, HEAD_DIM, 1)
    stacked = jnp.concatenate([jnp.where(first, v, other), jnp.where(first, other, v)], axis=0)
    return jnp.tile(stacked, (1, WIN // LANES))


def _swa_fwd(qkv3, band, name, sides=()):
    dil, length, _ = qkv3.shape
    scale = HEAD_DIM ** -0.5
    group = _residues_per_step(dil)

    def body(q_ref, kp, km, kn, vp, vm, vn, band_ref, o_ref, lse_ref, kwin, vwin):
        j = pl.program_id(1)
        _fill_window(kwin, kp, km, kn)
        _fill_window(vwin, vp, vm, vn)
        bias = _window_bias(band_ref, j, length)
        first = _first_head()
        pairs = [(g, slice(pr * LANES, (pr + 1) * LANES)) for g in range(group) for pr in range(D_ATT // LANES)]
        scores = [_dot_nt(_stack_heads(q_ref[g, :, cols] * scale, first), kwin[g, :, cols]) + bias for g, cols in pairs]
        stats, probs = [], []
        for sc in scores:
            m = jnp.max(sc, axis=-1, keepdims=True)
            p = jnp.exp(sc - m)
            stats.append((m, jnp.sum(p, axis=-1, keepdims=True)))
            probs.append(p.astype(BF16))
        for (g, cols), p, (m, den) in zip(pairs, probs, stats):
            pv = _dot(p, vwin[g, :, cols]) * (1.0 / den)
            o_ref[g, :, cols] = _unstack_heads(pv, first).astype(BF16)
            lse_ref[g, :, cols] = _unstack_heads(jnp.broadcast_to(m + jnp.log(den), (2 * BQ, LANES)), first)

    blk = pl.BlockSpec((group, BQ, D_ATT), lambda r, j: (r, j, 0))
    return _hosted_call(
        body, sides, name=name, grid=(dil // group, length // BQ),
        in_specs=[blk] + _win_in_specs(length, 1, D_ATT, group) + _win_in_specs(length, 2, D_ATT, group)
        + [_full((2 * BQ, WIN))],
        out_specs=[blk, blk],
        out_shape=[jax.ShapeDtypeStruct((dil, length, D_ATT), BF16),
                   jax.ShapeDtypeStruct((dil, length, D_ATT), F32)],
        scratch_shapes=[pltpu.VMEM((group, WIN, D_ATT), BF16)] * 2,
        args=[qkv3] * 7 + [band])


def _glu(v):
    return v[:, :D_CONV].astype(F32) * _sigmoid(v[:, D_CONV:].astype(F32))


def _halo_specs(tm, width, col=0):
    per = tm // BF16_ROWS
    return lambda nblk: [
        pl.BlockSpec((BF16_ROWS, width), lambda i: (jnp.maximum(i * per - 1, 0), col)),
        pl.BlockSpec((tm, width), lambda i: (i, col)),
        pl.BlockSpec((BF16_ROWS, width), lambda i: (jnp.minimum(i * per + per, nblk - 1), col)),
    ]


def _fill_halo(buf, i, ntiles, tm, prev, main, nxt):
    buf[0:BF16_ROWS] = jnp.where(i == 0, 0.0, prev)
    buf[BF16_ROWS:BF16_ROWS + tm] = main
    buf[BF16_ROWS + tm:] = jnp.where(i == ntiles - 1, 0.0, nxt)


def _halo_scratch(tm):
    return [pltpu.VMEM((tm + 2 * BF16_ROWS, D_CONV), F32),
            pltpu.VMEM((SUBLANES - 1, tm + 2 * BF16_ROWS - SUBLANES, D_CONV), F32)]


def _shift_copies(buf, shifted, tm):
    rows = tm + 2 * BF16_ROWS - SUBLANES
    for b in range(1, SUBLANES):
        shifted[b - 1] = buf[pl.ds(b, rows), :]


def _tap(buf, shifted, off, rows, base=0):
    a, b = divmod(off, SUBLANES)
    start = base + SUBLANES * a
    if not isinstance(start, int):
        start = pl.multiple_of(start, SUBLANES)
    if b == 0:
        return buf[pl.ds(start, rows), :]
    return shifted[b - 1, pl.ds(start, rows), :]


def _fwd_conv(ag, cw, cb, lg, lb, sides=()):
    s = ag.shape[0]
    tm = _row_tile(s)
    nt = s // tm

    def body(agp, agm, agn, cw_ref, cb_ref, lg_ref, lb_ref, c1_ref, co_ref, ubuf, ush):
        i = pl.program_id(0)
        _fill_halo(ubuf, i, nt, tm, _glu(agp[...]), _glu(agm[...]), _glu(agn[...]))
        _shift_copies(ubuf, ush, tm)
        w, cb = cw_ref[...], cb_ref[...]

        def chunk(c, carry):
            base = pl.multiple_of(c * CONV_ROWS, CONV_ROWS)
            acc = jnp.zeros((CONV_ROWS, D_CONV), F32)
            for k in range(CONV_WIDTH):
                acc = acc + _tap(ubuf, ush, k + 1, CONV_ROWS, base) * w[k:k + 1, :]
            c1_ref[pl.ds(base, CONV_ROWS), :] = acc + cb
            return carry

        lax.fori_loop(0, tm // CONV_ROWS, chunk, 0)
        c1 = c1_ref[...]
        xc = c1 - _mean(c1)
        ln = xc * lax.rsqrt(_mean(xc * xc) + EPS) * lg_ref[...] + lb_ref[...]
        co_ref[...] = (ln * _sigmoid(ln)).astype(BF16)

    vec = _full((1, D_CONV))
    return _hosted_call(
        body, sides, name="fwd_conv", grid=(nt,),
        in_specs=_halo_specs(tm, 2 * D_CONV)(s // BF16_ROWS) + [_full((32, D_CONV)), vec, vec, vec],
        out_specs=[pl.BlockSpec((tm, D_CONV), lambda i: (i, 0))] * 2,
        out_shape=[jax.ShapeDtypeStruct((s, D_CONV), F32), jax.ShapeDtypeStruct((s, D_CONV), BF16)],
        scratch_shapes=_halo_scratch(tm),
        args=[ag, ag, ag, cw, cb, lg, lb])


def _fwd_mix_out(outs, lses, conv_out, x, w_out):
    s, d = x.shape
    tm = _row_tile(s)
    dils = DILATIONS[1:]
    nd = len(dils)

    def body(o1, *rest):
        o_str, l1, l_str = rest[:nd], rest[nd], rest[nd + 1:2 * nd + 1]
        co, x_ref, w_ref, h_ref, cat_ref, lt_ref = rest[2 * nd + 1:2 * nd + 7]
        lt_str = rest[2 * nd + 7:3 * nd + 7]
        obufs, lbufs, ltbuf = rest[3 * nd + 7:4 * nd + 7], rest[4 * nd + 7:5 * nd + 7], rest[5 * nd + 7]
        for ref, buf in zip(o_str + l_str, obufs + lbufs):
            _read_strided(ref, buf)
        lse = [l1[0]] + [_load_blocks(buf) for buf in lbufs]
        out = [o1[0].astype(F32)] + [_load_blocks(buf) for buf in obufs]
        m = lse[0]
        for v in lse[1:]:
            m = jnp.maximum(m, v)
        e = [jnp.exp(v - m) for v in lse]
        den = sum(e[1:], e[0])
        att = (sum((ev * ov for ev, ov in zip(e[1:], out[1:])), e[0] * out[0]) / den).astype(BF16)
        lt = m + jnp.log(den)
        lt_ref[...] = lt
        _store_blocks(ltbuf, lt)
        for ref in lt_str:
            _write_strided(ltbuf, ref)
        cat_ref[:, :D_ATT] = att
        cat_ref[:, D_ATT:] = co[...]
        h_ref[...] = x_ref[...] + _dot(att, w_ref[:D_ATT, :]) + _dot(co[...], w_ref[D_ATT:, :])

    row = lambda w: pl.BlockSpec((tm, w), lambda i: (i, 0))
    nat = pl.BlockSpec((1, tm, D_ATT), lambda i: (0, i, 0))
    strided = [_strided_spec(dil, tm, D_ATT) for dil in dils]
    return pl.pallas_call(
        body, name="fwd_mix_out", grid=(s // tm,),
        in_specs=[nat] + strided + [nat] + strided + [row(D_ATT), row(d), _full((d, d))],
        out_specs=[row(d), row(d), row(D_ATT)] + strided,
        out_shape=[jax.ShapeDtypeStruct((s, d), F32), jax.ShapeDtypeStruct((s, d), BF16),
                   jax.ShapeDtypeStruct((s, D_ATT), F32)] + [_strided_shape(dil, s, D_ATT, F32) for dil in dils],
        scratch_shapes=[_lane_scratch(tm, D_ATT)] * (2 * nd + 1),
        compiler_params=_params("parallel"),
    )(*outs, *lses, conv_out, x, w_out)


def _fwd_mem(mem, g, wk, wv):
    m, d = mem.shape

    def body(mem_ref, g_ref, wk_ref, wv_ref, mn_ref, xk_ref, xv_ref):
        mn = _rms_fwd(mem_ref[...], g_ref[...])[0].astype(BF16)
        mn_ref[...] = mn
        xk_ref[...] = _dot(mn, wk_ref[...]).astype(BF16)
        xv_ref[...] = _dot(mn, wv_ref[...]).astype(BF16)

    return pl.pallas_call(
        body, name="fwd_mem",
        out_shape=[jax.ShapeDtypeStruct((m, d), BF16)] * 3,
        compiler_params=_params(),
    )(mem, g, wk, wv)


def _softmax(sc):
    p = jnp.exp(sc - jnp.max(sc, axis=-1, keepdims=True))
    return p / jnp.sum(p, axis=-1, keepdims=True)


def _fwd_xattn(h1, g, wq, xk, xv, wo):
    s, d = h1.shape
    m = xk.shape[0]
    tm = _row_tile(s)
    hd = d // XATT_HEADS

    def body(h_ref, g_ref, wq_ref, xk_ref, xv_ref, wo_ref, h2_ref, hn_ref, xq_ref, xo_ref):
        h = h_ref[...]
        hn = _rms_fwd(h, g_ref[...])[0].astype(BF16)
        hn_ref[...] = hn
        xq = _dot(hn, wq_ref[...]).astype(BF16)
        xq_ref[...] = xq
        heads = [slice(i * hd, (i + 1) * hd) for i in range(XATT_HEADS)]
        scores = [_dot_nt(xq[:, cols], xk_ref[:, cols]) * hd ** -0.5 for cols in heads]
        probs = [_softmax(sc).astype(BF16) for sc in scores]
        for cols, pr in zip(heads, probs):
            xo_ref[:, cols] = _dot(pr, xv_ref[:, cols]).astype(BF16)
        h2_ref[...] = h + _dot(xo_ref[...], wo_ref[...])

    row = pl.BlockSpec((tm, d), lambda i: (i, 0))
    return pl.pallas_call(
        body, name="fwd_xattn", grid=(s // tm,),
        in_specs=[row, _full((1, d)), _full((d, d)), _full((m, d)), _full((m, d)), _full((d, d))],
        out_specs=[row] * 4,
        out_shape=[jax.ShapeDtypeStruct((s, d), F32)] + [jax.ShapeDtypeStruct((s, d), BF16)] * 3,
        compiler_params=_params("parallel"),
    )(h1, g, wq, xk, xv, wo)


def _fwd_mlp_loss(h2, g, w_up, w_down, gf, target):
    s, d = h2.shape
    nsh, _, f = w_up.shape
    fb = MLP_SHARDS * f
    nb = nsh // MLP_SHARDS
    tm = _row_tile(s)

    def body(h_ref, g_ref, wu_ref, wd_ref, gf_ref, t_ref,
             hn_ref, act_ref, dh_ref, dhb_ref, loss_ref, ggf_ref, acc):
        i, k = pl.program_id(0), pl.program_id(1)

        @pl.when(k == 0)
        def _():
            hn_ref[...] = _rms_fwd(h_ref[...], g_ref[...])[0].astype(BF16)
            acc[...] = jnp.zeros_like(acc)

        @pl.when((i == 0) & (k == 0))
        def _():
            loss_ref[...] = jnp.zeros_like(loss_ref)
            ggf_ref[...] = jnp.zeros_like(ggf_ref)

        hn = hn_ref[...]
        ups = [_dot(hn, wu_ref[c]) for c in range(MLP_SHARDS)]
        for c, u in enumerate(ups):
            act_ref[:, c * f:(c + 1) * f] = jnp.square(jnp.maximum(u, 0.0)).astype(BF16)
        acc[...] += _dot(act_ref[...], wd_ref[...])

        @pl.when(k == nb - 1)
        def _():
            h3 = h_ref[...] + acc[...]
            gfv = gf_ref[...]
            y, _ = _rms_fwd(h3, gfv)
            err = y - t_ref[...]
            loss_ref[...] += 0.5 * jnp.sum(_mean(err * err))
            dh3, gg = _rms_bwd(h3, gfv, err * (1.0 / d))
            ggf_ref[...] += jnp.sum(gg, axis=0, keepdims=True)
            dh_ref[...] = dh3
            dhb_ref[...] = dh3.astype(BF16)

    row = pl.BlockSpec((tm, d), lambda i, k: (i, 0))
    return pl.pallas_call(
        body, name="fwd_mlp_loss", grid=(s // tm, nb),
        in_specs=[row, _full((1, d)),
                  pl.BlockSpec((MLP_SHARDS, d, f), lambda i, k: (k, 0, 0)),
                  pl.BlockSpec((fb, d), lambda i, k: (k, 0)),
                  _full((1, d)), row],
        out_specs=[row, pl.BlockSpec((tm, fb), lambda i, k: (i, k)), row, row,
                   _full((1, LANES)), _full((1, d))],
        out_shape=[jax.ShapeDtypeStruct((s, d), BF16), jax.ShapeDtypeStruct((s, nsh * f), BF16),
                   jax.ShapeDtypeStruct((s, d), F32), jax.ShapeDtypeStruct((s, d), BF16),
                   jax.ShapeDtypeStruct((1, LANES), F32), jax.ShapeDtypeStruct((1, d), F32)],
        scratch_shapes=[pltpu.VMEM((tm, d), F32)],
        compiler_params=_params("arbitrary", "arbitrary"),
    )(h2, g, w_up, w_down, gf, target)


def _bwd_mlp(dh3, dh3b, act, w_up_t, w_down, h2, g):
    s, d = h2.shape
    ff = w_down.shape[0]
    fb = MLP_SHARDS * (ff // N_DEV)
    nb = ff // fb
    tm = _row_tile(s)

    def body(dh_ref, dhb_ref, act_ref, wut_ref, wd_ref, h_ref, g_ref,
             du_ref, dh2_ref, dh2b_ref, gg_ref, acc):
        i, k = pl.program_id(0), pl.program_id(1)

        @pl.when(k == 0)
        def _():
            acc[...] = jnp.zeros_like(acc)

        @pl.when((i == 0) & (k == 0))
        def _():
            gg_ref[...] = jnp.zeros_like(gg_ref)

        dhb = dhb_ref[...]
        f = fb // MLP_SHARDS
        shards = [slice(c * f, (c + 1) * f) for c in range(MLP_SHARDS)]
        dacts = [_dot_nt(dhb, wd_ref[cols, :]) for cols in shards]
        for cols, dact in zip(shards, dacts):
            du_ref[:, cols] = (dact * (2.0 * jnp.sqrt(act_ref[:, cols].astype(F32)))).astype(BF16)
        acc[...] += _dot(du_ref[...], wut_ref[...])

        @pl.when(k == nb - 1)
        def _():
            dh, gg = _rms_bwd(h_ref[...], g_ref[...], acc[...])
            gg_ref[...] += jnp.sum(gg, axis=0, keepdims=True)
            dh2 = dh_ref[...] + dh
            dh2_ref[...] = dh2
            dh2b_ref[...] = dh2.astype(BF16)

    row = pl.BlockSpec((tm, d), lambda i, k: (i, 0))
    col = pl.BlockSpec((tm, fb), lambda i, k: (i, k))
    wblk = pl.BlockSpec((fb, d), lambda i, k: (k, 0))
    return pl.pallas_call(
        body, name="bwd_mlp", grid=(s // tm, nb),
        in_specs=[row, row, col, wblk, wblk, row, _full((1, d))],
        out_specs=[col, row, row, _full((1, d))],
        out_shape=[jax.ShapeDtypeStruct((s, ff), BF16), jax.ShapeDtypeStruct((s, d), F32),
                   jax.ShapeDtypeStruct((s, d), BF16), jax.ShapeDtypeStruct((1, d), F32)],
        scratch_shapes=[pltpu.VMEM((tm, d), F32)],
        compiler_params=_params("arbitrary", "arbitrary"),
    )(dh3, dh3b, act, w_up_t, w_down, h2, g)


def _bwd_xattn(dh2, dh2b, h1, g, xq, xk, xv, wq, wo, sides=()):
    s, d = h1.shape
    m = xk.shape[0]
    tm = _row_tile(s)
    hd = d // XATT_HEADS
    scale = hd ** -0.5

    def body(dh_ref, dhb_ref, h_ref, g_ref, xq_ref, xk_ref, xv_ref, wq_ref, wo_ref,
             dh1_ref, dh1b_ref, dxq_ref, dxk_ref, dxv_ref, gg_ref):
        @pl.when(pl.program_id(0) == 0)
        def _():
            dxk_ref[...] = jnp.zeros_like(dxk_ref)
            dxv_ref[...] = jnp.zeros_like(dxv_ref)
            gg_ref[...] = jnp.zeros_like(gg_ref)

        dxo = _dot_nt(dhb_ref[...], wo_ref[...])
        heads = [slice(i * hd, (i + 1) * hd) for i in range(XATT_HEADS)]
        dxos = [dxo[:, cols].astype(BF16) for cols in heads]
        scores = [_dot_nt(xq_ref[:, cols], xk_ref[:, cols]) * scale for cols in heads]
        dprs = [_dot_nt(dxo_h, xv_ref[:, cols]) for dxo_h, cols in zip(dxos, heads)]
        probs, dscs = [], []
        for sc, dpr in zip(scores, dprs):
            pr = _softmax(sc)
            dscs.append((pr * (dpr - jnp.sum(dpr * pr, axis=-1, keepdims=True)) * scale).astype(BF16))
            probs.append(pr.astype(BF16))
        for cols, dsc, pr, dxo_h in zip(heads, dscs, probs, dxos):
            dxq_ref[:, cols] = _dot(dsc, xk_ref[:, cols]).astype(BF16)
            dxk_ref[:, cols] += _dot_tn(dsc, xq_ref[:, cols])
            dxv_ref[:, cols] += _dot_tn(pr, dxo_h)
        dh, gg = _rms_bwd(h_ref[...], g_ref[...], _dot_nt(dxq_ref[...], wq_ref[...]))
        gg_ref[...] += jnp.sum(gg, axis=0, keepdims=True)
        dh1 = dh_ref[...] + dh
        dh1_ref[...] = dh1
        dh1b_ref[...] = dh1.astype(BF16)

    row = pl.BlockSpec((tm, d), lambda i: (i, 0))
    return _hosted_call(
        body, sides, name="bwd_xattn", grid=(s // tm,),
        in_specs=[row, row, row, _full((1, d)), row, _full((m, d)), _full((m, d)), _full((d, d)), _full((d, d))],
        out_specs=[row, row, row, _full((m, d)), _full((m, d)), _full((1, d))],
        out_shape=[jax.ShapeDtypeStruct((s, d), F32), jax.ShapeDtypeStruct((s, d), BF16),
                   jax.ShapeDtypeStruct((s, d), BF16), jax.ShapeDtypeStruct((m, d), F32),
                   jax.ShapeDtypeStruct((m, d), F32), jax.ShapeDtypeStruct((1, d), F32)],
        scratch_shapes=[],
        args=[dh2, dh2b, h1, g, xq, xk, xv, wq, wo])


def _bwd_mem(mem, g, mn, dxk, dxv, wk, wv):
    m, d = mem.shape

    def body(mem_ref, g_ref, mn_ref, dxk_ref, dxv_ref, wk_ref, wv_ref, gk_ref, gv_ref, gg_ref):
        dk, dv = dxk_ref[...].astype(BF16), dxv_ref[...].astype(BF16)
        gk_ref[...] = _dot_tn(mn_ref[...], dk).astype(BF16)
        gv_ref[...] = _dot_tn(mn_ref[...], dv).astype(BF16)
        dmn = _dot_nt(dk, wk_ref[...]) + _dot_nt(dv, wv_ref[...])
        _, gg = _rms_bwd(mem_ref[...], g_ref[...], dmn)
        gg_ref[...] = jnp.sum(gg, axis=0, keepdims=True)

    return pl.pallas_call(
        body, name="bwd_mem",
        out_shape=[jax.ShapeDtypeStruct((d, d), BF16), jax.ShapeDtypeStruct((d, d), BF16),
                   jax.ShapeDtypeStruct((1, d), F32)],
        compiler_params=_params(),
    )(mem, g, mn, dxk, dxv, wk, wv)


def _bwd_mix_out(dh1b, w_out, cat, head_ones):
    s, d = dh1b.shape
    tm = _row_tile(s)
    dils = DILATIONS[1:]
    nd = len(dils)

    def body(dh_ref, w_ref, cat_ref, ones_ref, dcat_ref, dsum_ref, *rest):
        da_str, ds_str, dbuf, sbuf = rest[:nd], rest[nd:2 * nd], rest[2 * nd], rest[2 * nd + 1]
        dcat = _dot_nt(dh_ref[...], w_ref[...])
        dcat_ref[...] = dcat.astype(BF16)
        datt = dcat[:, :D_ATT]
        prod = datt * cat_ref[...].astype(F32)
        hi = prod.astype(BF16)
        lo = (prod - hi.astype(F32)).astype(BF16)
        dsum = _dot(hi, ones_ref[...]) + _dot(lo, ones_ref[...])
        dsum_ref[...] = dsum
        _store_blocks(dbuf, datt)
        _store_blocks(sbuf, dsum)
        for da_ref, ds_ref in zip(da_str, ds_str):
            _write_strided(dbuf, da_ref)
            _write_strided(sbuf, ds_ref)

    row = lambda w: pl.BlockSpec((tm, w), lambda i: (i, 0))
    strided = [_strided_spec(dil, tm, D_ATT) for dil in dils]
    return pl.pallas_call(
        body, name="bwd_mix_out", grid=(s // tm,),
        in_specs=[row(d), _full((d, d)), row(D_ATT), _full((D_ATT, D_ATT))],
        out_specs=[row(d), row(D_ATT)] + strided + strided,
        out_shape=[jax.ShapeDtypeStruct((s, d), BF16), jax.ShapeDtypeStruct((s, D_ATT), F32)]
        + [_strided_shape(dil, s, D_ATT, BF16) for dil in dils]
        + [_strided_shape(dil, s, D_ATT, F32) for dil in dils],
        scratch_shapes=[_lane_scratch(tm, D_ATT)] * 2,
        compiler_params=_params("parallel"),
    )(dh1b, w_out, cat, head_ones)


def _bwd_conv(dcat, c1, ag, cw, lg, lb, sides=()):
    s = ag.shape[0]
    tm = _row_tile(s)
    nt = s // tm

    def body(dp, dm, dn, cp, cm, cn, agp, agm, agn, cw_ref, lg_ref, lb_ref,
             dag_ref, gcw_ref, gcb_ref, glg_ref, glb_ref, ubuf, ush, dbuf, dsh, gacc):
        i = pl.program_id(0)

        @pl.when(i == 0)
        def _():
            gacc[...] = jnp.zeros_like(gacc)
            gcb_ref[...] = jnp.zeros_like(gcb_ref)
            glg_ref[...] = jnp.zeros_like(glg_ref)
            glb_ref[...] = jnp.zeros_like(glb_ref)

        lgv, lbv = lg_ref[...], lb_ref[...]

        def norm_bwd(dco, c1v):
            xc = c1v - _mean(c1v)
            rs = lax.rsqrt(_mean(xc * xc) + EPS)
            z = xc * rs
            ln = z * lgv + lbv
            sg = _sigmoid(ln)
            dln = dco.astype(F32) * (sg * (1.0 + ln * (1.0 - sg)))
            dz = dln * lgv
            return rs * (dz - _mean(dz) - z * _mean(dz * z)), dln, z

        def sublane_sums(v):
            out = v[0:SUBLANES]
            for r in range(SUBLANES, CONV_ROWS, SUBLANES):
                out = out + v[r:r + SUBLANES]
            return out

        dc_m, dln, z = norm_bwd(dm[...], cm[...])
        glg_ref[...] += jnp.sum(dln * z, axis=0, keepdims=True)
        glb_ref[...] += jnp.sum(dln, axis=0, keepdims=True)
        gcb_ref[...] += jnp.sum(dc_m, axis=0, keepdims=True)
        _fill_halo(dbuf, i, nt, tm, norm_bwd(dp[...], cp[...])[0], dc_m, norm_bwd(dn[...], cn[...])[0])
        _fill_halo(ubuf, i, nt, tm, _glu(agp[...]), _glu(agm[...]), _glu(agn[...]))
        _shift_copies(dbuf, dsh, tm)
        _shift_copies(ubuf, ush, tm)

        w = cw_ref[...]

        def chunk(c, carry):
            base = pl.multiple_of(c * CONV_ROWS, CONV_ROWS)
            rows = pl.ds(base, CONV_ROWS)
            dc = dbuf[pl.ds(pl.multiple_of(base + BF16_ROWS, SUBLANES), CONV_ROWS), :]
            du = jnp.zeros((CONV_ROWS, D_CONV), F32)
            for k in range(CONV_WIDTH):
                du = du + _tap(dbuf, dsh, CONV_WIDTH - k, CONV_ROWS, base) * w[k:k + 1, :]
                gacc[k] += sublane_sums(dc * _tap(ubuf, ush, k + 1, CONV_ROWS, base))
            a = agm[rows, :D_CONV].astype(F32)
            sg = _sigmoid(agm[rows, D_CONV:].astype(F32))
            dag_ref[rows, :D_CONV] = (du * sg).astype(BF16)
            dag_ref[rows, D_CONV:] = (du * a * sg * (1.0 - sg)).astype(BF16)
            return carry

        lax.fori_loop(0, tm // CONV_ROWS, chunk, 0)

        @pl.when(i == nt - 1)
        def _():
            tap = lax.broadcasted_iota(jnp.int32, (32, D_CONV), 0)
            gcw = jnp.zeros((32, D_CONV), F32)
            for k in range(CONV_WIDTH):
                gcw = jnp.where(tap == k, jnp.sum(gacc[k], axis=0, keepdims=True), gcw)
            gcw_ref[...] = gcw

    vec = _full((1, D_CONV))
    nblk = s // BF16_ROWS
    return _hosted_call(
        body, sides, name="bwd_conv", grid=(nt,),
        in_specs=_halo_specs(tm, D_CONV, 1)(nblk) + _halo_specs(tm, D_CONV)(nblk) + _halo_specs(tm, 2 * D_CONV)(nblk)
        + [_full((32, D_CONV)), vec, vec],
        out_specs=[pl.BlockSpec((tm, 2 * D_CONV), lambda i: (i, 0)), _full((32, D_CONV)), vec, vec, vec],
        out_shape=[jax.ShapeDtypeStruct((s, 2 * D_CONV), BF16), jax.ShapeDtypeStruct((32, D_CONV), F32)]
        + [jax.ShapeDtypeStruct((1, D_CONV), F32)] * 3,
        scratch_shapes=_halo_scratch(tm) + _halo_scratch(tm) + [pltpu.VMEM((32, SUBLANES, D_CONV), F32)],
        args=[dcat, dcat, dcat, c1, c1, c1, ag, ag, ag, cw, lg, lb])


def _swa_bwd(qkv3, do3, lt3, ds3, band, name, sides=()):
    dil, length, _ = qkv3.shape
    nb = length // BQ
    scale = HEAD_DIM ** -0.5
    group = _residues_per_step(dil)
    assert WIN == 2 * BQ and BQ == 2 * HALF

    def body(q_ref, kp, km, kn, vp, vm, vn, do_ref, l_ref, s_ref, band_ref, dq_ref, dkv_ref, kwin, vwin, pend, keep):
        j = pl.program_id(1)

        @pl.when(j == 0)
        def _():
            pend[...] = jnp.zeros_like(pend)
            keep[...] = jnp.zeros_like(keep)

        @pl.when(j < nb)
        def _():
            _fill_window(kwin, kp, km, kn)
            _fill_window(vwin, vp, vm, vn)
            bias = _window_bias(band_ref, j, length)
            first = _first_head()
            pairs = [(g, slice(pr * LANES, (pr + 1) * LANES)) for g in range(group) for pr in range(D_ATT // LANES)]
            qs = [_stack_heads(q_ref[g, :, cols] * scale, first) for g, cols in pairs]
            dos = [_stack_heads(do_ref[g, :, cols], first) for g, cols in pairs]
            scores = [_dot_nt(q, kwin[g, :, cols]) for q, (g, cols) in zip(qs, pairs)]
            dps = [_dot_nt(do, vwin[g, :, cols]) for do, (g, cols) in zip(dos, pairs)]
            probs, dscs = [], []
            for (g, cols), sc, dp in zip(pairs, scores, dps):
                p = jnp.exp(sc + (bias - _stack_cols(l_ref[g, :, cols], first)))
                dscs.append((p * (dp - _stack_cols(s_ref[g, :, cols], first))).astype(BF16))
                probs.append(p.astype(BF16))
            dqs = [_dot(dsc, kwin[g, :, cols] * scale) for dsc, (g, cols) in zip(dscs, pairs)]
            dks = [_dot_tn(dsc, q) for dsc, q in zip(dscs, qs)]
            dvs = [_dot_tn(p, do) for p, do in zip(probs, dos)]
            for (g, cols), dq, dk, dv in zip(pairs, dqs, dks, dvs):
                dq_ref[g, :, cols] = _unstack_heads(dq, first).astype(BF16)
                for part, at in ((dk, cols), (dv, slice(D_ATT + cols.start, D_ATT + cols.stop))):
                    dkv_ref[g, :HALF, at] = keep[g, :, at].astype(BF16)
                    dkv_ref[g, HALF:, at] = (pend[g, :HALF, at] + part[:HALF]).astype(BF16)
                    keep[g, :, at] = pend[g, HALF:, at] + part[HALF:BQ]
                    pend[g, :, at] = part[BQ:]

        @pl.when(j == nb)
        def _():
            dkv_ref[:, :HALF] = keep[...].astype(BF16)
            dkv_ref[:, HALF:] = pend[:, :HALF].astype(BF16)

    def clamp(idx):
        return lambda r, j: idx(r, jnp.minimum(j, nb - 1))

    main = pl.BlockSpec((group, BQ, D_ATT), clamp(lambda r, j: (r, j, 0)))
    wins = [pl.BlockSpec(sp.block_shape, clamp(sp.index_map))
            for c in (1, 2) for sp in _win_in_specs(length, c, D_ATT, group)]
    return _hosted_call(
        body, sides, name=name, grid=(dil // group, nb + 1),
        in_specs=[main] + wins + [main] * 3 + [_full((2 * BQ, WIN))],
        out_specs=[main, pl.BlockSpec((group, BQ, 2 * D_ATT), lambda r, j: (r, jnp.maximum(j - 1, 0), 0))],
        out_shape=[jax.ShapeDtypeStruct((dil, length, D_ATT), BF16),
                   jax.ShapeDtypeStruct((dil, length, 2 * D_ATT), BF16)],
        scratch_shapes=[pltpu.VMEM((group, WIN, D_ATT), BF16)] * 2
        + [pltpu.VMEM((group, BQ, 2 * D_ATT), F32), pltpu.VMEM((group, HALF, 2 * D_ATT), F32)],
        args=[qkv3] * 7 + [do3, lt3, ds3, band])


def _bwd_in(dqs, dkvs, dag, w_in, x, g, dh1, rot):
    s, d = x.shape
    n = w_in.shape[1]
    tm = _row_tile(s)
    dils = DILATIONS[1:]
    nd = len(dils)

    def body(q1, *rest):
        q_str, kv1, kv_str = rest[:nd], rest[nd], rest[nd + 1:2 * nd + 1]
        dag_ref, w_ref, x_ref, g_ref, dh_ref, c_ref, a_ref, b_ref, gx_ref, dy_ref, gg_ref, qbuf, kvbuf = rest[2 * nd + 1:]

        @pl.when(pl.program_id(0) == 0)
        def _():
            gg_ref[...] = jnp.zeros_like(gg_ref)

        _store_blocks(qbuf, q1[0].astype(F32))
        _store_blocks(kvbuf, kv1[0].astype(F32))
        for ref in q_str:
            _read_strided(ref, qbuf, add=True)
        for ref in kv_str:
            _read_strided(ref, kvbuf, add=True)
        dq, dkv = _load_blocks(qbuf), _load_blocks(kvbuf)
        reps = (1, D_ATT // LANES)
        cc, aa, bb = jnp.tile(c_ref[...], reps), jnp.tile(a_ref[...], reps), jnp.tile(b_ref[...], reps)
        for blk, t in enumerate((dq, dkv[:, :D_ATT])):
            dt = t * cc + pltpu.roll(t * aa, ROT_DIM // 2, 1) + pltpu.roll(t * bb, D_ATT - ROT_DIM // 2, 1)
            dy_ref[:, blk * D_ATT:(blk + 1) * D_ATT] = dt.astype(BF16)
        dy_ref[:, 2 * D_ATT:3 * D_ATT] = dkv[:, D_ATT:].astype(BF16)
        dy_ref[:, 3 * D_ATT:] = dag_ref[...]
        dx, gg = _rms_bwd(x_ref[...], g_ref[...], _dot_nt(dy_ref[...], w_ref[...]))
        gg_ref[...] += jnp.sum(gg, axis=0, keepdims=True)
        gx_ref[...] = dh_ref[...] + dx

    row = lambda w: pl.BlockSpec((tm, w), lambda i: (i, 0))
    def strided(width):
        return [pl.BlockSpec((1, tm, width), lambda i: (0, i, 0))] + [_strided_spec(dil, tm, width) for dil in dils]

    return pl.pallas_call(
        body, name="bwd_in", grid=(s // tm,),
        in_specs=strided(D_ATT) + strided(2 * D_ATT)
        + [row(2 * D_CONV), _full((d, n)), row(d), _full((1, d)), row(d)] + [row(LANES)] * 3,
        out_specs=[row(d), row(n), _full((1, d))],
        out_shape=[jax.ShapeDtypeStruct((s, d), F32), jax.ShapeDtypeStruct((s, n), BF16),
                   jax.ShapeDtypeStruct((1, d), F32)],
        scratch_shapes=[_lane_scratch(tm, D_ATT), _lane_scratch(tm, 2 * D_ATT)],
        compiler_params=_params("arbitrary"),
    )(*dqs, *dkvs, dag, w_in, x, g, dh1, *rot)


def _wgrad(a, b, name, a_blk=None, b_blk=None, stack=None, tm=1024):
    s, ka = a.shape
    nb = b.shape[1]
    a_blk, b_blk = a_blk or ka, b_blk or nb
    na, nbl = ka // a_blk, nb // b_blk
    assert na == 1 or nbl == 1
    tm = min(tm, s)
    nt = s // tm
    per = b_blk // stack if stack else 0

    def body(a_ref, b_ref, o_ref, acc):
        t = pl.program_id(1)

        @pl.when(t == 0)
        def _():
            acc[...] = jnp.zeros_like(acc)

        acc[...] += _dot_tn(a_ref[...], b_ref[...])

        @pl.when(t == nt - 1)
        def _():
            if stack:
                for c in range(per):
                    o_ref[c] = acc[:, c * stack:(c + 1) * stack].astype(BF16)
            else:
                o_ref[...] = acc[...].astype(BF16)

    if stack:
        out_spec = pl.BlockSpec((per, ka, stack), lambda k, t: (k, 0, 0))
        out_shape = jax.ShapeDtypeStruct((nb // stack, ka, stack), BF16)
    elif na > 1:
        out_spec = pl.BlockSpec((a_blk, nb), lambda k, t: (k, 0))
        out_shape = jax.ShapeDtypeStruct((ka, nb), BF16)
    else:
        out_spec = pl.BlockSpec((ka, b_blk), lambda k, t: (0, k))
        out_shape = jax.ShapeDtypeStruct((ka, nb), BF16)
    return pl.pallas_call(
        body, name=name, grid=(na * nbl, nt),
        in_specs=[pl.BlockSpec((tm, a_blk), (lambda k, t: (t, k)) if na > 1 else (lambda k, t: (t, 0))),
                  pl.BlockSpec((tm, b_blk), (lambda k, t: (t, k)) if nbl > 1 else (lambda k, t: (t, 0)))],
        out_specs=out_spec, out_shape=out_shape,
        scratch_shapes=[pltpu.VMEM((a_blk, b_blk), F32)],
        compiler_params=_params("parallel", "arbitrary"),
    )(a, b)


def _adamw(w, gsrc, m, v, name, transposed=False):
    summed = gsrc.ndim == w.ndim + 1
    rows, cols = w.shape
    assert gsrc.shape[-2:] == ((cols, rows) if transposed else (rows, cols)) and (summed or not transposed)
    tr = rows if rows <= 256 else 256
    assert rows % tr == 0
    c1 = 1.0 - ADAM_B1 ** ADAM_STEP
    c2 = 1.0 - ADAM_B2 ** ADAM_STEP

    def body(w_ref, g_ref, m_ref, v_ref, go_ref, d_ref, mo_ref, vo_ref):
        if summed:
            g = g_ref[0].astype(F32)
            for i in range(1, N_DEV):
                g = g + g_ref[i].astype(F32)
            if transposed:
                g = g.T
        else:
            g = g_ref[...]
        mn = ADAM_B1 * m_ref[...] + (1.0 - ADAM_B1) * g
        vn = ADAM_B2 * v_ref[...] + (1.0 - ADAM_B2) * jnp.square(g)
        go_ref[...] = g
        mo_ref[...] = mn
        vo_ref[...] = vn
        d_ref[...] = -ADAM_LR * ((mn / c1) / (jnp.sqrt(vn / c2) + ADAM_EPS) + ADAM_WD * w_ref[...])

    blk = pl.BlockSpec((tr, cols), lambda i: (i, 0))
    if transposed:
        gblk = pl.BlockSpec((N_DEV, cols, tr), lambda i: (0, 0, i))
    else:
        gblk = pl.BlockSpec((N_DEV, tr, cols), lambda i: (0, i, 0)) if summed else blk
    return pl.pallas_call(
        body, name=name, grid=(rows // tr,),
        in_specs=[blk, gblk, blk, blk], out_specs=[blk] * 4,
        out_shape=[jax.ShapeDtypeStruct(w.shape, F32)] * 4,
        compiler_params=_params("parallel"),
    )(w, gsrc, m, v)


def _sum_slots(g, name):
    _, rows, cols = g.shape

    def body(g_ref, o_ref):
        acc = g_ref[0]
        for i in range(1, N_DEV):
            acc = acc + g_ref[i]
        o_ref[...] = acc

    return pl.pallas_call(body, name=name, out_shape=jax.ShapeDtypeStruct((rows, cols), F32),
                          compiler_params=_params())(g)


def kernel(x, mem, norm_mix_g, w_in, conv_w, conv_b, conv_ln_g, conv_ln_b, w_out, norm_x_g, norm_mem_g, w_xq, w_xk, w_xv, w_xo, norm_mlp_g, w_up, w_down, norm_final_g, loss_target, m_norm_mix_g, m_w_in, m_conv_w, m_conv_b, m_conv_ln_g, m_conv_ln_b, m_w_out, m_norm_x_g, m_norm_mem_g, m_w_xq, m_w_xk, m_w_xv, m_w_xo, m_norm_mlp_g, m_w_up, m_w_down, m_norm_final_g, v_norm_mix_g, v_w_in, v_conv_w, v_conv_b, v_conv_ln_g, v_conv_ln_b, v_w_out, v_norm_x_g, v_norm_mem_g, v_w_xq, v_w_xk, v_w_xv, v_w_xo, v_norm_mlp_g, v_w_up, v_w_down, v_norm_final_g):
    x2, mem2, tgt = x[0], mem[0], loss_target[0]
    s, d = x2.shape
    gf = norm_final_g[None, :]

    cw_local = jnp.pad(conv_w[0], ((0, 1), (0, LANES - conv_w.shape[2])))
    win_g, cw_g = _exchange_call([_Exchange([w_in[0].astype(BF16), cw_local], gather=True)], "gather_w_in")
    w_in_f = jnp.transpose(win_g, (1, 0, 2)).reshape(d, -1)
    cw_f = jnp.transpose(cw_g[:, :, :conv_w.shape[2]], (1, 0, 2)).reshape(32, D_CONV)
    def stacked(*ws):
        return jnp.concatenate([w[0].astype(BF16) for w in ws], axis=0)

    def unstacked(g, *ws):
        offs = [0]
        for w in ws:
            offs.append(offs[-1] + w.shape[1])
        return [g[:, a:b, :].reshape(N_DEV * (b - a), d) for a, b in zip(offs, offs[1:])]

    late = [_Exchange([stacked(w_out, w_xq)], gather=True), _Exchange([w_up[0].astype(BF16)], gather=True),
            _Exchange([w_down[0].astype(BF16)], gather=True)]

    rot = _rotary_tables(s)
    band = _band_bias()
    xn, qkv, ag, *qkv_strided = _fwd_in(x2, norm_mix_g, w_in_f, rot)
    qkv3 = [qkv[None]] + qkv_strided
    outs, lses, gathered = [], [], []
    for dil, q3, side in zip(DILATIONS, qkv3, late):
        (o3, l3), got = _swa_fwd(q3, band, f"swa_fwd_d{dil}", [side])
        outs.append(o3)
        lses.append(l3)
        gathered += got
    rows_g, wup_g, wdown_g = gathered
    w_out_f, w_xq_f = unstacked(rows_g, w_out, w_xq)
    w_down_f = wdown_g.reshape(-1, d)
    w_up_t = jnp.swapaxes(wup_g, 1, 2).reshape(-1, d)
    (c1, conv_out), (rows_g,) = _fwd_conv(ag, cw_f, conv_b, conv_ln_g, conv_ln_b,
                                          [_Exchange([stacked(w_xk, w_xv, w_xo)], gather=True)])
    w_xk_f, w_xv_f, w_xo_f = unstacked(rows_g, w_xk, w_xv, w_xo)
    h1, cat, ltot, *lt_strided = _fwd_mix_out(outs, lses, conv_out, x2, w_out_f)
    mn, xk, xv = _fwd_mem(mem2, norm_mem_g, w_xk_f, w_xv_f)
    h2, hn2, xq, xo = _fwd_xattn(h1, norm_x_g, w_xq_f, xk, xv, w_xo_f)
    hn3, act, dh3, dh3b, loss_part, g_final = _fwd_mlp_loss(h2, norm_mlp_g, wup_g, w_down_f, gf, tgt)

    def scatter(*grads):
        return _Exchange([g.reshape(N_DEV, -1, g.shape[-1]) for g in grads], gather=False)

    f_blk = w_up.shape[2]
    du, dh2, dh2b, g_mlp = _bwd_mlp(dh3, dh3b, act, w_up_t, w_down_f, h2, norm_mlp_g)
    gw_up = _wgrad(hn3, du, "wgrad_up", b_blk=4 * f_blk, stack=f_blk)
    gw_down = _wgrad(dh3b, act, "wgrad_down", b_blk=4 * f_blk, stack=f_blk)
    (dh1, dh1b, dxq, dxk, dxv, g_x), (r_up,) = _bwd_xattn(
        dh2, dh2b, h1, norm_x_g, xq, xk, xv, w_xq_f, w_xo_f, [scatter(gw_up)])
    gw_xq = _wgrad(hn2, dxq, "wgrad_xq")
    gw_xo = _wgrad(xo, dh2b, "wgrad_xo")
    gw_xk, gw_xv, g_mem = _bwd_mem(mem2, norm_mem_g, mn, dxk, dxv, w_xk_f, w_xv_f)
    head = jnp.arange(D_ATT) // HEAD_DIM
    head_ones = (head[:, None] == head[None, :]).astype(BF16)
    dcat, dsum, *strided = _bwd_mix_out(dh1b, w_out_f, cat, head_ones)
    n_str = len(DILATIONS) - 1
    do3, lt3, ds3 = [dcat[None]] + strided[:n_str], [ltot[None]] + lt_strided, [dsum[None]] + strided[n_str:]
    gw_out = _wgrad(cat, dh1b, "wgrad_out")
    (dag, g_cw, g_cb, g_lg, g_lb), (r_down,) = _bwd_conv(dcat, c1, ag, cw_f, conv_ln_g, conv_ln_b, [scatter(gw_down)])
    hosted = [[scatter(gw_out, gw_xq)], [scatter(gw_xk, gw_xv)], [scatter(gw_xo)]]
    dqs, dkvs, landed = [], [], []
    for i, dil in enumerate(DILATIONS):
        (dq3, dkv3), got = _swa_bwd(qkv3[i], do3[i], lt3[i], ds3[i], band, f"swa_bwd_d{dil}", hosted[i])
        dqs.append(dq3)
        dkvs.append(dkv3)
        landed += got
    r_out, r_xq, r_xk, r_xv, r_xo = landed
    grad_x, dy, g_mix = _bwd_in(dqs, dkvs, dag, w_in_f, x2, norm_mix_g, dh1, rot)
    gw_in = _wgrad(xn, dy, "wgrad_in", b_blk=dy.shape[1] // 2)

    def widen(t):
        return jnp.pad(t, ((0, 0), (0, d - t.shape[1])))

    n_in = w_in.shape[2]
    small = jnp.concatenate([g_mix, g_x, g_mem, g_mlp, g_final, widen(g_cb), widen(g_lg), widen(g_lb),
                             g_cw.reshape(16, d), widen(loss_part), jnp.zeros((7, d), F32)], axis=0)
    r_in, small_g = _exchange_call(
        [_Exchange([jnp.transpose(gw_in.reshape(d, N_DEV, n_in), (1, 0, 2))], gather=False),
         _Exchange([small], gather=True)], "scatter_w_in_gather_small")
    small_sum = _sum_slots(small_g, "sum_small_grads")
    loss = small_sum[24, 0]

    res = {}

    def step(name, w, gsrc, m, v, transposed=False):
        shape = w.shape
        w2, m2, v2 = (t.reshape(-1, shape[-1]) for t in (w, m, v))
        res[name] = [t.reshape(shape) for t in _adamw(w2, gsrc, m2, v2, "adamw_" + name, transposed)]

    step("w_in", w_in, r_in, m_w_in, v_w_in)
    step("w_up", w_up, r_up, m_w_up, v_w_up)
    step("w_out", w_out, r_out, m_w_out, v_w_out)
    step("w_xq", w_xq, r_xq, m_w_xq, v_w_xq)
    step("w_xk", w_xk, r_xk, m_w_xk, v_w_xk)
    step("w_xv", w_xv, r_xv, m_w_xv, v_w_xv)
    step("w_xo", w_xo, r_xo, m_w_xo, v_w_xo)
    step("w_down", w_down, r_down, m_w_down, v_w_down, transposed=True)

    me = _dev_index((lax.axis_index("x"), lax.axis_index("y"), lax.axis_index("c")))
    n_cw = conv_w.shape[2]
    g_cw_full = small_sum[8:24].reshape(32, D_CONV)[:CONV_WIDTH]
    g_cw_mine = lax.dynamic_slice_in_dim(g_cw_full, me * n_cw, n_cw, axis=1)
    step("conv_w", conv_w, g_cw_mine, m_conv_w, v_conv_w)

    vec_names = ["norm_mix_g", "norm_x_g", "norm_mem_g", "norm_mlp_g", "norm_final_g", "conv_b", "conv_ln_g", "conv_ln_b"]
    vec_w = [norm_mix_g, norm_x_g, norm_mem_g, norm_mlp_g, gf, conv_b, conv_ln_g, conv_ln_b]
    vec_m = [m_norm_mix_g, m_norm_x_g, m_norm_mem_g, m_norm_mlp_g, m_norm_final_g[None, :], m_conv_b, m_conv_ln_g, m_conv_ln_b]
    vec_v = [v_norm_mix_g, v_norm_x_g, v_norm_mem_g, v_norm_mlp_g, v_norm_final_g[None, :], v_conv_b, v_conv_ln_g, v_conv_ln_b]

    def pack(ts):
        return jnp.concatenate([widen(t) for t in ts], axis=0)

    packed = _adamw(pack(vec_w), small_sum[0:8], pack(vec_m), pack(vec_v), "adamw_vectors")
    for i, name in enumerate(vec_names):
        width = vec_w[i].shape[1]
        shape = (width,) if name == "norm_final_g" else (1, width)
        res[name] = [t[i, :width].reshape(shape) for t in packed]

    order = ["norm_mix_g", "w_in", "conv_w", "conv_b", "conv_ln_g", "conv_ln_b", "w_out", "norm_x_g", "norm_mem_g",
             "w_xq", "w_xk", "w_xv", "w_xo", "norm_mlp_g", "w_up", "w_down", "norm_final_g"]
    out = [loss, grad_x[None]]
    for kind in range(4):
        out += [res[name][kind] for name in order]
    return tuple(out)
```

```python
import jax
import jax.numpy as jnp
from jax import lax
from jax.experimental import pallas as pl
from jax.experimental.pallas import tpu as pltpu

F32 = jnp.float32
BF16 = jnp.bfloat16

N_DEV = 8
EPS = 1e-6
NEG_INF = -1e30
ATT_HEADS = 8
HEAD_DIM = 64
D_ATT = ATT_HEADS * HEAD_DIM
D_CONV = 512
DILATIONS = (1, 4, 16)
HALF = 64
ROPE_THETA = 500000.0
ROT_DIM = HEAD_DIM // 4
CONV_WIDTH = 31
CONV_PAD = (CONV_WIDTH - 1) // 2
XATT_HEADS = 4
ADAM_LR = 0.001
ADAM_B1 = 0.9
ADAM_B2 = 0.999
ADAM_EPS = 1e-08
ADAM_WD = 0.01
ADAM_STEP = 10

LANES = 128
SUBLANES = 8
BF16_ROWS = 16
BQ = 128
WIN = BQ + 2 * HALF
MLP_SHARDS = 4
CONV_ROWS = 32
VMEM_LIMIT = 56 * 1024 * 1024
MESH = pl.DeviceIdType.MESH
ANY = pl.BlockSpec(memory_space=pl.ANY)

_NT = (((1,), (1,)), ((), ()))
_TN = (((0,), (0,)), ((), ()))


def _dot(a, b):
    return jnp.dot(a, b, preferred_element_type=F32)


def _dot_nt(a, b):
    return lax.dot_general(a, b, _NT, preferred_element_type=F32)


def _dot_tn(a, b):
    return lax.dot_general(a, b, _TN, preferred_element_type=F32)


def _params(*sem):
    return pltpu.CompilerParams(dimension_semantics=sem or None, vmem_limit_bytes=VMEM_LIMIT)


def _sigmoid(v):
    return 1.0 / (1.0 + jnp.exp(-v))


def _mean(v):
    return jnp.mean(v, axis=-1, keepdims=True)


def _rms_fwd(h, g):
    r = lax.rsqrt(_mean(h * h) + EPS)
    return h * r * g, r


def _rms_bwd(h, g, d_out):
    r = lax.rsqrt(_mean(h * h) + EPS)
    hn = h * r
    gd = d_out * g
    return r * (gd - hn * _mean(gd * hn)), d_out * hn


def _row_tile(s):
    return min(512, s)


def _full(shape):
    return pl.BlockSpec(shape, lambda *_: (0,) * len(shape))


def _mesh_pos():
    return lax.axis_index("x"), lax.axis_index("y"), lax.axis_index("c")


def _dev_index(p):
    return 4 * p[0] + 2 * p[1] + p[2]


class _Exchange:
    def __init__(self, arrays, gather):
        self.arrays, self.gather, self.n = list(arrays), gather, len(arrays)

    def out_shapes(self):
        return [jax.ShapeDtypeStruct(((N_DEV,) + a.shape) if self.gather else a.shape, a.dtype)
                for a in self.arrays]

    def sem_shapes(self):
        return [pltpu.SemaphoreType.DMA((7 * self.n,)), pltpu.SemaphoreType.DMA((7 * self.n,)),
                pltpu.SemaphoreType.DMA((self.n,))]

    def phases(self, x_refs, o_refs, send_sems, recv_sems, local_sems):
        n = self.n
        x, y, c = _mesh_pos()
        me, sibling = (x, y, c), (x, y, 1 - c)

        if self.gather:
            chips = [(1 - x, y), (x, 1 - y), (1 - x, 1 - y)]

            def copy(a, k, block, to, src=None):
                slot = o_refs[a].at[_dev_index(block)]
                return pltpu.make_async_remote_copy(
                    src_ref=slot if src is None else src, dst_ref=slot,
                    send_sem=send_sems.at[7 * a + k], recv_sem=recv_sems.at[7 * a + k],
                    device_id=to, device_id_type=MESH)

            def mine(a):
                return pltpu.make_async_copy(x_refs[a], o_refs[a].at[_dev_index(me)], local_sems.at[a])

            def first(a):
                return [copy(a, 0, me, sibling, src=x_refs[a])] + [
                    copy(a, 1 + j, me, (*chip, c), src=x_refs[a]) for j, chip in enumerate(chips)]

            def relayed(a, j):
                return copy(a, 4 + j, (*chips[j], c), sibling)

            def start():
                for a in range(n):
                    mine(a).start()
                    for cp in first(a):
                        cp.start()

            def relay():
                for j, chip in enumerate(chips):
                    for a in range(n):
                        copy(a, 1 + j, (*chip, c), me).wait_recv()
                        relayed(a, j).start()

            def finish():
                for a in range(n):
                    copy(a, 0, sibling, me).wait_recv()
                    for j, chip in enumerate(chips):
                        copy(a, 4 + j, (*chip, 1 - c), me).wait_recv()
                    for cp in first(a) + [relayed(a, j) for j in range(3)]:
                        cp.wait_send()
                    mine(a).wait()

            return start, relay, finish

        flips = [(dx, dy, dc) for dx in (0, 1) for dy in (0, 1) for dc in (0, 1)][1:]

        def peer(k):
            return tuple(1 - v if fl else v for v, fl in zip(me, flips[k]))

        def send(a, k):
            return pltpu.make_async_remote_copy(
                src_ref=x_refs[a].at[_dev_index(peer(k))], dst_ref=o_refs[a].at[_dev_index(me)],
                send_sem=send_sems.at[7 * a + k], recv_sem=recv_sems.at[7 * a + k],
                device_id=peer(k), device_id_type=MESH)

        def landed(a, k):
            slot = o_refs[a].at[_dev_index(peer(k))]
            return pltpu.make_async_remote_copy(
                src_ref=slot, dst_ref=slot, send_sem=send_sems.at[7 * a + k], recv_sem=recv_sems.at[7 * a + k],
                device_id=peer(k), device_id_type=MESH)

        def own(a):
            return pltpu.make_async_copy(x_refs[a].at[_dev_index(me)], o_refs[a].at[_dev_index(me)],
                                         local_sems.at[a])

        def start():
            for a in range(n):
                own(a).start()
            for k in range(7):
                for a in range(n):
                    send(a, k).start()

        def finish():
            for k in range(7):
                for a in range(n):
                    landed(a, k).wait_recv()
            for k in range(7):
                for a in range(n):
                    send(a, k).wait_send()
            for a in range(n):
                own(a).wait()

        return start, (lambda: None), finish


def _hosted_call(body, sides, *, name, grid, in_specs, out_specs, out_shape, scratch_shapes, args):
    n_in, n_out, ns = len(in_specs), len(out_specs), sum(s.n for s in sides)
    steps = 1
    for g in grid:
        steps *= g

    def wrapped(*refs):
        ins, s_ins = refs[:n_in], refs[n_in:n_in + ns]
        outs = refs[n_in + ns:n_in + ns + n_out]
        s_outs = refs[n_in + ns + n_out:n_in + 2 * ns + n_out]
        rest = refs[n_in + 2 * ns + n_out:]
        scratch, sems = rest[:len(rest) - 3 * len(sides)], rest[len(rest) - 3 * len(sides):]
        phases, off = [], 0
        for i, s in enumerate(sides):
            phases.append(s.phases(s_ins[off:off + s.n], s_outs[off:off + s.n], *sems[3 * i:3 * i + 3]))
            off += s.n
        lin = 0
        for ax, g in enumerate(grid):
            lin = lin * g + pl.program_id(ax)

        if sides:
            @pl.when(lin == 0)
            def _():
                for start, _, _ in phases:
                    start()

        body(*ins, *outs, *scratch)

        if sides:
            @pl.when(lin == min((3 * steps) // 4, steps - 1))
            def _():
                for _, relay, _ in phases:
                    relay()

            @pl.when(lin == steps - 1)
            def _():
                for _, _, finish in phases:
                    finish()

    res = pl.pallas_call(
        wrapped, name=name, grid=grid,
        in_specs=list(in_specs) + [ANY] * ns, out_specs=list(out_specs) + [ANY] * ns,
        out_shape=list(out_shape) + [sh for s in sides for sh in s.out_shapes()],
        scratch_shapes=list(scratch_shapes) + [sh for s in sides for sh in s.sem_shapes()],
        compiler_params=_params(*(("arbitrary",) * len(grid))),
    )(*args, *[a for s in sides for a in s.arrays])
    return res[:n_out], res[n_out:]


def _exchange_call(sides, name):
    ns = sum(s.n for s in sides)

    def body(*refs):
        x_refs, o_refs, sems = refs[:ns], refs[ns:2 * ns], refs[2 * ns:]
        phases, off = [], 0
        for i, s in enumerate(sides):
            phases.append(s.phases(x_refs[off:off + s.n], o_refs[off:off + s.n], *sems[3 * i:3 * i + 3]))
            off += s.n
        for step in range(3):
            for ph in phases:
                ph[step]()

    return pl.pallas_call(
        body, name=name,
        out_shape=[sh for s in sides for sh in s.out_shapes()],
        in_specs=[ANY] * ns, out_specs=[ANY] * ns,
        scratch_shapes=[sh for s in sides for sh in s.sem_shapes()],
    )(*[a for s in sides for a in s.arrays])


def _rotary_tables(s):
    half = ROT_DIM // 2
    freqs = ROPE_THETA ** (-jnp.arange(0, ROT_DIM, 2, dtype=F32) / ROT_DIM)
    ang = jnp.arange(s, dtype=F32)[:, None] * freqs[None, :]
    cos, sin = jnp.cos(ang), jnp.sin(ang)
    one = jnp.ones((s, HEAD_DIM - ROT_DIM), F32)
    zero = jnp.zeros((s, HEAD_DIM - ROT_DIM), F32)
    zh = jnp.zeros((s, half), F32)
    c64 = jnp.concatenate([cos, cos, one], axis=1)
    a64 = jnp.concatenate([-sin, zh, zero], axis=1)
    b64 = jnp.concatenate([zh, sin, zero], axis=1)
    return tuple(jnp.tile(t, (1, LANES // HEAD_DIM)) for t in (c64, a64, b64))


def _strided_spec(dil, tm, width):
    return pl.BlockSpec((dil, tm // dil, width), lambda i: (0, i, 0))


def _strided_shape(dil, s, width, dtype):
    return jax.ShapeDtypeStruct((dil, s // dil, width), dtype)


def _lane_scratch(tm, width):
    return pltpu.VMEM((width // LANES, tm, LANES), F32)


def _store_blocks(buf, v):
    for cb in range(buf.shape[0]):
        buf[cb] = v[:, cb * LANES:(cb + 1) * LANES]


def _load_blocks(buf):
    return jnp.concatenate([buf[cb] for cb in range(buf.shape[0])], axis=1)


def _write_strided(buf, dst_ref):
    dil, rows, _ = dst_ref.shape
    for r in range(dil):
        for cb in range(buf.shape[0]):
            dst_ref[r, :, cb * LANES:(cb + 1) * LANES] = buf[cb, pl.ds(r, rows, stride=dil), :].astype(dst_ref.dtype)


def _read_strided(src_ref, buf, add=False):
    dil, rows, _ = src_ref.shape
    for r in range(dil):
        for cb in range(buf.shape[0]):
            v = src_ref[r, :, cb * LANES:(cb + 1) * LANES].astype(F32)
            if add:
                v = v + buf[cb, pl.ds(r, rows, stride=dil), :]
            buf[cb, pl.ds(r, rows, stride=dil), :] = v


def _fwd_in(x, g, w_in, rot):
    s, d = x.shape
    n = w_in.shape[1]
    tm = _row_tile(s)
    dils = DILATIONS[1:]

    def body(x_ref, g_ref, w_ref, c_ref, a_ref, b_ref, xn_ref, qkv_ref, ag_ref, *rest):
        strided, ybuf = rest[:len(dils)], rest[len(dils)]
        xn = _rms_fwd(x_ref[...], g_ref[...])[0].astype(BF16)
        xn_ref[...] = xn
        y = _dot(xn, w_ref[...])
        reps = (1, D_ATT // LANES)
        cc, aa, bb = jnp.tile(c_ref[...], reps), jnp.tile(a_ref[...], reps), jnp.tile(b_ref[...], reps)
        parts = []
        for blk in range(2):
            t = y[:, blk * D_ATT:(blk + 1) * D_ATT]
            parts.append(t * cc + pltpu.roll(t, D_ATT - ROT_DIM // 2, 1) * aa + pltpu.roll(t, ROT_DIM // 2, 1) * bb)
        qkv = jnp.concatenate(parts + [y[:, 2 * D_ATT:3 * D_ATT]], axis=1)
        qkv_ref[...] = qkv.astype(BF16)
        _store_blocks(ybuf, qkv)
        for ref in strided:
            _write_strided(ybuf, ref)
        ag_ref[...] = y[:, 3 * D_ATT:].astype(BF16)

    row = lambda w: pl.BlockSpec((tm, w), lambda i: (i, 0))
    return pl.pallas_call(
        body, name="fwd_in", grid=(s // tm,),
        in_specs=[row(d), _full((1, d)), _full((d, n)), row(LANES), row(LANES), row(LANES)],
        out_specs=[row(d), row(3 * D_ATT), row(2 * D_CONV)] + [_strided_spec(dil, tm, 3 * D_ATT) for dil in dils],
        out_shape=[jax.ShapeDtypeStruct((s, d), BF16), jax.ShapeDtypeStruct((s, 3 * D_ATT), BF16),
                   jax.ShapeDtypeStruct((s, 2 * D_CONV), BF16)]
        + [_strided_shape(dil, s, 3 * D_ATT, BF16) for dil in dils],
        scratch_shapes=[_lane_scratch(tm, 3 * D_ATT)],
        compiler_params=_params("parallel"),
    )(x, g, w_in, *rot)


def _residues_per_step(dil):
    return 4 if dil % 4 == 0 else 1


def _win_in_specs(length, col, width, group):
    per = BQ // HALF
    last = length // HALF - 1
    return [
        pl.BlockSpec((group, HALF, width), lambda r, j: (r, jnp.maximum(j * per - 1, 0), col)),
        pl.BlockSpec((group, BQ, width), lambda r, j: (r, j, col)),
        pl.BlockSpec((group, HALF, width), lambda r, j: (r, jnp.minimum(j * per + per, last), col)),
    ]


def _fill_window(win, prev_ref, main_ref, next_ref):
    for g in range(win.shape[0]):
        win[g, 0:HALF] = prev_ref[g]
        win[g, HALF:HALF + BQ] = main_ref[g]
        win[g, HALF + BQ:] = next_ref[g]


def _band_bias():
    blk = jnp.arange(2 * BQ)[:, None] & (BQ - 1)
    win = jnp.arange(WIN)[None, :]
    return jnp.where(jnp.abs(win - HALF - blk) <= HALF, 0.0, NEG_INF).astype(F32)


def _window_bias(band_ref, j, length):
    pos = j * BQ - HALF + lax.broadcasted_iota(jnp.int32, (1, WIN), 1)
    return band_ref[...] + jnp.where((pos >= 0) & (pos < length), 0.0, NEG_INF)


def _first_head():
    return lax.broadcasted_iota(jnp.int32, (1, LANES), 1) < HEAD_DIM


def _stack_heads(v, first):
    zero = jnp.zeros((), v.dtype)
    return jnp.concatenate([jnp.where(first, v, zero), jnp.where(first, zero, v)], axis=0)


def _unstack_heads(v, first):
    rows = v.shape[0] // 2
    return jnp.where(first, v[:rows], v[rows:])


def _stack_cols(v, first):
    other = pltpu.roll(v, HEAD_DIM, 1)
    stacked = jnp.concatenate([jnp.where(first, v, other), jnp.where(first, other, v)], axis=0)
    return jnp.tile(stacked, (1, WIN // LANES))


def _swa_fwd(qkv3, band, name, sides=()):
    dil, length, _ = qkv3.shape
    scale = HEAD_DIM ** -0.5
    group = _residues_per_step(dil)

    def body(q_ref, kp, km, kn, vp, vm, vn, band_ref, o_ref, lse_ref, kwin, vwin):
        j = pl.program_id(1)
        _fill_window(kwin, kp, km, kn)
        _fill_window(vwin, vp, vm, vn)
        bias = _window_bias(band_ref, j, length)
        first = _first_head()
        pairs = [(g, slice(pr * LANES, (pr + 1) * LANES)) for g in range(group) for pr in range(D_ATT // LANES)]
        scores = [_dot_nt(_stack_heads(q_ref[g, :, cols] * scale, first), kwin[g, :, cols]) + bias for g, cols in pairs]
        stats, probs = [], []
        for sc in scores:
            m = jnp.max(sc, axis=-1, keepdims=True)
            p = jnp.exp(sc - m)
            stats.append((m, jnp.sum(p, axis=-1, keepdims=True)))
            probs.append(p.astype(BF16))
        for (g, cols), p, (m, den) in zip(pairs, probs, stats):
            pv = _dot(p, vwin[g, :, cols]) * (1.0 / den)
            o_ref[g, :, cols] = _unstack_heads(pv, first).astype(BF16)
            lse_ref[g, :, cols] = _unstack_heads(jnp.broadcast_to(m + jnp.log(den), (2 * BQ, LANES)), first)

    blk = pl.BlockSpec((group, BQ, D_ATT), lambda r, j: (r, j, 0))
    return _hosted_call(
        body, sides, name=name, grid=(dil // group, length // BQ),
        in_specs=[blk] + _win_in_specs(length, 1, D_ATT, group) + _win_in_specs(length, 2, D_ATT, group)
        + [_full((2 * BQ, WIN))],
        out_specs=[blk, blk],
        out_shape=[jax.ShapeDtypeStruct((dil, length, D_ATT), BF16),
                   jax.ShapeDtypeStruct((dil, length, D_ATT), F32)],
        scratch_shapes=[pltpu.VMEM((group, WIN, D_ATT), BF16)] * 2,
        args=[qkv3] * 7 + [band])


def _glu(v):
    return v[:, :D_CONV].astype(F32) * _sigmoid(v[:, D_CONV:].astype(F32))


def _halo_specs(tm, width, col=0):
    per = tm // BF16_ROWS
    return lambda nblk: [
        pl.BlockSpec((BF16_ROWS, width), lambda i: (jnp.maximum(i * per - 1, 0), col)),
        pl.BlockSpec((tm, width), lambda i: (i, col)),
        pl.BlockSpec((BF16_ROWS, width), lambda i: (jnp.minimum(i * per + per, nblk - 1), col)),
    ]


def _fill_halo(buf, i, ntiles, tm, prev, main, nxt):
    buf[0:BF16_ROWS] = jnp.where(i == 0, 0.0, prev)
    buf[BF16_ROWS:BF16_ROWS + tm] = main
    buf[BF16_ROWS + tm:] = jnp.where(i == ntiles - 1, 0.0, nxt)


def _halo_scratch(tm):
    return [pltpu.VMEM((tm + 2 * BF16_ROWS, D_CONV), F32),
            pltpu.VMEM((SUBLANES - 1, tm + 2 * BF16_ROWS - SUBLANES, D_CONV), F32)]


def _shift_copies(buf, shifted, tm):
    rows = tm + 2 * BF16_ROWS - SUBLANES
    for b in range(1, SUBLANES):
        shifted[b - 1] = buf[pl.ds(b, rows), :]


def _tap(buf, shifted, off, rows, base=0):
    a, b = divmod(off, SUBLANES)
    start = base + SUBLANES * a
    if not isinstance(start, int):
        start = pl.multiple_of(start, SUBLANES)
    if b == 0:
        return buf[pl.ds(start, rows), :]
    return shifted[b - 1, pl.ds(start, rows), :]


def _fwd_conv(ag, cw, cb, lg, lb, sides=()):
    s = ag.shape[0]
    tm = _row_tile(s)
    nt = s // tm

    def body(agp, agm, agn, cw_ref, cb_ref, lg_ref, lb_ref, c1_ref, co_ref, ubuf, ush):
        i = pl.program_id(0)
        _fill_halo(ubuf, i, nt, tm, _glu(agp[...]), _glu(agm[...]), _glu(agn[...]))
        _shift_copies(ubuf, ush, tm)
        w, cb = cw_ref[...], cb_ref[...]

        def chunk(c, carry):
            base = pl.multiple_of(c * CONV_ROWS, CONV_ROWS)
            acc = jnp.zeros((CONV_ROWS, D_CONV), F32)
            for k in range(CONV_WIDTH):
                acc = acc + _tap(ubuf, ush, k + 1, CONV_ROWS, base) * w[k:k + 1, :]
            c1_ref[pl.ds(base, CONV_ROWS), :] = acc + cb
            return carry

        lax.fori_loop(0, tm // CONV_ROWS, chunk, 0)
        c1 = c1_ref[...]
        xc = c1 - _mean(c1)
        ln = xc * lax.rsqrt(_mean(xc * xc) + EPS) * lg_ref[...] + lb_ref[...]
        co_ref[...] = (ln * _sigmoid(ln)).astype(BF16)

    vec = _full((1, D_CONV))
    return _hosted_call(
        body, sides, name="fwd_conv", grid=(nt,),
        in_specs=_halo_specs(tm, 2 * D_CONV)(s // BF16_ROWS) + [_full((32, D_CONV)), vec, vec, vec],
        out_specs=[pl.BlockSpec((tm, D_CONV), lambda i: (i, 0))] * 2,
        out_shape=[jax.ShapeDtypeStruct((s, D_CONV), F32), jax.ShapeDtypeStruct((s, D_CONV), BF16)],
        scratch_shapes=_halo_scratch(tm),
        args=[ag, ag, ag, cw, cb, lg, lb])


def _fwd_mix_out(outs, lses, conv_out, x, w_out):
    s, d = x.shape
    tm = _row_tile(s)
    dils = DILATIONS[1:]
    nd = len(dils)

    def body(o1, *rest):
        o_str, l1, l_str = rest[:nd], rest[nd], rest[nd + 1:2 * nd + 1]
        co, x_ref, w_ref, h_ref, cat_ref, lt_ref = rest[2 * nd + 1:2 * nd + 7]
        lt_str = rest[2 * nd + 7:3 * nd + 7]
        obufs, lbufs, ltbuf = rest[3 * nd + 7:4 * nd + 7], rest[4 * nd + 7:5 * nd + 7], rest[5 * nd + 7]
        for ref, buf in zip(o_str + l_str, obufs + lbufs):
            _read_strided(ref, buf)
        lse = [l1[0]] + [_load_blocks(buf) for buf in lbufs]
        out = [o1[0].astype(F32)] + [_load_blocks(buf) for buf in obufs]
        m = lse[0]
        for v in lse[1:]:
            m = jnp.maximum(m, v)
        e = [jnp.exp(v - m) for v in lse]
        den = sum(e[1:], e[0])
        att = (sum((ev * ov for ev, ov in zip(e[1:], out[1:])), e[0] * out[0]) / den).astype(BF16)
        lt = m + jnp.log(den)
        lt_ref[...] = lt
        _store_blocks(ltbuf, lt)
        for ref in lt_str:
            _write_strided(ltbuf, ref)
        cat_ref[:, :D_ATT] = att
        cat_ref[:, D_ATT:] = co[...]
        h_ref[...] = x_ref[...] + _dot(att, w_ref[:D_ATT, :]) + _dot(co[...], w_ref[D_ATT:, :])

    row = lambda w: pl.BlockSpec((tm, w), lambda i: (i, 0))
    nat = pl.BlockSpec((1, tm, D_ATT), lambda i: (0, i, 0))
    strided = [_strided_spec(dil, tm, D_ATT) for dil in dils]
    return pl.pallas_call(
        body, name="fwd_mix_out", grid=(s // tm,),
        in_specs=[nat] + strided + [nat] + strided + [row(D_ATT), row(d), _full((d, d))],
        out_specs=[row(d), row(d), row(D_ATT)] + strided,
        out_shape=[jax.ShapeDtypeStruct((s, d), F32), jax.ShapeDtypeStruct((s, d), BF16),
                   jax.ShapeDtypeStruct((s, D_ATT), F32)] + [_strided_shape(dil, s, D_ATT, F32) for dil in dils],
        scratch_shapes=[_lane_scratch(tm, D_ATT)] * (2 * nd + 1),
        compiler_params=_params("parallel"),
    )(*outs, *lses, conv_out, x, w_out)


def _fwd_mem(mem, g, wk, wv):
    m, d = mem.shape

    def body(mem_ref, g_ref, wk_ref, wv_ref, mn_ref, xk_ref, xv_ref):
        mn = _rms_fwd(mem_ref[...], g_ref[...])[0].astype(BF16)
        mn_ref[...] = mn
        xk_ref[...] = _dot(mn, wk_ref[...]).astype(BF16)
        xv_ref[...] = _dot(mn, wv_ref[...]).astype(BF16)

    return pl.pallas_call(
        body, name="fwd_mem",
        out_shape=[jax.ShapeDtypeStruct((m, d), BF16)] * 3,
        compiler_params=_params(),
    )(mem, g, wk, wv)


def _softmax(sc):
    p = jnp.exp(sc - jnp.max(sc, axis=-1, keepdims=True))
    return p / jnp.sum(p, axis=-1, keepdims=True)


def _fwd_xattn(h1, g, wq, xk, xv, wo):
    s, d = h1.shape
    m = xk.shape[0]
    tm = _row_tile(s)
    hd = d // XATT_HEADS

    def body(h_ref, g_ref, wq_ref, xk_ref, xv_ref, wo_ref, h2_ref, hn_ref, xq_ref, xo_ref):
        h = h_ref[...]
        hn = _rms_fwd(h, g_ref[...])[0].astype(BF16)
        hn_ref[...] = hn
        xq = _dot(hn, wq_ref[...]).astype(BF16)
        xq_ref[...] = xq
        heads = [slice(i * hd, (i + 1) * hd) for i in range(XATT_HEADS)]
        scores = [_dot_nt(xq[:, cols], xk_ref[:, cols]) * hd ** -0.5 for cols in heads]
        probs = [_softmax(sc).astype(BF16) for sc in scores]
        for cols, pr in zip(heads, probs):
            xo_ref[:, cols] = _dot(pr, xv_ref[:, cols]).astype(BF16)
        h2_ref[...] = h + _dot(xo_ref[...], wo_ref[...])

    row = pl.BlockSpec((tm, d), lambda i: (i, 0))
    return pl.pallas_call(
        body, name="fwd_xattn", grid=(s // tm,),
        in_specs=[row, _full((1, d)), _full((d, d)), _full((m, d)), _full((m, d)), _full((d, d))],
        out_specs=[row] * 4,
        out_shape=[jax.ShapeDtypeStruct((s, d), F32)] + [jax.ShapeDtypeStruct((s, d), BF16)] * 3,
        compiler_params=_params("parallel"),
    )(h1, g, wq, xk, xv, wo)


def _fwd_mlp_loss(h2, g, w_up, w_down, gf, target):
    s, d = h2.shape
    nsh, _, f = w_up.shape
    fb = MLP_SHARDS * f
    nb = nsh // MLP_SHARDS
    tm = _row_tile(s)

    def body(h_ref, g_ref, wu_ref, wd_ref, gf_ref, t_ref,
             hn_ref, act_ref, dh_ref, dhb_ref, loss_ref, ggf_ref, acc):
        i, k = pl.program_id(0), pl.program_id(1)

        @pl.when(k == 0)
        def _():
            hn_ref[...] = _rms_fwd(h_ref[...], g_ref[...])[0].astype(BF16)
            acc[...] = jnp.zeros_like(acc)

        @pl.when((i == 0) & (k == 0))
        def _():
            loss_ref[...] = jnp.zeros_like(loss_ref)
            ggf_ref[...] = jnp.zeros_like(ggf_ref)

        hn = hn_ref[...]
        ups = [_dot(hn, wu_ref[c]) for c in range(MLP_SHARDS)]
        for c, u in enumerate(ups):
            act_ref[:, c * f:(c + 1) * f] = jnp.square(jnp.maximum(u, 0.0)).astype(BF16)
        acc[...] += _dot(act_ref[...], wd_ref[...])

        @pl.when(k == nb - 1)
        def _():
            h3 = h_ref[...] + acc[...]
            gfv = gf_ref[...]
            y, _ = _rms_fwd(h3, gfv)
            err = y - t_ref[...]
            loss_ref[...] += 0.5 * jnp.sum(_mean(err * err))
            dh3, gg = _rms_bwd(h3, gfv, err * (1.0 / d))
            ggf_ref[...] += jnp.sum(gg, axis=0, keepdims=True)
            dh_ref[...] = dh3
            dhb_ref[...] = dh3.astype(BF16)

    row = pl.BlockSpec((tm, d), lambda i, k: (i, 0))
    return pl.pallas_call(
        body, name="fwd_mlp_loss", grid=(s // tm, nb),
        in_specs=[row, _full((1, d)),
                  pl.BlockSpec((MLP_SHARDS, d, f), lambda i, k: (k, 0, 0)),
                  pl.BlockSpec((fb, d), lambda i, k: (k, 0)),
                  _full((1, d)), row],
        out_specs=[row, pl.BlockSpec((tm, fb), lambda i, k: (i, k)), row, row,
                   _full((1, LANES)), _full((1, d))],
        out_shape=[jax.ShapeDtypeStruct((s, d), BF16), jax.ShapeDtypeStruct((s, nsh * f), BF16),
                   jax.ShapeDtypeStruct((s, d), F32), jax.ShapeDtypeStruct((s, d), BF16),
                   jax.ShapeDtypeStruct((1, LANES), F32), jax.ShapeDtypeStruct((1, d), F32)],
        scratch_shapes=[pltpu.VMEM((tm, d), F32)],
        compiler_params=_params("arbitrary", "arbitrary"),
    )(h2, g, w_up, w_down, gf, target)


def _bwd_mlp(dh3, dh3b, act, w_up_t, w_down, h2, g):
    s, d = h2.shape
    ff = w_down.shape[0]
    fb = MLP_SHARDS * (ff // N_DEV)
    nb = ff // fb
    tm = _row_tile(s)

    def body(dh_ref, dhb_ref, act_ref, wut_ref, wd_ref, h_ref, g_ref,
             du_ref, dh2_ref, dh2b_ref, gg_ref, acc):
        i, k = pl.program_id(0), pl.program_id(1)

        @pl.when(k == 0)
        def _():
            acc[...] = jnp.zeros_like(acc)

        @pl.when((i == 0) & (k == 0))
        def _():
            gg_ref[...] = jnp.zeros_like(gg_ref)

        dhb = dhb_ref[...]
        f = fb // MLP_SHARDS
        shards = [slice(c * f, (c + 1) * f) for c in range(MLP_SHARDS)]
        dacts = [_dot_nt(dhb, wd_ref[cols, :]) for cols in shards]
        for cols, dact in zip(shards, dacts):
            du_ref[:, cols] = (dact * (2.0 * jnp.sqrt(act_ref[:, cols].astype(F32)))).astype(BF16)
        acc[...] += _dot(du_ref[...], wut_ref[...])

        @pl.when(k == nb - 1)
        def _():
            dh, gg = _rms_bwd(h_ref[...], g_ref[...], acc[...])
            gg_ref[...] += jnp.sum(gg, axis=0, keepdims=True)
            dh2 = dh_ref[...] + dh
            dh2_ref[...] = dh2
            dh2b_ref[...] = dh2.astype(BF16)

    row = pl.BlockSpec((tm, d), lambda i, k: (i, 0))
    col = pl.BlockSpec((tm, fb), lambda i, k: (i, k))
    wblk = pl.BlockSpec((fb, d), lambda i, k: (k, 0))
    return pl.pallas_call(
        body, name="bwd_mlp", grid=(s // tm, nb),
        in_specs=[row, row, col, wblk, wblk, row, _full((1, d))],
        out_specs=[col, row, row, _full((1, d))],
        out_shape=[jax.ShapeDtypeStruct((s, ff), BF16), jax.ShapeDtypeStruct((s, d), F32),
                   jax.ShapeDtypeStruct((s, d), BF16), jax.ShapeDtypeStruct((1, d), F32)],
        scratch_shapes=[pltpu.VMEM((tm, d), F32)],
        compiler_params=_params("arbitrary", "arbitrary"),
    )(dh3, dh3b, act, w_up_t, w_down, h2, g)


def _bwd_xattn(dh2, dh2b, h1, g, xq, xk, xv, wq, wo, sides=()):
    s, d = h1.shape
    m = xk.shape[0]
    tm = _row_tile(s)
    hd = d // XATT_HEADS
    scale = hd ** -0.5

    def body(dh_ref, dhb_ref, h_ref, g_ref, xq_ref, xk_ref, xv_ref, wq_ref, wo_ref,
             dh1_ref, dh1b_ref, dxq_ref, dxk_ref, dxv_ref, gg_ref):
        @pl.when(pl.program_id(0) == 0)
        def _():
            dxk_ref[...] = jnp.zeros_like(dxk_ref)
            dxv_ref[...] = jnp.zeros_like(dxv_ref)
            gg_ref[...] = jnp.zeros_like(gg_ref)

        dxo = _dot_nt(dhb_ref[...], wo_ref[...])
        heads = [slice(i * hd, (i + 1) * hd) for i in range(XATT_HEADS)]
        dxos = [dxo[:, cols].astype(BF16) for cols in heads]
        scores = [_dot_nt(xq_ref[:, cols], xk_ref[:, cols]) * scale for cols in heads]
        dprs = [_dot_nt(dxo_h, xv_ref[:, cols]) for dxo_h, cols in zip(dxos, heads)]
        probs, dscs = [], []
        for sc, dpr in zip(scores, dprs):
            pr = _softmax(sc)
            dscs.append((pr * (dpr - jnp.sum(dpr * pr, axis=-1, keepdims=True)) * scale).astype(BF16))
            probs.append(pr.astype(BF16))
        for cols, dsc, pr, dxo_h in zip(heads, dscs, probs, dxos):
            dxq_ref[:, cols] = _dot(dsc, xk_ref[:, cols]).astype(BF16)
            dxk_ref[:, cols] += _dot_tn(dsc, xq_ref[:, cols])
            dxv_ref[:, cols] += _dot_tn(pr, dxo_h)
        dh, gg = _rms_bwd(h_ref[...], g_ref[...], _dot_nt(dxq_ref[...], wq_ref[...]))
        gg_ref[...] += jnp.sum(gg, axis=0, keepdims=True)
        dh1 = dh_ref[...] + dh
        dh1_ref[...] = dh1
        dh1b_ref[...] = dh1.astype(BF16)

    row = pl.BlockSpec((tm, d), lambda i: (i, 0))
    return _hosted_call(
        body, sides, name="bwd_xattn", grid=(s // tm,),
        in_specs=[row, row, row, _full((1, d)), row, _full((m, d)), _full((m, d)), _full((d, d)), _full((d, d))],
        out_specs=[row, row, row, _full((m, d)), _full((m, d)), _full((1, d))],
        out_shape=[jax.ShapeDtypeStruct((s, d), F32), jax.ShapeDtypeStruct((s, d), BF16),
                   jax.ShapeDtypeStruct((s, d), BF16), jax.ShapeDtypeStruct((m, d), F32),
                   jax.ShapeDtypeStruct((m, d), F32), jax.ShapeDtypeStruct((1, d), F32)],
        scratch_shapes=[],
        args=[dh2, dh2b, h1, g, xq, xk, xv, wq, wo])


def _bwd_mem(mem, g, mn, dxk, dxv, wk, wv):
    m, d = mem.shape

    def body(mem_ref, g_ref, mn_ref, dxk_ref, dxv_ref, wk_ref, wv_ref, gk_ref, gv_ref, gg_ref):
        dk, dv = dxk_ref[...].astype(BF16), dxv_ref[...].astype(BF16)
        gk_ref[...] = _dot_tn(mn_ref[...], dk).astype(BF16)
        gv_ref[...] = _dot_tn(mn_ref[...], dv).astype(BF16)
        dmn = _dot_nt(dk, wk_ref[...]) + _dot_nt(dv, wv_ref[...])
        _, gg = _rms_bwd(mem_ref[...], g_ref[...], dmn)
        gg_ref[...] = jnp.sum(gg, axis=0, keepdims=True)

    return pl.pallas_call(
        body, name="bwd_mem",
        out_shape=[jax.ShapeDtypeStruct((d, d), BF16), jax.ShapeDtypeStruct((d, d), BF16),
                   jax.ShapeDtypeStruct((1, d), F32)],
        compiler_params=_params(),
    )(mem, g, mn, dxk, dxv, wk, wv)


def _bwd_mix_out(dh1b, w_out, cat, head_ones):
    s, d = dh1b.shape
    tm = _row_tile(s)
    dils = DILATIONS[1:]
    nd = len(dils)

    def body(dh_ref, w_ref, cat_ref, ones_ref, dcat_ref, dsum_ref, *rest):
        da_str, ds_str, dbuf, sbuf = rest[:nd], rest[nd:2 * nd], rest[2 * nd], rest[2 * nd + 1]
        dcat = _dot_nt(dh_ref[...], w_ref[...])
        dcat_ref[...] = dcat.astype(BF16)
        datt = dcat[:, :D_ATT]
        prod = datt * cat_ref[...].astype(F32)
        hi = prod.astype(BF16)
        lo = (prod - hi.astype(F32)).astype(BF16)
        dsum = _dot(hi, ones_ref[...]) + _dot(lo, ones_ref[...])
        dsum_ref[...] = dsum
        _store_blocks(dbuf, datt)
        _store_blocks(sbuf, dsum)
        for da_ref, ds_ref in zip(da_str, ds_str):
            _write_strided(dbuf, da_ref)
            _write_strided(sbuf, ds_ref)

    row = lambda w: pl.BlockSpec((tm, w), lambda i: (i, 0))
    strided = [_strided_spec(dil, tm, D_ATT) for dil in dils]
    return pl.pallas_call(
        body, name="bwd_mix_out", grid=(s // tm,),
        in_specs=[row(d), _full((d, d)), row(D_ATT), _full((D_ATT, D_ATT))],
        out_specs=[row(d), row(D_ATT)] + strided + strided,
        out_shape=[jax.ShapeDtypeStruct((s, d), BF16), jax.ShapeDtypeStruct((s, D_ATT), F32)]
        + [_strided_shape(dil, s, D_ATT, BF16) for dil in dils]
        + [_strided_shape(dil, s, D_ATT, F32) for dil in dils],
        scratch_shapes=[_lane_scratch(tm, D_ATT)] * 2,
        compiler_params=_params("parallel"),
    )(dh1b, w_out, cat, head_ones)


def _bwd_conv(dcat, c1, ag, cw, lg, lb, sides=()):
    s = ag.shape[0]
    tm = _row_tile(s)
    nt = s // tm

    def body(dp, dm, dn, cp, cm, cn, agp, agm, agn, cw_ref, lg_ref, lb_ref,
             dag_ref, gcw_ref, gcb_ref, glg_ref, glb_ref, ubuf, ush, dbuf, dsh, gacc):
        i = pl.program_id(0)

        @pl.when(i == 0)
        def _():
            gacc[...] = jnp.zeros_like(gacc)
            gcb_ref[...] = jnp.zeros_like(gcb_ref)
            glg_ref[...] = jnp.zeros_like(glg_ref)
            glb_ref[...] = jnp.zeros_like(glb_ref)

        lgv, lbv = lg_ref[...], lb_ref[...]

        def norm_bwd(dco, c1v):
            xc = c1v - _mean(c1v)
            rs = lax.rsqrt(_mean(xc * xc) + EPS)
            z = xc * rs
            ln = z * lgv + lbv
            sg = _sigmoid(ln)
            dln = dco.astype(F32) * (sg * (1.0 + ln * (1.0 - sg)))
            dz = dln * lgv
            return rs * (dz - _mean(dz) - z * _mean(dz * z)), dln, z

        def sublane_sums(v):
            out = v[0:SUBLANES]
            for r in range(SUBLANES, CONV_ROWS, SUBLANES):
                out = out + v[r:r + SUBLANES]
            return out

        dc_m, dln, z = norm_bwd(dm[...], cm[...])
        glg_ref[...] += jnp.sum(dln * z, axis=0, keepdims=True)
        glb_ref[...] += jnp.sum(dln, axis=0, keepdims=True)
        gcb_ref[...] += jnp.sum(dc_m, axis=0, keepdims=True)
        _fill_halo(dbuf, i, nt, tm, norm_bwd(dp[...], cp[...])[0], dc_m, norm_bwd(dn[...], cn[...])[0])
        _fill_halo(ubuf, i, nt, tm, _glu(agp[...]), _glu(agm[...]), _glu(agn[...]))
        _shift_copies(dbuf, dsh, tm)
        _shift_copies(ubuf, ush, tm)

        w = cw_ref[...]

        def chunk(c, carry):
            base = pl.multiple_of(c * CONV_ROWS, CONV_ROWS)
            rows = pl.ds(base, CONV_ROWS)
            dc = dbuf[pl.ds(pl.multiple_of(base + BF16_ROWS, SUBLANES), CONV_ROWS), :]
            du = jnp.zeros((CONV_ROWS, D_CONV), F32)
            for k in range(CONV_WIDTH):
                du = du + _tap(dbuf, dsh, CONV_WIDTH - k, CONV_ROWS, base) * w[k:k + 1, :]
                gacc[k] += sublane_sums(dc * _tap(ubuf, ush, k + 1, CONV_ROWS, base))
            a = agm[rows, :D_CONV].astype(F32)
            sg = _sigmoid(agm[rows, D_CONV:].astype(F32))
            dag_ref[rows, :D_CONV] = (du * sg).astype(BF16)
            dag_ref[rows, D_CONV:] = (du * a * sg * (1.0 - sg)).astype(BF16)
            return carry

        lax.fori_loop(0, tm // CONV_ROWS, chunk, 0)

        @pl.when(i == nt - 1)
        def _():
            tap = lax.broadcasted_iota(jnp.int32, (32, D_CONV), 0)
            gcw = jnp.zeros((32, D_CONV), F32)
            for k in range(CONV_WIDTH):
                gcw = jnp.where(tap == k, jnp.sum(gacc[k], axis=0, keepdims=True), gcw)
            gcw_ref[...] = gcw

    vec = _full((1, D_CONV))
    nblk = s // BF16_ROWS
    return _hosted_call(
        body, sides, name="bwd_conv", grid=(nt,),
        in_specs=_halo_specs(tm, D_CONV, 1)(nblk) + _halo_specs(tm, D_CONV)(nblk) + _halo_specs(tm, 2 * D_CONV)(nblk)
        + [_full((32, D_CONV)), vec, vec],
        out_specs=[pl.BlockSpec((tm, 2 * D_CONV), lambda i: (i, 0)), _full((32, D_CONV)), vec, vec, vec],
        out_shape=[jax.ShapeDtypeStruct((s, 2 * D_CONV), BF16), jax.ShapeDtypeStruct((32, D_CONV), F32)]
        + [jax.ShapeDtypeStruct((1, D_CONV), F32)] * 3,
        scratch_shapes=_halo_scratch(tm) + _halo_scratch(tm) + [pltpu.VMEM((32, SUBLANES, D_CONV), F32)],
        args=[dcat, dcat, dcat, c1, c1, c1, ag, ag, ag, cw, lg, lb])


def _swa_bwd(qkv3, do3, lt3, ds3, band, name, sides=()):
    dil, length, _ = qkv3.shape
    nb = length // BQ
    scale = HEAD_DIM ** -0.5
    group = _residues_per_step(dil)
    assert WIN == 2 * BQ and BQ == 2 * HALF

    def body(q_ref, kp, km, kn, vp, vm, vn, do_ref, l_ref, s_ref, band_ref, dq_ref, dkv_ref, kwin, vwin, pend, keep):
        j = pl.program_id(1)

        @pl.when(j == 0)
        def _():
            pend[...] = jnp.zeros_like(pend)
            keep[...] = jnp.zeros_like(keep)

        @pl.when(j < nb)
        def _():
            _fill_window(kwin, kp, km, kn)
            _fill_window(vwin, vp, vm, vn)
            bias = _window_bias(band_ref, j, length)
            first = _first_head()
            pairs = [(g, slice(pr * LANES, (pr + 1) * LANES)) for g in range(group) for pr in range(D_ATT // LANES)]
            qs = [_stack_heads(q_ref[g, :, cols] * scale, first) for g, cols in pairs]
            dos = [_stack_heads(do_ref[g, :, cols], first) for g, cols in pairs]
            scores = [_dot_nt(q, kwin[g, :, cols]) for q, (g, cols) in zip(qs, pairs)]
            dps = [_dot_nt(do, vwin[g, :, cols]) for do, (g, cols) in zip(dos, pairs)]
            probs, dscs = [], []
            for (g, cols), sc, dp in zip(pairs, scores, dps):
                p = jnp.exp(sc + (bias - _stack_cols(l_ref[g, :, cols], first)))
                dscs.append((p * (dp - _stack_cols(s_ref[g, :, cols], first))).astype(BF16))
                probs.append(p.astype(BF16))
            dqs = [_dot(dsc, kwin[g, :, cols] * scale) for dsc, (g, cols) in zip(dscs, pairs)]
            dks = [_dot_tn(dsc, q) for dsc, q in zip(dscs, qs)]
            dvs = [_dot_tn(p, do) for p, do in zip(probs, dos)]
            for (g, cols), dq, dk, dv in zip(pairs, dqs, dks, dvs):
                dq_ref[g, :, cols] = _unstack_heads(dq, first).astype(BF16)
                for part, at in ((dk, cols), (dv, slice(D_ATT + cols.start, D_ATT + cols.stop))):
                    dkv_ref[g, :HALF, at] = keep[g, :, at].astype(BF16)
                    dkv_ref[g, HALF:, at] = (pend[g, :HALF, at] + part[:HALF]).astype(BF16)
                    keep[g, :, at] = pend[g, HALF:, at] + part[HALF:BQ]
                    pend[g, :, at] = part[BQ:]

        @pl.when(j == nb)
        def _():
            dkv_ref[:, :HALF] = keep[...].astype(BF16)
            dkv_ref[:, HALF:] = pend[:, :HALF].astype(BF16)

    def clamp(idx):
        return lambda r, j: idx(r, jnp.minimum(j, nb - 1))

    main = pl.BlockSpec((group, BQ, D_ATT), clamp(lambda r, j: (r, j, 0)))
    wins = [pl.BlockSpec(sp.block_shape, clamp(sp.index_map))
            for c in (1, 2) for sp in _win_in_specs(length, c, D_ATT, group)]
    return _hosted_call(
        body, sides, name=name, grid=(dil // group, nb + 1),
        in_specs=[main] + wins + [main] * 3 + [_full((2 * BQ, WIN))],
        out_specs=[main, pl.BlockSpec((group, BQ, 2 * D_ATT), lambda r, j: (r, jnp.maximum(j - 1, 0), 0))],
        out_shape=[jax.ShapeDtypeStruct((dil, length, D_ATT), BF16),
                   jax.ShapeDtypeStruct((dil, length, 2 * D_ATT), BF16)],
        scratch_shapes=[pltpu.VMEM((group, WIN, D_ATT), BF16)] * 2
        + [pltpu.VMEM((group, BQ, 2 * D_ATT), F32), pltpu.VMEM((group, HALF, 2 * D_ATT), F32)],
        args=[qkv3] * 7 + [do3, lt3, ds3, band])


def _bwd_in(dqs, dkvs, dag, w_in, x, g, dh1, rot):
    s, d = x.shape
    n = w_in.shape[1]
    tm = _row_tile(s)
    dils = DILATIONS[1:]
    nd = len(dils)

    def body(q1, *rest):
        q_str, kv1, kv_str = rest[:nd], rest[nd], rest[nd + 1:2 * nd + 1]
        dag_ref, w_ref, x_ref, g_ref, dh_ref, c_ref, a_ref, b_ref, gx_ref, dy_ref, gg_ref, qbuf, kvbuf = rest[2 * nd + 1:]

        @pl.when(pl.program_id(0) == 0)
        def _():
            gg_ref[...] = jnp.zeros_like(gg_ref)

        _store_blocks(qbuf, q1[0].astype(F32))
        _store_blocks(kvbuf, kv1[0].astype(F32))
        for ref in q_str:
            _read_strided(ref, qbuf, add=True)
        for ref in kv_str:
            _read_strided(ref, kvbuf, add=True)
        dq, dkv = _load_blocks(qbuf), _load_blocks(kvbuf)
        reps = (1, D_ATT // LANES)
        cc, aa, bb = jnp.tile(c_ref[...], reps), jnp.tile(a_ref[...], reps), jnp.tile(b_ref[...], reps)
        for blk, t in enumerate((dq, dkv[:, :D_ATT])):
            dt = t * cc + pltpu.roll(t * aa, ROT_DIM // 2, 1) + pltpu.roll(t * bb, D_ATT - ROT_DIM // 2, 1)
            dy_ref[:, blk * D_ATT:(blk + 1) * D_ATT] = dt.astype(BF16)
        dy_ref[:, 2 * D_ATT:3 * D_ATT] = dkv[:, D_ATT:].astype(BF16)
        dy_ref[:, 3 * D_ATT:] = dag_ref[...]
        dx, gg = _rms_bwd(x_ref[...], g_ref[...], _dot_nt(dy_ref[...], w_ref[...]))
        gg_ref[...] += jnp.sum(gg, axis=0, keepdims=True)
        gx_ref[...] = dh_ref[...] + dx

    row = lambda w: pl.BlockSpec((tm, w), lambda i: (i, 0))
    def strided(width):
        return [pl.BlockSpec((1, tm, width), lambda i: (0, i, 0))] + [_strided_spec(dil, tm, width) for dil in dils]

    return pl.pallas_call(
        body, name="bwd_in", grid=(s // tm,),
        in_specs=strided(D_ATT) + strided(2 * D_ATT)
        + [row(2 * D_CONV), _full((d, n)), row(d), _full((1, d)), row(d)] + [row(LANES)] * 3,
        out_specs=[row(d), row(n), _full((1, d))],
        out_shape=[jax.ShapeDtypeStruct((s, d), F32), jax.ShapeDtypeStruct((s, n), BF16),
                   jax.ShapeDtypeStruct((1, d), F32)],
        scratch_shapes=[_lane_scratch(tm, D_ATT), _lane_scratch(tm, 2 * D_ATT)],
        compiler_params=_params("arbitrary"),
    )(*dqs, *dkvs, dag, w_in, x, g, dh1, *rot)


def _wgrad(a, b, name, a_blk=None, b_blk=None, stack=None, tm=1024):
    s, ka = a.shape
    nb = b.shape[1]
    a_blk, b_blk = a_blk or ka, b_blk or nb
    na, nbl = ka // a_blk, nb // b_blk
    assert na == 1 or nbl == 1
    tm = min(tm, s)
    nt = s // tm
    per = b_blk // stack if stack else 0

    def body(a_ref, b_ref, o_ref, acc):
        t = pl.program_id(1)

        @pl.when(t == 0)
        def _():
            acc[...] = jnp.zeros_like(acc)

        acc[...] += _dot_tn(a_ref[...], b_ref[...])

        @pl.when(t == nt - 1)
        def _():
            if stack:
                for c in range(per):
                    o_ref[c] = acc[:, c * stack:(c + 1) * stack].astype(BF16)
            else:
                o_ref[...] = acc[...].astype(BF16)

    if stack:
        out_spec = pl.BlockSpec((per, ka, stack), lambda k, t: (k, 0, 0))
        out_shape = jax.ShapeDtypeStruct((nb // stack, ka, stack), BF16)
    elif na > 1:
        out_spec = pl.BlockSpec((a_blk, nb), lambda k, t: (k, 0))
        out_shape = jax.ShapeDtypeStruct((ka, nb), BF16)
    else:
        out_spec = pl.BlockSpec((ka, b_blk), lambda k, t: (0, k))
        out_shape = jax.ShapeDtypeStruct((ka, nb), BF16)
    return pl.pallas_call(
        body, name=name, grid=(na * nbl, nt),
        in_specs=[pl.BlockSpec((tm, a_blk), (lambda k, t: (t, k)) if na > 1 else (lambda k, t: (t, 0))),
                  pl.BlockSpec((tm, b_blk), (lambda k, t: (t, k)) if nbl > 1 else (lambda k, t: (t, 0)))],
        out_specs=out_spec, out_shape=out_shape,
        scratch_shapes=[pltpu.VMEM((a_blk, b_blk), F32)],
        compiler_params=_params("parallel", "arbitrary"),
    )(a, b)


def _adamw(w, gsrc, m, v, name, transposed=False):
    summed = gsrc.ndim == w.ndim + 1
    rows, cols = w.shape
    assert gsrc.shape[-2:] == ((cols, rows) if transposed else (rows, cols)) and (summed or not transposed)
    tr = rows if rows <= 256 else 256
    assert rows % tr == 0
    c1 = 1.0 - ADAM_B1 ** ADAM_STEP
    c2 = 1.0 - ADAM_B2 ** ADAM_STEP

    def body(w_ref, g_ref, m_ref, v_ref, go_ref, d_ref, mo_ref, vo_ref):
        if summed:
            g = g_ref[0].astype(F32)
            for i in range(1, N_DEV):
                g = g + g_ref[i].astype(F32)
            if transposed:
                g = g.T
        else:
            g = g_ref[...]
        mn = ADAM_B1 * m_ref[...] + (1.0 - ADAM_B1) * g
        vn = ADAM_B2 * v_ref[...] + (1.0 - ADAM_B2) * jnp.square(g)
        go_ref[...] = g
        mo_ref[...] = mn
        vo_ref[...] = vn
        d_ref[...] = -ADAM_LR * ((mn / c1) / (jnp.sqrt(vn / c2) + ADAM_EPS) + ADAM_WD * w_ref[...])

    blk = pl.BlockSpec((tr, cols), lambda i: (i, 0))
    if transposed:
        gblk = pl.BlockSpec((N_DEV, cols, tr), lambda i: (0, 0, i))
    else:
        gblk = pl.BlockSpec((N_DEV, tr, cols), lambda i: (0, i, 0)) if summed else blk
    return pl.pallas_call(
        body, name=name, grid=(rows // tr,),
        in_specs=[blk, gblk, blk, blk], out_specs=[blk] * 4,
        out_shape=[jax.ShapeDtypeStruct(w.shape, F32)] * 4,
        compiler_params=_params("parallel"),
    )(w, gsrc, m, v)


def _sum_slots(g, name):
    _, rows, cols = g.shape

    def body(g_ref, o_ref):
        acc = g_ref[0]
        for i in range(1, N_DEV):
            acc = acc + g_ref[i]
        o_ref[...] = acc

    return pl.pallas_call(body, name=name, out_shape=jax.ShapeDtypeStruct((rows, cols), F32),
                          compiler_params=_params())(g)


def kernel(x, mem, norm_mix_g, w_in, conv_w, conv_b, conv_ln_g, conv_ln_b, w_out, norm_x_g, norm_mem_g, w_xq, w_xk, w_xv, w_xo, norm_mlp_g, w_up, w_down, norm_final_g, loss_target, m_norm_mix_g, m_w_in, m_conv_w, m_conv_b, m_conv_ln_g, m_conv_ln_b, m_w_out, m_norm_x_g, m_norm_mem_g, m_w_xq, m_w_xk, m_w_xv, m_w_xo, m_norm_mlp_g, m_w_up, m_w_down, m_norm_final_g, v_norm_mix_g, v_w_in, v_conv_w, v_conv_b, v_conv_ln_g, v_conv_ln_b, v_w_out, v_norm_x_g, v_norm_mem_g, v_w_xq, v_w_xk, v_w_xv, v_w_xo, v_norm_mlp_g, v_w_up, v_w_down, v_norm_final_g):
    x2, mem2, tgt = x[0], mem[0], loss_target[0]
    s, d = x2.shape
    gf = norm_final_g[None, :]

    cw_local = jnp.pad(conv_w[0], ((0, 1), (0, LANES - conv_w.shape[2])))
    win_g, cw_g = _exchange_call([_Exchange([w_in[0].astype(BF16), cw_local], gather=True)], "gather_w_in")
    w_in_f = jnp.transpose(win_g, (1, 0, 2)).reshape(d, -1)
    cw_f = jnp.transpose(cw_g[:, :, :conv_w.shape[2]], (1, 0, 2)).reshape(32, D_CONV)
    def stacked(*ws):
        return jnp.concatenate([w[0].astype(BF16) for w in ws], axis=0)

    def unstacked(g, *ws):
        offs = [0]
        for w in ws:
            offs.append(offs[-1] + w.shape[1])
        return [g[:, a:b, :].reshape(N_DEV * (b - a), d) for a, b in zip(offs, offs[1:])]

    late = [_Exchange([stacked(w_out, w_xq)], gather=True), _Exchange([w_up[0].astype(BF16)], gather=True),
            _Exchange([w_down[0].astype(BF16)], gather=True)]

    rot = _rotary_tables(s)
    band = _band_bias()
    xn, qkv, ag, *qkv_strided = _fwd_in(x2, norm_mix_g, w_in_f, rot)
    qkv3 = [qkv[None]] + qkv_strided
    outs, lses, gathered = [], [], []
    for dil, q3, side in zip(DILATIONS, qkv3, late):
        (o3, l3), got = _swa_fwd(q3, band, f"swa_fwd_d{dil}", [side])
        outs.append(o3)
        lses.append(l3)
        gathered += got
    rows_g, wup_g, wdown_g = gathered
    w_out_f, w_xq_f = unstacked(rows_g, w_out, w_xq)
    w_down_f = wdown_g.reshape(-1, d)
    w_up_t = jnp.swapaxes(wup_g, 1, 2).reshape(-1, d)
    (c1, conv_out), (rows_g,) = _fwd_conv(ag, cw_f, conv_b, conv_ln_g, conv_ln_b,
                                          [_Exchange([stacked(w_xk, w_xv, w_xo)], gather=True)])
    w_xk_f, w_xv_f, w_xo_f = unstacked(rows_g, w_xk, w_xv, w_xo)
    h1, cat, ltot, *lt_strided = _fwd_mix_out(outs, lses, conv_out, x2, w_out_f)
    mn, xk, xv = _fwd_mem(mem2, norm_mem_g, w_xk_f, w_xv_f)
    h2, hn2, xq, xo = _fwd_xattn(h1, norm_x_g, w_xq_f, xk, xv, w_xo_f)
    hn3, act, dh3, dh3b, loss_part, g_final = _fwd_mlp_loss(h2, norm_mlp_g, wup_g, w_down_f, gf, tgt)

    def scatter(*grads):
        return _Exchange([g.reshape(N_DEV, -1, g.shape[-1]) for g in grads], gather=False)

    f_blk = w_up.shape[2]
    du, dh2, dh2b, g_mlp = _bwd_mlp(dh3, dh3b, act, w_up_t, w_down_f, h2, norm_mlp_g)
    gw_up = _wgrad(hn3, du, "wgrad_up", b_blk=4 * f_blk, stack=f_blk)
    gw_down = _wgrad(dh3b, act, "wgrad_down", b_blk=4 * f_blk, stack=f_blk)
    (dh1, dh1b, dxq, dxk, dxv, g_x), (r_up,) = _bwd_xattn(
        dh2, dh2b, h1, norm_x_g, xq, xk, xv, w_xq_f, w_xo_f, [scatter(gw_up)])
    gw_xq = _wgrad(hn2, dxq, "wgrad_xq")
    gw_xo = _wgrad(xo, dh2b, "wgrad_xo")
    gw_xk, gw_xv, g_mem = _bwd_mem(mem2, norm_mem_g, mn, dxk, dxv, w_xk_f, w_xv_f)
    head = jnp.arange(D_ATT) // HEAD_DIM
    head_ones = (head[:, None] == head[None, :]).astype(BF16)
    dcat, dsum, *strided = _bwd_mix_out(dh1b, w_out_f, cat, head_ones)
    n_str = len(DILATIONS) - 1
    do3, lt3, ds3 = [dcat[None]] + strided[:n_str], [ltot[None]] + lt_strided, [dsum[None]] + strided[n_str:]
    gw_out = _wgrad(cat, dh1b, "wgrad_out")
    (dag, g_cw, g_cb, g_lg, g_lb), (r_down,) = _bwd_conv(dcat, c1, ag, cw_f, conv_ln_g, conv_ln_b, [scatter(gw_down)])
    hosted = [[scatter(gw_out, gw_xq)], [scatter(gw_xk, gw_xv)], [scatter(gw_xo)]]
    dqs, dkvs, landed = [], [], []
    for i, dil in enumerate(DILATIONS):
        (dq3, dkv3), got = _swa_bwd(qkv3[i], do3[i], lt3[i], ds3[i], band, f"swa_bwd_d{dil}", hosted[i])
        dqs.append(dq3)
        dkvs.append(dkv3)
        landed += got
    r_out, r_xq, r_xk, r_xv, r_xo = landed
    grad_x, dy, g_mix = _bwd_in(dqs, dkvs, dag, w_in_f, x2, norm_mix_g, dh1, rot)
    gw_in = _wgrad(xn, dy, "wgrad_in", b_blk=dy.shape[1] // 2)

    def widen(t):
        return jnp.pad(t, ((0, 0), (0, d - t.shape[1])))

    n_in = w_in.shape[2]
    small = jnp.concatenate([g_mix, g_x, g_mem, g_mlp, g_final, widen(g_cb), widen(g_lg), widen(g_lb),
                             g_cw.reshape(16, d), widen(loss_part), jnp.zeros((7, d), F32)], axis=0)
    r_in, small_g = _exchange_call(
        [_Exchange([jnp.transpose(gw_in.reshape(d, N_DEV, n_in), (1, 0, 2))], gather=False),
         _Exchange([small], gather=True)], "scatter_w_in_gather_small")
    small_sum = _sum_slots(small_g, "sum_small_grads")
    loss = small_sum[24, 0]

    res = {}

    def step(name, w, gsrc, m, v, transposed=False):
        shape = w.shape
        w2, m2, v2 = (t.reshape(-1, shape[-1]) for t in (w, m, v))
        res[name] = [t.reshape(shape) for t in _adamw(w2, gsrc, m2, v2, "adamw_" + name, transposed)]

    step("w_in", w_in, r_in, m_w_in, v_w_in)
    step("w_up", w_up, r_up, m_w_up, v_w_up)
    step("w_out", w_out, r_out, m_w_out, v_w_out)
    step("w_xq", w_xq, r_xq, m_w_xq, v_w_xq)
    step("w_xk", w_xk, r_xk, m_w_xk, v_w_xk)
    step("w_xv", w_xv, r_xv, m_w_xv, v_w_xv)
    step("w_xo", w_xo, r_xo, m_w_xo, v_w_xo)
    step("w_down", w_down, r_down, m_w_down, v_w_down, transposed=True)

    me = _dev_index((lax.axis_index("x"), lax.axis_index("y"), lax.axis_index("c")))
    n_cw = conv_w.shape[2]
    g_cw_full = small_sum[8:24].reshape(32, D_CONV)[:CONV_WIDTH]
    g_cw_mine = lax.dynamic_slice_in_dim(g_cw_full, me * n_cw, n_cw, axis=1)
    step("conv_w", conv_w, g_cw_mine, m_conv_w, v_conv_w)

    vec_names = ["norm_mix_g", "norm_x_g", "norm_mem_g", "norm_mlp_g", "norm_final_g", "conv_b", "conv_ln_g", "conv_ln_b"]
    vec_w = [norm_mix_g, norm_x_g, norm_mem_g, norm_mlp_g, gf, conv_b, conv_ln_g, conv_ln_b]
    vec_m = [m_norm_mix_g, m_norm_x_g, m_norm_mem_g, m_norm_mlp_g, m_norm_final_g[None, :], m_conv_b, m_conv_ln_g, m_conv_ln_b]
    vec_v = [v_norm_mix_g, v_norm_x_g, v_norm_mem_g, v_norm_mlp_g, v_norm_final_g[None, :], v_conv_b, v_conv_ln_g, v_conv_ln_b]

    def pack(ts):
        return jnp.concatenate([widen(t) for t in ts], axis=0)

    packed = _adamw(pack(vec_w), small_sum[0:8], pack(vec_m), pack(vec_v), "adamw_vectors")
    for i, name in enumerate(vec_names):
        width = vec_w[i].shape[1]
        shape = (width,) if name == "norm_final_g" else (1, width)
        res[name] = [t[i, :width].reshape(shape) for t in packed]

    order = ["norm_mix_g", "w_in", "conv_w", "conv_b", "conv_ln_g", "conv_ln_b", "w_out", "norm_x_g", "norm_mem_g",
             "w_xq", "w_xk", "w_xv", "w_xo", "norm_mlp_g", "w_up", "w_down", "norm_final_g"]
    out = [loss, grad_x[None]]
    for kind in range(4):
        out += [res[name][kind] for name in order]
    return tuple(out)
```

```python
import jax
import jax.numpy as jnp
from jax import lax
from jax.experimental import pallas as pl
from jax.experimental.pallas import tpu as pltpu

F32 = jnp.float32
BF16 = jnp.bfloat16

N_DEV = 8
EPS = 1e-6
NEG_INF = -1e30
ATT_HEADS = 8
HEAD_DIM = 64
D_ATT = ATT_HEADS * HEAD_DIM
D_CONV = 512
DILATIONS = (1, 4, 16)
HALF = 64
ROPE_THETA = 500000.0
ROT_DIM = HEAD_DIM // 4
CONV_WIDTH = 31
CONV_PAD = (CONV_WIDTH - 1) // 2
XATT_HEADS = 4
ADAM_LR = 0.001
ADAM_B1 = 0.9
ADAM_B2 = 0.999
ADAM_EPS = 1e-08
ADAM_WD = 0.01
ADAM_STEP = 10

LANES = 128
SUBLANES = 8
BF16_ROWS = 16
BQ = 128
WIN = BQ + 2 * HALF
MLP_SHARDS = 4
CONV_ROWS = 32
VMEM_LIMIT = 56 * 1024 * 1024
MESH = pl.DeviceIdType.MESH
ANY = pl.BlockSpec(memory_space=pl.ANY)

_NT = (((1,), (1,)), ((), ()))
_TN = (((0,), (0,)), ((), ()))


def _dot(a, b):
    return jnp.dot(a, b, preferred_element_type=F32)


def _dot_nt(a, b):
    return lax.dot_general(a, b, _NT, preferred_element_type=F32)


def _dot_tn(a, b):
    return lax.dot_general(a, b, _TN, preferred_element_type=F32)


def _params(*sem):
    return pltpu.CompilerParams(dimension_semantics=sem or None, vmem_limit_bytes=VMEM_LIMIT)


def _sigmoid(v):
    return 1.0 / (1.0 + jnp.exp(-v))


def _mean(v):
    return jnp.mean(v, axis=-1, keepdims=True)


def _rms_fwd(h, g):
    r = lax.rsqrt(_mean(h * h) + EPS)
    return h * r * g, r


def _rms_bwd(h, g, d_out):
    r = lax.rsqrt(_mean(h * h) + EPS)
    hn = h * r
    gd = d_out * g
    return r * (gd - hn * _mean(gd * hn)), d_out * hn


def _row_tile(s):
    return min(512, s)


def _full(shape):
    return pl.BlockSpec(shape, lambda *_: (0,) * len(shape))


def _mesh_pos():
    return lax.axis_index("x"), lax.axis_index("y"), lax.axis_index("c")


def _dev_index(p):
    return 4 * p[0] + 2 * p[1] + p[2]


class _Exchange:
    def __init__(self, arrays, gather):
        self.arrays, self.gather, self.n = list(arrays), gather, len(arrays)

    def out_shapes(self):
        return [jax.ShapeDtypeStruct(((N_DEV,) + a.shape) if self.gather else a.shape, a.dtype)
                for a in self.arrays]

    def sem_shapes(self):
        return [pltpu.SemaphoreType.DMA((7 * self.n,)), pltpu.SemaphoreType.DMA((7 * self.n,)),
                pltpu.SemaphoreType.DMA((self.n,))]

    def phases(self, x_refs, o_refs, send_sems, recv_sems, local_sems):
        n = self.n
        x, y, c = _mesh_pos()
        me, sibling = (x, y, c), (x, y, 1 - c)

        if self.gather:
            chips = [(1 - x, y), (x, 1 - y), (1 - x, 1 - y)]

            def copy(a, k, block, to, src=None):
                slot = o_refs[a].at[_dev_index(block)]
                return pltpu.make_async_remote_copy(
                    src_ref=slot if src is None else src, dst_ref=slot,
                    send_sem=send_sems.at[7 * a + k], recv_sem=recv_sems.at[7 * a + k],
                    device_id=to, device_id_type=MESH)

            def mine(a):
                return pltpu.make_async_copy(x_refs[a], o_refs[a].at[_dev_index(me)], local_sems.at[a])

            def first(a):
                return [copy(a, 0, me, sibling, src=x_refs[a])] + [
                    copy(a, 1 + j, me, (*chip, c), src=x_refs[a]) for j, chip in enumerate(chips)]

            def relayed(a, j):
                return copy(a, 4 + j, (*chips[j], c), sibling)

            def start():
                for a in range(n):
                    mine(a).start()
                    for cp in first(a):
                        cp.start()

            def relay():
                for j, chip in enumerate(chips):
                    for a in range(n):
                        copy(a, 1 + j, (*chip, c), me).wait_recv()
                        relayed(a, j).start()

            def finish():
                for a in range(n):
                    copy(a, 0, sibling, me).wait_recv()
                    for j, chip in enumerate(chips):
                        copy(a, 4 + j, (*chip, 1 - c), me).wait_recv()
                    for cp in first(a) + [relayed(a, j) for j in range(3)]:
                        cp.wait_send()
                    mine(a).wait()

            return start, relay, finish

        flips = [(dx, dy, dc) for dx in (0, 1) for dy in (0, 1) for dc in (0, 1)][1:]

        def peer(k):
            return tuple(1 - v if fl else v for v, fl in zip(me, flips[k]))

        def send(a, k):
            return pltpu.make_async_remote_copy(
                src_ref=x_refs[a].at[_dev_index(peer(k))], dst_ref=o_refs[a].at[_dev_index(me)],
                send_sem=send_sems.at[7 * a + k], recv_sem=recv_sems.at[7 * a + k],
                device_id=peer(k), device_id_type=MESH)

        def landed(a, k):
            slot = o_refs[a].at[_dev_index(peer(k))]
            return pltpu.make_async_remote_copy(
                src_ref=slot, dst_ref=slot, send_sem=send_sems.at[7 * a + k], recv_sem=recv_sems.at[7 * a + k],
                device_id=peer(k), device_id_type=MESH)

        def own(a):
            return pltpu.make_async_copy(x_refs[a].at[_dev_index(me)], o_refs[a].at[_dev_index(me)],
                                         local_sems.at[a])

        def start():
            for a in range(n):
                own(a).start()
            for k in range(7):
                for a in range(n):
                    send(a, k).start()

        def finish():
            for k in range(7):
                for a in range(n):
                    landed(a, k).wait_recv()
            for k in range(7):
                for a in range(n):
                    send(a, k).wait_send()
            for a in range(n):
                own(a).wait()

        return start, (lambda: None), finish


def _hosted_call(body, sides, *, name, grid, in_specs, out_specs, out_shape, scratch_shapes, args):
    n_in, n_out, ns = len(in_specs), len(out_specs), sum(s.n for s in sides)
    steps = 1
    for g in grid:
        steps *= g

    def wrapped(*refs):
        ins, s_ins = refs[:n_in], refs[n_in:n_in + ns]
        outs = refs[n_in + ns:n_in + ns + n_out]
        s_outs = refs[n_in + ns + n_out:n_in + 2 * ns + n_out]
        rest = refs[n_in + 2 * ns + n_out:]
        scratch, sems = rest[:len(rest) - 3 * len(sides)], rest[len(rest) - 3 * len(sides):]
        phases, off = [], 0
        for i, s in enumerate(sides):
            phases.append(s.phases(s_ins[off:off + s.n], s_outs[off:off + s.n], *sems[3 * i:3 * i + 3]))
            off += s.n
        lin = 0
        for ax, g in enumerate(grid):
            lin = lin * g + pl.program_id(ax)

        if sides:
            @pl.when(lin == 0)
            def _():
                for start, _, _ in phases:
                    start()

        body(*ins, *outs, *scratch)

        if sides:
            @pl.when(lin == min((3 * steps) // 4, steps - 1))
            def _():
                for _, relay, _ in phases:
                    relay()

            @pl.when(lin == steps - 1)
            def _():
                for _, _, finish in phases:
                    finish()

    res = pl.pallas_call(
        wrapped, name=name, grid=grid,
        in_specs=list(in_specs) + [ANY] * ns, out_specs=list(out_specs) + [ANY] * ns,
        out_shape=list(out_shape) + [sh for s in sides for sh in s.out_shapes()],
        scratch_shapes=list(scratch_shapes) + [sh for s in sides for sh in s.sem_shapes()],
        compiler_params=_params(*(("arbitrary",) * len(grid))),
    )(*args, *[a for s in sides for a in s.arrays])
    return res[:n_out], res[n_out:]


def _exchange_call(sides, name):
    ns = sum(s.n for s in sides)

    def body(*refs):
        x_refs, o_refs, sems = refs[:ns], refs[ns:2 * ns], refs[2 * ns:]
        phases, off = [], 0
        for i, s in enumerate(sides):
            phases.append(s.phases(x_refs[off:off + s.n], o_refs[off:off + s.n], *sems[3 * i:3 * i + 3]))
            off += s.n
        for step in range(3):
            for ph in phases:
                ph[step]()

    return pl.pallas_call(
        body, name=name,
        out_shape=[sh for s in sides for sh in s.out_shapes()],
        in_specs=[ANY] * ns, out_specs=[ANY] * ns,
        scratch_shapes=[sh for s in sides for sh in s.sem_shapes()],
    )(*[a for s in sides for a in s.arrays])


def _rotary_tables(s):
    half = ROT_DIM // 2
    freqs = ROPE_THETA ** (-jnp.arange(0, ROT_DIM, 2, dtype=F32) / ROT_DIM)
    ang = jnp.arange(s, dtype=F32)[:, None] * freqs[None, :]
    cos, sin = jnp.cos(ang), jnp.sin(ang)
    one = jnp.ones((s, HEAD_DIM - ROT_DIM), F32)
    zero = jnp.zeros((s, HEAD_DIM - ROT_DIM), F32)
    zh = jnp.zeros((s, half), F32)
    c64 = jnp.concatenate([cos, cos, one], axis=1)
    a64 = jnp.concatenate([-sin, zh, zero], axis=1)
    b64 = jnp.concatenate([zh, sin, zero], axis=1)
    return tuple(jnp.tile(t, (1, LANES // HEAD_DIM)) for t in (c64, a64, b64))


def _strided_spec(dil, tm, width):
    return pl.BlockSpec((dil, tm // dil, width), lambda i: (0, i, 0))


def _strided_shape(dil, s, width, dtype):
    return jax.ShapeDtypeStruct((dil, s // dil, width), dtype)


def _lane_scratch(tm, width):
    return pltpu.VMEM((width // LANES, tm, LANES), F32)


def _store_blocks(buf, v):
    for cb in range(buf.shape[0]):
        buf[cb] = v[:, cb * LANES:(cb + 1) * LANES]


def _load_blocks(buf):
    return jnp.concatenate([buf[cb] for cb in range(buf.shape[0])], axis=1)


def _write_strided(buf, dst_ref):
    dil, rows, _ = dst_ref.shape
    for r in range(dil):
        for cb in range(buf.shape[0]):
            dst_ref[r, :, cb * LANES:(cb + 1) * LANES] = buf[cb, pl.ds(r, rows, stride=dil), :].astype(dst_ref.dtype)


def _read_strided(src_ref, buf, add=False):
    dil, rows, _ = src_ref.shape
    for r in range(dil):
        for cb in range(buf.shape[0]):
            v = src_ref[r, :, cb * LANES:(cb + 1) * LANES].astype(F32)
            if add:
                v = v + buf[cb, pl.ds(r, rows, stride=dil), :]
            buf[cb, pl.ds(r, rows, stride=dil), :] = v


def _fwd_in(x, g, w_in, rot):
    s, d = x.shape
    n = w_in.shape[1]
    tm = _row_tile(s)
    dils = DILATIONS[1:]

    def body(x_ref, g_ref, w_ref, c_ref, a_ref, b_ref, xn_ref, qkv_ref, ag_ref, *rest):
        strided, ybuf = rest[:len(dils)], rest[len(dils)]
        xn = _rms_fwd(x_ref[...], g_ref[...])[0].astype(BF16)
        xn_ref[...] = xn
        y = _dot(xn, w_ref[...])
        reps = (1, D_ATT // LANES)
        cc, aa, bb = jnp.tile(c_ref[...], reps), jnp.tile(a_ref[...], reps), jnp.tile(b_ref[...], reps)
        parts = []
        for blk in range(2):
            t = y[:, blk * D_ATT:(blk + 1) * D_ATT]
            parts.append(t * cc + pltpu.roll(t, D_ATT - ROT_DIM // 2, 1) * aa + pltpu.roll(t, ROT_DIM // 2, 1) * bb)
        qkv = jnp.concatenate(parts + [y[:, 2 * D_ATT:3 * D_ATT]], axis=1)
        qkv_ref[...] = qkv.astype(BF16)
        _store_blocks(ybuf, qkv)
        for ref in strided:
            _write_strided(ybuf, ref)
        ag_ref[...] = y[:, 3 * D_ATT:].astype(BF16)

    row = lambda w: pl.BlockSpec((tm, w), lambda i: (i, 0))
    return pl.pallas_call(
        body, name="fwd_in", grid=(s // tm,),
        in_specs=[row(d), _full((1, d)), _full((d, n)), row(LANES), row(LANES), row(LANES)],
        out_specs=[row(d), row(3 * D_ATT), row(2 * D_CONV)] + [_strided_spec(dil, tm, 3 * D_ATT) for dil in dils],
        out_shape=[jax.ShapeDtypeStruct((s, d), BF16), jax.ShapeDtypeStruct((s, 3 * D_ATT), BF16),
                   jax.ShapeDtypeStruct((s, 2 * D_CONV), BF16)]
        + [_strided_shape(dil, s, 3 * D_ATT, BF16) for dil in dils],
        scratch_shapes=[_lane_scratch(tm, 3 * D_ATT)],
        compiler_params=_params("parallel"),
    )(x, g, w_in, *rot)


def _residues_per_step(dil):
    return 4 if dil % 4 == 0 else 1


def _win_in_specs(length, col, width, group):
    per = BQ // HALF
    last = length // HALF - 1
    return [
        pl.BlockSpec((group, HALF, width), lambda r, j: (r, jnp.maximum(j * per - 1, 0), col)),
        pl.BlockSpec((group, BQ, width), lambda r, j: (r, j, col)),
        pl.BlockSpec((group, HALF, width), lambda r, j: (r, jnp.minimum(j * per + per, last), col)),
    ]


def _fill_window(win, prev_ref, main_ref, next_ref):
    for g in range(win.shape[0]):
        win[g, 0:HALF] = prev_ref[g]
        win[g, HALF:HALF + BQ] = main_ref[g]
        win[g, HALF + BQ:] = next_ref[g]


def _band_bias():
    blk = jnp.arange(2 * BQ)[:, None] & (BQ - 1)
    win = jnp.arange(WIN)[None, :]
    return jnp.where(jnp.abs(win - HALF - blk) <= HALF, 0.0, NEG_INF).astype(F32)


def _window_bias(band_ref, j, length):
    pos = j * BQ - HALF + lax.broadcasted_iota(jnp.int32, (1, WIN), 1)
    return band_ref[...] + jnp.where((pos >= 0) & (pos < length), 0.0, NEG_INF)


def _first_head():
    return lax.broadcasted_iota(jnp.int32, (1, LANES), 1) < HEAD_DIM


def _stack_heads(v, first):
    zero = jnp.zeros((), v.dtype)
    return jnp.concatenate([jnp.where(first, v, zero), jnp.where(first, zero, v)], axis=0)


def _unstack_heads(v, first):
    rows = v.shape[0] // 2
    return jnp.where(first, v[:rows], v[rows:])


def _stack_cols(v, first):
    other = pltpu.roll(v, HEAD_DIM, 1)
    stacked = jnp.concatenate([jnp.where(first, v, other), jnp.where(first, other, v)], axis=0)
    return jnp.tile(stacked, (1, WIN // LANES))


def _swa_fwd(qkv3, band, name, sides=()):
    dil, length, _ = qkv3.shape
    scale = HEAD_DIM ** -0.5
    group = _residues_per_step(dil)

    def body(q_ref, kp, km, kn, vp, vm, vn, band_ref, o_ref, lse_ref, kwin, vwin):
        j = pl.program_id(1)
        _fill_window(kwin, kp, km, kn)
        _fill_window(vwin, vp, vm, vn)
        bias = _window_bias(band_ref, j, length)
        first = _first_head()
        pairs = [(g, slice(pr * LANES, (pr + 1) * LANES)) for g in range(group) for pr in range(D_ATT // LANES)]
        scores = [_dot_nt(_stack_heads(q_ref[g, :, cols] * scale, first), kwin[g, :, cols]) + bias for g, cols in pairs]
        stats, probs = [], []
        for sc in scores:
            m = jnp.max(sc, axis=-1, keepdims=True)
            p = jnp.exp(sc - m)
            stats.append((m, jnp.sum(p, axis=-1, keepdims=True)))
            probs.append(p.astype(BF16))
        for (g, cols), p, (m, den) in zip(pairs, probs, stats):
            pv = _dot(p, vwin[g, :, cols]) * (1.0 / den)
            o_ref[g, :, cols] = _unstack_heads(pv, first).astype(BF16)
            lse_ref[g, :, cols] = _unstack_heads(jnp.broadcast_to(m + jnp.log(den), (2 * BQ, LANES)), first)

    blk = pl.BlockSpec((group, BQ, D_ATT), lambda r, j: (r, j, 0))
    return _hosted_call(
        body, sides, name=name, grid=(dil // group, length // BQ),
        in_specs=[blk] + _win_in_specs(length, 1, D_ATT, group) + _win_in_specs(length, 2, D_ATT, group)
        + [_full((2 * BQ, WIN))],
        out_specs=[blk, blk],
        out_shape=[jax.ShapeDtypeStruct((dil, length, D_ATT), BF16),
                   jax.ShapeDtypeStruct((dil, length, D_ATT), F32)],
        scratch_shapes=[pltpu.VMEM((group, WIN, D_ATT), BF16)] * 2,
        args=[qkv3] * 7 + [band])


def _glu(v):
    return v[:, :D_CONV].astype(F32) * _sigmoid(v[:, D_CONV:].astype(F32))


def _halo_specs(tm, width, col=0):
    per = tm // BF16_ROWS
    return lambda nblk: [
        pl.BlockSpec((BF16_ROWS, width), lambda i: (jnp.maximum(i * per - 1, 0), col)),
        pl.BlockSpec((tm, width), lambda i: (i, col)),
        pl.BlockSpec((BF16_ROWS, width), lambda i: (jnp.minimum(i * per + per, nblk - 1), col)),
    ]


def _fill_halo(buf, i, ntiles, tm, prev, main, nxt):
    buf[0:BF16_ROWS] = jnp.where(i == 0, 0.0, prev)
    buf[BF16_ROWS:BF16_ROWS + tm] = main
    buf[BF16_ROWS + tm:] = jnp.where(i == ntiles - 1, 0.0, nxt)


def _halo_scratch(tm):
    return [pltpu.VMEM((tm + 2 * BF16_ROWS, D_CONV), F32),
            pltpu.VMEM((SUBLANES - 1, tm + 2 * BF16_ROWS - SUBLANES, D_CONV), F32)]


def _shift_copies(buf, shifted, tm):
    rows = tm + 2 * BF16_ROWS - SUBLANES
    for b in range(1, SUBLANES):
        shifted[b - 1] = buf[pl.ds(b, rows), :]


def _tap(buf, shifted, off, rows, base=0):
    a, b = divmod(off, SUBLANES)
    start = base + SUBLANES * a
    if not isinstance(start, int):
        start = pl.multiple_of(start, SUBLANES)
    if b == 0:
        return buf[pl.ds(start, rows), :]
    return shifted[b - 1, pl.ds(start, rows), :]


def _fwd_conv(ag, cw, cb, lg, lb, sides=()):
    s = ag.shape[0]
    tm = _row_tile(s)
    nt = s // tm

    def body(agp, agm, agn, cw_ref, cb_ref, lg_ref, lb_ref, c1_ref, co_ref, ubuf, ush):
        i = pl.program_id(0)
        _fill_halo(ubuf, i, nt, tm, _glu(agp[...]), _glu(agm[...]), _glu(agn[...]))
        _shift_copies(ubuf, ush, tm)
        w, cb = cw_ref[...], cb_ref[...]

        def chunk(c, carry):
            base = pl.multiple_of(c * CONV_ROWS, CONV_ROWS)
            acc = jnp.zeros((CONV_ROWS, D_CONV), F32)
            for k in range(CONV_WIDTH):
                acc = acc + _tap(ubuf, ush, k + 1, CONV_ROWS, base) * w[k:k + 1, :]
            c1_ref[pl.ds(base, CONV_ROWS), :] = acc + cb
            return carry

        lax.fori_loop(0, tm // CONV_ROWS, chunk, 0)
        c1 = c1_ref[...]
        xc = c1 - _mean(c1)
        ln = xc * lax.rsqrt(_mean(xc * xc) + EPS) * lg_ref[...] + lb_ref[...]
        co_ref[...] = (ln * _sigmoid(ln)).astype(BF16)

    vec = _full((1, D_CONV))
    return _hosted_call(
        body, sides, name="fwd_conv", grid=(nt,),
        in_specs=_halo_specs(tm, 2 * D_CONV)(s // BF16_ROWS) + [_full((32, D_CONV)), vec, vec, vec],
        out_specs=[pl.BlockSpec((tm, D_CONV), lambda i: (i, 0))] * 2,
        out_shape=[jax.ShapeDtypeStruct((s, D_CONV), F32), jax.ShapeDtypeStruct((s, D_CONV), BF16)],
        scratch_shapes=_halo_scratch(tm),
        args=[ag, ag, ag, cw, cb, lg, lb])


def _fwd_mix_out(outs, lses, conv_out, x, w_out):
    s, d = x.shape
    tm = _row_tile(s)
    dils = DILATIONS[1:]
    nd = len(dils)

    def body(o1, *rest):
        o_str, l1, l_str = rest[:nd], rest[nd], rest[nd + 1:2 * nd + 1]
        co, x_ref, w_ref, h_ref, cat_ref, lt_ref = rest[2 * nd + 1:2 * nd + 7]
        lt_str = rest[2 * nd + 7:3 * nd + 7]
        obufs, lbufs, ltbuf = rest[3 * nd + 7:4 * nd + 7], rest[4 * nd + 7:5 * nd + 7], rest[5 * nd + 7]
        for ref, buf in zip(o_str + l_str, obufs + lbufs):
            _read_strided(ref, buf)
        lse = [l1[0]] + [_load_blocks(buf) for buf in lbufs]
        out = [o1[0].astype(F32)] + [_load_blocks(buf) for buf in obufs]
        m = lse[0]
        for v in lse[1:]:
            m = jnp.maximum(m, v)
        e = [jnp.exp(v - m) for v in lse]
        den = sum(e[1:], e[0])
        att = (sum((ev * ov for ev, ov in zip(e[1:], out[1:])), e[0] * out[0]) / den).astype(BF16)
        lt = m + jnp.log(den)
        lt_ref[...] = lt
        _store_blocks(ltbuf, lt)
        for ref in lt_str:
            _write_strided(ltbuf, ref)
        cat_ref[:, :D_ATT] = att
        cat_ref[:, D_ATT:] = co[...]
        h_ref[...] = x_ref[...] + _dot(att, w_ref[:D_ATT, :]) + _dot(co[...], w_ref[D_ATT:, :])

    row = lambda w: pl.BlockSpec((tm, w), lambda i: (i, 0))
    nat = pl.BlockSpec((1, tm, D_ATT), lambda i: (0, i, 0))
    strided = [_strided_spec(dil, tm, D_ATT) for dil in dils]
    return pl.pallas_call(
        body, name="fwd_mix_out", grid=(s // tm,),
        in_specs=[nat] + strided + [nat] + strided + [row(D_ATT), row(d), _full((d, d))],
        out_specs=[row(d), row(d), row(D_ATT)] + strided,
        out_shape=[jax.ShapeDtypeStruct((s, d), F32), jax.ShapeDtypeStruct((s, d), BF16),
                   jax.ShapeDtypeStruct((s, D_ATT), F32)] + [_strided_shape(dil, s, D_ATT, F32) for dil in dils],
        scratch_shapes=[_lane_scratch(tm, D_ATT)] * (2 * nd + 1),
        compiler_params=_params("parallel"),
    )(*outs, *lses, conv_out, x, w_out)


def _fwd_mem(mem, g, wk, wv):
    m, d = mem.shape

    def body(mem_ref, g_ref, wk_ref, wv_ref, mn_ref, xk_ref, xv_ref):
        mn = _rms_fwd(mem_ref[...], g_ref[...])[0].astype(BF16)
        mn_ref[...] = mn
        xk_ref[...] = _dot(mn, wk_ref[...]).astype(BF16)
        xv_ref[...] = _dot(mn, wv_ref[...]).astype(BF16)

    return pl.pallas_call(
        body, name="fwd_mem",
        out_shape=[jax.ShapeDtypeStruct((m, d), BF16)] * 3,
        compiler_params=_params(),
    )(mem, g, wk, wv)


def _softmax(sc):
    p = jnp.exp(sc - jnp.max(sc, axis=-1, keepdims=True))
    return p / jnp.sum(p, axis=-1, keepdims=True)


def _fwd_xattn(h1, g, wq, xk, xv, wo):
    s, d = h1.shape
    m = xk.shape[0]
    tm = _row_tile(s)
    hd = d // XATT_HEADS

    def body(h_ref, g_ref, wq_ref, xk_ref, xv_ref, wo_ref, h2_ref, hn_ref, xq_ref, xo_ref):
        h = h_ref[...]
        hn = _rms_fwd(h, g_ref[...])[0].astype(BF16)
        hn_ref[...] = hn
        xq = _dot(hn, wq_ref[...]).astype(BF16)
        xq_ref[...] = xq
        heads = [slice(i * hd, (i + 1) * hd) for i in range(XATT_HEADS)]
        scores = [_dot_nt(xq[:, cols], xk_ref[:, cols]) * hd ** -0.5 for cols in heads]
        probs = [_softmax(sc).astype(BF16) for sc in scores]
        for cols, pr in zip(heads, probs):
            xo_ref[:, cols] = _dot(pr, xv_ref[:, cols]).astype(BF16)
        h2_ref[...] = h + _dot(xo_ref[...], wo_ref[...])

    row = pl.BlockSpec((tm, d), lambda i: (i, 0))
    return pl.pallas_call(
        body, name="fwd_xattn", grid=(s // tm,),
        in_specs=[row, _full((1, d)), _full((d, d)), _full((m, d)), _full((m, d)), _full((d, d))],
        out_specs=[row] * 4,
        out_shape=[jax.ShapeDtypeStruct((s, d), F32)] + [jax.ShapeDtypeStruct((s, d), BF16)] * 3,
        compiler_params=_params("parallel"),
    )(h1, g, wq, xk, xv, wo)


def _fwd_mlp_loss(h2, g, w_up, w_down, gf, target):
    s, d = h2.shape
    nsh, _, f = w_up.shape
    per = nsh
    fb = per * f
    nb = nsh // per
    tm = _row_tile(s)
    once = pl.Buffered(1)

    def body(h_ref, g_ref, wu_ref, wd_ref, gf_ref, t_ref,
             hn_ref, act_ref, dh_ref, dhb_ref, loss_ref, ggf_ref, acc):
        i, k = pl.program_id(0), pl.program_id(1)

        @pl.when(k == 0)
        def _():
            hn_ref[...] = _rms_fwd(h_ref[...], g_ref[...])[0].astype(BF16)
            acc[...] = jnp.zeros_like(acc)

        @pl.when((i == 0) & (k == 0))
        def _():
            loss_ref[...] = jnp.zeros_like(loss_ref)
            ggf_ref[...] = jnp.zeros_like(ggf_ref)

        hn = hn_ref[...]
        ups = [_dot(hn, wu_ref[c]) for c in range(per)]
        for c, u in enumerate(ups):
            act_ref[:, c * f:(c + 1) * f] = jnp.square(jnp.maximum(u, 0.0)).astype(BF16)
        acc[...] += _dot(act_ref[...], wd_ref[...])

        @pl.when(k == nb - 1)
        def _():
            h3 = h_ref[...] + acc[...]
            gfv = gf_ref[...]
            y, _ = _rms_fwd(h3, gfv)
            err = y - t_ref[...]
            loss_ref[...] += 0.5 * jnp.sum(_mean(err * err))
            dh3, gg = _rms_bwd(h3, gfv, err * (1.0 / d))
            ggf_ref[...] += jnp.sum(gg, axis=0, keepdims=True)
            dh_ref[...] = dh3
            dhb_ref[...] = dh3.astype(BF16)

    row = pl.BlockSpec((tm, d), lambda i, k: (i, 0))
    return pl.pallas_call(
        body, name="fwd_mlp_loss", grid=(s // tm, nb),
        in_specs=[row, _full((1, d)),
                  pl.BlockSpec((per, d, f), lambda i, k: (k, 0, 0), pipeline_mode=once),
                  pl.BlockSpec((fb, d), lambda i, k: (k, 0), pipeline_mode=once),
                  _full((1, d)), row],
        out_specs=[row, pl.BlockSpec((tm, fb), lambda i, k: (i, k)), row, row,
                   _full((1, LANES)), _full((1, d))],
        out_shape=[jax.ShapeDtypeStruct((s, d), BF16), jax.ShapeDtypeStruct((s, nsh * f), BF16),
                   jax.ShapeDtypeStruct((s, d), F32), jax.ShapeDtypeStruct((s, d), BF16),
                   jax.ShapeDtypeStruct((1, LANES), F32), jax.ShapeDtypeStruct((1, d), F32)],
        scratch_shapes=[pltpu.VMEM((tm, d), F32)],
        compiler_params=_params("arbitrary", "arbitrary"),
    )(h2, g, w_up, w_down, gf, target)


def _bwd_mlp(dh3, dh3b, act, w_up_t, w_down, h2, g):
    s, d = h2.shape
    ff = w_down.shape[0]
    per = N_DEV
    fb = per * (ff // N_DEV)
    nb = ff // fb
    tm = _row_tile(s)
    once = pl.Buffered(1)

    def body(dh_ref, dhb_ref, act_ref, wut_ref, wd_ref, h_ref, g_ref,
             du_ref, dh2_ref, dh2b_ref, gg_ref, acc):
        i, k = pl.program_id(0), pl.program_id(1)

        @pl.when(k == 0)
        def _():
            acc[...] = jnp.zeros_like(acc)

        @pl.when((i == 0) & (k == 0))
        def _():
            gg_ref[...] = jnp.zeros_like(gg_ref)

        dhb = dhb_ref[...]
        f = fb // per
        shards = [slice(c * f, (c + 1) * f) for c in range(per)]
        dacts = [_dot_nt(dhb, wd_ref[cols, :]) for cols in shards]
        for cols, dact in zip(shards, dacts):
            du_ref[:, cols] = (dact * (2.0 * jnp.sqrt(act_ref[:, cols].astype(F32)))).astype(BF16)
        acc[...] += _dot(du_ref[...], wut_ref[...])

        @pl.when(k == nb - 1)
        def _():
            dh, gg = _rms_bwd(h_ref[...], g_ref[...], acc[...])
            gg_ref[...] += jnp.sum(gg, axis=0, keepdims=True)
            dh2 = dh_ref[...] + dh
            dh2_ref[...] = dh2
            dh2b_ref[...] = dh2.astype(BF16)

    row = pl.BlockSpec((tm, d), lambda i, k: (i, 0))
    col = pl.BlockSpec((tm, fb), lambda i, k: (i, k))
    wblk = pl.BlockSpec((fb, d), lambda i, k: (k, 0), pipeline_mode=once)
    return pl.pallas_call(
        body, name="bwd_mlp", grid=(s // tm, nb),
        in_specs=[row, row, col, wblk, wblk, row, _full((1, d))],
        out_specs=[col, row, row, _full((1, d))],
        out_shape=[jax.ShapeDtypeStruct((s, ff), BF16), jax.ShapeDtypeStruct((s, d), F32),
                   jax.ShapeDtypeStruct((s, d), BF16), jax.ShapeDtypeStruct((1, d), F32)],
        scratch_shapes=[pltpu.VMEM((tm, d), F32)],
        compiler_params=_params("arbitrary", "arbitrary"),
    )(dh3, dh3b, act, w_up_t, w_down, h2, g)


def _bwd_xattn(dh2, dh2b, h1, g, xq, xk, xv, wq, wo, sides=()):
    s, d = h1.shape
    m = xk.shape[0]
    tm = _row_tile(s)
    hd = d // XATT_HEADS
    scale = hd ** -0.5

    def body(dh_ref, dhb_ref, h_ref, g_ref, xq_ref, xk_ref, xv_ref, wq_ref, wo_ref,
             dh1_ref, dh1b_ref, dxq_ref, dxk_ref, dxv_ref, gg_ref):
        @pl.when(pl.program_id(0) == 0)
        def _():
            dxk_ref[...] = jnp.zeros_like(dxk_ref)
            dxv_ref[...] = jnp.zeros_like(dxv_ref)
            gg_ref[...] = jnp.zeros_like(gg_ref)

        dxo = _dot_nt(dhb_ref[...], wo_ref[...])
        heads = [slice(i * hd, (i + 1) * hd) for i in range(XATT_HEADS)]
        dxos = [dxo[:, cols].astype(BF16) for cols in heads]
        scores = [_dot_nt(xq_ref[:, cols], xk_ref[:, cols]) * scale for cols in heads]
        dprs = [_dot_nt(dxo_h, xv_ref[:, cols]) for dxo_h, cols in zip(dxos, heads)]
        probs, dscs = [], []
        for sc, dpr in zip(scores, dprs):
            pr = _softmax(sc)
            dscs.append((pr * (dpr - jnp.sum(dpr * pr, axis=-1, keepdims=True)) * scale).astype(BF16))
            probs.append(pr.astype(BF16))
        for cols, dsc, pr, dxo_h in zip(heads, dscs, probs, dxos):
            dxq_ref[:, cols] = _dot(dsc, xk_ref[:, cols]).astype(BF16)
            dxk_ref[:, cols] += _dot_tn(dsc, xq_ref[:, cols])
            dxv_ref[:, cols] += _dot_tn(pr, dxo_h)
        dh, gg = _rms_bwd(h_ref[...], g_ref[...], _dot_nt(dxq_ref[...], wq_ref[...]))
        gg_ref[...] += jnp.sum(gg, axis=0, keepdims=True)
        dh1 = dh_ref[...] + dh
        dh1_ref[...] = dh1
        dh1b_ref[...] = dh1.astype(BF16)

    row = pl.BlockSpec((tm, d), lambda i: (i, 0))
    return _hosted_call(
        body, sides, name="bwd_xattn", grid=(s // tm,),
        in_specs=[row, row, row, _full((1, d)), row, _full((m, d)), _full((m, d)), _full((d, d)), _full((d, d))],
        out_specs=[row, row, row, _full((m, d)), _full((m, d)), _full((1, d))],
        out_shape=[jax.ShapeDtypeStruct((s, d), F32), jax.ShapeDtypeStruct((s, d), BF16),
                   jax.ShapeDtypeStruct((s, d), BF16), jax.ShapeDtypeStruct((m, d), F32),
                   jax.ShapeDtypeStruct((m, d), F32), jax.ShapeDtypeStruct((1, d), F32)],
        scratch_shapes=[],
        args=[dh2, dh2b, h1, g, xq, xk, xv, wq, wo])


def _bwd_mem(mem, g, mn, dxk, dxv, wk, wv):
    m, d = mem.shape

    def body(mem_ref, g_ref, mn_ref, dxk_ref, dxv_ref, wk_ref, wv_ref, gk_ref, gv_ref, gg_ref):
        dk, dv = dxk_ref[...].astype(BF16), dxv_ref[...].astype(BF16)
        gk_ref[...] = _dot_tn(mn_ref[...], dk).astype(BF16)
        gv_ref[...] = _dot_tn(mn_ref[...], dv).astype(BF16)
        dmn = _dot_nt(dk, wk_ref[...]) + _dot_nt(dv, wv_ref[...])
        _, gg = _rms_bwd(mem_ref[...], g_ref[...], dmn)
        gg_ref[...] = jnp.sum(gg, axis=0, keepdims=True)

    return pl.pallas_call(
        body, name="bwd_mem",
        out_shape=[jax.ShapeDtypeStruct((d, d), BF16), jax.ShapeDtypeStruct((d, d), BF16),
                   jax.ShapeDtypeStruct((1, d), F32)],
        compiler_params=_params(),
    )(mem, g, mn, dxk, dxv, wk, wv)


def _bwd_mix_out(dh1b, w_out, cat, head_ones):
    s, d = dh1b.shape
    tm = _row_tile(s)
    dils = DILATIONS[1:]
    nd = len(dils)

    def body(dh_ref, w_ref, cat_ref, ones_ref, dcat_ref, dsum_ref, *rest):
        da_str, ds_str, dbuf, sbuf = rest[:nd], rest[nd:2 * nd], rest[2 * nd], rest[2 * nd + 1]
        dcat = _dot_nt(dh_ref[...], w_ref[...])
        dcat_ref[...] = dcat.astype(BF16)
        datt = dcat[:, :D_ATT]
        prod = datt * cat_ref[...].astype(F32)
        hi = prod.astype(BF16)
        lo = (prod - hi.astype(F32)).astype(BF16)
        dsum = _dot(hi, ones_ref[...]) + _dot(lo, ones_ref[...])
        dsum_ref[...] = dsum
        _store_blocks(dbuf, datt)
        _store_blocks(sbuf, dsum)
        for da_ref, ds_ref in zip(da_str, ds_str):
            _write_strided(dbuf, da_ref)
            _write_strided(sbuf, ds_ref)

    row = lambda w: pl.BlockSpec((tm, w), lambda i: (i, 0))
    strided = [_strided_spec(dil, tm, D_ATT) for dil in dils]
    return pl.pallas_call(
        body, name="bwd_mix_out", grid=(s // tm,),
        in_specs=[row(d), _full((d, d)), row(D_ATT), _full((D_ATT, D_ATT))],
        out_specs=[row(d), row(D_ATT)] + strided + strided,
        out_shape=[jax.ShapeDtypeStruct((s, d), BF16), jax.ShapeDtypeStruct((s, D_ATT), F32)]
        + [_strided_shape(dil, s, D_ATT, BF16) for dil in dils]
        + [_strided_shape(dil, s, D_ATT, F32) for dil in dils],
        scratch_shapes=[_lane_scratch(tm, D_ATT)] * 2,
        compiler_params=_params("parallel"),
    )(dh1b, w_out, cat, head_ones)


def _bwd_conv(dcat, c1, ag, cw, lg, lb, sides=()):
    s = ag.shape[0]
    tm = _row_tile(s)
    nt = s // tm

    def body(dp, dm, dn, cp, cm, cn, agp, agm, agn, cw_ref, lg_ref, lb_ref,
             dag_ref, gcw_ref, gcb_ref, glg_ref, glb_ref, ubuf, ush, dbuf, dsh, gacc):
        i = pl.program_id(0)

        @pl.when(i == 0)
        def _():
            gacc[...] = jnp.zeros_like(gacc)
            gcb_ref[...] = jnp.zeros_like(gcb_ref)
            glg_ref[...] = jnp.zeros_like(glg_ref)
            glb_ref[...] = jnp.zeros_like(glb_ref)

        lgv, lbv = lg_ref[...], lb_ref[...]

        def norm_bwd(dco, c1v):
            xc = c1v - _mean(c1v)
            rs = lax.rsqrt(_mean(xc * xc) + EPS)
            z = xc * rs
            ln = z * lgv + lbv
            sg = _sigmoid(ln)
            dln = dco.astype(F32) * (sg * (1.0 + ln * (1.0 - sg)))
            dz = dln * lgv
            return rs * (dz - _mean(dz) - z * _mean(dz * z)), dln, z

        def sublane_sums(v):
            out = v[0:SUBLANES]
            for r in range(SUBLANES, CONV_ROWS, SUBLANES):
                out = out + v[r:r + SUBLANES]
            return out

        dc_m, dln, z = norm_bwd(dm[...], cm[...])
        glg_ref[...] += jnp.sum(dln * z, axis=0, keepdims=True)
        glb_ref[...] += jnp.sum(dln, axis=0, keepdims=True)
        gcb_ref[...] += jnp.sum(dc_m, axis=0, keepdims=True)
        _fill_halo(dbuf, i, nt, tm, norm_bwd(dp[...], cp[...])[0], dc_m, norm_bwd(dn[...], cn[...])[0])
        _fill_halo(ubuf, i, nt, tm, _glu(agp[...]), _glu(agm[...]), _glu(agn[...]))
        _shift_copies(dbuf, dsh, tm)
        _shift_copies(ubuf, ush, tm)

        w = cw_ref[...]

        def chunk(c, carry):
            base = pl.multiple_of(c * CONV_ROWS, CONV_ROWS)
            rows = pl.ds(base, CONV_ROWS)
            dc = dbuf[pl.ds(pl.multiple_of(base + BF16_ROWS, SUBLANES), CONV_ROWS), :]
            du = jnp.zeros((CONV_ROWS, D_CONV), F32)
            for k in range(CONV_WIDTH):
                du = du + _tap(dbuf, dsh, CONV_WIDTH - k, CONV_ROWS, base) * w[k:k + 1, :]
                gacc[k] += sublane_sums(dc * _tap(ubuf, ush, k + 1, CONV_ROWS, base))
            a = agm[rows, :D_CONV].astype(F32)
            sg = _sigmoid(agm[rows, D_CONV:].astype(F32))
            dag_ref[rows, :D_CONV] = (du * sg).astype(BF16)
            dag_ref[rows, D_CONV:] = (du * a * sg * (1.0 - sg)).astype(BF16)
            return carry

        lax.fori_loop(0, tm // CONV_ROWS, chunk, 0)

        @pl.when(i == nt - 1)
        def _():
            tap = lax.broadcasted_iota(jnp.int32, (32, D_CONV), 0)
            gcw = jnp.zeros((32, D_CONV), F32)
            for k in range(CONV_WIDTH):
                gcw = jnp.where(tap == k, jnp.sum(gacc[k], axis=0, keepdims=True), gcw)
            gcw_ref[...] = gcw

    vec = _full((1, D_CONV))
    nblk = s // BF16_ROWS
    return _hosted_call(
        body, sides, name="bwd_conv", grid=(nt,),
        in_specs=_halo_specs(tm, D_CONV, 1)(nblk) + _halo_specs(tm, D_CONV)(nblk) + _halo_specs(tm, 2 * D_CONV)(nblk)
        + [_full((32, D_CONV)), vec, vec],
        out_specs=[pl.BlockSpec((tm, 2 * D_CONV), lambda i: (i, 0)), _full((32, D_CONV)), vec, vec, vec],
        out_shape=[jax.ShapeDtypeStruct((s, 2 * D_CONV), BF16), jax.ShapeDtypeStruct((32, D_CONV), F32)]
        + [jax.ShapeDtypeStruct((1, D_CONV), F32)] * 3,
        scratch_shapes=_halo_scratch(tm) + _halo_scratch(tm) + [pltpu.VMEM((32, SUBLANES, D_CONV), F32)],
        args=[dcat, dcat, dcat, c1, c1, c1, ag, ag, ag, cw, lg, lb])


def _swa_bwd(qkv3, do3, lt3, ds3, band, name, sides=()):
    dil, length, _ = qkv3.shape
    nb = length // BQ
    scale = HEAD_DIM ** -0.5
    group = _residues_per_step(dil)
    assert WIN == 2 * BQ and BQ == 2 * HALF

    def body(q_ref, kp, km, kn, vp, vm, vn, do_ref, l_ref, s_ref, band_ref, dq_ref, dkv_ref, kwin, vwin, pend, keep):
        j = pl.program_id(1)

        @pl.when(j == 0)
        def _():
            pend[...] = jnp.zeros_like(pend)
            keep[...] = jnp.zeros_like(keep)

        @pl.when(j < nb)
        def _():
            _fill_window(kwin, kp, km, kn)
            _fill_window(vwin, vp, vm, vn)
            bias = _window_bias(band_ref, j, length)
            first = _first_head()
            pairs = [(g, slice(pr * LANES, (pr + 1) * LANES)) for g in range(group) for pr in range(D_ATT // LANES)]
            qs = [_stack_heads(q_ref[g, :, cols] * scale, first) for g, cols in pairs]
            dos = [_stack_heads(do_ref[g, :, cols], first) for g, cols in pairs]
            scores = [_dot_nt(q, kwin[g, :, cols]) for q, (g, cols) in zip(qs, pairs)]
            dps = [_dot_nt(do, vwin[g, :, cols]) for do, (g, cols) in zip(dos, pairs)]
            probs, dscs = [], []
            for (g, cols), sc, dp in zip(pairs, scores, dps):
                p = jnp.exp(sc + (bias - _stack_cols(l_ref[g, :, cols], first)))
                dscs.append((p * (dp - _stack_cols(s_ref[g, :, cols], first))).astype(BF16))
                probs.append(p.astype(BF16))
            dqs = [_dot(dsc, kwin[g, :, cols] * scale) for dsc, (g, cols) in zip(dscs, pairs)]
            dks = [_dot_tn(dsc, q) for dsc, q in zip(dscs, qs)]
            dvs = [_dot_tn(p, do) for p, do in zip(probs, dos)]
            for (g, cols), dq, dk, dv in zip(pairs, dqs, dks, dvs):
                dq_ref[g, :, cols] = _unstack_heads(dq, first).astype(BF16)
                for part, at in ((dk, cols), (dv, slice(D_ATT + cols.start, D_ATT + cols.stop))):
                    dkv_ref[g, :HALF, at] = keep[g, :, at].astype(BF16)
                    dkv_ref[g, HALF:, at] = (pend[g, :HALF, at] + part[:HALF]).astype(BF16)
                    keep[g, :, at] = pend[g, HALF:, at] + part[HALF:BQ]
                    pend[g, :, at] = part[BQ:]

        @pl.when(j == nb)
        def _():
            dkv_ref[:, :HALF] = keep[...].astype(BF16)
            dkv_ref[:, HALF:] = pend[:, :HALF].astype(BF16)

    def clamp(idx):
        return lambda r, j: idx(r, jnp.minimum(j, nb - 1))

    main = pl.BlockSpec((group, BQ, D_ATT), clamp(lambda r, j: (r, j, 0)))
    wins = [pl.BlockSpec(sp.block_shape, clamp(sp.index_map))
            for c in (1, 2) for sp in _win_in_specs(length, c, D_ATT, group)]
    return _hosted_call(
        body, sides, name=name, grid=(dil // group, nb + 1),
        in_specs=[main] + wins + [main] * 3 + [_full((2 * BQ, WIN))],
        out_specs=[main, pl.BlockSpec((group, BQ, 2 * D_ATT), lambda r, j: (r, jnp.maximum(j - 1, 0), 0))],
        out_shape=[jax.ShapeDtypeStruct((dil, length, D_ATT), BF16),
                   jax.ShapeDtypeStruct((dil, length, 2 * D_ATT), BF16)],
        scratch_shapes=[pltpu.VMEM((group, WIN, D_ATT), BF16)] * 2
        + [pltpu.VMEM((group, BQ, 2 * D_ATT), F32), pltpu.VMEM((group, HALF, 2 * D_ATT), F32)],
        args=[qkv3] * 7 + [do3, lt3, ds3, band])


def _bwd_in(dqs, dkvs, dag, w_in, x, g, dh1, rot):
    s, d = x.shape
    n = w_in.shape[1]
    tm = _row_tile(s)
    dils = DILATIONS[1:]
    nd = len(dils)

    def body(q1, *rest):
        q_str, kv1, kv_str = rest[:nd], rest[nd], rest[nd + 1:2 * nd + 1]
        dag_ref, w_ref, x_ref, g_ref, dh_ref, c_ref, a_ref, b_ref, gx_ref, dy_ref, gg_ref, qbuf, kvbuf = rest[2 * nd + 1:]

        @pl.when(pl.program_id(0) == 0)
        def _():
            gg_ref[...] = jnp.zeros_like(gg_ref)

        _store_blocks(qbuf, q1[0].astype(F32))
        _store_blocks(kvbuf, kv1[0].astype(F32))
        for ref in q_str:
            _read_strided(ref, qbuf, add=True)
        for ref in kv_str:
            _read_strided(ref, kvbuf, add=True)
        dq, dkv = _load_blocks(qbuf), _load_blocks(kvbuf)
        reps = (1, D_ATT // LANES)
        cc, aa, bb = jnp.tile(c_ref[...], reps), jnp.tile(a_ref[...], reps), jnp.tile(b_ref[...], reps)
        for blk, t in enumerate((dq, dkv[:, :D_ATT])):
            dt = t * cc + pltpu.roll(t * aa, ROT_DIM // 2, 1) + pltpu.roll(t * bb, D_ATT - ROT_DIM // 2, 1)
            dy_ref[:, blk * D_ATT:(blk + 1) * D_ATT] = dt.astype(BF16)
        dy_ref[:, 2 * D_ATT:3 * D_ATT] = dkv[:, D_ATT:].astype(BF16)
        dy_ref[:, 3 * D_ATT:] = dag_ref[...]
        dx, gg = _rms_bwd(x_ref[...], g_ref[...], _dot_nt(dy_ref[...], w_ref[...]))
        gg_ref[...] += jnp.sum(gg, axis=0, keepdims=True)
        gx_ref[...] = dh_ref[...] + dx

    row = lambda w: pl.BlockSpec((tm, w), lambda i: (i, 0))
    def strided(width):
        return [pl.BlockSpec((1, tm, width), lambda i: (0, i, 0))] + [_strided_spec(dil, tm, width) for dil in dils]

    return pl.pallas_call(
        body, name="bwd_in", grid=(s // tm,),
        in_specs=strided(D_ATT) + strided(2 * D_ATT)
        + [row(2 * D_CONV), _full((d, n)), row(d), _full((1, d)), row(d)] + [row(LANES)] * 3,
        out_specs=[row(d), row(n), _full((1, d))],
        out_shape=[jax.ShapeDtypeStruct((s, d), F32), jax.ShapeDtypeStruct((s, n), BF16),
                   jax.ShapeDtypeStruct((1, d), F32)],
        scratch_shapes=[_lane_scratch(tm, D_ATT), _lane_scratch(tm, 2 * D_ATT)],
        compiler_params=_params("arbitrary"),
    )(*dqs, *dkvs, dag, w_in, x, g, dh1, *rot)


def _wgrad(a, b, name, a_blk=None, b_blk=None, stack=None, tm=1024):
    s, ka = a.shape
    nb = b.shape[1]
    a_blk, b_blk = a_blk or ka, b_blk or nb
    na, nbl = ka // a_blk, nb // b_blk
    assert na == 1 or nbl == 1
    tm = min(tm, s)
    nt = s // tm
    per = b_blk // stack if stack else 0

    def body(a_ref, b_ref, o_ref, acc):
        t = pl.program_id(1)

        @pl.when(t == 0)
        def _():
            acc[...] = jnp.zeros_like(acc)

        acc[...] += _dot_tn(a_ref[...], b_ref[...])

        @pl.when(t == nt - 1)
        def _():
            if stack:
                for c in range(per):
                    o_ref[c] = acc[:, c * stack:(c + 1) * stack].astype(BF16)
            else:
                o_ref[...] = acc[...].astype(BF16)

    if stack:
        out_spec = pl.BlockSpec((per, ka, stack), lambda k, t: (k, 0, 0))
        out_shape = jax.ShapeDtypeStruct((nb // stack, ka, stack), BF16)
    elif na > 1:
        out_spec = pl.BlockSpec((a_blk, nb), lambda k, t: (k, 0))
        out_shape = jax.ShapeDtypeStruct((ka, nb), BF16)
    else:
        out_spec = pl.BlockSpec((ka, b_blk), lambda k, t: (0, k))
        out_shape = jax.ShapeDtypeStruct((ka, nb), BF16)
    return pl.pallas_call(
        body, name=name, grid=(na * nbl, nt),
        in_specs=[pl.BlockSpec((tm, a_blk), (lambda k, t: (t, k)) if na > 1 else (lambda k, t: (t, 0))),
                  pl.BlockSpec((tm, b_blk), (lambda k, t: (t, k)) if nbl > 1 else (lambda k, t: (t, 0)))],
        out_specs=out_spec, out_shape=out_shape,
        scratch_shapes=[pltpu.VMEM((a_blk, b_blk), F32)],
        compiler_params=_params("parallel", "arbitrary"),
    )(a, b)


def _adamw(w, gsrc, m, v, name, transposed=False):
    summed = gsrc.ndim == w.ndim + 1
    rows, cols = w.shape
    assert gsrc.shape[-2:] == ((cols, rows) if transposed else (rows, cols)) and (summed or not transposed)
    tr = rows if rows <= 256 else 256
    assert rows % tr == 0
    c1 = 1.0 - ADAM_B1 ** ADAM_STEP
    c2 = 1.0 - ADAM_B2 ** ADAM_STEP

    def body(w_ref, g_ref, m_ref, v_ref, go_ref, d_ref, mo_ref, vo_ref):
        if summed:
            g = g_ref[0].astype(F32)
            for i in range(1, N_DEV):
                g = g + g_ref[i].astype(F32)
            if transposed:
                g = g.T
        else:
            g = g_ref[...]
        mn = ADAM_B1 * m_ref[...] + (1.0 - ADAM_B1) * g
        vn = ADAM_B2 * v_ref[...] + (1.0 - ADAM_B2) * jnp.square(g)
        go_ref[...] = g
        mo_ref[...] = mn
        vo_ref[...] = vn
        d_ref[...] = -ADAM_LR * ((mn / c1) / (jnp.sqrt(vn / c2) + ADAM_EPS) + ADAM_WD * w_ref[...])

    blk = pl.BlockSpec((tr, cols), lambda i: (i, 0))
    if transposed:
        gblk = pl.BlockSpec((N_DEV, cols, tr), lambda i: (0, 0, i))
    else:
        gblk = pl.BlockSpec((N_DEV, tr, cols), lambda i: (0, i, 0)) if summed else blk
    return pl.pallas_call(
        body, name=name, grid=(rows // tr,),
        in_specs=[blk, gblk, blk, blk], out_specs=[blk] * 4,
        out_shape=[jax.ShapeDtypeStruct(w.shape, F32)] * 4,
        compiler_params=_params("parallel"),
    )(w, gsrc, m, v)


def _sum_slots(g, name):
    _, rows, cols = g.shape

    def body(g_ref, o_ref):
        acc = g_ref[0]
        for i in range(1, N_DEV):
            acc = acc + g_ref[i]
        o_ref[...] = acc

    return pl.pallas_call(body, name=name, out_shape=jax.ShapeDtypeStruct((rows, cols), F32),
                          compiler_params=_params())(g)


def kernel(x, mem, norm_mix_g, w_in, conv_w, conv_b, conv_ln_g, conv_ln_b, w_out, norm_x_g, norm_mem_g, w_xq, w_xk, w_xv, w_xo, norm_mlp_g, w_up, w_down, norm_final_g, loss_target, m_norm_mix_g, m_w_in, m_conv_w, m_conv_b, m_conv_ln_g, m_conv_ln_b, m_w_out, m_norm_x_g, m_norm_mem_g, m_w_xq, m_w_xk, m_w_xv, m_w_xo, m_norm_mlp_g, m_w_up, m_w_down, m_norm_final_g, v_norm_mix_g, v_w_in, v_conv_w, v_conv_b, v_conv_ln_g, v_conv_ln_b, v_w_out, v_norm_x_g, v_norm_mem_g, v_w_xq, v_w_xk, v_w_xv, v_w_xo, v_norm_mlp_g, v_w_up, v_w_down, v_norm_final_g):
    x2, mem2, tgt = x[0], mem[0], loss_target[0]
    s, d = x2.shape
    gf = norm_final_g[None, :]

    cw_local = jnp.pad(conv_w[0], ((0, 1), (0, LANES - conv_w.shape[2])))
    win_g, cw_g = _exchange_call([_Exchange([w_in[0].astype(BF16), cw_local], gather=True)], "gather_w_in")
    w_in_f = jnp.transpose(win_g, (1, 0, 2)).reshape(d, -1)
    cw_f = jnp.transpose(cw_g[:, :, :conv_w.shape[2]], (1, 0, 2)).reshape(32, D_CONV)
    def stacked(*ws):
        return jnp.concatenate([w[0].astype(BF16) for w in ws], axis=0)

    def unstacked(g, *ws):
        offs = [0]
        for w in ws:
            offs.append(offs[-1] + w.shape[1])
        return [g[:, a:b, :].reshape(N_DEV * (b - a), d) for a, b in zip(offs, offs[1:])]

    late = [_Exchange([stacked(w_out, w_xq)], gather=True), _Exchange([w_up[0].astype(BF16)], gather=True),
            _Exchange([w_down[0].astype(BF16)], gather=True)]

    rot = _rotary_tables(s)
    band = _band_bias()
    xn, qkv, ag, *qkv_strided = _fwd_in(x2, norm_mix_g, w_in_f, rot)
    qkv3 = [qkv[None]] + qkv_strided
    outs, lses, gathered = [], [], []
    for dil, q3, side in zip(DILATIONS, qkv3, late):
        (o3, l3), got = _swa_fwd(q3, band, f"swa_fwd_d{dil}", [side])
        outs.append(o3)
        lses.append(l3)
        gathered += got
    rows_g, wup_g, wdown_g = gathered
    w_out_f, w_xq_f = unstacked(rows_g, w_out, w_xq)
    w_down_f = wdown_g.reshape(-1, d)
    w_up_t = jnp.swapaxes(wup_g, 1, 2).reshape(-1, d)
    (c1, conv_out), (rows_g,) = _fwd_conv(ag, cw_f, conv_b, conv_ln_g, conv_ln_b,
                                          [_Exchange([stacked(w_xk, w_xv, w_xo)], gather=True)])
    w_xk_f, w_xv_f, w_xo_f = unstacked(rows_g, w_xk, w_xv, w_xo)
    h1, cat, ltot, *lt_strided = _fwd_mix_out(outs, lses, conv_out, x2, w_out_f)
    mn, xk, xv = _fwd_mem(mem2, norm_mem_g, w_xk_f, w_xv_f)
    h2, hn2, xq, xo = _fwd_xattn(h1, norm_x_g, w_xq_f, xk, xv, w_xo_f)
    hn3, act, dh3, dh3b, loss_part, g_final = _fwd_mlp_loss(h2, norm_mlp_g, wup_g, w_down_f, gf, tgt)

    def scatter(*grads):
        return _Exchange([g.reshape(N_DEV, -1, g.shape[-1]) for g in grads], gather=False)

    f_blk = w_up.shape[2]
    du, dh2, dh2b, g_mlp = _bwd_mlp(dh3, dh3b, act, w_up_t, w_down_f, h2, norm_mlp_g)
    gw_up = _wgrad(hn3, du, "wgrad_up", b_blk=4 * f_blk, stack=f_blk)
    gw_down = _wgrad(dh3b, act, "wgrad_down", b_blk=4 * f_blk, stack=f_blk)
    (dh1, dh1b, dxq, dxk, dxv, g_x), (r_up,) = _bwd_xattn(
        dh2, dh2b, h1, norm_x_g, xq, xk, xv, w_xq_f, w_xo_f, [scatter(gw_up)])
    gw_xq = _wgrad(hn2, dxq, "wgrad_xq")
    gw_xo = _wgrad(xo, dh2b, "wgrad_xo")
    gw_xk, gw_xv, g_mem = _bwd_mem(mem2, norm_mem_g, mn, dxk, dxv, w_xk_f, w_xv_f)
    head = jnp.arange(D_ATT) // HEAD_DIM
    head_ones = (head[:, None] == head[None, :]).astype(BF16)
    dcat, dsum, *strided = _bwd_mix_out(dh1b, w_out_f, cat, head_ones)
    n_str = len(DILATIONS) - 1
    do3, lt3, ds3 = [dcat[None]] + strided[:n_str], [ltot[None]] + lt_strided, [dsum[None]] + strided[n_str:]
    gw_out = _wgrad(cat, dh1b, "wgrad_out")
    (dag, g_cw, g_cb, g_lg, g_lb), (r_down,) = _bwd_conv(dcat, c1, ag, cw_f, conv_ln_g, conv_ln_b, [scatter(gw_down)])
    hosted = [[scatter(gw_out, gw_xq)], [scatter(gw_xk, gw_xv)], [scatter(gw_xo)]]
    dqs, dkvs, landed = [], [], []
    for i, dil in enumerate(DILATIONS):
        (dq3, dkv3), got = _swa_bwd(qkv3[i], do3[i], lt3[i], ds3[i], band, f"swa_bwd_d{dil}", hosted[i])
        dqs.append(dq3)
        dkvs.append(dkv3)
        landed += got
    r_out, r_xq, r_xk, r_xv, r_xo = landed
    grad_x, dy, g_mix = _bwd_in(dqs, dkvs, dag, w_in_f, x2, norm_mix_g, dh1, rot)
    gw_in = _wgrad(xn, dy, "wgrad_in", b_blk=dy.shape[1] // 2)

    def widen(t):
        return jnp.pad(t, ((0, 0), (0, d - t.shape[1])))

    n_in = w_in.shape[2]
    small = jnp.concatenate([g_mix, g_x, g_mem, g_mlp, g_final, widen(g_cb), widen(g_lg), widen(g_lb),
                             g_cw.reshape(16, d), widen(loss_part), jnp.zeros((7, d), F32)], axis=0)
    r_in, small_g = _exchange_call(
        [_Exchange([jnp.transpose(gw_in.reshape(d, N_DEV, n_in), (1, 0, 2))], gather=False),
         _Exchange([small], gather=True)], "scatter_w_in_gather_small")
    small_sum = _sum_slots(small_g, "sum_small_grads")
    loss = small_sum[24, 0]

    res = {}

    def step(name, w, gsrc, m, v, transposed=False):
        shape = w.shape
        w2, m2, v2 = (t.reshape(-1, shape[-1]) for t in (w, m, v))
        res[name] = [t.reshape(shape) for t in _adamw(w2, gsrc, m2, v2, "adamw_" + name, transposed)]

    step("w_in", w_in, r_in, m_w_in, v_w_in)
    step("w_up", w_up, r_up, m_w_up, v_w_up)
    step("w_out", w_out, r_out, m_w_out, v_w_out)
    step("w_xq", w_xq, r_xq, m_w_xq, v_w_xq)
    step("w_xk", w_xk, r_xk, m_w_xk, v_w_xk)
    step("w_xv", w_xv, r_xv, m_w_xv, v_w_xv)
    step("w_xo", w_xo, r_xo, m_w_xo, v_w_xo)
    step("w_down", w_down, r_down, m_w_down, v_w_down, transposed=True)

    me = _dev_index((lax.axis_index("x"), lax.axis_index("y"), lax.axis_index("c")))
    n_cw = conv_w.shape[2]
    g_cw_full = small_sum[8:24].reshape(32, D_CONV)[:CONV_WIDTH]
    g_cw_mine = lax.dynamic_slice_in_dim(g_cw_full, me * n_cw, n_cw, axis=1)
    step("conv_w", conv_w, g_cw_mine, m_conv_w, v_conv_w)

    vec_names = ["norm_mix_g", "norm_x_g", "norm_mem_g", "norm_mlp_g", "norm_final_g", "conv_b", "conv_ln_g", "conv_ln_b"]
    vec_w = [norm_mix_g, norm_x_g, norm_mem_g, norm_mlp_g, gf, conv_b, conv_ln_g, conv_ln_b]
    vec_m = [m_norm_mix_g, m_norm_x_g, m_norm_mem_g, m_norm_mlp_g, m_norm_final_g[None, :], m_conv_b, m_conv_ln_g, m_conv_ln_b]
    vec_v = [v_norm_mix_g, v_norm_x_g, v_norm_mem_g, v_norm_mlp_g, v_norm_final_g[None, :], v_conv_b, v_conv_ln_g, v_conv_ln_b]

    def pack(ts):
        return jnp.concatenate([widen(t) for t in ts], axis=0)

    packed = _adamw(pack(vec_w), small_sum[0:8], pack(vec_m), pack(vec_v), "adamw_vectors")
    for i, name in enumerate(vec_names):
        width = vec_w[i].shape[1]
        shape = (width,) if name == "norm_final_g" else (1, width)
        res[name] = [t[i, :width].reshape(shape) for t in packed]

    order = ["norm_mix_g", "w_in", "conv_w", "conv_b", "conv_ln_g", "conv_ln_b", "w_out", "norm_x_g", "norm_mem_g",
             "w_xq", "w_xk", "w_xv", "w_xo", "norm_mlp_g", "w_up", "w_down", "norm_final_g"]
    out = [loss, grad_x[None]]
    for kind in range(4):
        out += [res[name][kind] for name in order]
    return tuple(out)
```

```python
import jax
import jax.numpy as jnp
from jax import lax
from jax.experimental import pallas as pl
from jax.experimental.pallas import tpu as pltpu

F32 = jnp.float32
BF16 = jnp.bfloat16

N_DEV = 8
EPS = 1e-6
NEG_INF = -1e30
ATT_HEADS = 8
HEAD_DIM = 64
D_ATT = ATT_HEADS * HEAD_DIM
D_CONV = 512
DILATIONS = (1, 4, 16)
HALF = 64
ROPE_THETA = 500000.0
ROT_DIM = HEAD_DIM // 4
CONV_WIDTH = 31
CONV_PAD = (CONV_WIDTH - 1) // 2
XATT_HEADS = 4
ADAM_LR = 0.001
ADAM_B1 = 0.9
ADAM_B2 = 0.999
ADAM_EPS = 1e-08
ADAM_WD = 0.01
ADAM_STEP = 10

LANES = 128
SUBLANES = 8
BF16_ROWS = 16
BQ = 128
WIN = BQ + 2 * HALF
MLP_SHARDS = 4
CONV_ROWS = 32
VMEM_LIMIT = 56 * 1024 * 1024
MESH = pl.DeviceIdType.MESH
ANY = pl.BlockSpec(memory_space=pl.ANY)

_NT = (((1,), (1,)), ((), ()))
_TN = (((0,), (0,)), ((), ()))


def _dot(a, b):
    return jnp.dot(a, b, preferred_element_type=F32)


def _dot_nt(a, b):
    return lax.dot_general(a, b, _NT, preferred_element_type=F32)


def _dot_tn(a, b):
    return lax.dot_general(a, b, _TN, preferred_element_type=F32)


def _params(*sem):
    return pltpu.CompilerParams(dimension_semantics=sem or None, vmem_limit_bytes=VMEM_LIMIT)


def _sigmoid(v):
    return 1.0 / (1.0 + jnp.exp(-v))


def _mean(v):
    return jnp.mean(v, axis=-1, keepdims=True)


def _rms_fwd(h, g):
    r = lax.rsqrt(_mean(h * h) + EPS)
    return h * r * g, r


def _rms_bwd(h, g, d_out):
    r = lax.rsqrt(_mean(h * h) + EPS)
    hn = h * r
    gd = d_out * g
    return r * (gd - hn * _mean(gd * hn)), d_out * hn


def _row_tile(s):
    return min(512, s)


def _full(shape):
    return pl.BlockSpec(shape, lambda *_: (0,) * len(shape))


def _mesh_pos():
    return lax.axis_index("x"), lax.axis_index("y"), lax.axis_index("c")


def _dev_index(p):
    return 4 * p[0] + 2 * p[1] + p[2]


class _Exchange:
    def __init__(self, arrays, gather):
        self.arrays, self.gather, self.n = list(arrays), gather, len(arrays)

    def out_shapes(self):
        return [jax.ShapeDtypeStruct(((N_DEV,) + a.shape) if self.gather else a.shape, a.dtype)
                for a in self.arrays]

    def sem_shapes(self):
        return [pltpu.SemaphoreType.DMA((7 * self.n,)), pltpu.SemaphoreType.DMA((7 * self.n,)),
                pltpu.SemaphoreType.DMA((self.n,))]

    def phases(self, x_refs, o_refs, send_sems, recv_sems, local_sems):
        n = self.n
        x, y, c = _mesh_pos()
        me, sibling = (x, y, c), (x, y, 1 - c)

        if self.gather:
            chips = [(1 - x, y), (x, 1 - y), (1 - x, 1 - y)]

            def copy(a, k, block, to, src=None):
                slot = o_refs[a].at[_dev_index(block)]
                return pltpu.make_async_remote_copy(
                    src_ref=slot if src is None else src, dst_ref=slot,
                    send_sem=send_sems.at[7 * a + k], recv_sem=recv_sems.at[7 * a + k],
                    device_id=to, device_id_type=MESH)

            def mine(a):
                return pltpu.make_async_copy(x_refs[a], o_refs[a].at[_dev_index(me)], local_sems.at[a])

            def first(a):
                return [copy(a, 0, me, sibling, src=x_refs[a])] + [
                    copy(a, 1 + j, me, (*chip, c), src=x_refs[a]) for j, chip in enumerate(chips)]

            def relayed(a, j):
                return copy(a, 4 + j, (*chips[j], c), sibling)

            def start():
                for a in range(n):
                    mine(a).start()
                    for cp in first(a):
                        cp.start()

            def relay():
                for j, chip in enumerate(chips):
                    for a in range(n):
                        copy(a, 1 + j, (*chip, c), me).wait_recv()
                        relayed(a, j).start()

            def finish():
                for a in range(n):
                    copy(a, 0, sibling, me).wait_recv()
                    for j, chip in enumerate(chips):
                        copy(a, 4 + j, (*chip, 1 - c), me).wait_recv()
                    for cp in first(a) + [relayed(a, j) for j in range(3)]:
                        cp.wait_send()
                    mine(a).wait()

            return start, relay, finish

        flips = [(dx, dy, dc) for dx in (0, 1) for dy in (0, 1) for dc in (0, 1)][1:]

        def peer(k):
            return tuple(1 - v if fl else v for v, fl in zip(me, flips[k]))

        def send(a, k):
            return pltpu.make_async_remote_copy(
                src_ref=x_refs[a].at[_dev_index(peer(k))], dst_ref=o_refs[a].at[_dev_index(me)],
                send_sem=send_sems.at[7 * a + k], recv_sem=recv_sems.at[7 * a + k],
                device_id=peer(k), device_id_type=MESH)

        def landed(a, k):
            slot = o_refs[a].at[_dev_index(peer(k))]
            return pltpu.make_async_remote_copy(
                src_ref=slot, dst_ref=slot, send_sem=send_sems.at[7 * a + k], recv_sem=recv_sems.at[7 * a + k],
                device_id=peer(k), device_id_type=MESH)

        def own(a):
            return pltpu.make_async_copy(x_refs[a].at[_dev_index(me)], o_refs[a].at[_dev_index(me)],
                                         local_sems.at[a])

        def start():
            for a in range(n):
                own(a).start()
            for k in range(7):
                for a in range(n):
                    send(a, k).start()

        def finish():
            for k in range(7):
                for a in range(n):
                    landed(a, k).wait_recv()
            for k in range(7):
                for a in range(n):
                    send(a, k).wait_send()
            for a in range(n):
                own(a).wait()

        return start, (lambda: None), finish


def _hosted_call(body, sides, *, name, grid, in_specs, out_specs, out_shape, scratch_shapes, args):
    n_in, n_out, ns = len(in_specs), len(out_specs), sum(s.n for s in sides)
    steps = 1
    for g in grid:
        steps *= g

    def wrapped(*refs):
        ins, s_ins = refs[:n_in], refs[n_in:n_in + ns]
        outs = refs[n_in + ns:n_in + ns + n_out]
        s_outs = refs[n_in + ns + n_out:n_in + 2 * ns + n_out]
        rest = refs[n_in + 2 * ns + n_out:]
        scratch, sems = rest[:len(rest) - 3 * len(sides)], rest[len(rest) - 3 * len(sides):]
        phases, off = [], 0
        for i, s in enumerate(sides):
            phases.append(s.phases(s_ins[off:off + s.n], s_outs[off:off + s.n], *sems[3 * i:3 * i + 3]))
            off += s.n
        lin = 0
        for ax, g in enumerate(grid):
            lin = lin * g + pl.program_id(ax)

        if sides:
            @pl.when(lin == 0)
            def _():
                for start, _, _ in phases:
                    start()

        body(*ins, *outs, *scratch)

        if sides:
            @pl.when(lin == min((3 * steps) // 4, steps - 1))
            def _():
                for _, relay, _ in phases:
                    relay()

            @pl.when(lin == steps - 1)
            def _():
                for _, _, finish in phases:
                    finish()

    res = pl.pallas_call(
        wrapped, name=name, grid=grid,
        in_specs=list(in_specs) + [ANY] * ns, out_specs=list(out_specs) + [ANY] * ns,
        out_shape=list(out_shape) + [sh for s in sides for sh in s.out_shapes()],
        scratch_shapes=list(scratch_shapes) + [sh for s in sides for sh in s.sem_shapes()],
        compiler_params=_params(*(("arbitrary",) * len(grid))),
    )(*args, *[a for s in sides for a in s.arrays])
    return res[:n_out], res[n_out:]


def _exchange_call(sides, name):
    ns = sum(s.n for s in sides)

    def body(*refs):
        x_refs, o_refs, sems = refs[:ns], refs[ns:2 * ns], refs[2 * ns:]
        phases, off = [], 0
        for i, s in enumerate(sides):
            phases.append(s.phases(x_refs[off:off + s.n], o_refs[off:off + s.n], *sems[3 * i:3 * i + 3]))
            off += s.n
        for step in range(3):
            for ph in phases:
                ph[step]()

    return pl.pallas_call(
        body, name=name,
        out_shape=[sh for s in sides for sh in s.out_shapes()],
        in_specs=[ANY] * ns, out_specs=[ANY] * ns,
        scratch_shapes=[sh for s in sides for sh in s.sem_shapes()],
    )(*[a for s in sides for a in s.arrays])


def _rotary_tables(s):
    half = ROT_DIM // 2
    freqs = ROPE_THETA ** (-jnp.arange(0, ROT_DIM, 2, dtype=F32) / ROT_DIM)
    ang = jnp.arange(s, dtype=F32)[:, None] * freqs[None, :]
    cos, sin = jnp.cos(ang), jnp.sin(ang)
    one = jnp.ones((s, HEAD_DIM - ROT_DIM), F32)
    zero = jnp.zeros((s, HEAD_DIM - ROT_DIM), F32)
    zh = jnp.zeros((s, half), F32)
    c64 = jnp.concatenate([cos, cos, one], axis=1)
    a64 = jnp.concatenate([-sin, zh, zero], axis=1)
    b64 = jnp.concatenate([zh, sin, zero], axis=1)
    return tuple(jnp.tile(t, (1, LANES // HEAD_DIM)) for t in (c64, a64, b64))


def _strided_spec(dil, tm, width):
    return pl.BlockSpec((dil, tm // dil, width), lambda i: (0, i, 0))


def _strided_shape(dil, s, width, dtype):
    return jax.ShapeDtypeStruct((dil, s // dil, width), dtype)


def _lane_scratch(tm, width):
    return pltpu.VMEM((width // LANES, tm, LANES), F32)


def _store_blocks(buf, v):
    for cb in range(buf.shape[0]):
        buf[cb] = v[:, cb * LANES:(cb + 1) * LANES]


def _load_blocks(buf):
    return jnp.concatenate([buf[cb] for cb in range(buf.shape[0])], axis=1)


def _write_strided(buf, dst_ref):
    dil, rows, _ = dst_ref.shape
    for r in range(dil):
        for cb in range(buf.shape[0]):
            dst_ref[r, :, cb * LANES:(cb + 1) * LANES] = buf[cb, pl.ds(r, rows, stride=dil), :].astype(dst_ref.dtype)


def _read_strided(src_ref, buf, add=False):
    dil, rows, _ = src_ref.shape
    for r in range(dil):
        for cb in range(buf.shape[0]):
            v = src_ref[r, :, cb * LANES:(cb + 1) * LANES].astype(F32)
            if add:
                v = v + buf[cb, pl.ds(r, rows, stride=dil), :]
            buf[cb, pl.ds(r, rows, stride=dil), :] = v


def _fwd_in(x, g, w_in, rot):
    s, d = x.shape
    n = w_in.shape[1]
    tm = _row_tile(s)
    dils = DILATIONS[1:]

    def body(x_ref, g_ref, w_ref, c_ref, a_ref, b_ref, xn_ref, qkv_ref, ag_ref, *rest):
        strided, ybuf = rest[:len(dils)], rest[len(dils)]
        xn = _rms_fwd(x_ref[...], g_ref[...])[0].astype(BF16)
        xn_ref[...] = xn
        y = _dot(xn, w_ref[...])
        reps = (1, D_ATT // LANES)
        cc, aa, bb = jnp.tile(c_ref[...], reps), jnp.tile(a_ref[...], reps), jnp.tile(b_ref[...], reps)
        parts = []
        for blk in range(2):
            t = y[:, blk * D_ATT:(blk + 1) * D_ATT]
            parts.append(t * cc + pltpu.roll(t, D_ATT - ROT_DIM // 2, 1) * aa + pltpu.roll(t, ROT_DIM // 2, 1) * bb)
        qkv = jnp.concatenate(parts + [y[:, 2 * D_ATT:3 * D_ATT]], axis=1)
        qkv_ref[...] = qkv.astype(BF16)
        _store_blocks(ybuf, qkv)
        for ref in strided:
            _write_strided(ybuf, ref)
        ag_ref[...] = y[:, 3 * D_ATT:].astype(BF16)

    row = lambda w: pl.BlockSpec((tm, w), lambda i: (i, 0))
    return pl.pallas_call(
        body, name="fwd_in", grid=(s // tm,),
        in_specs=[row(d), _full((1, d)), _full((d, n)), row(LANES), row(LANES), row(LANES)],
        out_specs=[row(d), row(3 * D_ATT), row(2 * D_CONV)] + [_strided_spec(dil, tm, 3 * D_ATT) for dil in dils],
        out_shape=[jax.ShapeDtypeStruct((s, d), BF16), jax.ShapeDtypeStruct((s, 3 * D_ATT), BF16),
                   jax.ShapeDtypeStruct((s, 2 * D_CONV), BF16)]
        + [_strided_shape(dil, s, 3 * D_ATT, BF16) for dil in dils],
        scratch_shapes=[_lane_scratch(tm, 3 * D_ATT)],
        compiler_params=_params("parallel"),
    )(x, g, w_in, *rot)


def _residues_per_step(dil):
    return 4 if dil % 4 == 0 else 1


def _win_in_specs(length, col, width, group):
    per = BQ // HALF
    last = length // HALF - 1
    return [
        pl.BlockSpec((group, HALF, width), lambda r, j: (r, jnp.maximum(j * per - 1, 0), col)),
        pl.BlockSpec((group, BQ, width), lambda r, j: (r, j, col)),
        pl.BlockSpec((group, HALF, width), lambda r, j: (r, jnp.minimum(j * per + per, last), col)),
    ]


def _fill_window(win, prev_ref, main_ref, next_ref):
    for g in range(win.shape[0]):
        win[g, 0:HALF] = prev_ref[g]
        win[g, HALF:HALF + BQ] = main_ref[g]
        win[g, HALF + BQ:] = next_ref[g]


def _band_bias():
    blk = jnp.arange(2 * BQ)[:, None] & (BQ - 1)
    win = jnp.arange(WIN)[None, :]
    return jnp.where(jnp.abs(win - HALF - blk) <= HALF, 0.0, NEG_INF).astype(F32)


def _window_bias(band_ref, j, length):
    pos = j * BQ - HALF + lax.broadcasted_iota(jnp.int32, (1, WIN), 1)
    return band_ref[...] + jnp.where((pos >= 0) & (pos < length), 0.0, NEG_INF)


def _first_head():
    return lax.broadcasted_iota(jnp.int32, (1, LANES), 1) < HEAD_DIM


def _stack_heads(v, first):
    zero = jnp.zeros((), v.dtype)
    return jnp.concatenate([jnp.where(first, v, zero), jnp.where(first, zero, v)], axis=0)


def _unstack_heads(v, first):
    rows = v.shape[0] // 2
    return jnp.where(first, v[:rows], v[rows:])


def _stack_cols(v, first):
    other = pltpu.roll(v, HEAD_DIM, 1)
    stacked = jnp.concatenate([jnp.where(first, v, other), jnp.where(first, other, v)], axis=0)
    return jnp.tile(stacked, (1, WIN // LANES))


def _swa_fwd(qkv3, band, name, sides=()):
    dil, length, _ = qkv3.shape
    scale = HEAD_DIM ** -0.5
    group = _residues_per_step(dil)

    def body(q_ref, kp, km, kn, vp, vm, vn, band_ref, o_ref, lse_ref, kwin, vwin):
        j = pl.program_id(1)
        _fill_window(kwin, kp, km, kn)
        _fill_window(vwin, vp, vm, vn)
        bias = _window_bias(band_ref, j, length)
        first = _first_head()
        pairs = [(g, slice(pr * LANES, (pr + 1) * LANES)) for g in range(group) for pr in range(D_ATT // LANES)]
        scores = [_dot_nt(_stack_heads(q_ref[g, :, cols] * scale, first), kwin[g, :, cols]) + bias for g, cols in pairs]
        stats, probs = [], []
        for sc in scores:
            m = jnp.max(sc, axis=-1, keepdims=True)
            p = jnp.exp(sc - m)
            stats.append((m, jnp.sum(p, axis=-1, keepdims=True)))
            probs.append(p.astype(BF16))
        for (g, cols), p, (m, den) in zip(pairs, probs, stats):
            pv = _dot(p, vwin[g, :, cols]) * (1.0 / den)
            o_ref[g, :, cols] = _unstack_heads(pv, first).astype(BF16)
            lse_ref[g, :, cols] = _unstack_heads(jnp.broadcast_to(m + jnp.log(den), (2 * BQ, LANES)), first)

    blk = pl.BlockSpec((group, BQ, D_ATT), lambda r, j: (r, j, 0))
    return _hosted_call(
        body, sides, name=name, grid=(dil // group, length // BQ),
        in_specs=[blk] + _win_in_specs(length, 1, D_ATT, group) + _win_in_specs(length, 2, D_ATT, group)
        + [_full((2 * BQ, WIN))],
        out_specs=[blk, blk],
        out_shape=[jax.ShapeDtypeStruct((dil, length, D_ATT), BF16),
                   jax.ShapeDtypeStruct((dil, length, D_ATT), F32)],
        scratch_shapes=[pltpu.VMEM((group, WIN, D_ATT), BF16)] * 2,
        args=[qkv3] * 7 + [band])


def _glu(v):
    return v[:, :D_CONV].astype(F32) * _sigmoid(v[:, D_CONV:].astype(F32))


def _halo_specs(tm, width, col=0):
    per = tm // BF16_ROWS
    return lambda nblk: [
        pl.BlockSpec((BF16_ROWS, width), lambda i: (jnp.maximum(i * per - 1, 0), col)),
        pl.BlockSpec((tm, width), lambda i: (i, col)),
        pl.BlockSpec((BF16_ROWS, width), lambda i: (jnp.minimum(i * per + per, nblk - 1), col)),
    ]


def _fill_halo(buf, i, ntiles, tm, prev, main, nxt):
    buf[0:BF16_ROWS] = jnp.where(i == 0, 0.0, prev)
    buf[BF16_ROWS:BF16_ROWS + tm] = main
    buf[BF16_ROWS + tm:] = jnp.where(i == ntiles - 1, 0.0, nxt)


def _halo_scratch(tm):
    return [pltpu.VMEM((tm + 2 * BF16_ROWS, D_CONV), F32),
            pltpu.VMEM((SUBLANES - 1, tm + 2 * BF16_ROWS - SUBLANES, D_CONV), F32)]


def _shift_copies(buf, shifted, tm):
    rows = tm + 2 * BF16_ROWS - SUBLANES
    for b in range(1, SUBLANES):
        shifted[b - 1] = buf[pl.ds(b, rows), :]


def _tap(buf, shifted, off, rows, base=0):
    a, b = divmod(off, SUBLANES)
    start = base + SUBLANES * a
    if not isinstance(start, int):
        start = pl.multiple_of(start, SUBLANES)
    if b == 0:
        return buf[pl.ds(start, rows), :]
    return shifted[b - 1, pl.ds(start, rows), :]


def _fwd_conv(ag, cw, cb, lg, lb, sides=()):
    s = ag.shape[0]
    tm = _row_tile(s)
    nt = s // tm

    def body(agp, agm, agn, cw_ref, cb_ref, lg_ref, lb_ref, c1_ref, co_ref, ubuf, ush):
        i = pl.program_id(0)
        _fill_halo(ubuf, i, nt, tm, _glu(agp[...]), _glu(agm[...]), _glu(agn[...]))
        _shift_copies(ubuf, ush, tm)
        w, cb = cw_ref[...], cb_ref[...]

        def chunk(c, carry):
            base = pl.multiple_of(c * CONV_ROWS, CONV_ROWS)
            acc = jnp.zeros((CONV_ROWS, D_CONV), F32)
            for k in range(CONV_WIDTH):
                acc = acc + _tap(ubuf, ush, k + 1, CONV_ROWS, base) * w[k:k + 1, :]
            c1_ref[pl.ds(base, CONV_ROWS), :] = acc + cb
            return carry

        lax.fori_loop(0, tm // CONV_ROWS, chunk, 0)
        c1 = c1_ref[...]
        xc = c1 - _mean(c1)
        ln = xc * lax.rsqrt(_mean(xc * xc) + EPS) * lg_ref[...] + lb_ref[...]
        co_ref[...] = (ln * _sigmoid(ln)).astype(BF16)

    vec = _full((1, D_CONV))
    return _hosted_call(
        body, sides, name="fwd_conv", grid=(nt,),
        in_specs=_halo_specs(tm, 2 * D_CONV)(s // BF16_ROWS) + [_full((32, D_CONV)), vec, vec, vec],
        out_specs=[pl.BlockSpec((tm, D_CONV), lambda i: (i, 0))] * 2,
        out_shape=[jax.ShapeDtypeStruct((s, D_CONV), F32), jax.ShapeDtypeStruct((s, D_CONV), BF16)],
        scratch_shapes=_halo_scratch(tm),
        args=[ag, ag, ag, cw, cb, lg, lb])


def _fwd_mix_out(outs, lses, conv_out, x, w_out):
    s, d = x.shape
    tm = _row_tile(s)
    dils = DILATIONS[1:]
    nd = len(dils)

    def body(o1, *rest):
        o_str, l1, l_str = rest[:nd], rest[nd], rest[nd + 1:2 * nd + 1]
        co, x_ref, w_ref, h_ref, cat_ref, lt_ref = rest[2 * nd + 1:2 * nd + 7]
        lt_str = rest[2 * nd + 7:3 * nd + 7]
        obufs, lbufs, ltbuf = rest[3 * nd + 7:4 * nd + 7], rest[4 * nd + 7:5 * nd + 7], rest[5 * nd + 7]
        for ref, buf in zip(o_str + l_str, obufs + lbufs):
            _read_strided(ref, buf)
        lse = [l1[0]] + [_load_blocks(buf) for buf in lbufs]
        out = [o1[0].astype(F32)] + [_load_blocks(buf) for buf in obufs]
        m = lse[0]
        for v in lse[1:]:
            m = jnp.maximum(m, v)
        e = [jnp.exp(v - m) for v in lse]
        den = sum(e[1:], e[0])
        att = (sum((ev * ov for ev, ov in zip(e[1:], out[1:])), e[0] * out[0]) / den).astype(BF16)
        lt = m + jnp.log(den)
        lt_ref[...] = lt
        _store_blocks(ltbuf, lt)
        for ref in lt_str:
            _write_strided(ltbuf, ref)
        cat_ref[:, :D_ATT] = att
        cat_ref[:, D_ATT:] = co[...]
        h_ref[...] = x_ref[...] + _dot(att, w_ref[:D_ATT, :]) + _dot(co[...], w_ref[D_ATT:, :])

    row = lambda w: pl.BlockSpec((tm, w), lambda i: (i, 0))
    nat = pl.BlockSpec((1, tm, D_ATT), lambda i: (0, i, 0))
    strided = [_strided_spec(dil, tm, D_ATT) for dil in dils]
    return pl.pallas_call(
        body, name="fwd_mix_out", grid=(s // tm,),
        in_specs=[nat] + strided + [nat] + strided + [row(D_ATT), row(d), _full((d, d))],
        out_specs=[row(d), row(d), row(D_ATT)] + strided,
        out_shape=[jax.ShapeDtypeStruct((s, d), F32), jax.ShapeDtypeStruct((s, d), BF16),
                   jax.ShapeDtypeStruct((s, D_ATT), F32)] + [_strided_shape(dil, s, D_ATT, F32) for dil in dils],
        scratch_shapes=[_lane_scratch(tm, D_ATT)] * (2 * nd + 1),
        compiler_params=_params("parallel"),
    )(*outs, *lses, conv_out, x, w_out)


def _fwd_mem(mem, g, wk, wv):
    m, d = mem.shape

    def body(mem_ref, g_ref, wk_ref, wv_ref, mn_ref, xk_ref, xv_ref):
        mn = _rms_fwd(mem_ref[...], g_ref[...])[0].astype(BF16)
        mn_ref[...] = mn
        xk_ref[...] = _dot(mn, wk_ref[...]).astype(BF16)
        xv_ref[...] = _dot(mn, wv_ref[...]).astype(BF16)

    return pl.pallas_call(
        body, name="fwd_mem",
        out_shape=[jax.ShapeDtypeStruct((m, d), BF16)] * 3,
        compiler_params=_params(),
    )(mem, g, wk, wv)


def _softmax(sc):
    p = jnp.exp(sc - jnp.max(sc, axis=-1, keepdims=True))
    return p / jnp.sum(p, axis=-1, keepdims=True)


def _fwd_xattn(h1, g, wq, xk, xv, wo):
    s, d = h1.shape
    m = xk.shape[0]
    tm = _row_tile(s)
    hd = d // XATT_HEADS

    def body(h_ref, g_ref, wq_ref, xk_ref, xv_ref, wo_ref, h2_ref, hn_ref, xq_ref, xo_ref):
        h = h_ref[...]
        hn = _rms_fwd(h, g_ref[...])[0].astype(BF16)
        hn_ref[...] = hn
        xq = _dot(hn, wq_ref[...]).astype(BF16)
        xq_ref[...] = xq
        heads = [slice(i * hd, (i + 1) * hd) for i in range(XATT_HEADS)]
        scores = [_dot_nt(xq[:, cols], xk_ref[:, cols]) * hd ** -0.5 for cols in heads]
        probs = [_softmax(sc).astype(BF16) for sc in scores]
        for cols, pr in zip(heads, probs):
            xo_ref[:, cols] = _dot(pr, xv_ref[:, cols]).astype(BF16)
        h2_ref[...] = h + _dot(xo_ref[...], wo_ref[...])

    row = pl.BlockSpec((tm, d), lambda i: (i, 0))
    return pl.pallas_call(
        body, name="fwd_xattn", grid=(s // tm,),
        in_specs=[row, _full((1, d)), _full((d, d)), _full((m, d)), _full((m, d)), _full((d, d))],
        out_specs=[row] * 4,
        out_shape=[jax.ShapeDtypeStruct((s, d), F32)] + [jax.ShapeDtypeStruct((s, d), BF16)] * 3,
        compiler_params=_params("parallel"),
    )(h1, g, wq, xk, xv, wo)


def _fwd_mlp_loss(h2, g, w_up, w_down, gf, target):
    s, d = h2.shape
    nsh, _, f = w_up.shape
    per = nsh
    fb = per * f
    nb = nsh // per
    tm = _row_tile(s)
    once = pl.Buffered(1)

    def body(h_ref, g_ref, wu_ref, wd_ref, gf_ref, t_ref,
             hn_ref, act_ref, dh_ref, dhb_ref, loss_ref, ggf_ref, acc):
        i, k = pl.program_id(0), pl.program_id(1)

        @pl.when(k == 0)
        def _():
            hn_ref[...] = _rms_fwd(h_ref[...], g_ref[...])[0].astype(BF16)
            acc[...] = jnp.zeros_like(acc)

        @pl.when((i == 0) & (k == 0))
        def _():
            loss_ref[...] = jnp.zeros_like(loss_ref)
            ggf_ref[...] = jnp.zeros_like(ggf_ref)

        hn = hn_ref[...]
        ups = [_dot(hn, wu_ref[c]) for c in range(per)]
        for c, u in enumerate(ups):
            act_ref[:, c * f:(c + 1) * f] = jnp.square(jnp.maximum(u, 0.0)).astype(BF16)
        acc[...] += _dot(act_ref[...], wd_ref[...])

        @pl.when(k == nb - 1)
        def _():
            h3 = h_ref[...] + acc[...]
            gfv = gf_ref[...]
            y, _ = _rms_fwd(h3, gfv)
            err = y - t_ref[...]
            loss_ref[...] += 0.5 * jnp.sum(_mean(err * err))
            dh3, gg = _rms_bwd(h3, gfv, err * (1.0 / d))
            ggf_ref[...] += jnp.sum(gg, axis=0, keepdims=True)
            dh_ref[...] = dh3
            dhb_ref[...] = dh3.astype(BF16)

    row = pl.BlockSpec((tm, d), lambda i, k: (i, 0))
    return pl.pallas_call(
        body, name="fwd_mlp_loss", grid=(s // tm, nb),
        in_specs=[row, _full((1, d)),
                  pl.BlockSpec((per, d, f), lambda i, k: (k, 0, 0), pipeline_mode=once),
                  pl.BlockSpec((fb, d), lambda i, k: (k, 0), pipeline_mode=once),
                  _full((1, d)), row],
        out_specs=[row, pl.BlockSpec((tm, fb), lambda i, k: (i, k)), row, row,
                   _full((1, LANES)), _full((1, d))],
        out_shape=[jax.ShapeDtypeStruct((s, d), BF16), jax.ShapeDtypeStruct((s, nsh * f), BF16),
                   jax.ShapeDtypeStruct((s, d), F32), jax.ShapeDtypeStruct((s, d), BF16),
                   jax.ShapeDtypeStruct((1, LANES), F32), jax.ShapeDtypeStruct((1, d), F32)],
        scratch_shapes=[pltpu.VMEM((tm, d), F32)],
        compiler_params=_params("arbitrary", "arbitrary"),
    )(h2, g, w_up, w_down, gf, target)


def _bwd_mlp(dh3, dh3b, act, w_up_t, w_down, h2, g):
    s, d = h2.shape
    ff = w_down.shape[0]
    per = N_DEV
    fb = per * (ff // N_DEV)
    nb = ff // fb
    tm = _row_tile(s)
    once = pl.Buffered(1)

    def body(dh_ref, dhb_ref, act_ref, wut_ref, wd_ref, h_ref, g_ref,
             du_ref, dh2_ref, dh2b_ref, gg_ref, acc):
        i, k = pl.program_id(0), pl.program_id(1)

        @pl.when(k == 0)
        def _():
            acc[...] = jnp.zeros_like(acc)

        @pl.when((i == 0) & (k == 0))
        def _():
            gg_ref[...] = jnp.zeros_like(gg_ref)

        dhb = dhb_ref[...]
        f = fb // per
        shards = [slice(c * f, (c + 1) * f) for c in range(per)]
        dacts = [_dot_nt(dhb, wd_ref[cols, :]) for cols in shards]
        for cols, dact in zip(shards, dacts):
            du_ref[:, cols] = (dact * (2.0 * jnp.sqrt(act_ref[:, cols].astype(F32)))).astype(BF16)
        acc[...] += _dot(du_ref[...], wut_ref[...])

        @pl.when(k == nb - 1)
        def _():
            dh, gg = _rms_bwd(h_ref[...], g_ref[...], acc[...])
            gg_ref[...] += jnp.sum(gg, axis=0, keepdims=True)
            dh2 = dh_ref[...] + dh
            dh2_ref[...] = dh2
            dh2b_ref[...] = dh2.astype(BF16)

    row = pl.BlockSpec((tm, d), lambda i, k: (i, 0))
    col = pl.BlockSpec((tm, fb), lambda i, k: (i, k))
    wblk = pl.BlockSpec((fb, d), lambda i, k: (k, 0), pipeline_mode=once)
    return pl.pallas_call(
        body, name="bwd_mlp", grid=(s // tm, nb),
        in_specs=[row, row, col, wblk, wblk, row, _full((1, d))],
        out_specs=[col, row, row, _full((1, d))],
        out_shape=[jax.ShapeDtypeStruct((s, ff), BF16), jax.ShapeDtypeStruct((s, d), F32),
                   jax.ShapeDtypeStruct((s, d), BF16), jax.ShapeDtypeStruct((1, d), F32)],
        scratch_shapes=[pltpu.VMEM((tm, d), F32)],
        compiler_params=_params("arbitrary", "arbitrary"),
    )(dh3, dh3b, act, w_up_t, w_down, h2, g)


def _bwd_xattn(dh2, dh2b, h1, g, xq, xk, xv, wq, wo, sides=()):
    s, d = h1.shape
    m = xk.shape[0]
    tm = _row_tile(s)
    hd = d // XATT_HEADS
    scale = hd ** -0.5

    def body(dh_ref, dhb_ref, h_ref, g_ref, xq_ref, xk_ref, xv_ref, wq_ref, wo_ref,
             dh1_ref, dh1b_ref, dxq_ref, dxk_ref, dxv_ref, gg_ref):
        @pl.when(pl.program_id(0) == 0)
        def _():
            dxk_ref[...] = jnp.zeros_like(dxk_ref)
            dxv_ref[...] = jnp.zeros_like(dxv_ref)
            gg_ref[...] = jnp.zeros_like(gg_ref)

        dxo = _dot_nt(dhb_ref[...], wo_ref[...])
        heads = [slice(i * hd, (i + 1) * hd) for i in range(XATT_HEADS)]
        dxos = [dxo[:, cols].astype(BF16) for cols in heads]
        scores = [_dot_nt(xq_ref[:, cols], xk_ref[:, cols]) * scale for cols in heads]
        dprs = [_dot_nt(dxo_h, xv_ref[:, cols]) for dxo_h, cols in zip(dxos, heads)]
        probs, dscs = [], []
        for sc, dpr in zip(scores, dprs):
            pr = _softmax(sc)
            dscs.append((pr * (dpr - jnp.sum(dpr * pr, axis=-1, keepdims=True)) * scale).astype(BF16))
            probs.append(pr.astype(BF16))
        for cols, dsc, pr, dxo_h in zip(heads, dscs, probs, dxos):
            dxq_ref[:, cols] = _dot(dsc, xk_ref[:, cols]).astype(BF16)
            dxk_ref[:, cols] += _dot_tn(dsc, xq_ref[:, cols])
            dxv_ref[:, cols] += _dot_tn(pr, dxo_h)
        dh, gg = _rms_bwd(h_ref[...], g_ref[...], _dot_nt(dxq_ref[...], wq_ref[...]))
        gg_ref[...] += jnp.sum(gg, axis=0, keepdims=True)
        dh1 = dh_ref[...] + dh
        dh1_ref[...] = dh1
        dh1b_ref[...] = dh1.astype(BF16)

    row = pl.BlockSpec((tm, d), lambda i: (i, 0))
    return _hosted_call(
        body, sides, name="bwd_xattn", grid=(s // tm,),
        in_specs=[row, row, row, _full((1, d)), row, _full((m, d)), _full((m, d)), _full((d, d)), _full((d, d))],
        out_specs=[row, row, row, _full((m, d)), _full((m, d)), _full((1, d))],
        out_shape=[jax.ShapeDtypeStruct((s, d), F32), jax.ShapeDtypeStruct((s, d), BF16),
                   jax.ShapeDtypeStruct((s, d), BF16), jax.ShapeDtypeStruct((m, d), F32),
                   jax.ShapeDtypeStruct((m, d), F32), jax.ShapeDtypeStruct((1, d), F32)],
        scratch_shapes=[],
        args=[dh2, dh2b, h1, g, xq, xk, xv, wq, wo])


def _bwd_mem(mem, g, mn, dxk, dxv, wk, wv):
    m, d = mem.shape

    def body(mem_ref, g_ref, mn_ref, dxk_ref, dxv_ref, wk_ref, wv_ref, gk_ref, gv_ref, gg_ref):
        dk, dv = dxk_ref[...].astype(BF16), dxv_ref[...].astype(BF16)
        gk_ref[...] = _dot_tn(mn_ref[...], dk).astype(BF16)
        gv_ref[...] = _dot_tn(mn_ref[...], dv).astype(BF16)
        dmn = _dot_nt(dk, wk_ref[...]) + _dot_nt(dv, wv_ref[...])
        _, gg = _rms_bwd(mem_ref[...], g_ref[...], dmn)
        gg_ref[...] = jnp.sum(gg, axis=0, keepdims=True)

    return pl.pallas_call(
        body, name="bwd_mem",
        out_shape=[jax.ShapeDtypeStruct((d, d), BF16), jax.ShapeDtypeStruct((d, d), BF16),
                   jax.ShapeDtypeStruct((1, d), F32)],
        compiler_params=_params(),
    )(mem, g, mn, dxk, dxv, wk, wv)


def _bwd_mix_out(dh1b, w_out, cat, head_ones):
    s, d = dh1b.shape
    tm = _row_tile(s)
    dils = DILATIONS[1:]
    nd = len(dils)

    def body(dh_ref, w_ref, cat_ref, ones_ref, dcat_ref, dsum_ref, *rest):
        da_str, ds_str, dbuf, sbuf = rest[:nd], rest[nd:2 * nd], rest[2 * nd], rest[2 * nd + 1]
        dcat = _dot_nt(dh_ref[...], w_ref[...])
        dcat_ref[...] = dcat.astype(BF16)
        datt = dcat[:, :D_ATT]
        prod = datt * cat_ref[...].astype(F32)
        hi = prod.astype(BF16)
        lo = (prod - hi.astype(F32)).astype(BF16)
        dsum = _dot(hi, ones_ref[...]) + _dot(lo, ones_ref[...])
        dsum_ref[...] = dsum
        _store_blocks(dbuf, datt)
        _store_blocks(sbuf, dsum)
        for da_ref, ds_ref in zip(da_str, ds_str):
            _write_strided(dbuf, da_ref)
            _write_strided(sbuf, ds_ref)

    row = lambda w: pl.BlockSpec((tm, w), lambda i: (i, 0))
    strided = [_strided_spec(dil, tm, D_ATT) for dil in dils]
    return pl.pallas_call(
        body, name="bwd_mix_out", grid=(s // tm,),
        in_specs=[row(d), _full((d, d)), row(D_ATT), _full((D_ATT, D_ATT))],
        out_specs=[row(d), row(D_ATT)] + strided + strided,
        out_shape=[jax.ShapeDtypeStruct((s, d), BF16), jax.ShapeDtypeStruct((s, D_ATT), F32)]
        + [_strided_shape(dil, s, D_ATT, BF16) for dil in dils]
        + [_strided_shape(dil, s, D_ATT, F32) for dil in dils],
        scratch_shapes=[_lane_scratch(tm, D_ATT)] * 2,
        compiler_params=_params("parallel"),
    )(dh1b, w_out, cat, head_ones)


def _bwd_conv(dcat, c1, ag, cw, lg, lb, sides=()):
    s = ag.shape[0]
    tm = _row_tile(s)
    nt = s // tm

    def body(dp, dm, dn, cp, cm, cn, agp, agm, agn, cw_ref, lg_ref, lb_ref,
             dag_ref, gcw_ref, gcb_ref, glg_ref, glb_ref, ubuf, ush, dbuf, dsh, gacc):
        i = pl.program_id(0)

        @pl.when(i == 0)
        def _():
            gacc[...] = jnp.zeros_like(gacc)
            gcb_ref[...] = jnp.zeros_like(gcb_ref)
            glg_ref[...] = jnp.zeros_like(glg_ref)
            glb_ref[...] = jnp.zeros_like(glb_ref)

        lgv, lbv = lg_ref[...], lb_ref[...]

        def norm_bwd(dco, c1v):
            xc = c1v - _mean(c1v)
            rs = lax.rsqrt(_mean(xc * xc) + EPS)
            z = xc * rs
            ln = z * lgv + lbv
            sg = _sigmoid(ln)
            dln = dco.astype(F32) * (sg * (1.0 + ln * (1.0 - sg)))
            dz = dln * lgv
            return rs * (dz - _mean(dz) - z * _mean(dz * z)), dln, z

        def sublane_sums(v):
            out = v[0:SUBLANES]
            for r in range(SUBLANES, CONV_ROWS, SUBLANES):
                out = out + v[r:r + SUBLANES]
            return out

        dc_m, dln, z = norm_bwd(dm[...], cm[...])
        glg_ref[...] += jnp.sum(dln * z, axis=0, keepdims=True)
        glb_ref[...] += jnp.sum(dln, axis=0, keepdims=True)
        gcb_ref[...] += jnp.sum(dc_m, axis=0, keepdims=True)
        _fill_halo(dbuf, i, nt, tm, norm_bwd(dp[...], cp[...])[0], dc_m, norm_bwd(dn[...], cn[...])[0])
        _fill_halo(ubuf, i, nt, tm, _glu(agp[...]), _glu(agm[...]), _glu(agn[...]))
        _shift_copies(dbuf, dsh, tm)
        _shift_copies(ubuf, ush, tm)

        w = cw_ref[...]

        def chunk(c, carry):
            base = pl.multiple_of(c * CONV_ROWS, CONV_ROWS)
            rows = pl.ds(base, CONV_ROWS)
            dc = dbuf[pl.ds(pl.multiple_of(base + BF16_ROWS, SUBLANES), CONV_ROWS), :]
            du = jnp.zeros((CONV_ROWS, D_CONV), F32)
            for k in range(CONV_WIDTH):
                du = du + _tap(dbuf, dsh, CONV_WIDTH - k, CONV_ROWS, base) * w[k:k + 1, :]
                gacc[k] += sublane_sums(dc * _tap(ubuf, ush, k + 1, CONV_ROWS, base))
            a = agm[rows, :D_CONV].astype(F32)
            sg = _sigmoid(agm[rows, D_CONV:].astype(F32))
            dag_ref[rows, :D_CONV] = (du * sg).astype(BF16)
            dag_ref[rows, D_CONV:] = (du * a * sg * (1.0 - sg)).astype(BF16)
            return carry

        lax.fori_loop(0, tm // CONV_ROWS, chunk, 0)

        @pl.when(i == nt - 1)
        def _():
            tap = lax.broadcasted_iota(jnp.int32, (32, D_CONV), 0)
            gcw = jnp.zeros((32, D_CONV), F32)
            for k in range(CONV_WIDTH):
                gcw = jnp.where(tap == k, jnp.sum(gacc[k], axis=0, keepdims=True), gcw)
            gcw_ref[...] = gcw

    vec = _full((1, D_CONV))
    nblk = s // BF16_ROWS
    return _hosted_call(
        body, sides, name="bwd_conv", grid=(nt,),
        in_specs=_halo_specs(tm, D_CONV, 1)(nblk) + _halo_specs(tm, D_CONV)(nblk) + _halo_specs(tm, 2 * D_CONV)(nblk)
        + [_full((32, D_CONV)), vec, vec],
        out_specs=[pl.BlockSpec((tm, 2 * D_CONV), lambda i: (i, 0)), _full((32, D_CONV)), vec, vec, vec],
        out_shape=[jax.ShapeDtypeStruct((s, 2 * D_CONV), BF16), jax.ShapeDtypeStruct((32, D_CONV), F32)]
        + [jax.ShapeDtypeStruct((1, D_CONV), F32)] * 3,
        scratch_shapes=_halo_scratch(tm) + _halo_scratch(tm) + [pltpu.VMEM((32, SUBLANES, D_CONV), F32)],
        args=[dcat, dcat, dcat, c1, c1, c1, ag, ag, ag, cw, lg, lb])


def _swa_bwd(qkv3, do3, lt3, ds3, band, name, sides=()):
    dil, length, _ = qkv3.shape
    nb = length // BQ
    scale = HEAD_DIM ** -0.5
    group = _residues_per_step(dil)
    assert WIN == 2 * BQ and BQ == 2 * HALF

    def body(q_ref, kp, km, kn, vp, vm, vn, do_ref, l_ref, s_ref, band_ref, dq_ref, dkv_ref, kwin, vwin, pend, keep):
        j = pl.program_id(1)

        @pl.when(j == 0)
        def _():
            pend[...] = jnp.zeros_like(pend)
            keep[...] = jnp.zeros_like(keep)

        @pl.when(j < nb)
        def _():
            _fill_window(kwin, kp, km, kn)
            _fill_window(vwin, vp, vm, vn)
            bias = _window_bias(band_ref, j, length)
            first = _first_head()
            pairs = [(g, slice(pr * LANES, (pr + 1) * LANES)) for g in range(group) for pr in range(D_ATT // LANES)]
            qs = [_stack_heads(q_ref[g, :, cols] * scale, first) for g, cols in pairs]
            dos = [_stack_heads(do_ref[g, :, cols], first) for g, cols in pairs]
            scores = [_dot_nt(q, kwin[g, :, cols]) for q, (g, cols) in zip(qs, pairs)]
            dps = [_dot_nt(do, vwin[g, :, cols]) for do, (g, cols) in zip(dos, pairs)]
            probs, dscs = [], []
            for (g, cols), sc, dp in zip(pairs, scores, dps):
                p = jnp.exp(sc + (bias - _stack_cols(l_ref[g, :, cols], first)))
                dscs.append((p * (dp - _stack_cols(s_ref[g, :, cols], first))).astype(BF16))
                probs.append(p.astype(BF16))
            dqs = [_dot(dsc, kwin[g, :, cols] * scale) for dsc, (g, cols) in zip(dscs, pairs)]
            dks = [_dot_tn(dsc, q) for dsc, q in zip(dscs, qs)]
            dvs = [_dot_tn(p, do) for p, do in zip(probs, dos)]
            for (g, cols), dq, dk, dv in zip(pairs, dqs, dks, dvs):
                dq_ref[g, :, cols] = _unstack_heads(dq, first).astype(BF16)
                for part, at in ((dk, cols), (dv, slice(D_ATT + cols.start, D_ATT + cols.stop))):
                    dkv_ref[g, :HALF, at] = keep[g, :, at].astype(BF16)
                    dkv_ref[g, HALF:, at] = (pend[g, :HALF, at] + part[:HALF]).astype(BF16)
                    keep[g, :, at] = pend[g, HALF:, at] + part[HALF:BQ]
                    pend[g, :, at] = part[BQ:]

        @pl.when(j == nb)
        def _():
            dkv_ref[:, :HALF] = keep[...].astype(BF16)
            dkv_ref[:, HALF:] = pend[:, :HALF].astype(BF16)

    def clamp(idx):
        return lambda r, j: idx(r, jnp.minimum(j, nb - 1))

    main = pl.BlockSpec((group, BQ, D_ATT), clamp(lambda r, j: (r, j, 0)))
    wins = [pl.BlockSpec(sp.block_shape, clamp(sp.index_map))
            for c in (1, 2) for sp in _win_in_specs(length, c, D_ATT, group)]
    return _hosted_call(
        body, sides, name=name, grid=(dil // group, nb + 1),
        in_specs=[main] + wins + [main] * 3 + [_full((2 * BQ, WIN))],
        out_specs=[main, pl.BlockSpec((group, BQ, 2 * D_ATT), lambda r, j: (r, jnp.maximum(j - 1, 0), 0))],
        out_shape=[jax.ShapeDtypeStruct((dil, length, D_ATT), BF16),
                   jax.ShapeDtypeStruct((dil, length, 2 * D_ATT), BF16)],
        scratch_shapes=[pltpu.VMEM((group, WIN, D_ATT), BF16)] * 2
        + [pltpu.VMEM((group, BQ, 2 * D_ATT), F32), pltpu.VMEM((group, HALF, 2 * D_ATT), F32)],
        args=[qkv3] * 7 + [do3, lt3, ds3, band])


def _bwd_in(dqs, dkvs, dag, w_in, x, g, dh1, rot):
    s, d = x.shape
    n = w_in.shape[1]
    tm = _row_tile(s)
    dils = DILATIONS[1:]
    nd = len(dils)

    def body(q1, *rest):
        q_str, kv1, kv_str = rest[:nd], rest[nd], rest[nd + 1:2 * nd + 1]
        dag_ref, w_ref, x_ref, g_ref, dh_ref, c_ref, a_ref, b_ref, gx_ref, dy_ref, gg_ref, qbuf, kvbuf = rest[2 * nd + 1:]

        @pl.when(pl.program_id(0) == 0)
        def _():
            gg_ref[...] = jnp.zeros_like(gg_ref)

        _store_blocks(qbuf, q1[0].astype(F32))
        _store_blocks(kvbuf, kv1[0].astype(F32))
        for ref in q_str:
            _read_strided(ref, qbuf, add=True)
        for ref in kv_str:
            _read_strided(ref, kvbuf, add=True)
        dq, dkv = _load_blocks(qbuf), _load_blocks(kvbuf)
        reps = (1, D_ATT // LANES)
        cc, aa, bb = jnp.tile(c_ref[...], reps), jnp.tile(a_ref[...], reps), jnp.tile(b_ref[...], reps)
        for blk, t in enumerate((dq, dkv[:, :D_ATT])):
            dt = t * cc + pltpu.roll(t * aa, ROT_DIM // 2, 1) + pltpu.roll(t * bb, D_ATT - ROT_DIM // 2, 1)
            dy_ref[:, blk * D_ATT:(blk + 1) * D_ATT] = dt.astype(BF16)
        dy_ref[:, 2 * D_ATT:3 * D_ATT] = dkv[:, D_ATT:].astype(BF16)
        dy_ref[:, 3 * D_ATT:] = dag_ref[...]
        dx, gg = _rms_bwd(x_ref[...], g_ref[...], _dot_nt(dy_ref[...], w_ref[...]))
        gg_ref[...] += jnp.sum(gg, axis=0, keepdims=True)
        gx_ref[...] = dh_ref[...] + dx

    row = lambda w: pl.BlockSpec((tm, w), lambda i: (i, 0))
    def strided(width):
        return [pl.BlockSpec((1, tm, width), lambda i: (0, i, 0))] + [_strided_spec(dil, tm, width) for dil in dils]

    return pl.pallas_call(
        body, name="bwd_in", grid=(s // tm,),
        in_specs=strided(D_ATT) + strided(2 * D_ATT)
        + [row(2 * D_CONV), _full((d, n)), row(d), _full((1, d)), row(d)] + [row(LANES)] * 3,
        out_specs=[row(d), row(n), _full((1, d))],
        out_shape=[jax.ShapeDtypeStruct((s, d), F32), jax.ShapeDtypeStruct((s, n), BF16),
                   jax.ShapeDtypeStruct((1, d), F32)],
        scratch_shapes=[_lane_scratch(tm, D_ATT), _lane_scratch(tm, 2 * D_ATT)],
        compiler_params=_params("arbitrary"),
    )(*dqs, *dkvs, dag, w_in, x, g, dh1, *rot)


def _wgrad(a, b, name, a_blk=None, b_blk=None, stack=None, tm=1024):
    s, ka = a.shape
    nb = b.shape[1]
    a_blk, b_blk = a_blk or ka, b_blk or nb
    na, nbl = ka // a_blk, nb // b_blk
    assert na == 1 or nbl == 1
    tm = min(tm, s)
    nt = s // tm
    per = b_blk // stack if stack else 0

    def body(a_ref, b_ref, o_ref, acc):
        t = pl.program_id(1)

        @pl.when(t == 0)
        def _():
            acc[...] = jnp.zeros_like(acc)

        acc[...] += _dot_tn(a_ref[...], b_ref[...])

        @pl.when(t == nt - 1)
        def _():
            if stack:
                for c in range(per):
                    o_ref[c] = acc[:, c * stack:(c + 1) * stack].astype(BF16)
            else:
                o_ref[...] = acc[...].astype(BF16)

    if stack:
        out_spec = pl.BlockSpec((per, ka, stack), lambda k, t: (k, 0, 0))
        out_shape = jax.ShapeDtypeStruct((nb // stack, ka, stack), BF16)
    elif na > 1:
        out_spec = pl.BlockSpec((a_blk, nb), lambda k, t: (k, 0))
        out_shape = jax.ShapeDtypeStruct((ka, nb), BF16)
    else:
        out_spec = pl.BlockSpec((ka, b_blk), lambda k, t: (0, k))
        out_shape = jax.ShapeDtypeStruct((ka, nb), BF16)
    return pl.pallas_call(
        body, name=name, grid=(na * nbl, nt),
        in_specs=[pl.BlockSpec((tm, a_blk), (lambda k, t: (t, k)) if na > 1 else (lambda k, t: (t, 0))),
                  pl.BlockSpec((tm, b_blk), (lambda k, t: (t, k)) if nbl > 1 else (lambda k, t: (t, 0)))],
        out_specs=out_spec, out_shape=out_shape,
        scratch_shapes=[pltpu.VMEM((a_blk, b_blk), F32)],
        compiler_params=_params("parallel", "arbitrary"),
    )(a, b)


def _adamw(w, gsrc, m, v, name, transposed=False):
    summed = gsrc.ndim == w.ndim + 1
    rows, cols = w.shape
    assert gsrc.shape[-2:] == ((cols, rows) if transposed else (rows, cols)) and (summed or not transposed)
    tr = rows if rows <= 256 else 256
    assert rows % tr == 0
    c1 = 1.0 - ADAM_B1 ** ADAM_STEP
    c2 = 1.0 - ADAM_B2 ** ADAM_STEP

    def body(w_ref, g_ref, m_ref, v_ref, go_ref, d_ref, mo_ref, vo_ref):
        if summed:
            g = g_ref[0].astype(F32)
            for i in range(1, N_DEV):
                g = g + g_ref[i].astype(F32)
            if transposed:
                g = g.T
        else:
            g = g_ref[...]
        mn = ADAM_B1 * m_ref[...] + (1.0 - ADAM_B1) * g
        vn = ADAM_B2 * v_ref[...] + (1.0 - ADAM_B2) * jnp.square(g)
        go_ref[...] = g
        mo_ref[...] = mn
        vo_ref[...] = vn
        d_ref[...] = -ADAM_LR * ((mn / c1) / (jnp.sqrt(vn / c2) + ADAM_EPS) + ADAM_WD * w_ref[...])

    blk = pl.BlockSpec((tr, cols), lambda i: (i, 0))
    if transposed:
        gblk = pl.BlockSpec((N_DEV, cols, tr), lambda i: (0, 0, i))
    else:
        gblk = pl.BlockSpec((N_DEV, tr, cols), lambda i: (0, i, 0)) if summed else blk
    return pl.pallas_call(
        body, name=name, grid=(rows // tr,),
        in_specs=[blk, gblk, blk, blk], out_specs=[blk] * 4,
        out_shape=[jax.ShapeDtypeStruct(w.shape, F32)] * 4,
        compiler_params=_params("parallel"),
    )(w, gsrc, m, v)


def _sum_slots(g, name):
    _, rows, cols = g.shape

    def body(g_ref, o_ref):
        acc = g_ref[0]
        for i in range(1, N_DEV):
            acc = acc + g_ref[i]
        o_ref[...] = acc

    return pl.pallas_call(body, name=name, out_shape=jax.ShapeDtypeStruct((rows, cols), F32),
                          compiler_params=_params())(g)


def kernel(x, mem, norm_mix_g, w_in, conv_w, conv_b, conv_ln_g, conv_ln_b, w_out, norm_x_g, norm_mem_g, w_xq, w_xk, w_xv, w_xo, norm_mlp_g, w_up, w_down, norm_final_g, loss_target, m_norm_mix_g, m_w_in, m_conv_w, m_conv_b, m_conv_ln_g, m_conv_ln_b, m_w_out, m_norm_x_g, m_norm_mem_g, m_w_xq, m_w_xk, m_w_xv, m_w_xo, m_norm_mlp_g, m_w_up, m_w_down, m_norm_final_g, v_norm_mix_g, v_w_in, v_conv_w, v_conv_b, v_conv_ln_g, v_conv_ln_b, v_w_out, v_norm_x_g, v_norm_mem_g, v_w_xq, v_w_xk, v_w_xv, v_w_xo, v_norm_mlp_g, v_w_up, v_w_down, v_norm_final_g):
    x2, mem2, tgt = x[0], mem[0], loss_target[0]
    s, d = x2.shape
    gf = norm_final_g[None, :]

    cw_local = jnp.pad(conv_w[0], ((0, 1), (0, LANES - conv_w.shape[2])))
    win_g, cw_g = _exchange_call([_Exchange([w_in[0].astype(BF16), cw_local], gather=True)], "gather_w_in")
    w_in_f = jnp.transpose(win_g, (1, 0, 2)).reshape(d, -1)
    cw_f = jnp.transpose(cw_g[:, :, :conv_w.shape[2]], (1, 0, 2)).reshape(32, D_CONV)
    def stacked(*ws):
        return jnp.concatenate([w[0].astype(BF16) for w in ws], axis=0)

    def unstacked(g, *ws):
        offs = [0]
        for w in ws:
            offs.append(offs[-1] + w.shape[1])
        return [g[:, a:b, :].reshape(N_DEV * (b - a), d) for a, b in zip(offs, offs[1:])]

    late = [_Exchange([stacked(w_out, w_xq)], gather=True), _Exchange([w_up[0].astype(BF16)], gather=True),
            _Exchange([w_down[0].astype(BF16)], gather=True)]

    rot = _rotary_tables(s)
    band = _band_bias()
    xn, qkv, ag, *qkv_strided = _fwd_in(x2, norm_mix_g, w_in_f, rot)
    qkv3 = [qkv[None]] + qkv_strided
    outs, lses, gathered = [], [], []
    for dil, q3, side in zip(DILATIONS, qkv3, late):
        (o3, l3), got = _swa_fwd(q3, band, f"swa_fwd_d{dil}", [side])
        outs.append(o3)
        lses.append(l3)
        gathered += got
    rows_g, wup_g, wdown_g = gathered
    w_out_f, w_xq_f = unstacked(rows_g, w_out, w_xq)
    w_down_f = wdown_g.reshape(-1, d)
    w_up_t = jnp.swapaxes(wup_g, 1, 2).reshape(-1, d)
    (c1, conv_out), (rows_g,) = _fwd_conv(ag, cw_f, conv_b, conv_ln_g, conv_ln_b,
                                          [_Exchange([stacked(w_xk, w_xv, w_xo)], gather=True)])
    w_xk_f, w_xv_f, w_xo_f = unstacked(rows_g, w_xk, w_xv, w_xo)
    h1, cat, ltot, *lt_strided = _fwd_mix_out(outs, lses, conv_out, x2, w_out_f)
    mn, xk, xv = _fwd_mem(mem2, norm_mem_g, w_xk_f, w_xv_f)
    h2, hn2, xq, xo = _fwd_xattn(h1, norm_x_g, w_xq_f, xk, xv, w_xo_f)
    hn3, act, dh3, dh3b, loss_part, g_final = _fwd_mlp_loss(h2, norm_mlp_g, wup_g, w_down_f, gf, tgt)

    def scatter(*grads):
        return _Exchange([g.reshape(N_DEV, -1, g.shape[-1]) for g in grads], gather=False)

    f_blk = w_up.shape[2]
    du, dh2, dh2b, g_mlp = _bwd_mlp(dh3, dh3b, act, w_up_t, w_down_f, h2, norm_mlp_g)
    gw_up = _wgrad(hn3, du, "wgrad_up", b_blk=4 * f_blk, stack=f_blk)
    gw_down = _wgrad(dh3b, act, "wgrad_down", b_blk=4 * f_blk, stack=f_blk)
    (dh1, dh1b, dxq, dxk, dxv, g_x), (r_up,) = _bwd_xattn(
        dh2, dh2b, h1, norm_x_g, xq, xk, xv, w_xq_f, w_xo_f, [scatter(gw_up)])
    gw_xq = _wgrad(hn2, dxq, "wgrad_xq")
    gw_xo = _wgrad(xo, dh2b, "wgrad_xo")
    gw_xk, gw_xv, g_mem = _bwd_mem(mem2, norm_mem_g, mn, dxk, dxv, w_xk_f, w_xv_f)
    head = jnp.arange(D_ATT) // HEAD_DIM
    head_ones = (head[:, None] == head[None, :]).astype(BF16)
    dcat, dsum, *strided = _bwd_mix_out(dh1b, w_out_f, cat, head_ones)
    n_str = len(DILATIONS) - 1
    do3, lt3, ds3 = [dcat[None]] + strided[:n_str], [ltot[None]] + lt_strided, [dsum[None]] + strided[n_str:]
    gw_out = _wgrad(cat, dh1b, "wgrad_out")
    (dag, g_cw, g_cb, g_lg, g_lb), (r_down,) = _bwd_conv(dcat, c1, ag, cw_f, conv_ln_g, conv_ln_b, [scatter(gw_down)])
    hosted = [[scatter(gw_out, gw_xq)], [scatter(gw_xk, gw_xv)], [scatter(gw_xo)]]
    dqs, dkvs, landed = [], [], []
    for i, dil in enumerate(DILATIONS):
        (dq3, dkv3), got = _swa_bwd(qkv3[i], do3[i], lt3[i], ds3[i], band, f"swa_bwd_d{dil}", hosted[i])
        dqs.append(dq3)
        dkvs.append(dkv3)
        landed += got
    r_out, r_xq, r_xk, r_xv, r_xo = landed
    grad_x, dy, g_mix = _bwd_in(dqs, dkvs, dag, w_in_f, x2, norm_mix_g, dh1, rot)
    gw_in = _wgrad(xn, dy, "wgrad_in")

    def widen(t):
        return jnp.pad(t, ((0, 0), (0, d - t.shape[1])))

    n_in = w_in.shape[2]
    small = jnp.concatenate([g_mix, g_x, g_mem, g_mlp, g_final, widen(g_cb), widen(g_lg), widen(g_lb),
                             g_cw.reshape(16, d), widen(loss_part), jnp.zeros((7, d), F32)], axis=0)
    r_in, small_g = _exchange_call(
        [_Exchange([jnp.transpose(gw_in.reshape(d, N_DEV, n_in), (1, 0, 2))], gather=False),
         _Exchange([small], gather=True)], "scatter_w_in_gather_small")
    small_sum = _sum_slots(small_g, "sum_small_grads")
    loss = small_sum[24, 0]

    res = {}

    def step(name, w, gsrc, m, v, transposed=False):
        shape = w.shape
        w2, m2, v2 = (t.reshape(-1, shape[-1]) for t in (w, m, v))
        res[name] = [t.reshape(shape) for t in _adamw(w2, gsrc, m2, v2, "adamw_" + name, transposed)]

    step("w_in", w_in, r_in, m_w_in, v_w_in)
    step("w_up", w_up, r_up, m_w_up, v_w_up)
    step("w_out", w_out, r_out, m_w_out, v_w_out)
    step("w_xq", w_xq, r_xq, m_w_xq, v_w_xq)
    step("w_xk", w_xk, r_xk, m_w_xk, v_w_xk)
    step("w_xv", w_xv, r_xv, m_w_xv, v_w_xv)
    step("w_xo", w_xo, r_xo, m_w_xo, v_w_xo)
    step("w_down", w_down, r_down, m_w_down, v_w_down, transposed=True)

    me = _dev_index((lax.axis_index("x"), lax.axis_index("y"), lax.axis_index("c")))
    n_cw = conv_w.shape[2]
    g_cw_full = small_sum[8:24].reshape(32, D_CONV)[:CONV_WIDTH]
    g_cw_mine = lax.dynamic_slice_in_dim(g_cw_full, me * n_cw, n_cw, axis=1)
    step("conv_w", conv_w, g_cw_mine, m_conv_w, v_conv_w)

    vec_names = ["norm_mix_g", "norm_x_g", "norm_mem_g", "norm_mlp_g", "norm_final_g", "conv_b", "conv_ln_g", "conv_ln_b"]
    vec_w = [norm_mix_g, norm_x_g, norm_mem_g, norm_mlp_g, gf, conv_b, conv_ln_g, conv_ln_b]
    vec_m = [m_norm_mix_g, m_norm_x_g, m_norm_mem_g, m_norm_mlp_g, m_norm_final_g[None, :], m_conv_b, m_conv_ln_g, m_conv_ln_b]
    vec_v = [v_norm_mix_g, v_norm_x_g, v_norm_mem_g, v_norm_mlp_g, v_norm_final_g[None, :], v_conv_b, v_conv_ln_g, v_conv_ln_b]

    def pack(ts):
        return jnp.concatenate([widen(t) for t in ts], axis=0)

    packed = _adamw(pack(vec_w), small_sum[0:8], pack(vec_m), pack(vec_v), "adamw_vectors")
    for i, name in enumerate(vec_names):
        width = vec_w[i].shape[1]
        shape = (width,) if name == "norm_final_g" else (1, width)
        res[name] = [t[i, :width].reshape(shape) for t in packed]

    order = ["norm_mix_g", "w_in", "conv_w", "conv_b", "conv_ln_g", "conv_ln_b", "w_out", "norm_x_g", "norm_mem_g",
             "w_xq", "w_xk", "w_xv", "w_xo", "norm_mlp_g", "w_up", "w_down", "norm_final_g"]
    out = [loss, grad_x[None]]
    for kind in range(4):
        out += [res[name][kind] for name in order]
    return tuple(out)
```

```python
import jax
import jax.numpy as jnp
from jax import lax
from jax.experimental import pallas as pl
from jax.experimental.pallas import tpu as pltpu

F32 = jnp.float32
BF16 = jnp.bfloat16

N_DEV = 8
EPS = 1e-6
NEG_INF = -1e30
ATT_HEADS = 8
HEAD_DIM = 64
D_ATT = ATT_HEADS * HEAD_DIM
D_CONV = 512
DILATIONS = (1, 4, 16)
HALF = 64
ROPE_THETA = 500000.0
ROT_DIM = HEAD_DIM // 4
CONV_WIDTH = 31
CONV_PAD = (CONV_WIDTH - 1) // 2
XATT_HEADS = 4
ADAM_LR = 0.001
ADAM_B1 = 0.9
ADAM_B2 = 0.999
ADAM_EPS = 1e-08
ADAM_WD = 0.01
ADAM_STEP = 10

LANES = 128
SUBLANES = 8
BF16_ROWS = 16
BQ = 128
WIN = BQ + 2 * HALF
MLP_SHARDS = 4
CONV_ROWS = 32
VMEM_LIMIT = 56 * 1024 * 1024
MESH = pl.DeviceIdType.MESH
ANY = pl.BlockSpec(memory_space=pl.ANY)

_NT = (((1,), (1,)), ((), ()))
_TN = (((0,), (0,)), ((), ()))


def _dot(a, b):
    return jnp.dot(a, b, preferred_element_type=F32)


def _dot_nt(a, b):
    return lax.dot_general(a, b, _NT, preferred_element_type=F32)


def _dot_tn(a, b):
    return lax.dot_general(a, b, _TN, preferred_element_type=F32)


def _params(*sem):
    return pltpu.CompilerParams(dimension_semantics=sem or None, vmem_limit_bytes=VMEM_LIMIT)


def _sigmoid(v):
    return 1.0 / (1.0 + jnp.exp(-v))


def _mean(v):
    return jnp.mean(v, axis=-1, keepdims=True)


def _rms_fwd(h, g):
    r = lax.rsqrt(_mean(h * h) + EPS)
    return h * r * g, r


def _rms_bwd(h, g, d_out):
    r = lax.rsqrt(_mean(h * h) + EPS)
    hn = h * r
    gd = d_out * g
    return r * (gd - hn * _mean(gd * hn)), d_out * hn


def _row_tile(s):
    return min(512, s)


def _full(shape):
    return pl.BlockSpec(shape, lambda *_: (0,) * len(shape))


def _mesh_pos():
    return lax.axis_index("x"), lax.axis_index("y"), lax.axis_index("c")


def _dev_index(p):
    return 4 * p[0] + 2 * p[1] + p[2]


class _Exchange:
    def __init__(self, arrays, gather):
        self.arrays, self.gather, self.n = list(arrays), gather, len(arrays)

    def out_shapes(self):
        return [jax.ShapeDtypeStruct(((N_DEV,) + a.shape) if self.gather else a.shape, a.dtype)
                for a in self.arrays]

    def sem_shapes(self):
        return [pltpu.SemaphoreType.DMA((7 * self.n,)), pltpu.SemaphoreType.DMA((7 * self.n,)),
                pltpu.SemaphoreType.DMA((self.n,))]

    def phases(self, x_refs, o_refs, send_sems, recv_sems, local_sems):
        n = self.n
        x, y, c = _mesh_pos()
        me, sibling = (x, y, c), (x, y, 1 - c)

        if self.gather:
            chips = [(1 - x, y), (x, 1 - y), (1 - x, 1 - y)]

            def copy(a, k, block, to, src=None):
                slot = o_refs[a].at[_dev_index(block)]
                return pltpu.make_async_remote_copy(
                    src_ref=slot if src is None else src, dst_ref=slot,
                    send_sem=send_sems.at[7 * a + k], recv_sem=recv_sems.at[7 * a + k],
                    device_id=to, device_id_type=MESH)

            def mine(a):
                return pltpu.make_async_copy(x_refs[a], o_refs[a].at[_dev_index(me)], local_sems.at[a])

            def first(a):
                return [copy(a, 0, me, sibling, src=x_refs[a])] + [
                    copy(a, 1 + j, me, (*chip, c), src=x_refs[a]) for j, chip in enumerate(chips)]

            def relayed(a, j):
                return copy(a, 4 + j, (*chips[j], c), sibling)

            def start():
                for a in range(n):
                    mine(a).start()
                    for cp in first(a):
                        cp.start()

            def relay():
                for j, chip in enumerate(chips):
                    for a in range(n):
                        copy(a, 1 + j, (*chip, c), me).wait_recv()
                        relayed(a, j).start()

            def finish():
                for a in range(n):
                    copy(a, 0, sibling, me).wait_recv()
                    for j, chip in enumerate(chips):
                        copy(a, 4 + j, (*chip, 1 - c), me).wait_recv()
                    for cp in first(a) + [relayed(a, j) for j in range(3)]:
                        cp.wait_send()
                    mine(a).wait()

            return start, relay, finish

        flips = [(dx, dy, dc) for dx in (0, 1) for dy in (0, 1) for dc in (0, 1)][1:]

        def peer(k):
            return tuple(1 - v if fl else v for v, fl in zip(me, flips[k]))

        def send(a, k):
            return pltpu.make_async_remote_copy(
                src_ref=x_refs[a].at[_dev_index(peer(k))], dst_ref=o_refs[a].at[_dev_index(me)],
                send_sem=send_sems.at[7 * a + k], recv_sem=recv_sems.at[7 * a + k],
                device_id=peer(k), device_id_type=MESH)

        def landed(a, k):
            slot = o_refs[a].at[_dev_index(peer(k))]
            return pltpu.make_async_remote_copy(
                src_ref=slot, dst_ref=slot, send_sem=send_sems.at[7 * a + k], recv_sem=recv_sems.at[7 * a + k],
                device_id=peer(k), device_id_type=MESH)

        def own(a):
            return pltpu.make_async_copy(x_refs[a].at[_dev_index(me)], o_refs[a].at[_dev_index(me)],
                                         local_sems.at[a])

        def start():
            for a in range(n):
                own(a).start()
            for k in range(7):
                for a in range(n):
                    send(a, k).start()

        def finish():
            for k in range(7):
                for a in range(n):
                    landed(a, k).wait_recv()
            for k in range(7):
                for a in range(n):
                    send(a, k).wait_send()
            for a in range(n):
                own(a).wait()

        return start, (lambda: None), finish


def _hosted_call(body, sides, *, name, grid, in_specs, out_specs, out_shape, scratch_shapes, args):
    n_in, n_out, ns = len(in_specs), len(out_specs), sum(s.n for s in sides)
    steps = 1
    for g in grid:
        steps *= g

    def wrapped(*refs):
        ins, s_ins = refs[:n_in], refs[n_in:n_in + ns]
        outs = refs[n_in + ns:n_in + ns + n_out]
        s_outs = refs[n_in + ns + n_out:n_in + 2 * ns + n_out]
        rest = refs[n_in + 2 * ns + n_out:]
        scratch, sems = rest[:len(rest) - 3 * len(sides)], rest[len(rest) - 3 * len(sides):]
        phases, off = [], 0
        for i, s in enumerate(sides):
            phases.append(s.phases(s_ins[off:off + s.n], s_outs[off:off + s.n], *sems[3 * i:3 * i + 3]))
            off += s.n
        lin = 0
        for ax, g in enumerate(grid):
            lin = lin * g + pl.program_id(ax)

        if sides:
            @pl.when(lin == 0)
            def _():
                for start, _, _ in phases:
                    start()

        body(*ins, *outs, *scratch)

        if sides:
            @pl.when(lin == min((3 * steps) // 4, steps - 1))
            def _():
                for _, relay, _ in phases:
                    relay()

            @pl.when(lin == steps - 1)
            def _():
                for _, _, finish in phases:
                    finish()

    res = pl.pallas_call(
        wrapped, name=name, grid=grid,
        in_specs=list(in_specs) + [ANY] * ns, out_specs=list(out_specs) + [ANY] * ns,
        out_shape=list(out_shape) + [sh for s in sides for sh in s.out_shapes()],
        scratch_shapes=list(scratch_shapes) + [sh for s in sides for sh in s.sem_shapes()],
        compiler_params=_params(*(("arbitrary",) * len(grid))),
    )(*args, *[a for s in sides for a in s.arrays])
    return res[:n_out], res[n_out:]


def _exchange_call(sides, name):
    ns = sum(s.n for s in sides)

    def body(*refs):
        x_refs, o_refs, sems = refs[:ns], refs[ns:2 * ns], refs[2 * ns:]
        phases, off = [], 0
        for i, s in enumerate(sides):
            phases.append(s.phases(x_refs[off:off + s.n], o_refs[off:off + s.n], *sems[3 * i:3 * i + 3]))
            off += s.n
        for step in range(3):
            for ph in phases:
                ph[step]()

    return pl.pallas_call(
        body, name=name,
        out_shape=[sh for s in sides for sh in s.out_shapes()],
        in_specs=[ANY] * ns, out_specs=[ANY] * ns,
        scratch_shapes=[sh for s in sides for sh in s.sem_shapes()],
    )(*[a for s in sides for a in s.arrays])


def _rotary_tables(s):
    half = ROT_DIM // 2
    freqs = ROPE_THETA ** (-jnp.arange(0, ROT_DIM, 2, dtype=F32) / ROT_DIM)
    ang = jnp.arange(s, dtype=F32)[:, None] * freqs[None, :]
    cos, sin = jnp.cos(ang), jnp.sin(ang)
    one = jnp.ones((s, HEAD_DIM - ROT_DIM), F32)
    zero = jnp.zeros((s, HEAD_DIM - ROT_DIM), F32)
    zh = jnp.zeros((s, half), F32)
    c64 = jnp.concatenate([cos, cos, one], axis=1)
    a64 = jnp.concatenate([-sin, zh, zero], axis=1)
    b64 = jnp.concatenate([zh, sin, zero], axis=1)
    return tuple(jnp.tile(t, (1, LANES // HEAD_DIM)) for t in (c64, a64, b64))


def _strided_spec(dil, tm, width):
    return pl.BlockSpec((dil, tm // dil, width), lambda i: (0, i, 0))


def _strided_shape(dil, s, width, dtype):
    return jax.ShapeDtypeStruct((dil, s // dil, width), dtype)


def _lane_scratch(tm, width):
    return pltpu.VMEM((width // LANES, tm, LANES), F32)


def _store_blocks(buf, v):
    for cb in range(buf.shape[0]):
        buf[cb] = v[:, cb * LANES:(cb + 1) * LANES]


def _load_blocks(buf):
    return jnp.concatenate([buf[cb] for cb in range(buf.shape[0])], axis=1)


def _write_strided(buf, dst_ref):
    dil, rows, _ = dst_ref.shape
    for r in range(dil):
        for cb in range(buf.shape[0]):
            dst_ref[r, :, cb * LANES:(cb + 1) * LANES] = buf[cb, pl.ds(r, rows, stride=dil), :].astype(dst_ref.dtype)


def _read_strided(src_ref, buf, add=False):
    dil, rows, _ = src_ref.shape
    for r in range(dil):
        for cb in range(buf.shape[0]):
            v = src_ref[r, :, cb * LANES:(cb + 1) * LANES].astype(F32)
            if add:
                v = v + buf[cb, pl.ds(r, rows, stride=dil), :]
            buf[cb, pl.ds(r, rows, stride=dil), :] = v


def _fwd_in(x, g, w_in, rot):
    s, d = x.shape
    n = w_in.shape[1]
    tm = _row_tile(s)
    dils = DILATIONS[1:]

    def body(x_ref, g_ref, w_ref, c_ref, a_ref, b_ref, xn_ref, qkv_ref, ag_ref, *rest):
        strided, ybuf = rest[:len(dils)], rest[len(dils)]
        xn = _rms_fwd(x_ref[...], g_ref[...])[0].astype(BF16)
        xn_ref[...] = xn
        y = _dot(xn, w_ref[...])
        reps = (1, D_ATT // LANES)
        cc, aa, bb = jnp.tile(c_ref[...], reps), jnp.tile(a_ref[...], reps), jnp.tile(b_ref[...], reps)
        parts = []
        for blk in range(2):
            t = y[:, blk * D_ATT:(blk + 1) * D_ATT]
            parts.append(t * cc + pltpu.roll(t, D_ATT - ROT_DIM // 2, 1) * aa + pltpu.roll(t, ROT_DIM // 2, 1) * bb)
        qkv = jnp.concatenate(parts + [y[:, 2 * D_ATT:3 * D_ATT]], axis=1)
        qkv_ref[...] = qkv.astype(BF16)
        _store_blocks(ybuf, qkv)
        for ref in strided:
            _write_strided(ybuf, ref)
        ag_ref[...] = y[:, 3 * D_ATT:].astype(BF16)

    row = lambda w: pl.BlockSpec((tm, w), lambda i: (i, 0))
    return pl.pallas_call(
        body, name="fwd_in", grid=(s // tm,),
        in_specs=[row(d), _full((1, d)), _full((d, n)), row(LANES), row(LANES), row(LANES)],
        out_specs=[row(d), row(3 * D_ATT), row(2 * D_CONV)] + [_strided_spec(dil, tm, 3 * D_ATT) for dil in dils],
        out_shape=[jax.ShapeDtypeStruct((s, d), BF16), jax.ShapeDtypeStruct((s, 3 * D_ATT), BF16),
                   jax.ShapeDtypeStruct((s, 2 * D_CONV), BF16)]
        + [_strided_shape(dil, s, 3 * D_ATT, BF16) for dil in dils],
        scratch_shapes=[_lane_scratch(tm, 3 * D_ATT)],
        compiler_params=_params("parallel"),
    )(x, g, w_in, *rot)


def _residues_per_step(dil):
    return 4 if dil % 4 == 0 else 1


def _win_in_specs(length, col, width, group):
    per = BQ // HALF
    last = length // HALF - 1
    return [
        pl.BlockSpec((group, HALF, width), lambda r, j: (r, jnp.maximum(j * per - 1, 0), col)),
        pl.BlockSpec((group, BQ, width), lambda r, j: (r, j, col)),
        pl.BlockSpec((group, HALF, width), lambda r, j: (r, jnp.minimum(j * per + per, last), col)),
    ]


def _fill_window(win, prev_ref, main_ref, next_ref):
    for g in range(win.shape[0]):
        win[g, 0:HALF] = prev_ref[g]
        win[g, HALF:HALF + BQ] = main_ref[g]
        win[g, HALF + BQ:] = next_ref[g]


def _band_bias():
    blk = jnp.arange(2 * BQ)[:, None] & (BQ - 1)
    win = jnp.arange(WIN)[None, :]
    return jnp.where(jnp.abs(win - HALF - blk) <= HALF, 0.0, NEG_INF).astype(F32)


def _window_bias(band_ref, j, length):
    pos = j * BQ - HALF + lax.broadcasted_iota(jnp.int32, (1, WIN), 1)
    return band_ref[...] + jnp.where((pos >= 0) & (pos < length), 0.0, NEG_INF)


def _first_head():
    return lax.broadcasted_iota(jnp.int32, (1, LANES), 1) < HEAD_DIM


def _stack_heads(v, first):
    zero = jnp.zeros((), v.dtype)
    return jnp.concatenate([jnp.where(first, v, zero), jnp.where(first, zero, v)], axis=0)


def _unstack_heads(v, first):
    rows = v.shape[0] // 2
    return jnp.where(first, v[:rows], v[rows:])


def _stack_cols(v, first):
    other = pltpu.roll(v, HEAD_DIM, 1)
    stacked = jnp.concatenate([jnp.where(first, v, other), jnp.where(first, other, v)], axis=0)
    return jnp.tile(stacked, (1, WIN // LANES))


def _swa_fwd(qkv3, band, name, sides=()):
    dil, length, _ = qkv3.shape
    scale = HEAD_DIM ** -0.5
    group = 8 if dil % 8 == 0 else _residues_per_step(dil)

    def body(q_ref, kp, km, kn, vp, vm, vn, band_ref, o_ref, lse_ref, kwin, vwin):
        j = pl.program_id(1)
        _fill_window(kwin, kp, km, kn)
        _fill_window(vwin, vp, vm, vn)
        bias = _window_bias(band_ref, j, length)
        first = _first_head()
        pairs = [(g, slice(pr * LANES, (pr + 1) * LANES)) for g in range(group) for pr in range(D_ATT // LANES)]
        scores = [_dot_nt(_stack_heads(q_ref[g, :, cols] * scale, first), kwin[g, :, cols]) + bias for g, cols in pairs]
        stats, probs = [], []
        for sc in scores:
            m = jnp.max(sc, axis=-1, keepdims=True)
            p = jnp.exp(sc - m)
            stats.append((m, jnp.sum(p, axis=-1, keepdims=True)))
            probs.append(p.astype(BF16))
        for (g, cols), p, (m, den) in zip(pairs, probs, stats):
            pv = _dot(p, vwin[g, :, cols]) * (1.0 / den)
            o_ref[g, :, cols] = _unstack_heads(pv, first).astype(BF16)
            lse_ref[g, :, cols] = _unstack_heads(jnp.broadcast_to(m + jnp.log(den), (2 * BQ, LANES)), first)

    blk = pl.BlockSpec((group, BQ, D_ATT), lambda r, j: (r, j, 0))
    return _hosted_call(
        body, sides, name=name, grid=(dil // group, length // BQ),
        in_specs=[blk] + _win_in_specs(length, 1, D_ATT, group) + _win_in_specs(length, 2, D_ATT, group)
        + [_full((2 * BQ, WIN))],
        out_specs=[blk, blk],
        out_shape=[jax.ShapeDtypeStruct((dil, length, D_ATT), BF16),
                   jax.ShapeDtypeStruct((dil, length, D_ATT), F32)],
        scratch_shapes=[pltpu.VMEM((group, WIN, D_ATT), BF16)] * 2,
        args=[qkv3] * 7 + [band])


def _glu(v):
    return v[:, :D_CONV].astype(F32) * _sigmoid(v[:, D_CONV:].astype(F32))


def _halo_specs(tm, width, col=0):
    per = tm // BF16_ROWS
    return lambda nblk: [
        pl.BlockSpec((BF16_ROWS, width), lambda i: (jnp.maximum(i * per - 1, 0), col)),
        pl.BlockSpec((tm, width), lambda i: (i, col)),
        pl.BlockSpec((BF16_ROWS, width), lambda i: (jnp.minimum(i * per + per, nblk - 1), col)),
    ]


def _fill_halo(buf, i, ntiles, tm, prev, main, nxt):
    buf[0:BF16_ROWS] = jnp.where(i == 0, 0.0, prev)
    buf[BF16_ROWS:BF16_ROWS + tm] = main
    buf[BF16_ROWS + tm:] = jnp.where(i == ntiles - 1, 0.0, nxt)


def _halo_scratch(tm):
    return [pltpu.VMEM((tm + 2 * BF16_ROWS, D_CONV), F32),
            pltpu.VMEM((SUBLANES - 1, tm + 2 * BF16_ROWS - SUBLANES, D_CONV), F32)]


def _shift_copies(buf, shifted, tm):
    rows = tm + 2 * BF16_ROWS - SUBLANES
    for b in range(1, SUBLANES):
        shifted[b - 1] = buf[pl.ds(b, rows), :]


def _tap(buf, shifted, off, rows, base=0):
    a, b = divmod(off, SUBLANES)
    start = base + SUBLANES * a
    if not isinstance(start, int):
        start = pl.multiple_of(start, SUBLANES)
    if b == 0:
        return buf[pl.ds(start, rows), :]
    return shifted[b - 1, pl.ds(start, rows), :]


def _fwd_conv(ag, cw, cb, lg, lb, sides=()):
    s = ag.shape[0]
    tm = _row_tile(s)
    nt = s // tm

    def body(agp, agm, agn, cw_ref, cb_ref, lg_ref, lb_ref, c1_ref, co_ref, ubuf, ush):
        i = pl.program_id(0)
        _fill_halo(ubuf, i, nt, tm, _glu(agp[...]), _glu(agm[...]), _glu(agn[...]))
        _shift_copies(ubuf, ush, tm)
        w, cb = cw_ref[...], cb_ref[...]

        def chunk(c, carry):
            base = pl.multiple_of(c * CONV_ROWS, CONV_ROWS)
            acc = jnp.zeros((CONV_ROWS, D_CONV), F32)
            for k in range(CONV_WIDTH):
                acc = acc + _tap(ubuf, ush, k + 1, CONV_ROWS, base) * w[k:k + 1, :]
            c1_ref[pl.ds(base, CONV_ROWS), :] = acc + cb
            return carry

        lax.fori_loop(0, tm // CONV_ROWS, chunk, 0)
        c1 = c1_ref[...]
        xc = c1 - _mean(c1)
        ln = xc * lax.rsqrt(_mean(xc * xc) + EPS) * lg_ref[...] + lb_ref[...]
        co_ref[...] = (ln * _sigmoid(ln)).astype(BF16)

    vec = _full((1, D_CONV))
    return _hosted_call(
        body, sides, name="fwd_conv", grid=(nt,),
        in_specs=_halo_specs(tm, 2 * D_CONV)(s // BF16_ROWS) + [_full((32, D_CONV)), vec, vec, vec],
        out_specs=[pl.BlockSpec((tm, D_CONV), lambda i: (i, 0))] * 2,
        out_shape=[jax.ShapeDtypeStruct((s, D_CONV), F32), jax.ShapeDtypeStruct((s, D_CONV), BF16)],
        scratch_shapes=_halo_scratch(tm),
        args=[ag, ag, ag, cw, cb, lg, lb])


def _fwd_mix_out(outs, lses, conv_out, x, w_out):
    s, d = x.shape
    tm = _row_tile(s)
    dils = DILATIONS[1:]
    nd = len(dils)

    def body(o1, *rest):
        o_str, l1, l_str = rest[:nd], rest[nd], rest[nd + 1:2 * nd + 1]
        co, x_ref, w_ref, h_ref, cat_ref, lt_ref = rest[2 * nd + 1:2 * nd + 7]
        lt_str = rest[2 * nd + 7:3 * nd + 7]
        obufs, lbufs, ltbuf = rest[3 * nd + 7:4 * nd + 7], rest[4 * nd + 7:5 * nd + 7], rest[5 * nd + 7]
        for ref, buf in zip(o_str + l_str, obufs + lbufs):
            _read_strided(ref, buf)
        lse = [l1[0]] + [_load_blocks(buf) for buf in lbufs]
        out = [o1[0].astype(F32)] + [_load_blocks(buf) for buf in obufs]
        m = lse[0]
        for v in lse[1:]:
            m = jnp.maximum(m, v)
        e = [jnp.exp(v - m) for v in lse]
        den = sum(e[1:], e[0])
        att = (sum((ev * ov for ev, ov in zip(e[1:], out[1:])), e[0] * out[0]) / den).astype(BF16)
        lt = m + jnp.log(den)
        lt_ref[...] = lt
        _store_blocks(ltbuf, lt)
        for ref in lt_str:
            _write_strided(ltbuf, ref)
        cat_ref[:, :D_ATT] = att
        cat_ref[:, D_ATT:] = co[...]
        h_ref[...] = x_ref[...] + _dot(att, w_ref[:D_ATT, :]) + _dot(co[...], w_ref[D_ATT:, :])

    row = lambda w: pl.BlockSpec((tm, w), lambda i: (i, 0))
    nat = pl.BlockSpec((1, tm, D_ATT), lambda i: (0, i, 0))
    strided = [_strided_spec(dil, tm, D_ATT) for dil in dils]
    return pl.pallas_call(
        body, name="fwd_mix_out", grid=(s // tm,),
        in_specs=[nat] + strided + [nat] + strided + [row(D_ATT), row(d), _full((d, d))],
        out_specs=[row(d), row(d), row(D_ATT)] + strided,
        out_shape=[jax.ShapeDtypeStruct((s, d), F32), jax.ShapeDtypeStruct((s, d), BF16),
                   jax.ShapeDtypeStruct((s, D_ATT), F32)] + [_strided_shape(dil, s, D_ATT, F32) for dil in dils],
        scratch_shapes=[_lane_scratch(tm, D_ATT)] * (2 * nd + 1),
        compiler_params=_params("parallel"),
    )(*outs, *lses, conv_out, x, w_out)


def _fwd_mem(mem, g, wk, wv):
    m, d = mem.shape

    def body(mem_ref, g_ref, wk_ref, wv_ref, mn_ref, xk_ref, xv_ref):
        mn = _rms_fwd(mem_ref[...], g_ref[...])[0].astype(BF16)
        mn_ref[...] = mn
        xk_ref[...] = _dot(mn, wk_ref[...]).astype(BF16)
        xv_ref[...] = _dot(mn, wv_ref[...]).astype(BF16)

    return pl.pallas_call(
        body, name="fwd_mem",
        out_shape=[jax.ShapeDtypeStruct((m, d), BF16)] * 3,
        compiler_params=_params(),
    )(mem, g, wk, wv)


def _softmax(sc):
    p = jnp.exp(sc - jnp.max(sc, axis=-1, keepdims=True))
    return p / jnp.sum(p, axis=-1, keepdims=True)


def _fwd_xattn(h1, g, wq, xk, xv, wo):
    s, d = h1.shape
    m = xk.shape[0]
    tm = _row_tile(s)
    hd = d // XATT_HEADS

    def body(h_ref, g_ref, wq_ref, xk_ref, xv_ref, wo_ref, h2_ref, hn_ref, xq_ref, xo_ref):
        h = h_ref[...]
        hn = _rms_fwd(h, g_ref[...])[0].astype(BF16)
        hn_ref[...] = hn
        xq = _dot(hn, wq_ref[...]).astype(BF16)
        xq_ref[...] = xq
        heads = [slice(i * hd, (i + 1) * hd) for i in range(XATT_HEADS)]
        scores = [_dot_nt(xq[:, cols], xk_ref[:, cols]) * hd ** -0.5 for cols in heads]
        probs = [_softmax(sc).astype(BF16) for sc in scores]
        for cols, pr in zip(heads, probs):
            xo_ref[:, cols] = _dot(pr, xv_ref[:, cols]).astype(BF16)
        h2_ref[...] = h + _dot(xo_ref[...], wo_ref[...])

    row = pl.BlockSpec((tm, d), lambda i: (i, 0))
    return pl.pallas_call(
        body, name="fwd_xattn", grid=(s // tm,),
        in_specs=[row, _full((1, d)), _full((d, d)), _full((m, d)), _full((m, d)), _full((d, d))],
        out_specs=[row] * 4,
        out_shape=[jax.ShapeDtypeStruct((s, d), F32)] + [jax.ShapeDtypeStruct((s, d), BF16)] * 3,
        compiler_params=_params("parallel"),
    )(h1, g, wq, xk, xv, wo)


def _fwd_mlp_loss(h2, g, w_up, w_down, gf, target):
    s, d = h2.shape
    nsh, _, f = w_up.shape
    per = nsh
    fb = per * f
    nb = nsh // per
    tm = _row_tile(s)
    once = pl.Buffered(1)

    def body(h_ref, g_ref, wu_ref, wd_ref, gf_ref, t_ref,
             hn_ref, act_ref, dh_ref, dhb_ref, loss_ref, ggf_ref, acc):
        i, k = pl.program_id(0), pl.program_id(1)

        @pl.when(k == 0)
        def _():
            hn_ref[...] = _rms_fwd(h_ref[...], g_ref[...])[0].astype(BF16)
            acc[...] = jnp.zeros_like(acc)

        @pl.when((i == 0) & (k == 0))
        def _():
            loss_ref[...] = jnp.zeros_like(loss_ref)
            ggf_ref[...] = jnp.zeros_like(ggf_ref)

        hn = hn_ref[...]
        ups = [_dot(hn, wu_ref[c]) for c in range(per)]
        for c, u in enumerate(ups):
            act_ref[:, c * f:(c + 1) * f] = jnp.square(jnp.maximum(u, 0.0)).astype(BF16)
        acc[...] += _dot(act_ref[...], wd_ref[...])

        @pl.when(k == nb - 1)
        def _():
            h3 = h_ref[...] + acc[...]
            gfv = gf_ref[...]
            y, _ = _rms_fwd(h3, gfv)
            err = y - t_ref[...]
            loss_ref[...] += 0.5 * jnp.sum(_mean(err * err))
            dh3, gg = _rms_bwd(h3, gfv, err * (1.0 / d))
            ggf_ref[...] += jnp.sum(gg, axis=0, keepdims=True)
            dh_ref[...] = dh3
            dhb_ref[...] = dh3.astype(BF16)

    row = pl.BlockSpec((tm, d), lambda i, k: (i, 0))
    return pl.pallas_call(
        body, name="fwd_mlp_loss", grid=(s // tm, nb),
        in_specs=[row, _full((1, d)),
                  pl.BlockSpec((per, d, f), lambda i, k: (k, 0, 0), pipeline_mode=once),
                  pl.BlockSpec((fb, d), lambda i, k: (k, 0), pipeline_mode=once),
                  _full((1, d)), row],
        out_specs=[row, pl.BlockSpec((tm, fb), lambda i, k: (i, k)), row, row,
                   _full((1, LANES)), _full((1, d))],
        out_shape=[jax.ShapeDtypeStruct((s, d), BF16), jax.ShapeDtypeStruct((s, nsh * f), BF16),
                   jax.ShapeDtypeStruct((s, d), F32), jax.ShapeDtypeStruct((s, d), BF16),
                   jax.ShapeDtypeStruct((1, LANES), F32), jax.ShapeDtypeStruct((1, d), F32)],
        scratch_shapes=[pltpu.VMEM((tm, d), F32)],
        compiler_params=_params("arbitrary", "arbitrary"),
    )(h2, g, w_up, w_down, gf, target)


def _bwd_mlp(dh3, dh3b, act, w_up_t, w_down, h2, g):
    s, d = h2.shape
    ff = w_down.shape[0]
    per = N_DEV
    fb = per * (ff // N_DEV)
    nb = ff // fb
    tm = _row_tile(s)
    once = pl.Buffered(1)

    def body(dh_ref, dhb_ref, act_ref, wut_ref, wd_ref, h_ref, g_ref,
             du_ref, dh2_ref, dh2b_ref, gg_ref, acc):
        i, k = pl.program_id(0), pl.program_id(1)

        @pl.when(k == 0)
        def _():
            acc[...] = jnp.zeros_like(acc)

        @pl.when((i == 0) & (k == 0))
        def _():
            gg_ref[...] = jnp.zeros_like(gg_ref)

        dhb = dhb_ref[...]
        f = fb // per
        shards = [slice(c * f, (c + 1) * f) for c in range(per)]
        dacts = [_dot_nt(dhb, wd_ref[cols, :]) for cols in shards]
        for cols, dact in zip(shards, dacts):
            du_ref[:, cols] = (dact * (2.0 * jnp.sqrt(act_ref[:, cols].astype(F32)))).astype(BF16)
        acc[...] += _dot(du_ref[...], wut_ref[...])

        @pl.when(k == nb - 1)
        def _():
            dh, gg = _rms_bwd(h_ref[...], g_ref[...], acc[...])
            gg_ref[...] += jnp.sum(gg, axis=0, keepdims=True)
            dh2 = dh_ref[...] + dh
            dh2_ref[...] = dh2
            dh2b_ref[...] = dh2.astype(BF16)

    row = pl.BlockSpec((tm, d), lambda i, k: (i, 0))
    col = pl.BlockSpec((tm, fb), lambda i, k: (i, k))
    wblk = pl.BlockSpec((fb, d), lambda i, k: (k, 0), pipeline_mode=once)
    return pl.pallas_call(
        body, name="bwd_mlp", grid=(s // tm, nb),
        in_specs=[row, row, col, wblk, wblk, row, _full((1, d))],
        out_specs=[col, row, row, _full((1, d))],
        out_shape=[jax.ShapeDtypeStruct((s, ff), BF16), jax.ShapeDtypeStruct((s, d), F32),
                   jax.ShapeDtypeStruct((s, d), BF16), jax.ShapeDtypeStruct((1, d), F32)],
        scratch_shapes=[pltpu.VMEM((tm, d), F32)],
        compiler_params=_params("arbitrary", "arbitrary"),
    )(dh3, dh3b, act, w_up_t, w_down, h2, g)


def _bwd_xattn(dh2, dh2b, h1, g, xq, xk, xv, wq, wo, sides=()):
    s, d = h1.shape
    m = xk.shape[0]
    tm = _row_tile(s)
    hd = d // XATT_HEADS
    scale = hd ** -0.5

    def body(dh_ref, dhb_ref, h_ref, g_ref, xq_ref, xk_ref, xv_ref, wq_ref, wo_ref,
             dh1_ref, dh1b_ref, dxq_ref, dxk_ref, dxv_ref, gg_ref):
        @pl.when(pl.program_id(0) == 0)
        def _():
            dxk_ref[...] = jnp.zeros_like(dxk_ref)
            dxv_ref[...] = jnp.zeros_like(dxv_ref)
            gg_ref[...] = jnp.zeros_like(gg_ref)

        dxo = _dot_nt(dhb_ref[...], wo_ref[...])
        heads = [slice(i * hd, (i + 1) * hd) for i in range(XATT_HEADS)]
        dxos = [dxo[:, cols].astype(BF16) for cols in heads]
        scores = [_dot_nt(xq_ref[:, cols], xk_ref[:, cols]) * scale for cols in heads]
        dprs = [_dot_nt(dxo_h, xv_ref[:, cols]) for dxo_h, cols in zip(dxos, heads)]
        probs, dscs = [], []
        for sc, dpr in zip(scores, dprs):
            pr = _softmax(sc)
            dscs.append((pr * (dpr - jnp.sum(dpr * pr, axis=-1, keepdims=True)) * scale).astype(BF16))
            probs.append(pr.astype(BF16))
        for cols, dsc, pr, dxo_h in zip(heads, dscs, probs, dxos):
            dxq_ref[:, cols] = _dot(dsc, xk_ref[:, cols]).astype(BF16)
            dxk_ref[:, cols] += _dot_tn(dsc, xq_ref[:, cols])
            dxv_ref[:, cols] += _dot_tn(pr, dxo_h)
        dh, gg = _rms_bwd(h_ref[...], g_ref[...], _dot_nt(dxq_ref[...], wq_ref[...]))
        gg_ref[...] += jnp.sum(gg, axis=0, keepdims=True)
        dh1 = dh_ref[...] + dh
        dh1_ref[...] = dh1
        dh1b_ref[...] = dh1.astype(BF16)

    row = pl.BlockSpec((tm, d), lambda i: (i, 0))
    return _hosted_call(
        body, sides, name="bwd_xattn", grid=(s // tm,),
        in_specs=[row, row, row, _full((1, d)), row, _full((m, d)), _full((m, d)), _full((d, d)), _full((d, d))],
        out_specs=[row, row, row, _full((m, d)), _full((m, d)), _full((1, d))],
        out_shape=[jax.ShapeDtypeStruct((s, d), F32), jax.ShapeDtypeStruct((s, d), BF16),
                   jax.ShapeDtypeStruct((s, d), BF16), jax.ShapeDtypeStruct((m, d), F32),
                   jax.ShapeDtypeStruct((m, d), F32), jax.ShapeDtypeStruct((1, d), F32)],
        scratch_shapes=[],
        args=[dh2, dh2b, h1, g, xq, xk, xv, wq, wo])


def _bwd_mem(mem, g, mn, dxk, dxv, wk, wv):
    m, d = mem.shape

    def body(mem_ref, g_ref, mn_ref, dxk_ref, dxv_ref, wk_ref, wv_ref, gk_ref, gv_ref, gg_ref):
        dk, dv = dxk_ref[...].astype(BF16), dxv_ref[...].astype(BF16)
        gk_ref[...] = _dot_tn(mn_ref[...], dk).astype(BF16)
        gv_ref[...] = _dot_tn(mn_ref[...], dv).astype(BF16)
        dmn = _dot_nt(dk, wk_ref[...]) + _dot_nt(dv, wv_ref[...])
        _, gg = _rms_bwd(mem_ref[...], g_ref[...], dmn)
        gg_ref[...] = jnp.sum(gg, axis=0, keepdims=True)

    return pl.pallas_call(
        body, name="bwd_mem",
        out_shape=[jax.ShapeDtypeStruct((d, d), BF16), jax.ShapeDtypeStruct((d, d), BF16),
                   jax.ShapeDtypeStruct((1, d), F32)],
        compiler_params=_params(),
    )(mem, g, mn, dxk, dxv, wk, wv)


def _bwd_mix_out(dh1b, w_out, cat, head_ones):
    s, d = dh1b.shape
    tm = _row_tile(s)
    dils = DILATIONS[1:]
    nd = len(dils)

    def body(dh_ref, w_ref, cat_ref, ones_ref, dcat_ref, dsum_ref, *rest):
        da_str, ds_str, dbuf, sbuf = rest[:nd], rest[nd:2 * nd], rest[2 * nd], rest[2 * nd + 1]
        dcat = _dot_nt(dh_ref[...], w_ref[...])
        dcat_ref[...] = dcat.astype(BF16)
        datt = dcat[:, :D_ATT]
        prod = datt * cat_ref[...].astype(F32)
        hi = prod.astype(BF16)
        lo = (prod - hi.astype(F32)).astype(BF16)
        dsum = _dot(hi, ones_ref[...]) + _dot(lo, ones_ref[...])
        dsum_ref[...] = dsum
        _store_blocks(dbuf, datt)
        _store_blocks(sbuf, dsum)
        for da_ref, ds_ref in zip(da_str, ds_str):
            _write_strided(dbuf, da_ref)
            _write_strided(sbuf, ds_ref)

    row = lambda w: pl.BlockSpec((tm, w), lambda i: (i, 0))
    strided = [_strided_spec(dil, tm, D_ATT) for dil in dils]
    return pl.pallas_call(
        body, name="bwd_mix_out", grid=(s // tm,),
        in_specs=[row(d), _full((d, d)), row(D_ATT), _full((D_ATT, D_ATT))],
        out_specs=[row(d), row(D_ATT)] + strided + strided,
        out_shape=[jax.ShapeDtypeStruct((s, d), BF16), jax.ShapeDtypeStruct((s, D_ATT), F32)]
        + [_strided_shape(dil, s, D_ATT, BF16) for dil in dils]
        + [_strided_shape(dil, s, D_ATT, F32) for dil in dils],
        scratch_shapes=[_lane_scratch(tm, D_ATT)] * 2,
        compiler_params=_params("parallel"),
    )(dh1b, w_out, cat, head_ones)


def _bwd_conv(dcat, c1, ag, cw, lg, lb, sides=()):
    s = ag.shape[0]
    tm = _row_tile(s)
    nt = s // tm

    def body(dp, dm, dn, cp, cm, cn, agp, agm, agn, cw_ref, lg_ref, lb_ref,
             dag_ref, gcw_ref, gcb_ref, glg_ref, glb_ref, ubuf, ush, dbuf, dsh, gacc):
        i = pl.program_id(0)

        @pl.when(i == 0)
        def _():
            gacc[...] = jnp.zeros_like(gacc)
            gcb_ref[...] = jnp.zeros_like(gcb_ref)
            glg_ref[...] = jnp.zeros_like(glg_ref)
            glb_ref[...] = jnp.zeros_like(glb_ref)

        lgv, lbv = lg_ref[...], lb_ref[...]

        def norm_bwd(dco, c1v):
            xc = c1v - _mean(c1v)
            rs = lax.rsqrt(_mean(xc * xc) + EPS)
            z = xc * rs
            ln = z * lgv + lbv
            sg = _sigmoid(ln)
            dln = dco.astype(F32) * (sg * (1.0 + ln * (1.0 - sg)))
            dz = dln * lgv
            return rs * (dz - _mean(dz) - z * _mean(dz * z)), dln, z

        def sublane_sums(v):
            out = v[0:SUBLANES]
            for r in range(SUBLANES, CONV_ROWS, SUBLANES):
                out = out + v[r:r + SUBLANES]
            return out

        dc_m, dln, z = norm_bwd(dm[...], cm[...])
        glg_ref[...] += jnp.sum(dln * z, axis=0, keepdims=True)
        glb_ref[...] += jnp.sum(dln, axis=0, keepdims=True)
        gcb_ref[...] += jnp.sum(dc_m, axis=0, keepdims=True)
        _fill_halo(dbuf, i, nt, tm, norm_bwd(dp[...], cp[...])[0], dc_m, norm_bwd(dn[...], cn[...])[0])
        _fill_halo(ubuf, i, nt, tm, _glu(agp[...]), _glu(agm[...]), _glu(agn[...]))
        _shift_copies(dbuf, dsh, tm)
        _shift_copies(ubuf, ush, tm)

        w = cw_ref[...]

        def chunk(c, carry):
            base = pl.multiple_of(c * CONV_ROWS, CONV_ROWS)
            rows = pl.ds(base, CONV_ROWS)
            dc = dbuf[pl.ds(pl.multiple_of(base + BF16_ROWS, SUBLANES), CONV_ROWS), :]
            du = jnp.zeros((CONV_ROWS, D_CONV), F32)
            for k in range(CONV_WIDTH):
                du = du + _tap(dbuf, dsh, CONV_WIDTH - k, CONV_ROWS, base) * w[k:k + 1, :]
                gacc[k] += sublane_sums(dc * _tap(ubuf, ush, k + 1, CONV_ROWS, base))
            a = agm[rows, :D_CONV].astype(F32)
            sg = _sigmoid(agm[rows, D_CONV:].astype(F32))
            dag_ref[rows, :D_CONV] = (du * sg).astype(BF16)
            dag_ref[rows, D_CONV:] = (du * a * sg * (1.0 - sg)).astype(BF16)
            return carry

        lax.fori_loop(0, tm // CONV_ROWS, chunk, 0)

        @pl.when(i == nt - 1)
        def _():
            tap = lax.broadcasted_iota(jnp.int32, (32, D_CONV), 0)
            gcw = jnp.zeros((32, D_CONV), F32)
            for k in range(CONV_WIDTH):
                gcw = jnp.where(tap == k, jnp.sum(gacc[k], axis=0, keepdims=True), gcw)
            gcw_ref[...] = gcw

    vec = _full((1, D_CONV))
    nblk = s // BF16_ROWS
    return _hosted_call(
        body, sides, name="bwd_conv", grid=(nt,),
        in_specs=_halo_specs(tm, D_CONV, 1)(nblk) + _halo_specs(tm, D_CONV)(nblk) + _halo_specs(tm, 2 * D_CONV)(nblk)
        + [_full((32, D_CONV)), vec, vec],
        out_specs=[pl.BlockSpec((tm, 2 * D_CONV), lambda i: (i, 0)), _full((32, D_CONV)), vec, vec, vec],
        out_shape=[jax.ShapeDtypeStruct((s, 2 * D_CONV), BF16), jax.ShapeDtypeStruct((32, D_CONV), F32)]
        + [jax.ShapeDtypeStruct((1, D_CONV), F32)] * 3,
        scratch_shapes=_halo_scratch(tm) + _halo_scratch(tm) + [pltpu.VMEM((32, SUBLANES, D_CONV), F32)],
        args=[dcat, dcat, dcat, c1, c1, c1, ag, ag, ag, cw, lg, lb])


def _swa_bwd(qkv3, do3, lt3, ds3, band, name, sides=()):
    dil, length, _ = qkv3.shape
    nb = length // BQ
    scale = HEAD_DIM ** -0.5
    group = _residues_per_step(dil)
    assert WIN == 2 * BQ and BQ == 2 * HALF

    def body(q_ref, kp, km, kn, vp, vm, vn, do_ref, l_ref, s_ref, band_ref, dq_ref, dkv_ref, kwin, vwin, pend, keep):
        j = pl.program_id(1)

        @pl.when(j == 0)
        def _():
            pend[...] = jnp.zeros_like(pend)
            keep[...] = jnp.zeros_like(keep)

        @pl.when(j < nb)
        def _():
            _fill_window(kwin, kp, km, kn)
            _fill_window(vwin, vp, vm, vn)
            bias = _window_bias(band_ref, j, length)
            first = _first_head()
            pairs = [(g, slice(pr * LANES, (pr + 1) * LANES)) for g in range(group) for pr in range(D_ATT // LANES)]
            qs = [_stack_heads(q_ref[g, :, cols] * scale, first) for g, cols in pairs]
            dos = [_stack_heads(do_ref[g, :, cols], first) for g, cols in pairs]
            scores = [_dot_nt(q, kwin[g, :, cols]) for q, (g, cols) in zip(qs, pairs)]
            dps = [_dot_nt(do, vwin[g, :, cols]) for do, (g, cols) in zip(dos, pairs)]
            probs, dscs = [], []
            for (g, cols), sc, dp in zip(pairs, scores, dps):
                p = jnp.exp(sc + (bias - _stack_cols(l_ref[g, :, cols], first)))
                dscs.append((p * (dp - _stack_cols(s_ref[g, :, cols], first))).astype(BF16))
                probs.append(p.astype(BF16))
            dqs = [_dot(dsc, kwin[g, :, cols] * scale) for dsc, (g, cols) in zip(dscs, pairs)]
            dks = [_dot_tn(dsc, q) for dsc, q in zip(dscs, qs)]
            dvs = [_dot_tn(p, do) for p, do in zip(probs, dos)]
            for (g, cols), dq, dk, dv in zip(pairs, dqs, dks, dvs):
                dq_ref[g, :, cols] = _unstack_heads(dq, first).astype(BF16)
                for part, at in ((dk, cols), (dv, slice(D_ATT + cols.start, D_ATT + cols.stop))):
                    dkv_ref[g, :HALF, at] = keep[g, :, at].astype(BF16)
                    dkv_ref[g, HALF:, at] = (pend[g, :HALF, at] + part[:HALF]).astype(BF16)
                    keep[g, :, at] = pend[g, HALF:, at] + part[HALF:BQ]
                    pend[g, :, at] = part[BQ:]

        @pl.when(j == nb)
        def _():
            dkv_ref[:, :HALF] = keep[...].astype(BF16)
            dkv_ref[:, HALF:] = pend[:, :HALF].astype(BF16)

    def clamp(idx):
        return lambda r, j: idx(r, jnp.minimum(j, nb - 1))

    main = pl.BlockSpec((group, BQ, D_ATT), clamp(lambda r, j: (r, j, 0)))
    wins = [pl.BlockSpec(sp.block_shape, clamp(sp.index_map))
            for c in (1, 2) for sp in _win_in_specs(length, c, D_ATT, group)]
    return _hosted_call(
        body, sides, name=name, grid=(dil // group, nb + 1),
        in_specs=[main] + wins + [main] * 3 + [_full((2 * BQ, WIN))],
        out_specs=[main, pl.BlockSpec((group, BQ, 2 * D_ATT), lambda r, j: (r, jnp.maximum(j - 1, 0), 0))],
        out_shape=[jax.ShapeDtypeStruct((dil, length, D_ATT), BF16),
                   jax.ShapeDtypeStruct((dil, length, 2 * D_ATT), BF16)],
        scratch_shapes=[pltpu.VMEM((group, WIN, D_ATT), BF16)] * 2
        + [pltpu.VMEM((group, BQ, 2 * D_ATT), F32), pltpu.VMEM((group, HALF, 2 * D_ATT), F32)],
        args=[qkv3] * 7 + [do3, lt3, ds3, band])


def _bwd_in(dqs, dkvs, dag, w_in, x, g, dh1, rot):
    s, d = x.shape
    n = w_in.shape[1]
    tm = _row_tile(s)
    dils = DILATIONS[1:]
    nd = len(dils)

    def body(q1, *rest):
        q_str, kv1, kv_str = rest[:nd], rest[nd], rest[nd + 1:2 * nd + 1]
        dag_ref, w_ref, x_ref, g_ref, dh_ref, c_ref, a_ref, b_ref, gx_ref, dy_ref, gg_ref, qbuf, kvbuf = rest[2 * nd + 1:]

        @pl.when(pl.program_id(0) == 0)
        def _():
            gg_ref[...] = jnp.zeros_like(gg_ref)

        _store_blocks(qbuf, q1[0].astype(F32))
        _store_blocks(kvbuf, kv1[0].astype(F32))
        for ref in q_str:
            _read_strided(ref, qbuf, add=True)
        for ref in kv_str:
            _read_strided(ref, kvbuf, add=True)
        dq, dkv = _load_blocks(qbuf), _load_blocks(kvbuf)
        reps = (1, D_ATT // LANES)
        cc, aa, bb = jnp.tile(c_ref[...], reps), jnp.tile(a_ref[...], reps), jnp.tile(b_ref[...], reps)
        for blk, t in enumerate((dq, dkv[:, :D_ATT])):
            dt = t * cc + pltpu.roll(t * aa, ROT_DIM // 2, 1) + pltpu.roll(t * bb, D_ATT - ROT_DIM // 2, 1)
            dy_ref[:, blk * D_ATT:(blk + 1) * D_ATT] = dt.astype(BF16)
        dy_ref[:, 2 * D_ATT:3 * D_ATT] = dkv[:, D_ATT:].astype(BF16)
        dy_ref[:, 3 * D_ATT:] = dag_ref[...]
        dx, gg = _rms_bwd(x_ref[...], g_ref[...], _dot_nt(dy_ref[...], w_ref[...]))
        gg_ref[...] += jnp.sum(gg, axis=0, keepdims=True)
        gx_ref[...] = dh_ref[...] + dx

    row = lambda w: pl.BlockSpec((tm, w), lambda i: (i, 0))
    def strided(width):
        return [pl.BlockSpec((1, tm, width), lambda i: (0, i, 0))] + [_strided_spec(dil, tm, width) for dil in dils]

    return pl.pallas_call(
        body, name="bwd_in", grid=(s // tm,),
        in_specs=strided(D_ATT) + strided(2 * D_ATT)
        + [row(2 * D_CONV), _full((d, n)), row(d), _full((1, d)), row(d)] + [row(LANES)] * 3,
        out_specs=[row(d), row(n), _full((1, d))],
        out_shape=[jax.ShapeDtypeStruct((s, d), F32), jax.ShapeDtypeStruct((s, n), BF16),
                   jax.ShapeDtypeStruct((1, d), F32)],
        scratch_shapes=[_lane_scratch(tm, D_ATT), _lane_scratch(tm, 2 * D_ATT)],
        compiler_params=_params("arbitrary"),
    )(*dqs, *dkvs, dag, w_in, x, g, dh1, *rot)


def _wgrad(a, b, name, a_blk=None, b_blk=None, stack=None, tm=1024):
    s, ka = a.shape
    nb = b.shape[1]
    a_blk, b_blk = a_blk or ka, b_blk or nb
    na, nbl = ka // a_blk, nb // b_blk
    assert na == 1 or nbl == 1
    tm = min(tm, s)
    nt = s // tm
    per = b_blk // stack if stack else 0

    def body(a_ref, b_ref, o_ref, acc):
        t = pl.program_id(1)

        @pl.when(t == 0)
        def _():
            acc[...] = jnp.zeros_like(acc)

        acc[...] += _dot_tn(a_ref[...], b_ref[...])

        @pl.when(t == nt - 1)
        def _():
            if stack:
                for c in range(per):
                    o_ref[c] = acc[:, c * stack:(c + 1) * stack].astype(BF16)
            else:
                o_ref[...] = acc[...].astype(BF16)

    if stack:
        out_spec = pl.BlockSpec((per, ka, stack), lambda k, t: (k, 0, 0))
        out_shape = jax.ShapeDtypeStruct((nb // stack, ka, stack), BF16)
    elif na > 1:
        out_spec = pl.BlockSpec((a_blk, nb), lambda k, t: (k, 0))
        out_shape = jax.ShapeDtypeStruct((ka, nb), BF16)
    else:
        out_spec = pl.BlockSpec((ka, b_blk), lambda k, t: (0, k))
        out_shape = jax.ShapeDtypeStruct((ka, nb), BF16)
    return pl.pallas_call(
        body, name=name, grid=(na * nbl, nt),
        in_specs=[pl.BlockSpec((tm, a_blk), (lambda k, t: (t, k)) if na > 1 else (lambda k, t: (t, 0))),
                  pl.BlockSpec((tm, b_blk), (lambda k, t: (t, k)) if nbl > 1 else (lambda k, t: (t, 0)))],
        out_specs=out_spec, out_shape=out_shape,
        scratch_shapes=[pltpu.VMEM((a_blk, b_blk), F32)],
        compiler_params=_params("parallel", "arbitrary"),
    )(a, b)


def _adamw(w, gsrc, m, v, name, transposed=False):
    summed = gsrc.ndim == w.ndim + 1
    rows, cols = w.shape
    assert gsrc.shape[-2:] == ((cols, rows) if transposed else (rows, cols)) and (summed or not transposed)
    tr = rows if rows <= 256 else 256
    assert rows % tr == 0
    c1 = 1.0 - ADAM_B1 ** ADAM_STEP
    c2 = 1.0 - ADAM_B2 ** ADAM_STEP

    def body(w_ref, g_ref, m_ref, v_ref, go_ref, d_ref, mo_ref, vo_ref):
        if summed:
            g = g_ref[0].astype(F32)
            for i in range(1, N_DEV):
                g = g + g_ref[i].astype(F32)
            if transposed:
                g = g.T
        else:
            g = g_ref[...]
        mn = ADAM_B1 * m_ref[...] + (1.0 - ADAM_B1) * g
        vn = ADAM_B2 * v_ref[...] + (1.0 - ADAM_B2) * jnp.square(g)
        go_ref[...] = g
        mo_ref[...] = mn
        vo_ref[...] = vn
        d_ref[...] = -ADAM_LR * ((mn / c1) / (jnp.sqrt(vn / c2) + ADAM_EPS) + ADAM_WD * w_ref[...])

    blk = pl.BlockSpec((tr, cols), lambda i: (i, 0))
    if transposed:
        gblk = pl.BlockSpec((N_DEV, cols, tr), lambda i: (0, 0, i))
    else:
        gblk = pl.BlockSpec((N_DEV, tr, cols), lambda i: (0, i, 0)) if summed else blk
    return pl.pallas_call(
        body, name=name, grid=(rows // tr,),
        in_specs=[blk, gblk, blk, blk], out_specs=[blk] * 4,
        out_shape=[jax.ShapeDtypeStruct(w.shape, F32)] * 4,
        compiler_params=_params("parallel"),
    )(w, gsrc, m, v)


def _sum_slots(g, name):
    _, rows, cols = g.shape

    def body(g_ref, o_ref):
        acc = g_ref[0]
        for i in range(1, N_DEV):
            acc = acc + g_ref[i]
        o_ref[...] = acc

    return pl.pallas_call(body, name=name, out_shape=jax.ShapeDtypeStruct((rows, cols), F32),
                          compiler_params=_params())(g)


def kernel(x, mem, norm_mix_g, w_in, conv_w, conv_b, conv_ln_g, conv_ln_b, w_out, norm_x_g, norm_mem_g, w_xq, w_xk, w_xv, w_xo, norm_mlp_g, w_up, w_down, norm_final_g, loss_target, m_norm_mix_g, m_w_in, m_conv_w, m_conv_b, m_conv_ln_g, m_conv_ln_b, m_w_out, m_norm_x_g, m_norm_mem_g, m_w_xq, m_w_xk, m_w_xv, m_w_xo, m_norm_mlp_g, m_w_up, m_w_down, m_norm_final_g, v_norm_mix_g, v_w_in, v_conv_w, v_conv_b, v_conv_ln_g, v_conv_ln_b, v_w_out, v_norm_x_g, v_norm_mem_g, v_w_xq, v_w_xk, v_w_xv, v_w_xo, v_norm_mlp_g, v_w_up, v_w_down, v_norm_final_g):
    x2, mem2, tgt = x[0], mem[0], loss_target[0]
    s, d = x2.shape
    gf = norm_final_g[None, :]

    cw_local = jnp.pad(conv_w[0], ((0, 1), (0, LANES - conv_w.shape[2])))
    win_g, cw_g = _exchange_call([_Exchange([w_in[0].astype(BF16), cw_local], gather=True)], "gather_w_in")
    w_in_f = jnp.transpose(win_g, (1, 0, 2)).reshape(d, -1)
    cw_f = jnp.transpose(cw_g[:, :, :conv_w.shape[2]], (1, 0, 2)).reshape(32, D_CONV)
    def stacked(*ws):
        return jnp.concatenate([w[0].astype(BF16) for w in ws], axis=0)

    def unstacked(g, *ws):
        offs = [0]
        for w in ws:
            offs.append(offs[-1] + w.shape[1])
        return [g[:, a:b, :].reshape(N_DEV * (b - a), d) for a, b in zip(offs, offs[1:])]

    late = [_Exchange([stacked(w_out, w_xq)], gather=True), _Exchange([w_up[0].astype(BF16)], gather=True),
            _Exchange([w_down[0].astype(BF16)], gather=True)]

    rot = _rotary_tables(s)
    band = _band_bias()
    xn, qkv, ag, *qkv_strided = _fwd_in(x2, norm_mix_g, w_in_f, rot)
    qkv3 = [qkv[None]] + qkv_strided
    outs, lses, gathered = [], [], []
    for dil, q3, side in zip(DILATIONS, qkv3, late):
        (o3, l3), got = _swa_fwd(q3, band, f"swa_fwd_d{dil}", [side])
        outs.append(o3)
        lses.append(l3)
        gathered += got
    rows_g, wup_g, wdown_g = gathered
    w_out_f, w_xq_f = unstacked(rows_g, w_out, w_xq)
    w_down_f = wdown_g.reshape(-1, d)
    w_up_t = jnp.swapaxes(wup_g, 1, 2).reshape(-1, d)
    (c1, conv_out), (rows_g,) = _fwd_conv(ag, cw_f, conv_b, conv_ln_g, conv_ln_b,
                                          [_Exchange([stacked(w_xk, w_xv, w_xo)], gather=True)])
    w_xk_f, w_xv_f, w_xo_f = unstacked(rows_g, w_xk, w_xv, w_xo)
    h1, cat, ltot, *lt_strided = _fwd_mix_out(outs, lses, conv_out, x2, w_out_f)
    mn, xk, xv = _fwd_mem(mem2, norm_mem_g, w_xk_f, w_xv_f)
    h2, hn2, xq, xo = _fwd_xattn(h1, norm_x_g, w_xq_f, xk, xv, w_xo_f)
    hn3, act, dh3, dh3b, loss_part, g_final = _fwd_mlp_loss(h2, norm_mlp_g, wup_g, w_down_f, gf, tgt)

    def scatter(*grads):
        return _Exchange([g.reshape(N_DEV, -1, g.shape[-1]) for g in grads], gather=False)

    f_blk = w_up.shape[2]
    du, dh2, dh2b, g_mlp = _bwd_mlp(dh3, dh3b, act, w_up_t, w_down_f, h2, norm_mlp_g)
    gw_up = _wgrad(hn3, du, "wgrad_up", b_blk=4 * f_blk, stack=f_blk)
    gw_down = _wgrad(dh3b, act, "wgrad_down", b_blk=4 * f_blk, stack=f_blk)
    (dh1, dh1b, dxq, dxk, dxv, g_x), (r_up,) = _bwd_xattn(
        dh2, dh2b, h1, norm_x_g, xq, xk, xv, w_xq_f, w_xo_f, [scatter(gw_up)])
    gw_xq = _wgrad(hn2, dxq, "wgrad_xq")
    gw_xo = _wgrad(xo, dh2b, "wgrad_xo")
    gw_xk, gw_xv, g_mem = _bwd_mem(mem2, norm_mem_g, mn, dxk, dxv, w_xk_f, w_xv_f)
    head = jnp.arange(D_ATT) // HEAD_DIM
    head_ones = (head[:, None] == head[None, :]).astype(BF16)
    dcat, dsum, *strided = _bwd_mix_out(dh1b, w_out_f, cat, head_ones)
    n_str = len(DILATIONS) - 1
    do3, lt3, ds3 = [dcat[None]] + strided[:n_str], [ltot[None]] + lt_strided, [dsum[None]] + strided[n_str:]
    gw_out = _wgrad(cat, dh1b, "wgrad_out")
    (dag, g_cw, g_cb, g_lg, g_lb), (r_down,) = _bwd_conv(dcat, c1, ag, cw_f, conv_ln_g, conv_ln_b, [scatter(gw_down)])
    hosted = [[scatter(gw_out, gw_xq)], [scatter(gw_xk, gw_xv)], [scatter(gw_xo)]]
    dqs, dkvs, landed = [], [], []
    for i, dil in enumerate(DILATIONS):
        (dq3, dkv3), got = _swa_bwd(qkv3[i], do3[i], lt3[i], ds3[i], band, f"swa_bwd_d{dil}", hosted[i])
        dqs.append(dq3)
        dkvs.append(dkv3)
        landed += got
    r_out, r_xq, r_xk, r_xv, r_xo = landed
    grad_x, dy, g_mix = _bwd_in(dqs, dkvs, dag, w_in_f, x2, norm_mix_g, dh1, rot)
    gw_in = _wgrad(xn, dy, "wgrad_in")

    def widen(t):
        return jnp.pad(t, ((0, 0), (0, d - t.shape[1])))

    n_in = w_in.shape[2]
    small = jnp.concatenate([g_mix, g_x, g_mem, g_mlp, g_final, widen(g_cb), widen(g_lg), widen(g_lb),
                             g_cw.reshape(16, d), widen(loss_part), jnp.zeros((7, d), F32)], axis=0)
    r_in, small_g = _exchange_call(
        [_Exchange([jnp.transpose(gw_in.reshape(d, N_DEV, n_in), (1, 0, 2))], gather=False),
         _Exchange([small], gather=True)], "scatter_w_in_gather_small")
    small_sum = _sum_slots(small_g, "sum_small_grads")
    loss = small_sum[24, 0]

    res = {}

    def step(name, w, gsrc, m, v, transposed=False):
        shape = w.shape
        w2, m2, v2 = (t.reshape(-1, shape[-1]) for t in (w, m, v))
        res[name] = [t.reshape(shape) for t in _adamw(w2, gsrc, m2, v2, "adamw_" + name, transposed)]

    step("w_in", w_in, r_in, m_w_in, v_w_in)
    step("w_up", w_up, r_up, m_w_up, v_w_up)
    step("w_out", w_out, r_out, m_w_out, v_w_out)
    step("w_xq", w_xq, r_xq, m_w_xq, v_w_xq)
    step("w_xk", w_xk, r_xk, m_w_xk, v_w_xk)
    step("w_xv", w_xv, r_xv, m_w_xv, v_w_xv)
    step("w_xo", w_xo, r_xo, m_w_xo, v_w_xo)
    step("w_down", w_down, r_down, m_w_down, v_w_down, transposed=True)

    me = _dev_index((lax.axis_index("x"), lax.axis_index("y"), lax.axis_index("c")))
    n_cw = conv_w.shape[2]
    g_cw_full = small_sum[8:24].reshape(32, D_CONV)[:CONV_WIDTH]
    g_cw_mine = lax.dynamic_slice_in_dim(g_cw_full, me * n_cw, n_cw, axis=1)
    step("conv_w", conv_w, g_cw_mine, m_conv_w, v_conv_w)

    vec_names = ["norm_mix_g", "norm_x_g", "norm_mem_g", "norm_mlp_g", "norm_final_g", "conv_b", "conv_ln_g", "conv_ln_b"]
    vec_w = [norm_mix_g, norm_x_g, norm_mem_g, norm_mlp_g, gf, conv_b, conv_ln_g, conv_ln_b]
    vec_m = [m_norm_mix_g, m_norm_x_g, m_norm_mem_g, m_norm_mlp_g, m_norm_final_g[None, :], m_conv_b, m_conv_ln_g, m_conv_ln_b]
    vec_v = [v_norm_mix_g, v_norm_x_g, v_norm_mem_g, v_norm_mlp_g, v_norm_final_g[None, :], v_conv_b, v_conv_ln_g, v_conv_ln_b]

    def pack(ts):
        return jnp.concatenate([widen(t) for t in ts], axis=0)

    packed = _adamw(pack(vec_w), small_sum[0:8], pack(vec_m), pack(vec_v), "adamw_vectors")
    for i, name in enumerate(vec_names):
        width = vec_w[i].shape[1]
        shape = (width,) if name == "norm_final_g" else (1, width)
        res[name] = [t[i, :width].reshape(shape) for t in packed]

    order = ["norm_mix_g", "w_in", "conv_w", "conv_b", "conv_ln_g", "conv_ln_b", "w_out", "norm_x_g", "norm_mem_g",
             "w_xq", "w_xk", "w_xv", "w_xo", "norm_mlp_g", "w_up", "w_down", "norm_final_g"]
    out = [loss, grad_x[None]]
    for kind in range(4):
        out += [res[name][kind] for name in order]
    return tuple(out)
```
